```python
import math
import jax, jax.numpy as jnp
from jax import lax
import numpy as np

D_MODEL = 1024
BATCH = 8
SEQ = 4096
DEPTH = 2

D_MIX = D_MODEL
D_A = D_MIX // 2
D_B = D_MIX - D_A
GROUP_DIM = 64
K_A = 3
K_B = 31
N_MEM = 256
N_XHEADS = 4
XHEAD_DIM = D_MODEL // N_XHEADS
D_FF = 4 * D_MODEL
D_IN = 3 * D_A + 2 * D_B
EPS = 1e-6

kernel_name = "hybrid_shortconv_conformer_memxattn"


def rms_norm(x, g):
    x32 = x.astype(jnp.float32)
    y = x32 * lax.rsqrt(jnp.mean(x32 * x32, axis=-1, keepdims=True) + EPS)
    return (y * g.astype(jnp.float32)).astype(x.dtype)


def layer_norm(x, g, b):
    x32 = x.astype(jnp.float32)
    mu = jnp.mean(x32, axis=-1, keepdims=True)
    var = jnp.mean(jnp.square(x32 - mu), axis=-1, keepdims=True)
    y = (x32 - mu) * lax.rsqrt(var + EPS)
    return (y * g.astype(jnp.float32) + b.astype(jnp.float32)).astype(x.dtype)


def causal_dwconv(x, w):
    k, c = w.shape
    return lax.conv_general_dilated(
        x, w.reshape(k, 1, c).astype(x.dtype),
        window_strides=(1,), padding=[(k - 1, 0)],
        dimension_numbers=("NWC", "WIO", "NWC"),
        feature_group_count=c)


def mixer_block(u, w_in, conv_a_w, conv_b_w, conv_b_bias, ln_b_g, ln_b_b, w_out):
    z = jnp.einsum("bsd,de->bse", u, w_in)
    b_a, c_a, h_a, v_b, g_b = jnp.split(
        z, [D_A, 2 * D_A, 3 * D_A, 3 * D_A + D_B], axis=-1)
    y_a = b_a * causal_dwconv(c_a * h_a, conv_a_w)
    glu = v_b * jax.nn.sigmoid(g_b)
    cb = causal_dwconv(glu, conv_b_w) + conv_b_bias.astype(glu.dtype)
    y_b = jax.nn.silu(layer_norm(cb, ln_b_g, ln_b_b))
    y = jnp.concatenate([y_a, y_b], axis=-1)
    return jnp.einsum("bse,ed->bsd", y, w_out)


def memory_cross_attention(q_in, m, w_q, w_kv, w_xo):
    b, s, _ = q_in.shape
    q = jnp.einsum("bsd,de->bse", q_in, w_q).reshape(b, s, N_XHEADS, XHEAD_DIM)
    kv = jnp.einsum("bmd,de->bme", m, w_kv)
    k, v = jnp.split(kv, 2, axis=-1)
    k = k.reshape(b, N_MEM, N_XHEADS, XHEAD_DIM)
    v = v.reshape(b, N_MEM, N_XHEADS, XHEAD_DIM)
    scores = jnp.einsum("bshd,bmhd->bhsm", q, k).astype(jnp.float32) * (1.0 / math.sqrt(XHEAD_DIM))
    p = jax.nn.softmax(scores, axis=-1).astype(v.dtype)
    o = jnp.einsum("bhsm,bmhd->bshd", p, v).reshape(b, s, D_MODEL)
    return jnp.einsum("bse,ed->bsd", o, w_xo)


def sqrelu_mlp(u, w_up, w_down):
    h = jnp.square(jax.nn.relu(jnp.einsum("bsd,df->bsf", u, w_up)))
    return jnp.einsum("bsf,fd->bsd", h, w_down)


def _fwd_setup_inputs(seed: int = 0) -> dict:
    key = jax.random.key(seed)
    ks = jax.random.split(key, 20)
    f32 = jnp.float32

    def nrm(k, shape, scale):
        return jax.random.normal(k, shape, f32) * scale

    def gain(k, shape):
        return 1.0 + 0.02 * jax.random.normal(k, shape, f32)

    res_scale = (2.0 * DEPTH) ** -0.5
    return {
        "x": nrm(ks[0], (BATCH, SEQ, D_MODEL), 1.0),
        "mem": nrm(ks[1], (BATCH, N_MEM, D_MODEL), 1.0),
        "norm_mix_g": gain(ks[2], (DEPTH, D_MODEL)),
        "w_in": nrm(ks[3], (DEPTH, D_MODEL, D_IN), D_MODEL ** -0.5),
        "conv_a_w": nrm(ks[4], (DEPTH, K_A, D_A), K_A ** -0.5),
        "conv_b_w": nrm(ks[5], (DEPTH, K_B, D_B), K_B ** -0.5),
        "conv_b_bias": nrm(ks[6], (DEPTH, D_B), 0.02),
        "ln_b_g": gain(ks[7], (DEPTH, D_B)),
        "ln_b_b": nrm(ks[8], (DEPTH, D_B), 0.02),
        "w_out": nrm(ks[9], (DEPTH, D_MIX, D_MODEL), D_MIX ** -0.5 * res_scale),
        "norm_x_g": gain(ks[10], (DEPTH, D_MODEL)),
        "norm_mem_g": gain(ks[11], (DEPTH, D_MODEL)),
        "w_q": nrm(ks[12], (DEPTH, D_MODEL, D_MODEL), D_MODEL ** -0.5),
        "w_kv": nrm(ks[13], (DEPTH, D_MODEL, 2 * D_MODEL), D_MODEL ** -0.5),
        "w_xo": nrm(ks[14], (DEPTH, D_MODEL, D_MODEL), D_MODEL ** -0.5 * res_scale),
        "norm_ffn_g": gain(ks[15], (DEPTH, D_MODEL)),
        "w_up": nrm(ks[16], (DEPTH, D_MODEL, D_FF), D_MODEL ** -0.5),
        "w_down": nrm(ks[17], (DEPTH, D_FF, D_MODEL), D_FF ** -0.5 * res_scale),
        "final_g": gain(ks[18], (D_MODEL,)),
    }


def _fwd_reference(x, mem, norm_mix_g, w_in, conv_a_w, conv_b_w, conv_b_bias, ln_b_g, ln_b_b, w_out,
              norm_x_g, norm_mem_g, w_q, w_kv, w_xo, norm_ffn_g, w_up, w_down, final_g):
    h = x
    for l in range(DEPTH):
        u = rms_norm(h, norm_mix_g[l])
        h = h + mixer_block(u, w_in[l], conv_a_w[l], conv_b_w[l], conv_b_bias[l],
                            ln_b_g[l], ln_b_b[l], w_out[l])
        q_in = rms_norm(h, norm_x_g[l])
        m = rms_norm(mem, norm_mem_g[l])
        h = h + memory_cross_attention(q_in, m, w_q[l], w_kv[l], w_xo[l])
        h = h + sqrelu_mlp(rms_norm(h, norm_ffn_g[l]), w_up[l], w_down[l])
    return rms_norm(h, final_g)


import jax as _jax
import jax.numpy as _jnp

TWIN_FORMAT = 'train_step'
FWD_PARAMS = ['x', 'mem', 'norm_mix_g', 'w_in', 'conv_a_w', 'conv_b_w', 'conv_b_bias', 'ln_b_g', 'ln_b_b', 'w_out', 'norm_x_g', 'norm_mem_g', 'w_q', 'w_kv', 'w_xo', 'norm_ffn_g', 'w_up', 'w_down', 'final_g']
TWIN_WEIGHTS = ['norm_mix_g', 'w_in', 'conv_a_w', 'conv_b_w', 'conv_b_bias', 'ln_b_g', 'ln_b_b', 'w_out', 'norm_x_g', 'norm_mem_g', 'w_q', 'w_kv', 'w_xo', 'norm_ffn_g', 'w_up', 'w_down', 'final_g']
TWIN_DIFF_INPUT = 'x'
TWIN_INPUTS = ['x', 'mem', 'norm_mix_g', 'w_in', 'conv_a_w', 'conv_b_w', 'conv_b_bias', 'ln_b_g', 'ln_b_b', 'w_out', 'norm_x_g', 'norm_mem_g', 'w_q', 'w_kv', 'w_xo', 'norm_ffn_g', 'w_up', 'w_down', 'final_g', 'loss_target', 'm_norm_mix_g', 'm_w_in', 'm_conv_a_w', 'm_conv_b_w', 'm_conv_b_bias', 'm_ln_b_g', 'm_ln_b_b', 'm_w_out', 'm_norm_x_g', 'm_norm_mem_g', 'm_w_q', 'm_w_kv', 'm_w_xo', 'm_norm_ffn_g', 'm_w_up', 'm_w_down', 'm_final_g', 'v_norm_mix_g', 'v_w_in', 'v_conv_a_w', 'v_conv_b_w', 'v_conv_b_bias', 'v_ln_b_g', 'v_ln_b_b', 'v_w_out', 'v_norm_x_g', 'v_norm_mem_g', 'v_w_q', 'v_w_kv', 'v_w_xo', 'v_norm_ffn_g', 'v_w_up', 'v_w_down', 'v_final_g']
TWIN_OUTPUTS = ['loss', 'grad_x', 'grad_norm_mix_g', 'grad_w_in', 'grad_conv_a_w', 'grad_conv_b_w', 'grad_conv_b_bias', 'grad_ln_b_g', 'grad_ln_b_b', 'grad_w_out', 'grad_norm_x_g', 'grad_norm_mem_g', 'grad_w_q', 'grad_w_kv', 'grad_w_xo', 'grad_norm_ffn_g', 'grad_w_up', 'grad_w_down', 'grad_final_g', 'delta_norm_mix_g', 'delta_w_in', 'delta_conv_a_w', 'delta_conv_b_w', 'delta_conv_b_bias', 'delta_ln_b_g', 'delta_ln_b_b', 'delta_w_out', 'delta_norm_x_g', 'delta_norm_mem_g', 'delta_w_q', 'delta_w_kv', 'delta_w_xo', 'delta_norm_ffn_g', 'delta_w_up', 'delta_w_down', 'delta_final_g', 'new_m_norm_mix_g', 'new_m_w_in', 'new_m_conv_a_w', 'new_m_conv_b_w', 'new_m_conv_b_bias', 'new_m_ln_b_g', 'new_m_ln_b_b', 'new_m_w_out', 'new_m_norm_x_g', 'new_m_norm_mem_g', 'new_m_w_q', 'new_m_w_kv', 'new_m_w_xo', 'new_m_norm_ffn_g', 'new_m_w_up', 'new_m_w_down', 'new_m_final_g', 'new_v_norm_mix_g', 'new_v_w_in', 'new_v_conv_a_w', 'new_v_conv_b_w', 'new_v_conv_b_bias', 'new_v_ln_b_g', 'new_v_ln_b_b', 'new_v_w_out', 'new_v_norm_x_g', 'new_v_norm_mem_g', 'new_v_w_q', 'new_v_w_kv', 'new_v_w_xo', 'new_v_norm_ffn_g', 'new_v_w_up', 'new_v_w_down', 'new_v_final_g']
TWIN_LEAF_KINDS = {'loss': 'loss', 'grad_x': 'grad_x', 'grad_norm_mix_g': 'grad_w', 'grad_w_in': 'grad_w', 'grad_conv_a_w': 'grad_w', 'grad_conv_b_w': 'grad_w', 'grad_conv_b_bias': 'grad_w', 'grad_ln_b_g': 'grad_w', 'grad_ln_b_b': 'grad_w', 'grad_w_out': 'grad_w', 'grad_norm_x_g': 'grad_w', 'grad_norm_mem_g': 'grad_w', 'grad_w_q': 'grad_w', 'grad_w_kv': 'grad_w', 'grad_w_xo': 'grad_w', 'grad_norm_ffn_g': 'grad_w', 'grad_w_up': 'grad_w', 'grad_w_down': 'grad_w', 'grad_final_g': 'grad_w', 'delta_norm_mix_g': 'delta_w', 'delta_w_in': 'delta_w', 'delta_conv_a_w': 'delta_w', 'delta_conv_b_w': 'delta_w', 'delta_conv_b_bias': 'delta_w', 'delta_ln_b_g': 'delta_w', 'delta_ln_b_b': 'delta_w', 'delta_w_out': 'delta_w', 'delta_norm_x_g': 'delta_w', 'delta_norm_mem_g': 'delta_w', 'delta_w_q': 'delta_w', 'delta_w_kv': 'delta_w', 'delta_w_xo': 'delta_w', 'delta_norm_ffn_g': 'delta_w', 'delta_w_up': 'delta_w', 'delta_w_down': 'delta_w', 'delta_final_g': 'delta_w', 'new_m_norm_mix_g': 'new_m', 'new_m_w_in': 'new_m', 'new_m_conv_a_w': 'new_m', 'new_m_conv_b_w': 'new_m', 'new_m_conv_b_bias': 'new_m', 'new_m_ln_b_g': 'new_m', 'new_m_ln_b_b': 'new_m', 'new_m_w_out': 'new_m', 'new_m_norm_x_g': 'new_m', 'new_m_norm_mem_g': 'new_m', 'new_m_w_q': 'new_m', 'new_m_w_kv': 'new_m', 'new_m_w_xo': 'new_m', 'new_m_norm_ffn_g': 'new_m', 'new_m_w_up': 'new_m', 'new_m_w_down': 'new_m', 'new_m_final_g': 'new_m', 'new_v_norm_mix_g': 'new_v', 'new_v_w_in': 'new_v', 'new_v_conv_a_w': 'new_v', 'new_v_conv_b_w': 'new_v', 'new_v_conv_b_bias': 'new_v', 'new_v_ln_b_g': 'new_v', 'new_v_ln_b_b': 'new_v', 'new_v_w_out': 'new_v', 'new_v_norm_x_g': 'new_v', 'new_v_norm_mem_g': 'new_v', 'new_v_w_q': 'new_v', 'new_v_w_kv': 'new_v', 'new_v_w_xo': 'new_v', 'new_v_norm_ffn_g': 'new_v', 'new_v_w_up': 'new_v', 'new_v_w_down': 'new_v', 'new_v_final_g': 'new_v'}


def _forward(args):
    return _fwd_reference(*[args[k] for k in FWD_PARAMS])


def _output_shape():
    out = _jax.eval_shape(lambda: _forward(_fwd_setup_inputs(0)))
    return out.shape, out.dtype

N_MICROBATCH = 1
ADAM_LR = 0.001
ADAM_B1 = 0.9
ADAM_B2 = 0.999
ADAM_EPS = 1e-08
ADAM_WD = 0.01
ADAM_STEP = 10
PER_EXAMPLE_BATCH_AXIS = {'x': 0, 'mem': 0, 'loss_target': 0}
SHARED_INPUTS = []
_WEIGHT_DTYPES = {'norm_mix_g': _jnp.float32, 'w_in': _jnp.float32, 'conv_a_w': _jnp.float32, 'conv_b_w': _jnp.float32, 'conv_b_bias': _jnp.float32, 'ln_b_g': _jnp.float32, 'ln_b_b': _jnp.float32, 'w_out': _jnp.float32, 'norm_x_g': _jnp.float32, 'norm_mem_g': _jnp.float32, 'w_q': _jnp.float32, 'w_kv': _jnp.float32, 'w_xo': _jnp.float32, 'norm_ffn_g': _jnp.float32, 'w_up': _jnp.float32, 'w_down': _jnp.float32, 'final_g': _jnp.float32}
MOMENT_SCALE = {'norm_mix_g': 1.142094e-01, 'w_in': 6.906542e-02, 'conv_a_w': 8.505017e-02, 'conv_b_w': 5.176845e-02, 'conv_b_bias': 1.151806e-01, 'ln_b_g': 6.084885e-02, 'ln_b_b': 5.415758e-02, 'w_out': 1.380238e-01, 'norm_x_g': 8.967507e-03, 'norm_mem_g': 1.266079e-02, 'w_q': 8.671251e-03, 'w_kv': 8.910772e-03, 'w_xo': 1.799613e-02, 'norm_ffn_g': 1.001714e-01, 'w_up': 5.017789e-02, 'w_down': 1.839530e-01, 'final_g': 3.218088e+01}


def _to_microbatches(a, axis):
    t = _jnp.moveaxis(a, axis, 0)
    t = t.reshape((N_MICROBATCH, t.shape[0] // N_MICROBATCH) + t.shape[1:])
    return _jnp.moveaxis(t, 1, axis + 1)


def setup_inputs(seed: int = 0) -> dict:
    inp = _fwd_setup_inputs(seed)
    key = _jax.random.fold_in(_jax.random.key(seed), 7919)
    shape, _ = _output_shape()
    out = dict(inp)
    out["loss_target"] = _jax.random.normal(_jax.random.fold_in(key, 0), shape, _jnp.float32)
    for i, name in enumerate(TWIN_WEIGHTS):
        w = inp[name].astype(_jnp.float32)
        if MOMENT_SCALE is None:
            s = _jnp.sqrt(_jnp.mean(_jnp.square(w)) + 1e-30)
        else:
            s = MOMENT_SCALE[name]
        km, kv = _jax.random.split(_jax.random.fold_in(key, i + 1))
        out[name] = w
        out["m_" + name] = s * _jax.random.normal(km, w.shape, _jnp.float32)
        out["v_" + name] = (s * s) * _jax.random.uniform(kv, w.shape, _jnp.float32, 0.5, 1.5)
    if N_MICROBATCH > 1:
        for name, axis in PER_EXAMPLE_BATCH_AXIS.items():
            out[name] = _to_microbatches(out[name], axis)
    return {'x': out['x'], 'mem': out['mem'], 'norm_mix_g': out['norm_mix_g'], 'w_in': out['w_in'], 'conv_a_w': out['conv_a_w'], 'conv_b_w': out['conv_b_w'], 'conv_b_bias': out['conv_b_bias'], 'ln_b_g': out['ln_b_g'], 'ln_b_b': out['ln_b_b'], 'w_out': out['w_out'], 'norm_x_g': out['norm_x_g'], 'norm_mem_g': out['norm_mem_g'], 'w_q': out['w_q'], 'w_kv': out['w_kv'], 'w_xo': out['w_xo'], 'norm_ffn_g': out['norm_ffn_g'], 'w_up': out['w_up'], 'w_down': out['w_down'], 'final_g': out['final_g'], 'loss_target': out['loss_target'], 'm_norm_mix_g': out['m_norm_mix_g'], 'm_w_in': out['m_w_in'], 'm_conv_a_w': out['m_conv_a_w'], 'm_conv_b_w': out['m_conv_b_w'], 'm_conv_b_bias': out['m_conv_b_bias'], 'm_ln_b_g': out['m_ln_b_g'], 'm_ln_b_b': out['m_ln_b_b'], 'm_w_out': out['m_w_out'], 'm_norm_x_g': out['m_norm_x_g'], 'm_norm_mem_g': out['m_norm_mem_g'], 'm_w_q': out['m_w_q'], 'm_w_kv': out['m_w_kv'], 'm_w_xo': out['m_w_xo'], 'm_norm_ffn_g': out['m_norm_ffn_g'], 'm_w_up': out['m_w_up'], 'm_w_down': out['m_w_down'], 'm_final_g': out['m_final_g'], 'v_norm_mix_g': out['v_norm_mix_g'], 'v_w_in': out['v_w_in'], 'v_conv_a_w': out['v_conv_a_w'], 'v_conv_b_w': out['v_conv_b_w'], 'v_conv_b_bias': out['v_conv_b_bias'], 'v_ln_b_g': out['v_ln_b_g'], 'v_ln_b_b': out['v_ln_b_b'], 'v_w_out': out['v_w_out'], 'v_norm_x_g': out['v_norm_x_g'], 'v_norm_mem_g': out['v_norm_mem_g'], 'v_w_q': out['v_w_q'], 'v_w_kv': out['v_w_kv'], 'v_w_xo': out['v_w_xo'], 'v_norm_ffn_g': out['v_norm_ffn_g'], 'v_w_up': out['v_w_up'], 'v_w_down': out['v_w_down'], 'v_final_g': out['v_final_g']}


def _loss(weights, diff, rest, loss_target):
    with _jax.named_scope("forward"):
        args = {**rest, TWIN_DIFF_INPUT: diff, **{k: w.astype(_WEIGHT_DTYPES[k]) for k, w in weights.items()}}
        y = _forward(args)
    with _jax.named_scope("loss_head"):
        err = _jnp.square(y.astype(_jnp.float32) - loss_target)
        return 0.5 * _jnp.sum(_jnp.mean(err, axis=-1)) if err.ndim else 0.5 * err


def _adamw(w, g, m, v):
    m = ADAM_B1 * m + (1.0 - ADAM_B1) * g
    v = ADAM_B2 * v + (1.0 - ADAM_B2) * _jnp.square(g)
    m_hat = m / (1.0 - ADAM_B1 ** ADAM_STEP)
    v_hat = v / (1.0 - ADAM_B2 ** ADAM_STEP)
    delta = -ADAM_LR * (m_hat / (_jnp.sqrt(v_hat) + ADAM_EPS) + ADAM_WD * w)
    return delta, m, v


def reference(x, mem, norm_mix_g, w_in, conv_a_w, conv_b_w, conv_b_bias, ln_b_g, ln_b_b, w_out, norm_x_g, norm_mem_g, w_q, w_kv, w_xo, norm_ffn_g, w_up, w_down, final_g, loss_target, m_norm_mix_g, m_w_in, m_conv_a_w, m_conv_b_w, m_conv_b_bias, m_ln_b_g, m_ln_b_b, m_w_out, m_norm_x_g, m_norm_mem_g, m_w_q, m_w_kv, m_w_xo, m_norm_ffn_g, m_w_up, m_w_down, m_final_g, v_norm_mix_g, v_w_in, v_conv_a_w, v_conv_b_w, v_conv_b_bias, v_ln_b_g, v_ln_b_b, v_w_out, v_norm_x_g, v_norm_mem_g, v_w_q, v_w_kv, v_w_xo, v_norm_ffn_g, v_w_up, v_w_down, v_final_g):
    given = dict(x=x, mem=mem, norm_mix_g=norm_mix_g, w_in=w_in, conv_a_w=conv_a_w, conv_b_w=conv_b_w, conv_b_bias=conv_b_bias, ln_b_g=ln_b_g, ln_b_b=ln_b_b, w_out=w_out, norm_x_g=norm_x_g, norm_mem_g=norm_mem_g, w_q=w_q, w_kv=w_kv, w_xo=w_xo, norm_ffn_g=norm_ffn_g, w_up=w_up, w_down=w_down, final_g=final_g, loss_target=loss_target, m_norm_mix_g=m_norm_mix_g, m_w_in=m_w_in, m_conv_a_w=m_conv_a_w, m_conv_b_w=m_conv_b_w, m_conv_b_bias=m_conv_b_bias, m_ln_b_g=m_ln_b_g, m_ln_b_b=m_ln_b_b, m_w_out=m_w_out, m_norm_x_g=m_norm_x_g, m_norm_mem_g=m_norm_mem_g, m_w_q=m_w_q, m_w_kv=m_w_kv, m_w_xo=m_w_xo, m_norm_ffn_g=m_norm_ffn_g, m_w_up=m_w_up, m_w_down=m_w_down, m_final_g=m_final_g, v_norm_mix_g=v_norm_mix_g, v_w_in=v_w_in, v_conv_a_w=v_conv_a_w, v_conv_b_w=v_conv_b_w, v_conv_b_bias=v_conv_b_bias, v_ln_b_g=v_ln_b_g, v_ln_b_b=v_ln_b_b, v_w_out=v_w_out, v_norm_x_g=v_norm_x_g, v_norm_mem_g=v_norm_mem_g, v_w_q=v_w_q, v_w_kv=v_w_kv, v_w_xo=v_w_xo, v_norm_ffn_g=v_norm_ffn_g, v_w_up=v_w_up, v_w_down=v_w_down, v_final_g=v_final_g)
    weights = {n: given[n] for n in TWIN_WEIGHTS}
    shared = {n: given[n] for n in SHARED_INPUTS}
    per_example = {n: given[n] for n in ['x', 'mem']}
    grad_fn = _jax.value_and_grad(_loss, argnums=(0, 1))

    def one_microbatch(ex, loss_target):
        ex = dict(ex)
        diff = ex.pop(TWIN_DIFF_INPUT)
        return grad_fn(weights, diff, {**shared, **ex}, loss_target)

    if N_MICROBATCH == 1:
        loss, (grad_w, grad_x) = one_microbatch(per_example, given["loss_target"])
    else:
        def body(carry, xs):
            loss_sum, grad_sum = carry
            l_k, (gw_k, gx_k) = one_microbatch(xs[0], xs[1])
            with _jax.named_scope("update"):
                return (loss_sum + l_k, _jax.tree.map(_jnp.add, grad_sum, gw_k)), gx_k

        init = (_jnp.zeros((), _jnp.float32), _jax.tree.map(_jnp.zeros_like, weights))
        (loss, grad_w), grad_x = _jax.lax.scan(body, init, (per_example, given["loss_target"]))
    with _jax.named_scope("update"):
        delta_w, new_m, new_v = {}, {}, {}
        for n in TWIN_WEIGHTS:
            delta_w[n], new_m[n], new_v[n] = _adamw(weights[n], grad_w[n], given["m_" + n], given["v_" + n])
    return (loss, grad_x, *[grad_w[n] for n in TWIN_WEIGHTS], *[delta_w[n] for n in TWIN_WEIGHTS],
            *[new_m[n] for n in TWIN_WEIGHTS], *[new_v[n] for n in TWIN_WEIGHTS])
```

```python
import jax
import jax.numpy as jnp
from jax import lax
from jax.experimental import pallas as pl
from jax.experimental.pallas import tpu as pltpu

F32 = jnp.float32
BF16 = jnp.bfloat16
MESH = pl.DeviceIdType.MESH

EPS = 1e-6
N_XHEADS = 4
K_A = 3
K_B = 31
PAD_A = 8
PAD_B = 32
CONV_CHUNK = 256
ROW_TILE = 512
LANES = 1024
VMEM_LIMIT_BYTES = 48 * 1024 * 1024

ADAM_LR = 0.001
ADAM_B1 = 0.9
ADAM_B2 = 0.999
ADAM_EPS = 1e-08
ADAM_WD = 0.01
ADAM_STEP = 10

BIG = (("w_in", 1), ("w_out", 0), ("w_q", 0), ("w_kv", 1), ("w_xo", 0), ("w_up", 1), ("w_down", 0))
N_CHIPS = 4
N_DEV = 8


def _params(sem=None):
    return pltpu.CompilerParams(dimension_semantics=sem, vmem_limit_bytes=VMEM_LIMIT_BYTES)


def _mm(a, b, *, name, ta=False, tb=False, out_dtype=F32, res=None, epi=None, aux=None,
        bm=1024, bn=512, bk=1024):
    if ta:
        K, M = a.shape
    else:
        M, K = a.shape
    N = b.shape[0] if tb else b.shape[1]
    bm, bn, bk = min(bm, M), min(bn, N), min(bk, K)
    while N % bn:
        bn //= 2
    while K % bk:
        bk //= 2
    while M % bm:
        bm //= 2
    assert M % bm == 0 and N % bn == 0 and K % bk == 0 and bn % 128 == 0, (name, M, N, K)
    nk = K // bk
    a_spec = (pl.BlockSpec((bk, bm), lambda i, j, k: (k, i)) if ta
              else pl.BlockSpec((bm, bk), lambda i, j, k: (i, k)))
    b_spec = (pl.BlockSpec((bn, bk), lambda i, j, k: (j, k)) if tb
              else pl.BlockSpec((bk, bn), lambda i, j, k: (k, j)))
    o_spec = pl.BlockSpec((bm, bn), lambda i, j, k: (i, j))
    dims = (((0 if ta else 1,), (1 if tb else 0,)), ((), ()))
    ins, in_specs = [a, b], [a_spec, b_spec]
    if res is not None:
        ins.append(res)
        in_specs.append(o_spec)
    if aux is not None:
        ins.append(aux)
        in_specs.append(o_spec)
    n_out = 2 if epi == "sqrelu" else 1
    out_shape = [jax.ShapeDtypeStruct((M, N), out_dtype)] * n_out

    def body(*refs):
        a_ref, b_ref = refs[0], refs[1]
        pos = 2
        res_ref = aux_ref = None
        if res is not None:
            res_ref = refs[pos]
            pos += 1
        if aux is not None:
            aux_ref = refs[pos]
            pos += 1
        outs = refs[pos:pos + n_out]
        acc = refs[pos + n_out]
        k = pl.program_id(2)

        @pl.when(k == 0)
        def _():
            acc[...] = jnp.zeros_like(acc)

        acc[...] += lax.dot_general(a_ref[...], b_ref[...], dims, preferred_element_type=F32)

        @pl.when(k == nk - 1)
        def _():
            r = acc[...]
            if res_ref is not None:
                r = r + res_ref[...]
            if epi == "sqrelu":
                outs[0][...] = r.astype(out_dtype)
                rl = jnp.maximum(r, 0.0)
                outs[1][...] = (rl * rl).astype(out_dtype)
            elif epi == "dsqrelu":
                outs[0][...] = (r * (2.0 * jnp.maximum(aux_ref[...].astype(F32), 0.0))).astype(out_dtype)
            else:
                outs[0][...] = r.astype(out_dtype)

    out = pl.pallas_call(
        body, name=name, grid=(M // bm, N // bn, nk),
        in_specs=in_specs, out_specs=[o_spec] * n_out, out_shape=out_shape,
        scratch_shapes=[pltpu.VMEM((bm, bn), F32)],
        compiler_params=_params(("parallel", "parallel", "arbitrary")),
    )(*ins)
    return out if n_out == 2 else out[0]


def _row_tile(rows):
    return min(ROW_TILE, rows)


def _rms_fwd(x, g, *, name):
    S, D = x.shape
    tr = _row_tile(S)

    def body(x_ref, g_ref, o_ref):
        xv = x_ref[...]
        r = lax.rsqrt(jnp.mean(xv * xv, axis=-1, keepdims=True) + EPS)
        o_ref[...] = (xv * r * g_ref[...]).astype(BF16)

    return pl.pallas_call(
        body, name=name, grid=(S // tr,),
        in_specs=[pl.BlockSpec((tr, D), lambda i: (i, 0)), pl.BlockSpec((1, D), lambda i: (0, 0))],
        out_specs=pl.BlockSpec((tr, D), lambda i: (i, 0)),
        out_shape=jax.ShapeDtypeStruct((S, D), BF16),
        compiler_params=_params(("parallel",)),
    )(x, g)


def _rms_bwd(x, g, du, dres, *, name):
    S, D = x.shape
    tr = _row_tile(S)
    has_res = dres is not None

    def body(*refs):
        if has_res:
            x_ref, g_ref, du_ref, dres_ref, dx_ref, dxb_ref, dg_ref = refs
        else:
            x_ref, g_ref, du_ref, dx_ref, dxb_ref, dg_ref = refs
        xv = x_ref[...]
        r = lax.rsqrt(jnp.mean(xv * xv, axis=-1, keepdims=True) + EPS)
        xh = xv * r
        dy = du_ref[...]
        dxh = dy * g_ref[...]
        dx = r * (dxh - xh * jnp.mean(dxh * xh, axis=-1, keepdims=True))
        if has_res:
            dx = dx + dres_ref[...]
        dx_ref[...] = dx
        dxb_ref[...] = dx.astype(BF16)

        @pl.when(pl.program_id(0) == 0)
        def _():
            dg_ref[...] = jnp.zeros_like(dg_ref)

        dg_ref[...] += jnp.sum(dy * xh, axis=0, keepdims=True)

    row = pl.BlockSpec((tr, D), lambda i: (i, 0))
    vec = pl.BlockSpec((1, D), lambda i: (0, 0))
    ins = [x, g, du] + ([dres] if has_res else [])
    in_specs = [row, vec, row] + ([row] if has_res else [])
    return pl.pallas_call(
        body, name=name, grid=(S // tr,),
        in_specs=in_specs, out_specs=[row, row, vec],
        out_shape=[jax.ShapeDtypeStruct((S, D), F32), jax.ShapeDtypeStruct((S, D), BF16),
                   jax.ShapeDtypeStruct((1, D), F32)],
        compiler_params=_params(("arbitrary",)),
    )(*ins)


def _loss_head(h, g, target, *, name):
    S, D = h.shape
    tr = _row_tile(S)

    def body(x_ref, g_ref, t_ref, loss_ref, dx_ref, dxb_ref, dg_ref):
        xv = x_ref[...]
        r = lax.rsqrt(jnp.mean(xv * xv, axis=-1, keepdims=True) + EPS)
        xh = xv * r
        gv = g_ref[...]
        err = xh * gv - t_ref[...]
        part = 0.5 * jnp.sum(jnp.mean(err * err, axis=-1, keepdims=True), axis=0, keepdims=True)
        dy = err * (1.0 / D)
        dxh = dy * gv
        dx = r * (dxh - xh * jnp.mean(dxh * xh, axis=-1, keepdims=True))
        dx_ref[...] = dx
        dxb_ref[...] = dx.astype(BF16)

        @pl.when(pl.program_id(0) == 0)
        def _():
            dg_ref[...] = jnp.zeros_like(dg_ref)
            loss_ref[...] = jnp.zeros_like(loss_ref)

        dg_ref[...] += jnp.sum(dy * xh, axis=0, keepdims=True)
        loss_ref[...] += jnp.broadcast_to(part, loss_ref.shape)

    row = pl.BlockSpec((tr, D), lambda i: (i, 0))
    vec = pl.BlockSpec((1, D), lambda i: (0, 0))
    return pl.pallas_call(
        body, name=name, grid=(S // tr,),
        in_specs=[row, vec, row],
        out_specs=[pl.BlockSpec((1, 128), lambda i: (0, 0)), row, row, vec],
        out_shape=[jax.ShapeDtypeStruct((1, 128), F32), jax.ShapeDtypeStruct((S, D), F32),
                   jax.ShapeDtypeStruct((S, D), BF16), jax.ShapeDtypeStruct((1, D), F32)],
        compiler_params=_params(("arbitrary",)),
    )(h, g, target)


def _sigmoid(x):
    return 1.0 / (1.0 + jnp.exp(-x))


def _ln_silu_fwd(cb, g, b, *, name):
    S, C = cb.shape
    tr = _row_tile(S)

    def body(x_ref, g_ref, b_ref, o_ref):
        xv = x_ref[...]
        mu = jnp.mean(xv, axis=-1, keepdims=True)
        xc = xv - mu
        rs = lax.rsqrt(jnp.mean(xc * xc, axis=-1, keepdims=True) + EPS)
        l = xc * rs * g_ref[...] + b_ref[...]
        o_ref[...] = (l * _sigmoid(l)).astype(BF16)

    row = pl.BlockSpec((tr, C), lambda i: (i, 0))
    vec = pl.BlockSpec((1, C), lambda i: (0, 0))
    return pl.pallas_call(
        body, name=name, grid=(S // tr,), in_specs=[row, vec, vec], out_specs=row,
        out_shape=jax.ShapeDtypeStruct((S, C), BF16), compiler_params=_params(("parallel",)),
    )(cb, g, b)


def _ln_silu_bwd(cb, g, b, dy, col_block, *, name):
    S, C = cb.shape
    tr = _row_tile(S)

    def body(x_ref, g_ref, b_ref, dy_ref, dx_ref, dg_ref, db_ref):
        xv = x_ref[...]
        mu = jnp.mean(xv, axis=-1, keepdims=True)
        xc = xv - mu
        rs = lax.rsqrt(jnp.mean(xc * xc, axis=-1, keepdims=True) + EPS)
        xh = xc * rs
        gv = g_ref[...]
        l = xh * gv + b_ref[...]
        sg = _sigmoid(l)
        dl = dy_ref[...] * (sg + l * sg * (1.0 - sg))
        dxh = dl * gv
        dx_ref[...] = rs * (dxh - jnp.mean(dxh, axis=-1, keepdims=True)
                            - xh * jnp.mean(dxh * xh, axis=-1, keepdims=True))

        @pl.when(pl.program_id(0) == 0)
        def _():
            dg_ref[...] = jnp.zeros_like(dg_ref)
            db_ref[...] = jnp.zeros_like(db_ref)

        dg_ref[...] += jnp.sum(dl * xh, axis=0, keepdims=True)
        db_ref[...] += jnp.sum(dl, axis=0, keepdims=True)

    row = pl.BlockSpec((tr, C), lambda i: (i, 0))
    vec = pl.BlockSpec((1, C), lambda i: (0, 0))
    return pl.pallas_call(
        body, name=name, grid=(S // tr,),
        in_specs=[row, vec, vec, pl.BlockSpec((tr, C), lambda i: (i, col_block))],
        out_specs=[row, vec, vec],
        out_shape=[jax.ShapeDtypeStruct((S, C), F32), jax.ShapeDtypeStruct((1, C), F32),
                   jax.ShapeDtypeStruct((1, C), F32)],
        compiler_params=_params(("arbitrary",)),
    )(cb, g, b, dy)


def _attn_fwd(q, kv, *, name):
    S, D = q.shape
    M = kv.shape[0]
    hd = D // N_XHEADS
    scale = 1.0 / float(hd) ** 0.5
    tq = _row_tile(S)

    def body(q_ref, k_ref, v_ref, o_ref):
        for h in range(N_XHEADS):
            cols = slice(h * hd, (h + 1) * hd)
            s = lax.dot_general(q_ref[:, cols], k_ref[:, cols], (((1,), (1,)), ((), ())),
                                preferred_element_type=F32) * scale
            e = jnp.exp(s - jnp.max(s, axis=-1, keepdims=True))
            p = e / jnp.sum(e, axis=-1, keepdims=True)
            o = jnp.dot(p.astype(BF16), v_ref[:, cols], preferred_element_type=F32)
            o_ref[:, cols] = o.astype(BF16)

    return pl.pallas_call(
        body, name=name, grid=(S // tq,),
        in_specs=[pl.BlockSpec((tq, D), lambda i: (i, 0)), pl.BlockSpec((M, D), lambda i: (0, 0)),
                  pl.BlockSpec((M, D), lambda i: (0, 1))],
        out_specs=pl.BlockSpec((tq, D), lambda i: (i, 0)),
        out_shape=jax.ShapeDtypeStruct((S, D), BF16), compiler_params=_params(("parallel",)),
    )(q, kv, kv)


def _attn_bwd(q, kv, do, *, name):
    S, D = q.shape
    M = kv.shape[0]
    hd = D // N_XHEADS
    scale = 1.0 / float(hd) ** 0.5
    tq = _row_tile(S)

    def body(q_ref, k_ref, v_ref, do_ref, dq_ref, dkv_ref):
        @pl.when(pl.program_id(0) == 0)
        def _():
            dkv_ref[...] = jnp.zeros_like(dkv_ref)

        for h in range(N_XHEADS):
            cols = slice(h * hd, (h + 1) * hd)
            vcols = slice(D + h * hd, D + (h + 1) * hd)
            qh, kh, vh, doh = q_ref[:, cols], k_ref[:, cols], v_ref[:, cols], do_ref[:, cols]
            s = lax.dot_general(qh, kh, (((1,), (1,)), ((), ())), preferred_element_type=F32) * scale
            e = jnp.exp(s - jnp.max(s, axis=-1, keepdims=True))
            p = e / jnp.sum(e, axis=-1, keepdims=True)
            pb = p.astype(BF16)
            dp = lax.dot_general(doh, vh, (((1,), (1,)), ((), ())), preferred_element_type=F32)
            ds = (p * (dp - jnp.sum(dp * p, axis=-1, keepdims=True)) * scale).astype(BF16)
            dq_ref[:, cols] = jnp.dot(ds, kh, preferred_element_type=F32).astype(BF16)
            dkv_ref[:, cols] += lax.dot_general(ds, qh, (((0,), (0,)), ((), ())), preferred_element_type=F32)
            dkv_ref[:, vcols] += lax.dot_general(pb, doh, (((0,), (0,)), ((), ())), preferred_element_type=F32)

    row = pl.BlockSpec((tq, D), lambda i: (i, 0))
    return pl.pallas_call(
        body, name=name, grid=(S // tq,),
        in_specs=[row, pl.BlockSpec((M, D), lambda i: (0, 0)), pl.BlockSpec((M, D), lambda i: (0, 1)), row],
        out_specs=[row, pl.BlockSpec((M, 2 * D), lambda i: (0, 0))],
        out_shape=[jax.ShapeDtypeStruct((S, D), BF16), jax.ShapeDtypeStruct((M, 2 * D), F32)],
        compiler_params=_params(("arbitrary",)),
    )(q, kv, kv, do)


def _delayed(win, j, pad):
    return (win if j == 0 else pltpu.roll(win, j, 0))[pad:, :]


def _advanced(win, j, ch):
    return (win if j == 0 else pltpu.roll(win, win.shape[0] - j, 0))[:ch, :]


def _mixer_a_fwd(z, w, *, name):
    S = z.shape[0]
    C = w.shape[1]
    nb = C // 128
    ch = min(CONV_CHUNK, S)

    def body(b_ref, c_ref, h_ref, w_ref, y_ref, xp):
        xp[0:PAD_A, :] = jnp.zeros((PAD_A, 128), F32)
        xp[PAD_A:, :] = c_ref[...] * h_ref[...]

        def chunk(i, carry):
            base = pl.multiple_of(i * ch, ch)
            win = xp[pl.ds(base, ch + PAD_A), :]
            acc = _delayed(win, 0, PAD_A) * w_ref[K_A - 1:K_A, :]
            for j in range(1, K_A):
                acc = acc + _delayed(win, j, PAD_A) * w_ref[K_A - 1 - j:K_A - j, :]
            y_ref[pl.ds(base, ch), :] = (b_ref[pl.ds(base, ch), :] * acc).astype(BF16)
            return carry

        lax.fori_loop(0, S // ch, chunk, 0)

    def col(g):
        return pl.BlockSpec((S, 128), lambda j: (0, g * nb + j))

    return pl.pallas_call(
        body, name=name, grid=(nb,),
        in_specs=[col(0), col(1), col(2), pl.BlockSpec((K_A, 128), lambda j: (0, j))],
        out_specs=pl.BlockSpec((S, 128), lambda j: (0, j)),
        out_shape=jax.ShapeDtypeStruct((S, C), BF16),
        scratch_shapes=[pltpu.VMEM((PAD_A + S, 128), F32)],
        compiler_params=_params(("parallel",)),
    )(z, z, z, w)


def _mixer_a_bwd(z, w, dy, *, name):
    S = z.shape[0]
    C = w.shape[1]
    nb = C // 128
    ch = min(CONV_CHUNK, S)

    def body(b_ref, c_ref, h_ref, w_ref, dy_ref, db_ref, dc_ref, dh_ref, dw_ref, xp, dp):
        xp[0:PAD_A, :] = jnp.zeros((PAD_A, 128), F32)
        xp[PAD_A:, :] = c_ref[...] * h_ref[...]
        dp[S:, :] = jnp.zeros((PAD_A, 128), F32)
        dw_ref[...] = jnp.zeros_like(dw_ref)

        def chunk(i, carry):
            base = pl.multiple_of(i * ch, ch)
            win = xp[pl.ds(base, ch + PAD_A), :]
            dya = dy_ref[pl.ds(base, ch), :]
            dcv = dya * b_ref[pl.ds(base, ch), :]
            dp[pl.ds(base, ch), :] = dcv
            acc = None
            for j in range(K_A):
                xs = _delayed(win, j, PAD_A)
                k = K_A - 1 - j
                term = xs * w_ref[k:k + 1, :]
                acc = term if acc is None else acc + term
                dw_ref[k:k + 1, :] += jnp.sum(dcv * xs, axis=0, keepdims=True)
            db_ref[pl.ds(base, ch), :] = (dya * acc).astype(BF16)
            return carry

        lax.fori_loop(0, S // ch, chunk, 0)

        def chunk2(i, carry):
            base = pl.multiple_of(i * ch, ch)
            win = dp[pl.ds(base, ch + PAD_A), :]
            acc = None
            for j in range(K_A):
                term = _advanced(win, j, ch) * w_ref[K_A - 1 - j:K_A - j, :]
                acc = term if acc is None else acc + term
            dc_ref[pl.ds(base, ch), :] = (acc * h_ref[pl.ds(base, ch), :]).astype(BF16)
            dh_ref[pl.ds(base, ch), :] = (acc * c_ref[pl.ds(base, ch), :]).astype(BF16)
            return carry

        lax.fori_loop(0, S // ch, chunk2, 0)

    def col(g):
        return pl.BlockSpec((S, 128), lambda j: (0, g * nb + j))

    out_col = pl.BlockSpec((S, 128), lambda j: (0, j))
    wspec = pl.BlockSpec((K_A, 128), lambda j: (0, j))
    return pl.pallas_call(
        body, name=name, grid=(nb,),
        in_specs=[col(0), col(1), col(2), wspec, out_col],
        out_specs=[out_col, out_col, out_col, wspec],
        out_shape=[jax.ShapeDtypeStruct((S, C), BF16)] * 3 + [jax.ShapeDtypeStruct((K_A, C), F32)],
        scratch_shapes=[pltpu.VMEM((PAD_A + S, 128), F32), pltpu.VMEM((S + PAD_A, 128), F32)],
        compiler_params=_params(("parallel",)),
    )(z, z, z, w, dy)


def _mixer_b_fwd(z, w, bias, *, name):
    S = z.shape[0]
    C = w.shape[1]
    nb = C // 128
    ch = min(CONV_CHUNK, S)

    def body(v_ref, g_ref, w_ref, bias_ref, cb_ref, xp):
        xp[0:PAD_B, :] = jnp.zeros((PAD_B, 128), F32)
        xp[PAD_B:, :] = v_ref[...] * _sigmoid(g_ref[...])

        def chunk(i, carry):
            base = pl.multiple_of(i * ch, ch)
            win = xp[pl.ds(base, ch + PAD_B), :]
            acc = None
            for j in range(K_B):
                term = _delayed(win, j, PAD_B) * w_ref[K_B - 1 - j:K_B - j, :]
                acc = term if acc is None else acc + term
            cb_ref[pl.ds(base, ch), :] = acc + bias_ref[...]
            return carry

        lax.fori_loop(0, S // ch, chunk, 0)

    def col(g):
        return pl.BlockSpec((S, 128), lambda j: (0, g * nb + j))

    return pl.pallas_call(
        body, name=name, grid=(nb,),
        in_specs=[col(3), col(4), pl.BlockSpec((K_B, 128), lambda j: (0, j)),
                  pl.BlockSpec((1, 128), lambda j: (0, j))],
        out_specs=pl.BlockSpec((S, 128), lambda j: (0, j)),
        out_shape=jax.ShapeDtypeStruct((S, C), F32),
        scratch_shapes=[pltpu.VMEM((PAD_B + S, 128), F32)],
        compiler_params=_params(("parallel",)),
    )(z, z, w, bias)


def _mixer_b_bwd(z, w, dcb, *, name):
    S = z.shape[0]
    C = w.shape[1]
    nb = C // 128
    ch = min(CONV_CHUNK, S)

    def body(v_ref, g_ref, w_ref, dcb_ref, dv_ref, dg_ref, dw_ref, dbias_ref, xp, dp):
        xp[0:PAD_B, :] = jnp.zeros((PAD_B, 128), F32)
        xp[PAD_B:, :] = v_ref[...] * _sigmoid(g_ref[...])
        dp[0:S, :] = dcb_ref[...]
        dp[S:, :] = jnp.zeros((PAD_B, 128), F32)
        dw_ref[...] = jnp.zeros_like(dw_ref)
        dbias_ref[...] = jnp.sum(dcb_ref[...], axis=0, keepdims=True)

        def chunk(i, carry):
            base = pl.multiple_of(i * ch, ch)
            win = xp[pl.ds(base, ch + PAD_B), :]
            d = dcb_ref[pl.ds(base, ch), :]
            for j in range(K_B):
                k = K_B - 1 - j
                dw_ref[k:k + 1, :] += jnp.sum(d * _delayed(win, j, PAD_B), axis=0, keepdims=True)
            return carry

        lax.fori_loop(0, S // ch, chunk, 0)

        def chunk2(i, carry):
            base = pl.multiple_of(i * ch, ch)
            win = dp[pl.ds(base, ch + PAD_B), :]
            acc = None
            for j in range(K_B):
                term = _advanced(win, j, ch) * w_ref[K_B - 1 - j:K_B - j, :]
                acc = term if acc is None else acc + term
            sg = _sigmoid(g_ref[pl.ds(base, ch), :])
            vv = v_ref[pl.ds(base, ch), :]
            dv_ref[pl.ds(base, ch), :] = (acc * sg).astype(BF16)
            dg_ref[pl.ds(base, ch), :] = (acc * vv * sg * (1.0 - sg)).astype(BF16)
            return carry

        lax.fori_loop(0, S // ch, chunk2, 0)

    def col(g):
        return pl.BlockSpec((S, 128), lambda j: (0, g * nb + j))

    out_col = pl.BlockSpec((S, 128), lambda j: (0, j))
    wspec = pl.BlockSpec((K_B, 128), lambda j: (0, j))
    bspec = pl.BlockSpec((1, 128), lambda j: (0, j))
    return pl.pallas_call(
        body, name=name, grid=(nb,),
        in_specs=[col(3), col(4), wspec, out_col],
        out_specs=[out_col, out_col, wspec, bspec],
        out_shape=[jax.ShapeDtypeStruct((S, C), BF16)] * 2
        + [jax.ShapeDtypeStruct((K_B, C), F32), jax.ShapeDtypeStruct((1, C), F32)],
        scratch_shapes=[pltpu.VMEM((PAD_B + S, 128), F32), pltpu.VMEM((S + PAD_B, 128), F32)],
        compiler_params=_params(("parallel",)),
    )(z, z, w, dcb)


def _ew_tile(R):
    for t in (512, 256, 128, 64, 32, 16, 8):
        if R % t == 0:
            return t
    return R


def _add_to_bf16(a, b, *, name):
    R, C = a.shape
    tr = _ew_tile(R)

    def body(a_ref, b_ref, o_ref):
        o_ref[...] = (a_ref[...] + b_ref[...]).astype(BF16)

    row = pl.BlockSpec((tr, C), lambda i: (i, 0))
    return pl.pallas_call(
        body, name=name, grid=(R // tr,), in_specs=[row, row], out_specs=row,
        out_shape=jax.ShapeDtypeStruct((R, C), BF16), compiler_params=_params(("parallel",)),
    )(a, b)


def _sum_leading(x, *, name):
    n, R, C = x.shape
    tr = _ew_tile(R)

    def body(x_ref, o_ref):
        acc = x_ref[0].astype(F32)
        for k in range(1, n):
            acc = acc + x_ref[k].astype(F32)
        o_ref[...] = acc

    return pl.pallas_call(
        body, name=name, grid=(R // tr,),
        in_specs=[pl.BlockSpec((n, tr, C), lambda i: (0, i, 0))],
        out_specs=pl.BlockSpec((tr, C), lambda i: (i, 0)),
        out_shape=jax.ShapeDtypeStruct((R, C), F32), compiler_params=_params(("parallel",)),
    )(x)


def _adamw(w, g, m, v, *, name):
    R, C = w.shape
    tr = _ew_tile(R)

    def body(w_ref, g_ref, m_ref, v_ref, d_ref, nm_ref, nv_ref):
        gv = g_ref[...]
        nm = ADAM_B1 * m_ref[...] + (1.0 - ADAM_B1) * gv
        nv = ADAM_B2 * v_ref[...] + (1.0 - ADAM_B2) * (gv * gv)
        m_hat = nm / (1.0 - ADAM_B1 ** ADAM_STEP)
        v_hat = nv / (1.0 - ADAM_B2 ** ADAM_STEP)
        d_ref[...] = -ADAM_LR * (m_hat / (jnp.sqrt(v_hat) + ADAM_EPS) + ADAM_WD * w_ref[...])
        nm_ref[...] = nm
        nv_ref[...] = nv

    row = pl.BlockSpec((tr, C), lambda i: (i, 0))
    return pl.pallas_call(
        body, name=name, grid=(R // tr,), in_specs=[row] * 4, out_specs=[row] * 3,
        out_shape=[jax.ShapeDtypeStruct((R, C), F32)] * 3, compiler_params=_params(("parallel",)),
    )(w, g, m, v)


ANY = pl.BlockSpec(memory_space=pl.ANY)


def _place():
    x, y, c = lax.axis_index("x"), lax.axis_index("y"), lax.axis_index("c")
    return x, y, c, 2 * x + y


def _other_chip(x, y, mask):
    px = 1 - x if mask & 2 else x
    py = 1 - y if mask & 1 else y
    return px, py, 2 * px + py


def _gather_weights(own, *, name):
    R, L = own.shape

    def body(own_ref, out_ref, send_sems, recv_sems, local_sem):
        x, y, c, chip = _place()
        mine = pltpu.make_async_copy(own_ref, out_ref.at[c, chip], local_sem)
        mine.start()
        sends = []
        for k, mask in enumerate((1, 2, 3)):
            px, py, _ = _other_chip(x, y, mask)
            cp = pltpu.make_async_remote_copy(own_ref, out_ref.at[c, chip], send_sems.at[k], recv_sems.at[k],
                                              device_id=(px, py, c), device_id_type=MESH)
            cp.start()
            sends.append(cp)
        for k, mask in enumerate((1, 2, 3)):
            px, py, pchip = _other_chip(x, y, mask)
            pltpu.make_async_remote_copy(own_ref, out_ref.at[c, pchip], send_sems.at[k], recv_sems.at[k],
                                         device_id=(px, py, c), device_id_type=MESH).wait_recv()
        mine.wait()
        relay = pltpu.make_async_remote_copy(out_ref.at[c], out_ref.at[c], send_sems.at[3], recv_sems.at[3],
                                             device_id=(x, y, 1 - c), device_id_type=MESH)
        relay.start()
        pltpu.make_async_remote_copy(out_ref.at[1 - c], out_ref.at[1 - c], send_sems.at[3], recv_sems.at[3],
                                     device_id=(x, y, 1 - c), device_id_type=MESH).wait_recv()
        for cp in sends:
            cp.wait_send()
        relay.wait_send()

    return pl.pallas_call(
        body, name=name, in_specs=[ANY], out_specs=ANY,
        out_shape=jax.ShapeDtypeStruct((2, N_CHIPS, R, L), own.dtype),
        scratch_shapes=[pltpu.SemaphoreType.DMA((4,)), pltpu.SemaphoreType.DMA((4,)), pltpu.SemaphoreType.DMA],
    )(own)


def _swap_with_sibling(g, *, name):
    _, n, R, L = g.shape

    def body(g_ref, out_ref, send_sem, recv_sem):
        x, y, c, _ = _place()
        cp = pltpu.make_async_remote_copy(g_ref.at[1 - c], out_ref, send_sem, recv_sem,
                                          device_id=(x, y, 1 - c), device_id_type=MESH)
        cp.start()
        cp.wait()

    return pl.pallas_call(
        body, name=name, in_specs=[ANY], out_specs=ANY,
        out_shape=jax.ShapeDtypeStruct((n, R, L), g.dtype),
        scratch_shapes=[pltpu.SemaphoreType.DMA, pltpu.SemaphoreType.DMA],
    )(g)


def _scatter_to_chips(p, *, name):
    n, R, L = p.shape

    def body(p_ref, out_ref, send_sems, recv_sems, local_sem):
        x, y, c, chip = _place()
        mine = pltpu.make_async_copy(p_ref.at[chip], out_ref.at[chip], local_sem)
        mine.start()
        sends = []
        for k, mask in enumerate((1, 2, 3)):
            px, py, pchip = _other_chip(x, y, mask)
            cp = pltpu.make_async_remote_copy(p_ref.at[pchip], out_ref.at[chip], send_sems.at[k], recv_sems.at[k],
                                              device_id=(px, py, c), device_id_type=MESH)
            cp.start()
            sends.append(cp)
        for k, mask in enumerate((1, 2, 3)):
            px, py, pchip = _other_chip(x, y, mask)
            pltpu.make_async_remote_copy(p_ref.at[pchip], out_ref.at[pchip], send_sems.at[k], recv_sems.at[k],
                                         device_id=(px, py, c), device_id_type=MESH).wait_recv()
        for cp in sends:
            cp.wait_send()
        mine.wait()

    return pl.pallas_call(
        body, name=name, in_specs=[ANY], out_specs=ANY,
        out_shape=jax.ShapeDtypeStruct((n, R, L), p.dtype),
        scratch_shapes=[pltpu.SemaphoreType.DMA((3,)), pltpu.SemaphoreType.DMA((3,)), pltpu.SemaphoreType.DMA],
    )(p)


def _share_with_sibling(r, *, name):
    R, L = r.shape

    def body(r_ref, out_ref, send_sem, recv_sem, local_sem):
        x, y, c, _ = _place()
        mine = pltpu.make_async_copy(r_ref, out_ref.at[c], local_sem)
        mine.start()
        cp = pltpu.make_async_remote_copy(r_ref, out_ref.at[c], send_sem, recv_sem,
                                          device_id=(x, y, 1 - c), device_id_type=MESH)
        cp.start()
        pltpu.make_async_remote_copy(r_ref, out_ref.at[1 - c], send_sem, recv_sem,
                                     device_id=(x, y, 1 - c), device_id_type=MESH).wait_recv()
        cp.wait_send()
        mine.wait()

    return pl.pallas_call(
        body, name=name, in_specs=[ANY], out_specs=ANY,
        out_shape=jax.ShapeDtypeStruct((2, R, L), r.dtype),
        scratch_shapes=[pltpu.SemaphoreType.DMA, pltpu.SemaphoreType.DMA, pltpu.SemaphoreType.DMA],
    )(r)


def _gather_all(buf, *, name):
    r, L = buf.shape
    vmem = pl.BlockSpec(memory_space=pltpu.VMEM)
    masks = tuple(range(1, N_DEV))

    def body(buf_ref, out_ref, send_sems, recv_sems):
        x, y, c, _ = _place()
        me = 4 * x + 2 * y + c
        out_ref[me] = buf_ref[...]
        sends = []
        for k, mask in enumerate(masks):
            px = 1 - x if mask & 4 else x
            py = 1 - y if mask & 2 else y
            pc = 1 - c if mask & 1 else c
            cp = pltpu.make_async_remote_copy(buf_ref, out_ref.at[me], send_sems.at[k], recv_sems.at[k],
                                              device_id=(px, py, pc), device_id_type=MESH)
            cp.start()
            sends.append(cp)
        for k, mask in enumerate(masks):
            px = 1 - x if mask & 4 else x
            py = 1 - y if mask & 2 else y
            pc = 1 - c if mask & 1 else c
            pltpu.make_async_remote_copy(buf_ref, out_ref.at[4 * px + 2 * py + pc], send_sems.at[k],
                                         recv_sems.at[k], device_id=(px, py, pc), device_id_type=MESH).wait_recv()
        for cp in sends:
            cp.wait_send()

    return pl.pallas_call(
        body, name=name, in_specs=[vmem], out_specs=vmem,
        out_shape=jax.ShapeDtypeStruct((N_DEV, r, L), buf.dtype),
        scratch_shapes=[pltpu.SemaphoreType.DMA((N_DEV - 1,)), pltpu.SemaphoreType.DMA((N_DEV - 1,))],
    )(buf)


def _pack(arrs, lanes, row_mult=8):
    flat = jnp.concatenate([a.reshape(-1) for a in arrs])
    rows = -(-flat.shape[0] // lanes)
    rows = -(-rows // row_mult) * row_mult
    flat = jnp.pad(flat, (0, rows * lanes - flat.shape[0]))
    return flat.reshape(rows, lanes)


def _unpack(buf, shapes):
    flat = buf.reshape(-1)
    out, pos = [], 0
    for s in shapes:
        n = 1
        for d in s:
            n *= d
        out.append(flat[pos:pos + n].reshape(s))
        pos += n
    return out


def kernel(x, mem, norm_mix_g, w_in, conv_a_w, conv_b_w, conv_b_bias, ln_b_g, ln_b_b, w_out, norm_x_g, norm_mem_g, w_q, w_kv, w_xo, norm_ffn_g, w_up, w_down, final_g, loss_target, m_norm_mix_g, m_w_in, m_conv_a_w, m_conv_b_w, m_conv_b_bias, m_ln_b_g, m_ln_b_b, m_w_out, m_norm_x_g, m_norm_mem_g, m_w_q, m_w_kv, m_w_xo, m_norm_ffn_g, m_w_up, m_w_down, m_final_g, v_norm_mix_g, v_w_in, v_conv_a_w, v_conv_b_w, v_conv_b_bias, v_ln_b_g, v_ln_b_b, v_w_out, v_norm_x_g, v_norm_mem_g, v_w_q, v_w_kv, v_w_xo, v_norm_ffn_g, v_w_up, v_w_down, v_final_g):
    W = dict(norm_mix_g=norm_mix_g, w_in=w_in, conv_a_w=conv_a_w, conv_b_w=conv_b_w, conv_b_bias=conv_b_bias,
             ln_b_g=ln_b_g, ln_b_b=ln_b_b, w_out=w_out, norm_x_g=norm_x_g, norm_mem_g=norm_mem_g, w_q=w_q,
             w_kv=w_kv, w_xo=w_xo, norm_ffn_g=norm_ffn_g, w_up=w_up, w_down=w_down, final_g=final_g)
    MO = dict(norm_mix_g=m_norm_mix_g, w_in=m_w_in, conv_a_w=m_conv_a_w, conv_b_w=m_conv_b_w,
              conv_b_bias=m_conv_b_bias, ln_b_g=m_ln_b_g, ln_b_b=m_ln_b_b, w_out=m_w_out, norm_x_g=m_norm_x_g,
              norm_mem_g=m_norm_mem_g, w_q=m_w_q, w_kv=m_w_kv, w_xo=m_w_xo, norm_ffn_g=m_norm_ffn_g,
              w_up=m_w_up, w_down=m_w_down, final_g=m_final_g)
    VO = dict(norm_mix_g=v_norm_mix_g, w_in=v_w_in, conv_a_w=v_conv_a_w, conv_b_w=v_conv_b_w,
              conv_b_bias=v_conv_b_bias, ln_b_g=v_ln_b_g, ln_b_b=v_ln_b_b, w_out=v_w_out, norm_x_g=v_norm_x_g,
              norm_mem_g=v_norm_mem_g, w_q=v_w_q, w_kv=v_w_kv, w_xo=v_w_xo, norm_ffn_g=v_norm_ffn_g,
              w_up=v_w_up, w_down=v_w_down, final_g=v_final_g)
    names = list(W.keys())
    depth = norm_mix_g.shape[0]
    assert depth == 2, "the exchange splits the weights into one layer per core of a chip"
    c_idx = lax.axis_index("c")
    chip_idx = 2 * lax.axis_index("x") + lax.axis_index("y")

    xs = x[0]
    ms = mem[0]
    tgt = loss_target[0]
    S, D = xs.shape
    c_a = conv_a_w.shape[-1] * N_CHIPS
    c_loc = conv_a_w.shape[-1]

    rows = [W[n].shape[1] * W[n].shape[2] // LANES for n, _ in BIG]
    offs = [sum(rows[:i]) for i in range(len(rows))]
    R = sum(rows)
    flat = jnp.stack([jnp.concatenate([W[n][l].reshape(-1, LANES) for n, _ in BIG], axis=0)
                      for l in range(depth)]).astype(BF16)
    own = lax.dynamic_index_in_dim(flat, c_idx, 0, keepdims=False)
    gathered = _gather_weights(own, name="gather_weights")

    def full_weight(l, i):
        n, ax = BIG[i]
        shp = W[n].shape[1:]
        parts = [gathered[l, j, offs[i]:offs[i] + rows[i]].reshape(shp) for j in range(N_CHIPS)]
        return jnp.concatenate(parts, axis=ax)

    Wb = [{BIG[i][0]: full_weight(l, i) for i in range(len(BIG))} for l in range(depth)]

    conv_local = _pack([conv_a_w, conv_b_w], 128)
    conv_all = _gather_all(conv_local, name="gather_conv_weights")
    na = depth * K_A * c_loc
    nbw = depth * K_B * c_loc
    ca_parts, cb_parts = [], []
    for j in range(N_CHIPS):
        fl = conv_all[2 * j].reshape(-1)
        ca_parts.append(fl[:na].reshape(depth, K_A, c_loc))
        cb_parts.append(fl[na:na + nbw].reshape(depth, K_B, c_loc))
    conv_a_full = jnp.concatenate(ca_parts, axis=-1)
    conv_b_full = jnp.concatenate(cb_parts, axis=-1)

    saved = []
    h = xs
    for l in range(depth):
        wl = Wb[l]
        t = f"l{l}_"
        u = _rms_fwd(h, norm_mix_g[l:l + 1], name=t + "rms_mix")
        z = _mm(u, wl["w_in"], name=t + "mm_in")
        y_a = _mixer_a_fwd(z, conv_a_full[l], name=t + "mixer_a")
        cb = _mixer_b_fwd(z, conv_b_full[l], conv_b_bias[l:l + 1], name=t + "mixer_b")
        y_b = _ln_silu_fwd(cb, ln_b_g[l:l + 1], ln_b_b[l:l + 1], name=t + "ln_silu")
        yy = jnp.concatenate([y_a, y_b], axis=1)
        h2 = _mm(yy, wl["w_out"], res=h, name=t + "mm_out")
        q_in = _rms_fwd(h2, norm_x_g[l:l + 1], name=t + "rms_x")
        q = _mm(q_in, wl["w_q"], out_dtype=BF16, name=t + "mm_q")
        mn = _rms_fwd(ms, norm_mem_g[l:l + 1], name=t + "rms_mem")
        kv = _mm(mn, wl["w_kv"], out_dtype=BF16, name=t + "mm_kv")
        o = _attn_fwd(q, kv, name=t + "attn")
        h3 = _mm(o, wl["w_xo"], res=h2, name=t + "mm_xo")
        u3 = _rms_fwd(h3, norm_ffn_g[l:l + 1], name=t + "rms_ffn")
        a_pre, hh = _mm(u3, wl["w_up"], out_dtype=BF16, epi="sqrelu", name=t + "mm_up")
        h4 = _mm(hh, wl["w_down"], res=h3, name=t + "mm_down")
        saved.append(dict(h=h, u=u, z=z, cb=cb, yy=yy, h2=h2, q_in=q_in, q=q, mn=mn, kv=kv, o=o, h3=h3,
                          u3=u3, a_pre=a_pre, hh=hh))
        h = h4

    loss_vec, dh, dhb, d_final = _loss_head(h, final_g.reshape(1, D), tgt, name="loss_head")
    loss = lax.psum(loss_vec[0, 0], ("x", "y", "c"))

    GW = [dict() for _ in range(depth)]
    GS = [dict() for _ in range(depth)]
    for l in reversed(range(depth)):
        wl, sv = Wb[l], saved[l]
        t = f"l{l}_b_"
        GW[l]["w_down"] = _mm(sv["hh"], dhb, ta=True, name=t + "dw_down")
        da = _mm(dhb, wl["w_down"], tb=True, out_dtype=BF16, epi="dsqrelu", aux=sv["a_pre"], name=t + "d_hidden")
        GW[l]["w_up"] = _mm(sv["u3"], da, ta=True, name=t + "dw_up")
        du3 = _mm(da, wl["w_up"], tb=True, name=t + "d_u3")
        dh, dhb, GS[l]["norm_ffn_g"] = _rms_bwd(sv["h3"], norm_ffn_g[l:l + 1], du3, dh, name=t + "rms_ffn")
        GW[l]["w_xo"] = _mm(sv["o"], dhb, ta=True, name=t + "dw_xo")
        d_o = _mm(dhb, wl["w_xo"], tb=True, out_dtype=BF16, name=t + "d_o")
        dq, dkv = _attn_bwd(sv["q"], sv["kv"], d_o, name=t + "attn")
        GW[l]["w_q"] = _mm(sv["q_in"], dq, ta=True, name=t + "dw_q")
        dq_in = _mm(dq, wl["w_q"], tb=True, name=t + "d_q_in")
        dkvb = dkv.astype(BF16)
        GW[l]["w_kv"] = _mm(sv["mn"], dkvb, ta=True, name=t + "dw_kv")
        dmn = _mm(dkvb, wl["w_kv"], tb=True, name=t + "d_mem")
        _, _, GS[l]["norm_mem_g"] = _rms_bwd(ms, norm_mem_g[l:l + 1], dmn, None, name=t + "rms_mem")
        dh, dhb, GS[l]["norm_x_g"] = _rms_bwd(sv["h2"], norm_x_g[l:l + 1], dq_in, dh, name=t + "rms_x")
        GW[l]["w_out"] = _mm(sv["yy"], dhb, ta=True, name=t + "dw_out")
        dyy = _mm(dhb, wl["w_out"], tb=True, name=t + "d_y")
        dcb, GS[l]["ln_b_g"], GS[l]["ln_b_b"] = _ln_silu_bwd(sv["cb"], ln_b_g[l:l + 1], ln_b_b[l:l + 1], dyy, 1,
                                                             name=t + "ln_silu")
        db_, dc_, dh_, GS[l]["conv_a_w"] = _mixer_a_bwd(sv["z"], conv_a_full[l], dyy, name=t + "mixer_a")
        dv_, dg_, GS[l]["conv_b_w"], GS[l]["conv_b_bias"] = _mixer_b_bwd(sv["z"], conv_b_full[l], dcb,
                                                                         name=t + "mixer_b")
        dz = jnp.concatenate([db_, dc_, dh_, dv_, dg_], axis=1)
        GW[l]["w_in"] = _mm(sv["u"], dz, ta=True, name=t + "dw_in")
        du = _mm(dz, wl["w_in"], tb=True, name=t + "d_u")
        dh, dhb, GS[l]["norm_mix_g"] = _rms_bwd(sv["h"], norm_mix_g[l:l + 1], du, dh, name=t + "rms_mix")
    grad_x = dh[None]

    def shard_piece(gw, i, j):
        n, ax = BIG[i]
        width = W[n].shape[1 + ax]
        return lax.slice_in_dim(gw, j * width, (j + 1) * width, axis=ax).reshape(-1, LANES)

    gflat = jnp.stack([jnp.stack([jnp.concatenate([shard_piece(GW[l][BIG[i][0]], i, j) for i in range(len(BIG))],
                                                   axis=0) for j in range(N_CHIPS)]) for l in range(depth)])
    from_sibling = _swap_with_sibling(gflat, name="grad_swap_sibling")
    mine = lax.dynamic_index_in_dim(gflat, c_idx, 0, keepdims=False)
    pair = _add_to_bf16(mine.reshape(N_CHIPS * R, LANES), from_sibling.reshape(N_CHIPS * R, LANES),
                        name="grad_pair_sum").reshape(N_CHIPS, R, LANES)
    pieces = _scatter_to_chips(pair, name="grad_scatter_chips")
    reduced = _sum_leading(pieces, name="grad_chip_sum")
    g_both = _share_with_sibling(reduced, name="grad_share_sibling")

    grads, deltas, new_m, new_v = {}, {}, {}, {}
    for i, (n, _) in enumerate(BIG):
        shp = W[n].shape
        g = g_both[:, offs[i]:offs[i] + rows[i]].reshape(shp)
        grads[n] = g
        cols = shp[-1]
        d, nm, nv = _adamw(W[n].reshape(-1, cols), g.reshape(-1, cols), MO[n].reshape(-1, cols),
                           VO[n].reshape(-1, cols), name="adamw_" + n)
        deltas[n], new_m[n], new_v[n] = d.reshape(shp), nm.reshape(shp), nv.reshape(shp)

    small = [n for n in names if n not in dict(BIG)]
    full_shapes = {n: ((depth, W[n].shape[1], c_a) if n in ("conv_a_w", "conv_b_w") else W[n].shape)
                   for n in small}

    def small_grad(n):
        if n == "final_g":
            return d_final.reshape(W[n].shape)
        return jnp.stack([GS[l][n].reshape(full_shapes[n][1:]) for l in range(depth)])

    part = _pack([small_grad(n) for n in small], LANES)
    everyone = _gather_all(part, name="gather_small_grads")
    total = _sum_leading(everyone, name="small_grad_sum")
    full_grads = dict(zip(small, _unpack(total, [full_shapes[n] for n in small])))
    for n in ("conv_a_w", "conv_b_w"):
        full_grads[n] = lax.dynamic_slice_in_dim(full_grads[n], chip_idx * c_loc, c_loc, axis=2)
    shapes = [W[n].shape for n in small]
    d_s, m_s, v_s = _adamw(_pack([W[n] for n in small], 128), _pack([full_grads[n] for n in small], 128),
                           _pack([MO[n] for n in small], 128), _pack([VO[n] for n in small], 128),
                           name="adamw_small")
    for n, d, nm, nv in zip(small, _unpack(d_s, shapes), _unpack(m_s, shapes), _unpack(v_s, shapes)):
        grads[n], deltas[n], new_m[n], new_v[n] = full_grads[n], d, nm, nv

    return (loss, grad_x, *[grads[n] for n in names], *[deltas[n] for n in names],
            *[new_m[n] for n in names], *[new_v[n] for n in names])
```

```python
import jax
import jax.numpy as jnp
from jax import lax
from jax.experimental import pallas as pl
from jax.experimental.pallas import tpu as pltpu

F32 = jnp.float32
BF16 = jnp.bfloat16
MESH = pl.DeviceIdType.MESH

EPS = 1e-6
N_XHEADS = 4
K_A = 3
K_B = 31
PAD_A = 8
PAD_B = 32
CONV_CHUNK = 256
ROW_TILE = 512
LANES = 1024
VMEM_LIMIT_BYTES = 48 * 1024 * 1024

ADAM_LR = 0.001
ADAM_B1 = 0.9
ADAM_B2 = 0.999
ADAM_EPS = 1e-08
ADAM_WD = 0.01
ADAM_STEP = 10

BIG = (("w_in", 1), ("w_out", 0), ("w_q", 0), ("w_kv", 1), ("w_xo", 0), ("w_up", 1), ("w_down", 0))
N_CHIPS = 4
N_DEV = 8


def _params(sem=None):
    return pltpu.CompilerParams(dimension_semantics=sem, vmem_limit_bytes=VMEM_LIMIT_BYTES)


def _pick(cands, n):
    for c in cands:
        if c <= n and n % c == 0:
            return c
    return n


def _mm(a, b, *, name, ta=False, tb=False, out_dtype=F32, res=None, epi=None, aux=None,
        b_stack=False, o_stack=0, bm=1024, bn=512, bk=1024):
    if ta:
        K, M = a.shape
    else:
        M, K = a.shape
    if b_stack:
        n_st, d1, d2 = b.shape
        N, kb = (d1, d2) if tb else (n_st * d2, d1)
        assert K == (n_st * d2 if tb else d1), (name, a.shape, b.shape)
    else:
        N = b.shape[0] if tb else b.shape[1]
    n_unit = b.shape[2] if (b_stack and not tb) else (N // o_stack if o_stack else N)
    k_unit = b.shape[2] if (b_stack and tb) else K
    bm = _pick((bm, 512, 256, 128), M)
    bn = _pick((bn, 640, 256, 384, 128), n_unit)
    bk = _pick((bk, 640, 512, 256, 128), k_unit)
    assert M % bm == 0 and N % bn == 0 and K % bk == 0, (name, M, N, K)
    nk = K // bk
    per_n = n_unit // bn
    per_k = k_unit // bk
    a_spec = (pl.BlockSpec((bk, bm), lambda i, j, k: (k, i)) if ta
              else pl.BlockSpec((bm, bk), lambda i, j, k: (i, k)))
    if b_stack and tb:
        b_spec = pl.BlockSpec((None, bn, bk), lambda i, j, k: (k // per_k, j, k % per_k))
    elif b_stack:
        b_spec = pl.BlockSpec((None, bk, bn), lambda i, j, k: (j // per_n, k, j % per_n))
    elif tb:
        b_spec = pl.BlockSpec((bn, bk), lambda i, j, k: (j, k))
    else:
        b_spec = pl.BlockSpec((bk, bn), lambda i, j, k: (k, j))
    o_spec = pl.BlockSpec((bm, bn), lambda i, j, k: (i, j))
    dims = (((0 if ta else 1,), (1 if tb else 0,)), ((), ()))
    ins, in_specs = [a, b], [a_spec, b_spec]
    if res is not None:
        ins.append(res)
        in_specs.append(o_spec)
    if aux is not None:
        ins.append(aux)
        in_specs.append(o_spec)
    n_out = 2 if epi == "sqrelu" else 1
    out_shape = [jax.ShapeDtypeStruct((M, N), out_dtype)] * n_out
    out_specs = [o_spec] * n_out
    if o_stack:
        assert n_out == 1 and res is None and aux is None
        out_shape = [jax.ShapeDtypeStruct((o_stack, M, N // o_stack), out_dtype)]
        out_specs = [pl.BlockSpec((None, bm, bn), lambda i, j, k: (j // per_n, i, j % per_n))]

    def body(*refs):
        a_ref, b_ref = refs[0], refs[1]
        pos = 2
        res_ref = aux_ref = None
        if res is not None:
            res_ref = refs[pos]
            pos += 1
        if aux is not None:
            aux_ref = refs[pos]
            pos += 1
        outs = refs[pos:pos + n_out]
        acc = refs[pos + n_out]
        k = pl.program_id(2)

        @pl.when(k == 0)
        def _():
            acc[...] = jnp.zeros_like(acc)

        acc[...] += lax.dot_general(a_ref[...], b_ref[...], dims, preferred_element_type=F32)

        @pl.when(k == nk - 1)
        def _():
            r = acc[...]
            if res_ref is not None:
                r = r + res_ref[...]
            if epi == "sqrelu":
                outs[0][...] = r.astype(out_dtype)
                rl = jnp.maximum(r, 0.0)
                outs[1][...] = (rl * rl).astype(out_dtype)
            elif epi == "dsqrelu":
                outs[0][...] = (r * (2.0 * jnp.maximum(aux_ref[...].astype(F32), 0.0))).astype(out_dtype)
            else:
                outs[0][...] = r.astype(out_dtype)

    out = pl.pallas_call(
        body, name=name, grid=(M // bm, N // bn, nk),
        in_specs=in_specs, out_specs=out_specs, out_shape=out_shape,
        scratch_shapes=[pltpu.VMEM((bm, bn), F32)],
        compiler_params=_params(("parallel", "parallel", "arbitrary")),
    )(*ins)
    return out if n_out == 2 else out[0]


def _row_tile(rows):
    return min(ROW_TILE, rows)


def _rms_fwd(x, g, *, name):
    S, D = x.shape
    tr = _row_tile(S)

    def body(x_ref, g_ref, o_ref):
        xv = x_ref[...]
        r = lax.rsqrt(jnp.mean(xv * xv, axis=-1, keepdims=True) + EPS)
        o_ref[...] = (xv * r * g_ref[...]).astype(BF16)

    return pl.pallas_call(
        body, name=name, grid=(S // tr,),
        in_specs=[pl.BlockSpec((tr, D), lambda i: (i, 0)), pl.BlockSpec((1, D), lambda i: (0, 0))],
        out_specs=pl.BlockSpec((tr, D), lambda i: (i, 0)),
        out_shape=jax.ShapeDtypeStruct((S, D), BF16),
        compiler_params=_params(("parallel",)),
    )(x, g)


def _rms_bwd(x, g, du, dres, *, name):
    S, D = x.shape
    tr = _row_tile(S)
    has_res = dres is not None

    def body(*refs):
        if has_res:
            x_ref, g_ref, du_ref, dres_ref, dx_ref, dxb_ref, dg_ref = refs
        else:
            x_ref, g_ref, du_ref, dx_ref, dxb_ref, dg_ref = refs
        xv = x_ref[...]
        r = lax.rsqrt(jnp.mean(xv * xv, axis=-1, keepdims=True) + EPS)
        xh = xv * r
        dy = du_ref[...]
        dxh = dy * g_ref[...]
        dx = r * (dxh - xh * jnp.mean(dxh * xh, axis=-1, keepdims=True))
        if has_res:
            dx = dx + dres_ref[...]
        dx_ref[...] = dx
        dxb_ref[...] = dx.astype(BF16)

        @pl.when(pl.program_id(0) == 0)
        def _():
            dg_ref[...] = jnp.zeros_like(dg_ref)

        dg_ref[...] += jnp.sum(dy * xh, axis=0, keepdims=True)

    row = pl.BlockSpec((tr, D), lambda i: (i, 0))
    vec = pl.BlockSpec((1, D), lambda i: (0, 0))
    ins = [x, g, du] + ([dres] if has_res else [])
    in_specs = [row, vec, row] + ([row] if has_res else [])
    return pl.pallas_call(
        body, name=name, grid=(S // tr,),
        in_specs=in_specs, out_specs=[row, row, vec],
        out_shape=[jax.ShapeDtypeStruct((S, D), F32), jax.ShapeDtypeStruct((S, D), BF16),
                   jax.ShapeDtypeStruct((1, D), F32)],
        compiler_params=_params(("arbitrary",)),
    )(*ins)


def _loss_head(h, g, target, *, name):
    S, D = h.shape
    tr = _row_tile(S)

    def body(x_ref, g_ref, t_ref, loss_ref, dx_ref, dxb_ref, dg_ref):
        xv = x_ref[...]
        r = lax.rsqrt(jnp.mean(xv * xv, axis=-1, keepdims=True) + EPS)
        xh = xv * r
        gv = g_ref[...]
        err = xh * gv - t_ref[...]
        part = 0.5 * jnp.sum(jnp.mean(err * err, axis=-1, keepdims=True), axis=0, keepdims=True)
        dy = err * (1.0 / D)
        dxh = dy * gv
        dx = r * (dxh - xh * jnp.mean(dxh * xh, axis=-1, keepdims=True))
        dx_ref[...] = dx
        dxb_ref[...] = dx.astype(BF16)

        @pl.when(pl.program_id(0) == 0)
        def _():
            dg_ref[...] = jnp.zeros_like(dg_ref)
            loss_ref[...] = jnp.zeros_like(loss_ref)

        dg_ref[...] += jnp.sum(dy * xh, axis=0, keepdims=True)
        loss_ref[...] += jnp.broadcast_to(part, loss_ref.shape)

    row = pl.BlockSpec((tr, D), lambda i: (i, 0))
    vec = pl.BlockSpec((1, D), lambda i: (0, 0))
    return pl.pallas_call(
        body, name=name, grid=(S // tr,),
        in_specs=[row, vec, row],
        out_specs=[pl.BlockSpec((1, 128), lambda i: (0, 0)), row, row, vec],
        out_shape=[jax.ShapeDtypeStruct((1, 128), F32), jax.ShapeDtypeStruct((S, D), F32),
                   jax.ShapeDtypeStruct((S, D), BF16), jax.ShapeDtypeStruct((1, D), F32)],
        compiler_params=_params(("arbitrary",)),
    )(h, g, target)


def _sigmoid(x):
    return 1.0 / (1.0 + jnp.exp(-x))


def _ln_silu_fwd(cb, g, b, *, name):
    S, C = cb.shape
    tr = _row_tile(S)

    def body(x_ref, g_ref, b_ref, o_ref):
        xv = x_ref[...]
        mu = jnp.mean(xv, axis=-1, keepdims=True)
        xc = xv - mu
        rs = lax.rsqrt(jnp.mean(xc * xc, axis=-1, keepdims=True) + EPS)
        l = xc * rs * g_ref[...] + b_ref[...]
        o_ref[...] = (l * _sigmoid(l)).astype(BF16)

    row = pl.BlockSpec((tr, C), lambda i: (i, 0))
    vec = pl.BlockSpec((1, C), lambda i: (0, 0))
    return pl.pallas_call(
        body, name=name, grid=(S // tr,), in_specs=[row, vec, vec], out_specs=row,
        out_shape=jax.ShapeDtypeStruct((S, C), BF16), compiler_params=_params(("parallel",)),
    )(cb, g, b)


def _ln_silu_bwd(cb, g, b, dy, col_block, *, name):
    S, C = cb.shape
    tr = _row_tile(S)

    def body(x_ref, g_ref, b_ref, dy_ref, dx_ref, dg_ref, db_ref):
        xv = x_ref[...]
        mu = jnp.mean(xv, axis=-1, keepdims=True)
        xc = xv - mu
        rs = lax.rsqrt(jnp.mean(xc * xc, axis=-1, keepdims=True) + EPS)
        xh = xc * rs
        gv = g_ref[...]
        l = xh * gv + b_ref[...]
        sg = _sigmoid(l)
        dl = dy_ref[...] * (sg + l * sg * (1.0 - sg))
        dxh = dl * gv
        dx_ref[...] = rs * (dxh - jnp.mean(dxh, axis=-1, keepdims=True)
                            - xh * jnp.mean(dxh * xh, axis=-1, keepdims=True))

        @pl.when(pl.program_id(0) == 0)
        def _():
            dg_ref[...] = jnp.zeros_like(dg_ref)
            db_ref[...] = jnp.zeros_like(db_ref)

        dg_ref[...] += jnp.sum(dl * xh, axis=0, keepdims=True)
        db_ref[...] += jnp.sum(dl, axis=0, keepdims=True)

    row = pl.BlockSpec((tr, C), lambda i: (i, 0))
    vec = pl.BlockSpec((1, C), lambda i: (0, 0))
    return pl.pallas_call(
        body, name=name, grid=(S // tr,),
        in_specs=[row, vec, vec, pl.BlockSpec((tr, C), lambda i: (i, col_block))],
        out_specs=[row, vec, vec],
        out_shape=[jax.ShapeDtypeStruct((S, C), F32), jax.ShapeDtypeStruct((1, C), F32),
                   jax.ShapeDtypeStruct((1, C), F32)],
        compiler_params=_params(("arbitrary",)),
    )(cb, g, b, dy)


def _attn_fwd(q, kv, *, name):
    S, D = q.shape
    M = kv.shape[0]
    hd = D // N_XHEADS
    scale = 1.0 / float(hd) ** 0.5
    tq = _row_tile(S)

    def body(q_ref, k_ref, v_ref, o_ref):
        for h in range(N_XHEADS):
            cols = slice(h * hd, (h + 1) * hd)
            s = lax.dot_general(q_ref[:, cols], k_ref[:, cols], (((1,), (1,)), ((), ())),
                                preferred_element_type=F32) * scale
            e = jnp.exp(s - jnp.max(s, axis=-1, keepdims=True))
            p = e / jnp.sum(e, axis=-1, keepdims=True)
            o = jnp.dot(p.astype(BF16), v_ref[:, cols], preferred_element_type=F32)
            o_ref[:, cols] = o.astype(BF16)

    return pl.pallas_call(
        body, name=name, grid=(S // tq,),
        in_specs=[pl.BlockSpec((tq, D), lambda i: (i, 0)), pl.BlockSpec((M, D), lambda i: (0, 0)),
                  pl.BlockSpec((M, D), lambda i: (0, 1))],
        out_specs=pl.BlockSpec((tq, D), lambda i: (i, 0)),
        out_shape=jax.ShapeDtypeStruct((S, D), BF16), compiler_params=_params(("parallel",)),
    )(q, kv, kv)


def _attn_bwd(q, kv, do, *, name):
    S, D = q.shape
    M = kv.shape[0]
    hd = D // N_XHEADS
    scale = 1.0 / float(hd) ** 0.5
    tq = _row_tile(S)

    def body(q_ref, k_ref, v_ref, do_ref, dq_ref, dkv_ref):
        @pl.when(pl.program_id(0) == 0)
        def _():
            dkv_ref[...] = jnp.zeros_like(dkv_ref)

        for h in range(N_XHEADS):
            cols = slice(h * hd, (h + 1) * hd)
            vcols = slice(D + h * hd, D + (h + 1) * hd)
            qh, kh, vh, doh = q_ref[:, cols], k_ref[:, cols], v_ref[:, cols], do_ref[:, cols]
            s = lax.dot_general(qh, kh, (((1,), (1,)), ((), ())), preferred_element_type=F32) * scale
            e = jnp.exp(s - jnp.max(s, axis=-1, keepdims=True))
            p = e / jnp.sum(e, axis=-1, keepdims=True)
            pb = p.astype(BF16)
            dp = lax.dot_general(doh, vh, (((1,), (1,)), ((), ())), preferred_element_type=F32)
            ds = (p * (dp - jnp.sum(dp * p, axis=-1, keepdims=True)) * scale).astype(BF16)
            dq_ref[:, cols] = jnp.dot(ds, kh, preferred_element_type=F32).astype(BF16)
            dkv_ref[:, cols] += lax.dot_general(ds, qh, (((0,), (0,)), ((), ())), preferred_element_type=F32)
            dkv_ref[:, vcols] += lax.dot_general(pb, doh, (((0,), (0,)), ((), ())), preferred_element_type=F32)

    row = pl.BlockSpec((tq, D), lambda i: (i, 0))
    return pl.pallas_call(
        body, name=name, grid=(S // tq,),
        in_specs=[row, pl.BlockSpec((M, D), lambda i: (0, 0)), pl.BlockSpec((M, D), lambda i: (0, 1)), row],
        out_specs=[row, pl.BlockSpec((M, 2 * D), lambda i: (0, 0))],
        out_shape=[jax.ShapeDtypeStruct((S, D), BF16), jax.ShapeDtypeStruct((M, 2 * D), F32)],
        compiler_params=_params(("arbitrary",)),
    )(q, kv, kv, do)


def _delayed(win, j, pad):
    return (win if j == 0 else pltpu.roll(win, j, 0))[pad:, :]


def _advanced(win, j, ch):
    return (win if j == 0 else pltpu.roll(win, win.shape[0] - j, 0))[:ch, :]


def _mixer_a_fwd(z, w, *, name):
    S = z.shape[0]
    C = w.shape[1]
    nb = C // 128
    ch = min(CONV_CHUNK, S)

    def body(b_ref, c_ref, h_ref, w_ref, y_ref, xp):
        xp[0:PAD_A, :] = jnp.zeros((PAD_A, 128), F32)
        xp[PAD_A:, :] = c_ref[...] * h_ref[...]

        def chunk(i, carry):
            base = pl.multiple_of(i * ch, ch)
            win = xp[pl.ds(base, ch + PAD_A), :]
            acc = _delayed(win, 0, PAD_A) * w_ref[K_A - 1:K_A, :]
            for j in range(1, K_A):
                acc = acc + _delayed(win, j, PAD_A) * w_ref[K_A - 1 - j:K_A - j, :]
            y_ref[pl.ds(base, ch), :] = (b_ref[pl.ds(base, ch), :] * acc).astype(BF16)
            return carry

        lax.fori_loop(0, S // ch, chunk, 0)

    def col(g):
        return pl.BlockSpec((S, 128), lambda j: (0, g * nb + j))

    return pl.pallas_call(
        body, name=name, grid=(nb,),
        in_specs=[col(0), col(1), col(2), pl.BlockSpec((K_A, 128), lambda j: (0, j))],
        out_specs=pl.BlockSpec((S, 128), lambda j: (0, j)),
        out_shape=jax.ShapeDtypeStruct((S, C), BF16),
        scratch_shapes=[pltpu.VMEM((PAD_A + S, 128), F32)],
        compiler_params=_params(("parallel",)),
    )(z, z, z, w)


def _mixer_a_bwd(z, w, dy, *, name):
    S = z.shape[0]
    C = w.shape[1]
    nb = C // 128
    ch = min(CONV_CHUNK, S)

    def body(b_ref, c_ref, h_ref, w_ref, dy_ref, db_ref, dc_ref, dh_ref, dw_ref, xp, dp):
        xp[0:PAD_A, :] = jnp.zeros((PAD_A, 128), F32)
        xp[PAD_A:, :] = c_ref[...] * h_ref[...]
        dp[S:, :] = jnp.zeros((PAD_A, 128), F32)
        dw_ref[...] = jnp.zeros_like(dw_ref)

        def chunk(i, carry):
            base = pl.multiple_of(i * ch, ch)
            win = xp[pl.ds(base, ch + PAD_A), :]
            dya = dy_ref[pl.ds(base, ch), :]
            dcv = dya * b_ref[pl.ds(base, ch), :]
            dp[pl.ds(base, ch), :] = dcv
            acc = None
            for j in range(K_A):
                xs = _delayed(win, j, PAD_A)
                k = K_A - 1 - j
                term = xs * w_ref[k:k + 1, :]
                acc = term if acc is None else acc + term
                dw_ref[k:k + 1, :] += jnp.sum(dcv * xs, axis=0, keepdims=True)
            db_ref[pl.ds(base, ch), :] = (dya * acc).astype(BF16)
            return carry

        lax.fori_loop(0, S // ch, chunk, 0)

        def chunk2(i, carry):
            base = pl.multiple_of(i * ch, ch)
            win = dp[pl.ds(base, ch + PAD_A), :]
            acc = None
            for j in range(K_A):
                term = _advanced(win, j, ch) * w_ref[K_A - 1 - j:K_A - j, :]
                acc = term if acc is None else acc + term
            dc_ref[pl.ds(base, ch), :] = (acc * h_ref[pl.ds(base, ch), :]).astype(BF16)
            dh_ref[pl.ds(base, ch), :] = (acc * c_ref[pl.ds(base, ch), :]).astype(BF16)
            return carry

        lax.fori_loop(0, S // ch, chunk2, 0)

    def col(g):
        return pl.BlockSpec((S, 128), lambda j: (0, g * nb + j))

    out_col = pl.BlockSpec((S, 128), lambda j: (0, j))
    wspec = pl.BlockSpec((K_A, 128), lambda j: (0, j))
    return pl.pallas_call(
        body, name=name, grid=(nb,),
        in_specs=[col(0), col(1), col(2), wspec, out_col],
        out_specs=[out_col, out_col, out_col, wspec],
        out_shape=[jax.ShapeDtypeStruct((S, C), BF16)] * 3 + [jax.ShapeDtypeStruct((K_A, C), F32)],
        scratch_shapes=[pltpu.VMEM((PAD_A + S, 128), F32), pltpu.VMEM((S + PAD_A, 128), F32)],
        compiler_params=_params(("parallel",)),
    )(z, z, z, w, dy)


def _mixer_b_fwd(z, w, bias, *, name):
    S = z.shape[0]
    C = w.shape[1]
    nb = C // 128
    ch = min(CONV_CHUNK, S)

    def body(v_ref, g_ref, w_ref, bias_ref, cb_ref, xp):
        xp[0:PAD_B, :] = jnp.zeros((PAD_B, 128), F32)
        xp[PAD_B:, :] = v_ref[...] * _sigmoid(g_ref[...])

        def chunk(i, carry):
            base = pl.multiple_of(i * ch, ch)
            win = xp[pl.ds(base, ch + PAD_B), :]
            acc = None
            for j in range(K_B):
                term = _delayed(win, j, PAD_B) * w_ref[K_B - 1 - j:K_B - j, :]
                acc = term if acc is None else acc + term
            cb_ref[pl.ds(base, ch), :] = acc + bias_ref[...]
            return carry

        lax.fori_loop(0, S // ch, chunk, 0)

    def col(g):
        return pl.BlockSpec((S, 128), lambda j: (0, g * nb + j))

    return pl.pallas_call(
        body, name=name, grid=(nb,),
        in_specs=[col(3), col(4), pl.BlockSpec((K_B, 128), lambda j: (0, j)),
                  pl.BlockSpec((1, 128), lambda j: (0, j))],
        out_specs=pl.BlockSpec((S, 128), lambda j: (0, j)),
        out_shape=jax.ShapeDtypeStruct((S, C), F32),
        scratch_shapes=[pltpu.VMEM((PAD_B + S, 128), F32)],
        compiler_params=_params(("parallel",)),
    )(z, z, w, bias)


def _mixer_b_bwd(z, w, dcb, *, name):
    S = z.shape[0]
    C = w.shape[1]
    nb = C // 128
    ch = min(CONV_CHUNK, S)

    def body(v_ref, g_ref, w_ref, dcb_ref, dv_ref, dg_ref, dw_ref, dbias_ref, xp, dp):
        xp[0:PAD_B, :] = jnp.zeros((PAD_B, 128), F32)
        xp[PAD_B:, :] = v_ref[...] * _sigmoid(g_ref[...])
        dp[0:S, :] = dcb_ref[...]
        dp[S:, :] = jnp.zeros((PAD_B, 128), F32)
        dw_ref[...] = jnp.zeros_like(dw_ref)
        dbias_ref[...] = jnp.sum(dcb_ref[...], axis=0, keepdims=True)

        def chunk(i, carry):
            base = pl.multiple_of(i * ch, ch)
            win = xp[pl.ds(base, ch + PAD_B), :]
            d = dcb_ref[pl.ds(base, ch), :]
            for j in range(K_B):
                k = K_B - 1 - j
                dw_ref[k:k + 1, :] += jnp.sum(d * _delayed(win, j, PAD_B), axis=0, keepdims=True)
            return carry

        lax.fori_loop(0, S // ch, chunk, 0)

        def chunk2(i, carry):
            base = pl.multiple_of(i * ch, ch)
            win = dp[pl.ds(base, ch + PAD_B), :]
            acc = None
            for j in range(K_B):
                term = _advanced(win, j, ch) * w_ref[K_B - 1 - j:K_B - j, :]
                acc = term if acc is None else acc + term
            sg = _sigmoid(g_ref[pl.ds(base, ch), :])
            vv = v_ref[pl.ds(base, ch), :]
            dv_ref[pl.ds(base, ch), :] = (acc * sg).astype(BF16)
            dg_ref[pl.ds(base, ch), :] = (acc * vv * sg * (1.0 - sg)).astype(BF16)
            return carry

        lax.fori_loop(0, S // ch, chunk2, 0)

    def col(g):
        return pl.BlockSpec((S, 128), lambda j: (0, g * nb + j))

    out_col = pl.BlockSpec((S, 128), lambda j: (0, j))
    wspec = pl.BlockSpec((K_B, 128), lambda j: (0, j))
    bspec = pl.BlockSpec((1, 128), lambda j: (0, j))
    return pl.pallas_call(
        body, name=name, grid=(nb,),
        in_specs=[col(3), col(4), wspec, out_col],
        out_specs=[out_col, out_col, wspec, bspec],
        out_shape=[jax.ShapeDtypeStruct((S, C), BF16)] * 2
        + [jax.ShapeDtypeStruct((K_B, C), F32), jax.ShapeDtypeStruct((1, C), F32)],
        scratch_shapes=[pltpu.VMEM((PAD_B + S, 128), F32), pltpu.VMEM((S + PAD_B, 128), F32)],
        compiler_params=_params(("parallel",)),
    )(z, z, w, dcb)


def _ew_tile(R):
    for t in (512, 256, 128, 64, 32, 16, 8):
        if R % t == 0:
            return t
    return R


def _pair_sum(g, r, meta, *, name):
    n, a, b = g.shape
    ah = a // 2
    tr = _pick((256, 128, 64, 32, 16), ah)
    nh = ah // tr

    def body(meta_ref, g_ref, r_ref, o_ref):
        o_ref[...] = (g_ref[...] + r_ref[...]).astype(BF16)

    half = pl.BlockSpec((None, tr, b), lambda j, i, meta_ref: (j, i, 0))
    return pl.pallas_call(
        body, name=name,
        grid_spec=pltpu.PrefetchScalarGridSpec(
            num_scalar_prefetch=1, grid=(n, nh),
            in_specs=[pl.BlockSpec((None, tr, b), lambda j, i, meta_ref: (j, meta_ref[0] * nh + i, 0)), half],
            out_specs=half),
        out_shape=jax.ShapeDtypeStruct((n, ah, b), BF16), compiler_params=_params(("parallel", "parallel")),
    )(meta, g, r)


def _chip_sum(p, q, meta, *, name):
    n, ah, b = p.shape
    tr = _pick((256, 128, 64, 32, 16), ah)
    nh = ah // tr

    def body(meta_ref, p_ref, q1_ref, q2_ref, q3_ref, o_ref):
        o_ref[...] = ((p_ref[...].astype(F32) + q1_ref[...].astype(F32)) + q2_ref[...].astype(F32)
                      ) + q3_ref[...].astype(F32)

    def piece(mask):
        return pl.BlockSpec((None, tr, b), lambda i, meta_ref: (meta_ref[1] ^ mask, i, 0))

    return pl.pallas_call(
        body, name=name,
        grid_spec=pltpu.PrefetchScalarGridSpec(
            num_scalar_prefetch=1, grid=(nh,),
            in_specs=[piece(0), piece(1), piece(2), piece(3)],
            out_specs=pl.BlockSpec((tr, b), lambda i, meta_ref: (meta_ref[0] * nh + i, 0))),
        out_shape=jax.ShapeDtypeStruct((2 * ah, b), F32), compiler_params=_params(("parallel",)),
    )(meta, p, q, q, q)


def _sum_leading(x, *, name):
    n, R, C = x.shape
    tr = _ew_tile(R)

    def body(x_ref, o_ref):
        acc = x_ref[0].astype(F32)
        for k in range(1, n):
            acc = acc + x_ref[k].astype(F32)
        o_ref[...] = acc

    return pl.pallas_call(
        body, name=name, grid=(R // tr,),
        in_specs=[pl.BlockSpec((n, tr, C), lambda i: (0, i, 0))],
        out_specs=pl.BlockSpec((tr, C), lambda i: (i, 0)),
        out_shape=jax.ShapeDtypeStruct((R, C), F32), compiler_params=_params(("parallel",)),
    )(x)


def _adamw(w, g, m, v, *, name):
    R, C = w.shape
    tr = _ew_tile(R)

    def body(w_ref, g_ref, m_ref, v_ref, d_ref, nm_ref, nv_ref):
        gv = g_ref[...]
        nm = ADAM_B1 * m_ref[...] + (1.0 - ADAM_B1) * gv
        nv = ADAM_B2 * v_ref[...] + (1.0 - ADAM_B2) * (gv * gv)
        m_hat = nm / (1.0 - ADAM_B1 ** ADAM_STEP)
        v_hat = nv / (1.0 - ADAM_B2 ** ADAM_STEP)
        d_ref[...] = -ADAM_LR * (m_hat / (jnp.sqrt(v_hat) + ADAM_EPS) + ADAM_WD * w_ref[...])
        nm_ref[...] = nm
        nv_ref[...] = nv

    row = pl.BlockSpec((tr, C), lambda i: (i, 0))
    return pl.pallas_call(
        body, name=name, grid=(R // tr,), in_specs=[row] * 4, out_specs=[row] * 3,
        out_shape=[jax.ShapeDtypeStruct((R, C), F32)] * 3, compiler_params=_params(("parallel",)),
    )(w, g, m, v)


def _adamw_layers(w, g0, g1, m, v, *, name):
    _, a, b = w.shape
    tr = _pick((256, 128, 64, 32, 16, 8), a)

    def body(w_ref, g0_ref, g1_ref, m_ref, v_ref, g_ref, d_ref, nm_ref, nv_ref):
        gv = jnp.where(pl.program_id(1) == 0, g0_ref[...], g1_ref[...])
        nm = ADAM_B1 * m_ref[...] + (1.0 - ADAM_B1) * gv
        nv = ADAM_B2 * v_ref[...] + (1.0 - ADAM_B2) * (gv * gv)
        m_hat = nm / (1.0 - ADAM_B1 ** ADAM_STEP)
        v_hat = nv / (1.0 - ADAM_B2 ** ADAM_STEP)
        g_ref[...] = gv
        d_ref[...] = -ADAM_LR * (m_hat / (jnp.sqrt(v_hat) + ADAM_EPS) + ADAM_WD * w_ref[...])
        nm_ref[...] = nm
        nv_ref[...] = nv

    lay = pl.BlockSpec((None, tr, b), lambda i, l: (l, i, 0))
    row = pl.BlockSpec((tr, b), lambda i, l: (i, 0))
    return pl.pallas_call(
        body, name=name, grid=(a // tr, 2), in_specs=[lay, row, row, lay, lay], out_specs=[lay] * 4,
        out_shape=[jax.ShapeDtypeStruct(w.shape, F32)] * 4, compiler_params=_params(("parallel", "arbitrary")),
    )(w, g0, g1, m, v)


ANY = pl.BlockSpec(memory_space=pl.ANY)


def _place():
    x, y, c = lax.axis_index("x"), lax.axis_index("y"), lax.axis_index("c")
    return x, y, c, 2 * x + y


def _other_chip(x, y, mask):
    px = 1 - x if mask & 2 else x
    py = 1 - y if mask & 1 else y
    return px, py, 2 * px + py


MASKS = (1, 2, 3)


def _half(ref, c, lead=()):
    ah = ref.shape[-2] // 2
    return ref.at[(*lead, pl.ds(c * ah, ah), slice(None))]


def _gather_weights(owns, *, name):
    n = len(owns)

    def body(*refs):
        own, out = refs[:n], refs[n:2 * n]
        send_sems, recv_sems = refs[2 * n], refs[2 * n + 1]
        x, y, c, chip = _place()
        sends = []
        for i in range(n):
            for k, mask in enumerate(MASKS):
                px, py, _ = _other_chip(x, y, mask)
                cp = pltpu.make_async_remote_copy(_half(own[i], c), _half(out[i], c, (chip,)),
                                                  send_sems.at[i, k], recv_sems.at[i, k],
                                                  device_id=(px, py, c), device_id_type=MESH)
                cp.start()
                sends.append(cp)
        for i in range(n):
            for k, mask in enumerate(MASKS):
                px, py, pchip = _other_chip(x, y, mask)
                got = _half(out[i], c, (pchip,))
                pltpu.make_async_remote_copy(got, got, send_sems.at[i, k], recv_sems.at[i, k],
                                             device_id=(px, py, c), device_id_type=MESH).wait_recv()
                cp = pltpu.make_async_remote_copy(got, got, send_sems.at[i, 3 + k], recv_sems.at[i, 3 + k],
                                                  device_id=(x, y, 1 - c), device_id_type=MESH)
                cp.start()
                sends.append(cp)
        for i in range(n):
            for k, mask in enumerate(MASKS):
                _, _, pchip = _other_chip(x, y, mask)
                theirs = _half(out[i], 1 - c, (pchip,))
                pltpu.make_async_remote_copy(theirs, theirs, send_sems.at[i, 3 + k], recv_sems.at[i, 3 + k],
                                             device_id=(x, y, 1 - c), device_id_type=MESH).wait_recv()
        for cp in sends:
            cp.wait_send()

    return pl.pallas_call(
        body, name=name, in_specs=[ANY] * n, out_specs=[ANY] * n,
        out_shape=[jax.ShapeDtypeStruct((N_CHIPS, *o.shape), o.dtype) for o in owns],
        scratch_shapes=[pltpu.SemaphoreType.DMA((n, 6)), pltpu.SemaphoreType.DMA((n, 6))],
    )(*owns)


def _swap_halves(gs, *, name):
    n = len(gs)

    def body(*refs):
        g, out = refs[:n], refs[n:2 * n]
        send_sems, recv_sems = refs[2 * n], refs[2 * n + 1]
        x, y, c, _ = _place()
        cps = []
        for i in range(n):
            ah = g[i].shape[1] // 2
            cp = pltpu.make_async_remote_copy(g[i].at[:, pl.ds((1 - c) * ah, ah), :], out[i],
                                              send_sems.at[i], recv_sems.at[i],
                                              device_id=(x, y, 1 - c), device_id_type=MESH)
            cp.start()
            cps.append(cp)
        for cp in cps:
            cp.wait()

    return pl.pallas_call(
        body, name=name, in_specs=[ANY] * n, out_specs=[ANY] * n,
        out_shape=[jax.ShapeDtypeStruct((g.shape[0], g.shape[1] // 2, g.shape[2]), g.dtype) for g in gs],
        scratch_shapes=[pltpu.SemaphoreType.DMA((n,)), pltpu.SemaphoreType.DMA((n,))],
    )(*gs)


def _scatter_to_chips(ps, *, name):
    n = len(ps)

    def body(*refs):
        p, out = refs[:n], refs[n:2 * n]
        send_sems, recv_sems = refs[2 * n], refs[2 * n + 1]
        x, y, c, chip = _place()
        sends = []
        for i in range(n):
            for k, mask in enumerate(MASKS):
                px, py, pchip = _other_chip(x, y, mask)
                cp = pltpu.make_async_remote_copy(p[i].at[pchip], out[i].at[chip], send_sems.at[i, k],
                                                  recv_sems.at[i, k], device_id=(px, py, c), device_id_type=MESH)
                cp.start()
                sends.append(cp)
        for i in range(n):
            for k, mask in enumerate(MASKS):
                px, py, pchip = _other_chip(x, y, mask)
                pltpu.make_async_remote_copy(p[i].at[pchip], out[i].at[pchip], send_sems.at[i, k],
                                             recv_sems.at[i, k], device_id=(px, py, c),
                                             device_id_type=MESH).wait_recv()
        for cp in sends:
            cp.wait_send()

    return pl.pallas_call(
        body, name=name, in_specs=[ANY] * n, out_specs=[ANY] * n,
        out_shape=[jax.ShapeDtypeStruct(p.shape, p.dtype) for p in ps],
        scratch_shapes=[pltpu.SemaphoreType.DMA((n, 3)), pltpu.SemaphoreType.DMA((n, 3))],
    )(*ps)


def _share_halves(gs, *, name):
    n = len(gs)

    def body(*refs):
        out = refs[n:2 * n]
        send_sems, recv_sems = refs[2 * n], refs[2 * n + 1]
        x, y, c, _ = _place()
        cps = []
        for i in range(n):
            cp = pltpu.make_async_remote_copy(_half(out[i], c), _half(out[i], c), send_sems.at[i], recv_sems.at[i],
                                              device_id=(x, y, 1 - c), device_id_type=MESH)
            cp.start()
            cps.append(cp)
        for i in range(n):
            theirs = _half(out[i], 1 - c)
            pltpu.make_async_remote_copy(theirs, theirs, send_sems.at[i], recv_sems.at[i],
                                         device_id=(x, y, 1 - c), device_id_type=MESH).wait_recv()
        for cp in cps:
            cp.wait_send()

    return pl.pallas_call(
        body, name=name, in_specs=[ANY] * n, out_specs=[ANY] * n,
        out_shape=[jax.ShapeDtypeStruct(g.shape, g.dtype) for g in gs],
        input_output_aliases={i: i for i in range(n)},
        scratch_shapes=[pltpu.SemaphoreType.DMA((n,)), pltpu.SemaphoreType.DMA((n,))],
    )(*gs)


def _gather_all(buf, *, name):
    r, L = buf.shape
    vmem = pl.BlockSpec(memory_space=pltpu.VMEM)
    masks = tuple(range(1, N_DEV))

    def body(buf_ref, out_ref, send_sems, recv_sems):
        x, y, c, _ = _place()
        me = 4 * x + 2 * y + c
        out_ref[me] = buf_ref[...]
        sends = []
        for k, mask in enumerate(masks):
            px = 1 - x if mask & 4 else x
            py = 1 - y if mask & 2 else y
            pc = 1 - c if mask & 1 else c
            cp = pltpu.make_async_remote_copy(buf_ref, out_ref.at[me], send_sems.at[k], recv_sems.at[k],
                                              device_id=(px, py, pc), device_id_type=MESH)
            cp.start()
            sends.append(cp)
        for k, mask in enumerate(masks):
            px = 1 - x if mask & 4 else x
            py = 1 - y if mask & 2 else y
            pc = 1 - c if mask & 1 else c
            pltpu.make_async_remote_copy(buf_ref, out_ref.at[4 * px + 2 * py + pc], send_sems.at[k],
                                         recv_sems.at[k], device_id=(px, py, pc), device_id_type=MESH).wait_recv()
        for cp in sends:
            cp.wait_send()

    return pl.pallas_call(
        body, name=name, in_specs=[vmem], out_specs=vmem,
        out_shape=jax.ShapeDtypeStruct((N_DEV, r, L), buf.dtype),
        scratch_shapes=[pltpu.SemaphoreType.DMA((N_DEV - 1,)), pltpu.SemaphoreType.DMA((N_DEV - 1,))],
    )(buf)


def _pack(arrs, lanes, row_mult=8):
    flat = jnp.concatenate([a.reshape(-1) for a in arrs])
    rows = -(-flat.shape[0] // lanes)
    rows = -(-rows // row_mult) * row_mult
    flat = jnp.pad(flat, (0, rows * lanes - flat.shape[0]))
    return flat.reshape(rows, lanes)


def _unpack(buf, shapes):
    flat = buf.reshape(-1)
    out, pos = [], 0
    for s in shapes:
        n = 1
        for d in s:
            n *= d
        out.append(flat[pos:pos + n].reshape(s))
        pos += n
    return out


def kernel(x, mem, norm_mix_g, w_in, conv_a_w, conv_b_w, conv_b_bias, ln_b_g, ln_b_b, w_out, norm_x_g, norm_mem_g, w_q, w_kv, w_xo, norm_ffn_g, w_up, w_down, final_g, loss_target, m_norm_mix_g, m_w_in, m_conv_a_w, m_conv_b_w, m_conv_b_bias, m_ln_b_g, m_ln_b_b, m_w_out, m_norm_x_g, m_norm_mem_g, m_w_q, m_w_kv, m_w_xo, m_norm_ffn_g, m_w_up, m_w_down, m_final_g, v_norm_mix_g, v_w_in, v_conv_a_w, v_conv_b_w, v_conv_b_bias, v_ln_b_g, v_ln_b_b, v_w_out, v_norm_x_g, v_norm_mem_g, v_w_q, v_w_kv, v_w_xo, v_norm_ffn_g, v_w_up, v_w_down, v_final_g):
    W = dict(norm_mix_g=norm_mix_g, w_in=w_in, conv_a_w=conv_a_w, conv_b_w=conv_b_w, conv_b_bias=conv_b_bias,
             ln_b_g=ln_b_g, ln_b_b=ln_b_b, w_out=w_out, norm_x_g=norm_x_g, norm_mem_g=norm_mem_g, w_q=w_q,
             w_kv=w_kv, w_xo=w_xo, norm_ffn_g=norm_ffn_g, w_up=w_up, w_down=w_down, final_g=final_g)
    MO = dict(norm_mix_g=m_norm_mix_g, w_in=m_w_in, conv_a_w=m_conv_a_w, conv_b_w=m_conv_b_w,
              conv_b_bias=m_conv_b_bias, ln_b_g=m_ln_b_g, ln_b_b=m_ln_b_b, w_out=m_w_out, norm_x_g=m_norm_x_g,
              norm_mem_g=m_norm_mem_g, w_q=m_w_q, w_kv=m_w_kv, w_xo=m_w_xo, norm_ffn_g=m_norm_ffn_g,
              w_up=m_w_up, w_down=m_w_down, final_g=m_final_g)
    VO = dict(norm_mix_g=v_norm_mix_g, w_in=v_w_in, conv_a_w=v_conv_a_w, conv_b_w=v_conv_b_w,
              conv_b_bias=v_conv_b_bias, ln_b_g=v_ln_b_g, ln_b_b=v_ln_b_b, w_out=v_w_out, norm_x_g=v_norm_x_g,
              norm_mem_g=v_norm_mem_g, w_q=v_w_q, w_kv=v_w_kv, w_xo=v_w_xo, norm_ffn_g=v_norm_ffn_g,
              w_up=v_w_up, w_down=v_w_down, final_g=v_final_g)
    names = list(W.keys())
    depth = norm_mix_g.shape[0]
    assert depth == 2, "the exchange splits the weights into one layer per core of a chip"
    c_idx = lax.axis_index("c")
    chip_idx = 2 * lax.axis_index("x") + lax.axis_index("y")

    xs = x[0]
    ms = mem[0]
    tgt = loss_target[0]
    S, D = xs.shape
    c_a = conv_a_w.shape[-1] * N_CHIPS
    c_loc = conv_a_w.shape[-1]

    meta = jnp.stack([c_idx, chip_idx]).astype(jnp.int32)
    owns = [W[n][l].astype(BF16) for l in range(depth) for n, _ in BIG]
    gathered = _gather_weights(owns, name="gather_weights")
    gathered = [lax.dynamic_update_slice(g, o[None], (chip_idx, 0, 0)) for g, o in zip(gathered, owns)]
    Wb = []
    for l in range(depth):
        wl = {}
        for i, (n, ax) in enumerate(BIG):
            g = gathered[l * len(BIG) + i]
            wl[n] = g.reshape(-1, g.shape[-1]) if ax == 0 else g
        Wb.append(wl)

    conv_local = _pack([conv_a_w, conv_b_w], 128)
    conv_all = _gather_all(conv_local, name="gather_conv_weights")
    na = depth * K_A * c_loc
    nbw = depth * K_B * c_loc
    ca_parts, cb_parts = [], []
    for j in range(N_CHIPS):
        fl = conv_all[2 * j].reshape(-1)
        ca_parts.append(fl[:na].reshape(depth, K_A, c_loc))
        cb_parts.append(fl[na:na + nbw].reshape(depth, K_B, c_loc))
    conv_a_full = jnp.concatenate(ca_parts, axis=-1)
    conv_b_full = jnp.concatenate(cb_parts, axis=-1)

    saved = []
    h = xs
    for l in range(depth):
        wl = Wb[l]
        t = f"l{l}_"
        u = _rms_fwd(h, norm_mix_g[l:l + 1], name=t + "rms_mix")
        z = _mm(u, wl["w_in"], b_stack=True, name=t + "mm_in")
        y_a = _mixer_a_fwd(z, conv_a_full[l], name=t + "mixer_a")
        cb = _mixer_b_fwd(z, conv_b_full[l], conv_b_bias[l:l + 1], name=t + "mixer_b")
        y_b = _ln_silu_fwd(cb, ln_b_g[l:l + 1], ln_b_b[l:l + 1], name=t + "ln_silu")
        yy = jnp.concatenate([y_a, y_b], axis=1)
        h2 = _mm(yy, wl["w_out"], res=h, name=t + "mm_out")
        q_in = _rms_fwd(h2, norm_x_g[l:l + 1], name=t + "rms_x")
        q = _mm(q_in, wl["w_q"], out_dtype=BF16, name=t + "mm_q")
        mn = _rms_fwd(ms, norm_mem_g[l:l + 1], name=t + "rms_mem")
        kv = _mm(mn, wl["w_kv"], b_stack=True, out_dtype=BF16, name=t + "mm_kv")
        o = _attn_fwd(q, kv, name=t + "attn")
        h3 = _mm(o, wl["w_xo"], res=h2, name=t + "mm_xo")
        u3 = _rms_fwd(h3, norm_ffn_g[l:l + 1], name=t + "rms_ffn")
        a_pre, hh = _mm(u3, wl["w_up"], b_stack=True, out_dtype=BF16, epi="sqrelu", name=t + "mm_up")
        h4 = _mm(hh, wl["w_down"], res=h3, name=t + "mm_down")
        saved.append(dict(h=h, u=u, z=z, cb=cb, yy=yy, h2=h2, q_in=q_in, q=q, mn=mn, kv=kv, o=o, h3=h3,
                          u3=u3, a_pre=a_pre, hh=hh))
        h = h4

    loss_vec, dh, dhb, d_final = _loss_head(h, final_g.reshape(1, D), tgt, name="loss_head")
    loss = lax.psum(loss_vec[0, 0], ("x", "y", "c"))

    GW = [dict() for _ in range(depth)]
    GS = [dict() for _ in range(depth)]
    for l in reversed(range(depth)):
        wl, sv = Wb[l], saved[l]
        t = f"l{l}_b_"
        GW[l]["w_down"] = _mm(sv["hh"], dhb, ta=True, name=t + "dw_down")
        da = _mm(dhb, wl["w_down"], tb=True, out_dtype=BF16, epi="dsqrelu", aux=sv["a_pre"], name=t + "d_hidden")
        GW[l]["w_up"] = _mm(sv["u3"], da, ta=True, o_stack=N_CHIPS, name=t + "dw_up")
        du3 = _mm(da, wl["w_up"], tb=True, b_stack=True, name=t + "d_u3")
        dh, dhb, GS[l]["norm_ffn_g"] = _rms_bwd(sv["h3"], norm_ffn_g[l:l + 1], du3, dh, name=t + "rms_ffn")
        GW[l]["w_xo"] = _mm(sv["o"], dhb, ta=True, name=t + "dw_xo")
        d_o = _mm(dhb, wl["w_xo"], tb=True, out_dtype=BF16, name=t + "d_o")
        dq, dkv = _attn_bwd(sv["q"], sv["kv"], d_o, name=t + "attn")
        GW[l]["w_q"] = _mm(sv["q_in"], dq, ta=True, name=t + "dw_q")
        dq_in = _mm(dq, wl["w_q"], tb=True, name=t + "d_q_in")
        dkvb = dkv.astype(BF16)
        GW[l]["w_kv"] = _mm(sv["mn"], dkvb, ta=True, o_stack=N_CHIPS, name=t + "dw_kv")
        dmn = _mm(dkvb, wl["w_kv"], tb=True, b_stack=True, name=t + "d_mem")
        _, _, GS[l]["norm_mem_g"] = _rms_bwd(ms, norm_mem_g[l:l + 1], dmn, None, name=t + "rms_mem")
        dh, dhb, GS[l]["norm_x_g"] = _rms_bwd(sv["h2"], norm_x_g[l:l + 1], dq_in, dh, name=t + "rms_x")
        GW[l]["w_out"] = _mm(sv["yy"], dhb, ta=True, name=t + "dw_out")
        dyy = _mm(dhb, wl["w_out"], tb=True, name=t + "d_y")
        dcb, GS[l]["ln_b_g"], GS[l]["ln_b_b"] = _ln_silu_bwd(sv["cb"], ln_b_g[l:l + 1], ln_b_b[l:l + 1], dyy, 1,
                                                             name=t + "ln_silu")
        db_, dc_, dh_, GS[l]["conv_a_w"] = _mixer_a_bwd(sv["z"], conv_a_full[l], dyy, name=t + "mixer_a")
        dv_, dg_, GS[l]["conv_b_w"], GS[l]["conv_b_bias"] = _mixer_b_bwd(sv["z"], conv_b_full[l], dcb,
                                                                         name=t + "mixer_b")
        dz = jnp.concatenate([db_, dc_, dh_, dv_, dg_], axis=1)
        GW[l]["w_in"] = _mm(sv["u"], dz, ta=True, o_stack=N_CHIPS, name=t + "dw_in")
        du = _mm(dz, wl["w_in"], tb=True, b_stack=True, name=t + "d_u")
        dh, dhb, GS[l]["norm_mix_g"] = _rms_bwd(sv["h"], norm_mix_g[l:l + 1], du, dh, name=t + "rms_mix")
    grad_x = dh[None]

    keys = [(l, n) for l in range(depth) for n, _ in BIG]
    gs = [GW[l][n] if GW[l][n].ndim == 3 else GW[l][n].reshape(N_CHIPS, *W[n].shape[1:]) for l, n in keys]
    from_sibling = _swap_halves(gs, name="grad_swap_sibling")
    pairs = [_pair_sum(g, r, meta, name=f"grad_pair_sum_l{l}_{n}") for g, r, (l, n) in zip(gs, from_sibling, keys)]
    pieces = _scatter_to_chips(pairs, name="grad_scatter_chips")
    halves = [_chip_sum(p, q, meta, name=f"grad_chip_sum_l{l}_{n}") for p, q, (l, n) in zip(pairs, pieces, keys)]
    reduced = dict(zip(keys, _share_halves(halves, name="grad_share_sibling")))

    grads, deltas, new_m, new_v = {}, {}, {}, {}
    for n, _ in BIG:
        grads[n], deltas[n], new_m[n], new_v[n] = _adamw_layers(W[n], reduced[(0, n)], reduced[(1, n)], MO[n], VO[n],
                                                               name="adamw_" + n)

    small = [n for n in names if n not in dict(BIG)]
    full_shapes = {n: ((depth, W[n].shape[1], c_a) if n in ("conv_a_w", "conv_b_w") else W[n].shape)
                   for n in small}

    def small_grad(n):
        if n == "final_g":
            return d_final.reshape(W[n].shape)
        return jnp.stack([GS[l][n].reshape(full_shapes[n][1:]) for l in range(depth)])

    part = _pack([small_grad(n) for n in small], LANES)
    everyone = _gather_all(part, name="gather_small_grads")
    total = _sum_leading(everyone, name="small_grad_sum")
    full_grads = dict(zip(small, _unpack(total, [full_shapes[n] for n in small])))
    for n in ("conv_a_w", "conv_b_w"):
        full_grads[n] = lax.dynamic_slice_in_dim(full_grads[n], chip_idx * c_loc, c_loc, axis=2)
    shapes = [W[n].shape for n in small]
    d_s, m_s, v_s = _adamw(_pack([W[n] for n in small], 128), _pack([full_grads[n] for n in small], 128),
                           _pack([MO[n] for n in small], 128), _pack([VO[n] for n in small], 128),
                           name="adamw_small")
    for n, d, nm, nv in zip(small, _unpack(d_s, shapes), _unpack(m_s, shapes), _unpack(v_s, shapes)):
        grads[n], deltas[n], new_m[n], new_v[n] = full_grads[n], d, nm, nv

    return (loss, grad_x, *[grads[n] for n in names], *[deltas[n] for n in names],
            *[new_m[n] for n in names], *[new_v[n] for n in names])
```

```python
import jax
import jax.numpy as jnp
from jax import lax
from jax.experimental import pallas as pl
from jax.experimental.pallas import tpu as pltpu

F32 = jnp.float32
BF16 = jnp.bfloat16
MESH = pl.DeviceIdType.MESH

EPS = 1e-6
N_XHEADS = 4
K_A = 3
K_B = 31
PAD_A = 8
PAD_B = 32
CONV_CHUNK = 256
ROW_TILE = 512
LANES = 1024
VMEM_LIMIT_BYTES = 48 * 1024 * 1024

ADAM_LR = 0.001
ADAM_B1 = 0.9
ADAM_B2 = 0.999
ADAM_EPS = 1e-08
ADAM_WD = 0.01
ADAM_STEP = 10

BIG = (("w_in", 1), ("w_out", 0), ("w_q", 0), ("w_kv", 1), ("w_xo", 0), ("w_up", 1), ("w_down", 0))
N_CHIPS = 4
N_DEV = 8


def _params(sem=None):
    return pltpu.CompilerParams(dimension_semantics=sem, vmem_limit_bytes=VMEM_LIMIT_BYTES)


def _pick(cands, n):
    for c in cands:
        if c <= n and n % c == 0:
            return c
    return n


def _mm(a, b, *, name, ta=False, tb=False, out_dtype=F32, res=None, epi=None, aux=None,
        b_stack=False, o_stack=0, bm=1024, bn=512, bk=1024):
    if ta:
        K, M = a.shape
    else:
        M, K = a.shape
    if b_stack:
        n_st, d1, d2 = b.shape
        N, kb = (d1, d2) if tb else (n_st * d2, d1)
        assert K == (n_st * d2 if tb else d1), (name, a.shape, b.shape)
    else:
        N = b.shape[0] if tb else b.shape[1]
    n_unit = b.shape[2] if (b_stack and not tb) else (N // o_stack if o_stack else N)
    k_unit = b.shape[2] if (b_stack and tb) else K
    bm = _pick((bm, 512, 256, 128), M)
    bn = _pick((bn, 640, 256, 384, 128), n_unit)
    bk = _pick((bk, 640, 512, 256, 128), k_unit)
    assert M % bm == 0 and N % bn == 0 and K % bk == 0, (name, M, N, K)
    nk = K // bk
    per_n = n_unit // bn
    per_k = k_unit // bk
    a_spec = (pl.BlockSpec((bk, bm), lambda i, j, k: (k, i)) if ta
              else pl.BlockSpec((bm, bk), lambda i, j, k: (i, k)))
    if b_stack and tb:
        b_spec = pl.BlockSpec((None, bn, bk), lambda i, j, k: (k // per_k, j, k % per_k))
    elif b_stack:
        b_spec = pl.BlockSpec((None, bk, bn), lambda i, j, k: (j // per_n, k, j % per_n))
    elif tb:
        b_spec = pl.BlockSpec((bn, bk), lambda i, j, k: (j, k))
    else:
        b_spec = pl.BlockSpec((bk, bn), lambda i, j, k: (k, j))
    o_spec = pl.BlockSpec((bm, bn), lambda i, j, k: (i, j))
    dims = (((0 if ta else 1,), (1 if tb else 0,)), ((), ()))
    ins, in_specs = [a, b], [a_spec, b_spec]
    if res is not None:
        ins.append(res)
        in_specs.append(o_spec)
    if aux is not None:
        ins.append(aux)
        in_specs.append(o_spec)
    n_out = 2 if epi == "sqrelu" else 1
    out_shape = [jax.ShapeDtypeStruct((M, N), out_dtype)] * n_out
    out_specs = [o_spec] * n_out
    if o_stack:
        assert n_out == 1 and res is None and aux is None
        out_shape = [jax.ShapeDtypeStruct((o_stack, M, N // o_stack), out_dtype)]
        out_specs = [pl.BlockSpec((None, bm, bn), lambda i, j, k: (j // per_n, i, j % per_n))]

    def body(*refs):
        a_ref, b_ref = refs[0], refs[1]
        pos = 2
        res_ref = aux_ref = None
        if res is not None:
            res_ref = refs[pos]
            pos += 1
        if aux is not None:
            aux_ref = refs[pos]
            pos += 1
        outs = refs[pos:pos + n_out]

        def product():
            return lax.dot_general(a_ref[...], b_ref[...], dims, preferred_element_type=F32)

        def finish(r):
            if res_ref is not None:
                r = r + res_ref[...]
            if epi == "sqrelu":
                outs[0][...] = r.astype(out_dtype)
                rl = jnp.maximum(r, 0.0)
                outs[1][...] = (rl * rl).astype(out_dtype)
            elif epi == "dsqrelu":
                outs[0][...] = (r * (2.0 * jnp.maximum(aux_ref[...].astype(F32), 0.0))).astype(out_dtype)
            else:
                outs[0][...] = r.astype(out_dtype)

        if nk == 1:
            finish(product())
            return
        acc = refs[pos + n_out]
        k = pl.program_id(2)

        @pl.when(k == 0)
        def _():
            acc[...] = product()

        @pl.when(jnp.logical_and(k > 0, k < nk - 1))
        def _():
            acc[...] += product()

        @pl.when(k == nk - 1)
        def _():
            finish(acc[...] + product())

    out = pl.pallas_call(
        body, name=name, grid=(M // bm, N // bn, nk),
        in_specs=in_specs, out_specs=out_specs, out_shape=out_shape,
        scratch_shapes=[pltpu.VMEM((bm, bn), F32)] if nk > 1 else [],
        compiler_params=_params(("parallel", "parallel", "arbitrary")),
    )(*ins)
    return out if n_out == 2 else out[0]


def _row_tile(rows):
    return min(ROW_TILE, rows)


def _rms_fwd(x, g, *, name, after=None):
    S, D = x.shape
    tr = _row_tile(S)

    def body(x_ref, g_ref, *rest):
        o_ref = rest[-1]
        xv = x_ref[...]
        r = lax.rsqrt(jnp.mean(xv * xv, axis=-1, keepdims=True) + EPS)
        o_ref[...] = (xv * r * g_ref[...]).astype(BF16)

    deps = [] if after is None else [after]
    return pl.pallas_call(
        body, name=name, grid=(S // tr,),
        in_specs=[pl.BlockSpec((tr, D), lambda i: (i, 0)), pl.BlockSpec((1, D), lambda i: (0, 0))]
        + [ANY] * len(deps),
        out_specs=pl.BlockSpec((tr, D), lambda i: (i, 0)),
        out_shape=jax.ShapeDtypeStruct((S, D), BF16),
        compiler_params=_params(("parallel",)),
    )(x, g, *deps)


def _rms_bwd(x, g, du, dres, *, name, after=None):
    S, D = x.shape
    tr = _row_tile(S)
    has_res = dres is not None
    deps = [] if after is None else [after]

    def body(*refs):
        dx_ref, dxb_ref, dg_ref = refs[-3:]
        if has_res:
            x_ref, g_ref, du_ref, dres_ref = refs[:4]
        else:
            x_ref, g_ref, du_ref = refs[:3]
        xv = x_ref[...]
        r = lax.rsqrt(jnp.mean(xv * xv, axis=-1, keepdims=True) + EPS)
        xh = xv * r
        dy = du_ref[...]
        dxh = dy * g_ref[...]
        dx = r * (dxh - xh * jnp.mean(dxh * xh, axis=-1, keepdims=True))
        if has_res:
            dx = dx + dres_ref[...]
        dx_ref[...] = dx
        dxb_ref[...] = dx.astype(BF16)

        @pl.when(pl.program_id(0) == 0)
        def _():
            dg_ref[...] = jnp.zeros_like(dg_ref)

        dg_ref[...] += jnp.sum(dy * xh, axis=0, keepdims=True)

    row = pl.BlockSpec((tr, D), lambda i: (i, 0))
    vec = pl.BlockSpec((1, D), lambda i: (0, 0))
    ins = [x, g, du] + ([dres] if has_res else []) + deps
    in_specs = [row, vec, row] + ([row] if has_res else []) + [ANY] * len(deps)
    return pl.pallas_call(
        body, name=name, grid=(S // tr,),
        in_specs=in_specs, out_specs=[row, row, vec],
        out_shape=[jax.ShapeDtypeStruct((S, D), F32), jax.ShapeDtypeStruct((S, D), BF16),
                   jax.ShapeDtypeStruct((1, D), F32)],
        compiler_params=_params(("arbitrary",)),
    )(*ins)


def _loss_head(h, g, target, *, name):
    S, D = h.shape
    tr = _row_tile(S)

    def body(x_ref, g_ref, t_ref, loss_ref, dx_ref, dxb_ref, dg_ref):
        xv = x_ref[...]
        r = lax.rsqrt(jnp.mean(xv * xv, axis=-1, keepdims=True) + EPS)
        xh = xv * r
        gv = g_ref[...]
        err = xh * gv - t_ref[...]
        part = 0.5 * jnp.sum(jnp.mean(err * err, axis=-1, keepdims=True), axis=0, keepdims=True)
        dy = err * (1.0 / D)
        dxh = dy * gv
        dx = r * (dxh - xh * jnp.mean(dxh * xh, axis=-1, keepdims=True))
        dx_ref[...] = dx
        dxb_ref[...] = dx.astype(BF16)

        @pl.when(pl.program_id(0) == 0)
        def _():
            dg_ref[...] = jnp.zeros_like(dg_ref)
            loss_ref[...] = jnp.zeros_like(loss_ref)

        dg_ref[...] += jnp.sum(dy * xh, axis=0, keepdims=True)
        loss_ref[...] += jnp.broadcast_to(part, loss_ref.shape)

    row = pl.BlockSpec((tr, D), lambda i: (i, 0))
    vec = pl.BlockSpec((1, D), lambda i: (0, 0))
    return pl.pallas_call(
        body, name=name, grid=(S // tr,),
        in_specs=[row, vec, row],
        out_specs=[pl.BlockSpec((1, 128), lambda i: (0, 0)), row, row, vec],
        out_shape=[jax.ShapeDtypeStruct((1, 128), F32), jax.ShapeDtypeStruct((S, D), F32),
                   jax.ShapeDtypeStruct((S, D), BF16), jax.ShapeDtypeStruct((1, D), F32)],
        compiler_params=_params(("arbitrary",)),
    )(h, g, target)


def _sigmoid(x):
    return 1.0 / (1.0 + jnp.exp(-x))


def _ln_silu_fwd(cb, g, b, *, name):
    S, C = cb.shape
    tr = _row_tile(S)

    def body(x_ref, g_ref, b_ref, o_ref):
        xv = x_ref[...]
        mu = jnp.mean(xv, axis=-1, keepdims=True)
        xc = xv - mu
        rs = lax.rsqrt(jnp.mean(xc * xc, axis=-1, keepdims=True) + EPS)
        l = xc * rs * g_ref[...] + b_ref[...]
        o_ref[...] = (l * _sigmoid(l)).astype(BF16)

    row = pl.BlockSpec((tr, C), lambda i: (i, 0))
    vec = pl.BlockSpec((1, C), lambda i: (0, 0))
    return pl.pallas_call(
        body, name=name, grid=(S // tr,), in_specs=[row, vec, vec], out_specs=row,
        out_shape=jax.ShapeDtypeStruct((S, C), BF16), compiler_params=_params(("parallel",)),
    )(cb, g, b)


def _ln_silu_bwd(cb, g, b, dy, col_block, *, name):
    S, C = cb.shape
    tr = _row_tile(S)

    def body(x_ref, g_ref, b_ref, dy_ref, dx_ref, dg_ref, db_ref):
        xv = x_ref[...]
        mu = jnp.mean(xv, axis=-1, keepdims=True)
        xc = xv - mu
        rs = lax.rsqrt(jnp.mean(xc * xc, axis=-1, keepdims=True) + EPS)
        xh = xc * rs
        gv = g_ref[...]
        l = xh * gv + b_ref[...]
        sg = _sigmoid(l)
        dl = dy_ref[...] * (sg + l * sg * (1.0 - sg))
        dxh = dl * gv
        dx_ref[...] = rs * (dxh - jnp.mean(dxh, axis=-1, keepdims=True)
                            - xh * jnp.mean(dxh * xh, axis=-1, keepdims=True))

        @pl.when(pl.program_id(0) == 0)
        def _():
            dg_ref[...] = jnp.zeros_like(dg_ref)
            db_ref[...] = jnp.zeros_like(db_ref)

        dg_ref[...] += jnp.sum(dl * xh, axis=0, keepdims=True)
        db_ref[...] += jnp.sum(dl, axis=0, keepdims=True)

    row = pl.BlockSpec((tr, C), lambda i: (i, 0))
    vec = pl.BlockSpec((1, C), lambda i: (0, 0))
    return pl.pallas_call(
        body, name=name, grid=(S // tr,),
        in_specs=[row, vec, vec, pl.BlockSpec((tr, C), lambda i: (i, col_block))],
        out_specs=[row, vec, vec],
        out_shape=[jax.ShapeDtypeStruct((S, C), F32), jax.ShapeDtypeStruct((1, C), F32),
                   jax.ShapeDtypeStruct((1, C), F32)],
        compiler_params=_params(("arbitrary",)),
    )(cb, g, b, dy)


def _attn_fwd(q, kv, *, name):
    S, D = q.shape
    M = kv.shape[0]
    hd = D // N_XHEADS
    scale = 1.0 / float(hd) ** 0.5
    tq = _row_tile(S)

    def body(q_ref, k_ref, v_ref, o_ref):
        for h in range(N_XHEADS):
            cols = slice(h * hd, (h + 1) * hd)
            s = lax.dot_general(q_ref[:, cols], k_ref[:, cols], (((1,), (1,)), ((), ())),
                                preferred_element_type=F32) * scale
            e = jnp.exp(s - jnp.max(s, axis=-1, keepdims=True))
            p = e / jnp.sum(e, axis=-1, keepdims=True)
            o = jnp.dot(p.astype(BF16), v_ref[:, cols], preferred_element_type=F32)
            o_ref[:, cols] = o.astype(BF16)

    return pl.pallas_call(
        body, name=name, grid=(S // tq,),
        in_specs=[pl.BlockSpec((tq, D), lambda i: (i, 0)), pl.BlockSpec((M, D), lambda i: (0, 0)),
                  pl.BlockSpec((M, D), lambda i: (0, 1))],
        out_specs=pl.BlockSpec((tq, D), lambda i: (i, 0)),
        out_shape=jax.ShapeDtypeStruct((S, D), BF16), compiler_params=_params(("parallel",)),
    )(q, kv, kv)


def _attn_bwd(q, kv, do, *, name):
    S, D = q.shape
    M = kv.shape[0]
    hd = D // N_XHEADS
    scale = 1.0 / float(hd) ** 0.5
    tq = _row_tile(S)

    def body(q_ref, k_ref, v_ref, do_ref, dq_ref, dkv_ref):
        @pl.when(pl.program_id(0) == 0)
        def _():
            dkv_ref[...] = jnp.zeros_like(dkv_ref)

        for h in range(N_XHEADS):
            cols = slice(h * hd, (h + 1) * hd)
            vcols = slice(D + h * hd, D + (h + 1) * hd)
            qh, kh, vh, doh = q_ref[:, cols], k_ref[:, cols], v_ref[:, cols], do_ref[:, cols]
            s = lax.dot_general(qh, kh, (((1,), (1,)), ((), ())), preferred_element_type=F32) * scale
            e = jnp.exp(s - jnp.max(s, axis=-1, keepdims=True))
            p = e / jnp.sum(e, axis=-1, keepdims=True)
            pb = p.astype(BF16)
            dp = lax.dot_general(doh, vh, (((1,), (1,)), ((), ())), preferred_element_type=F32)
            ds = (p * (dp - jnp.sum(dp * p, axis=-1, keepdims=True)) * scale).astype(BF16)
            dq_ref[:, cols] = jnp.dot(ds, kh, preferred_element_type=F32).astype(BF16)
            dkv_ref[:, cols] += lax.dot_general(ds, qh, (((0,), (0,)), ((), ())), preferred_element_type=F32)
            dkv_ref[:, vcols] += lax.dot_general(pb, doh, (((0,), (0,)), ((), ())), preferred_element_type=F32)

    row = pl.BlockSpec((tq, D), lambda i: (i, 0))
    return pl.pallas_call(
        body, name=name, grid=(S // tq,),
        in_specs=[row, pl.BlockSpec((M, D), lambda i: (0, 0)), pl.BlockSpec((M, D), lambda i: (0, 1)), row],
        out_specs=[row, pl.BlockSpec((M, 2 * D), lambda i: (0, 0))],
        out_shape=[jax.ShapeDtypeStruct((S, D), BF16), jax.ShapeDtypeStruct((M, 2 * D), F32)],
        compiler_params=_params(("arbitrary",)),
    )(q, kv, kv, do)


def _delayed(win, j, pad):
    return (win if j == 0 else pltpu.roll(win, j, 0))[pad:, :]


def _advanced(win, j, ch):
    return (win if j == 0 else pltpu.roll(win, win.shape[0] - j, 0))[:ch, :]


def _mixer_a_fwd(z, w, *, name):
    S = z.shape[0]
    C = w.shape[1]
    nb = C // 128
    ch = min(CONV_CHUNK, S)

    def body(b_ref, c_ref, h_ref, w_ref, y_ref, xp):
        xp[0:PAD_A, :] = jnp.zeros((PAD_A, 128), F32)
        xp[PAD_A:, :] = c_ref[...] * h_ref[...]

        def chunk(i, carry):
            base = pl.multiple_of(i * ch, ch)
            win = xp[pl.ds(base, ch + PAD_A), :]
            acc = _delayed(win, 0, PAD_A) * w_ref[K_A - 1:K_A, :]
            for j in range(1, K_A):
                acc = acc + _delayed(win, j, PAD_A) * w_ref[K_A - 1 - j:K_A - j, :]
            y_ref[pl.ds(base, ch), :] = (b_ref[pl.ds(base, ch), :] * acc).astype(BF16)
            return carry

        lax.fori_loop(0, S // ch, chunk, 0)

    def col(g):
        return pl.BlockSpec((S, 128), lambda j: (0, g * nb + j))

    return pl.pallas_call(
        body, name=name, grid=(nb,),
        in_specs=[col(0), col(1), col(2), pl.BlockSpec((K_A, 128), lambda j: (0, j))],
        out_specs=pl.BlockSpec((S, 128), lambda j: (0, j)),
        out_shape=jax.ShapeDtypeStruct((S, C), BF16),
        scratch_shapes=[pltpu.VMEM((PAD_A + S, 128), F32)],
        compiler_params=_params(("parallel",)),
    )(z, z, z, w)


def _mixer_a_bwd(z, w, dy, *, name):
    S = z.shape[0]
    C = w.shape[1]
    nb = C // 128
    ch = min(CONV_CHUNK, S)

    def body(b_ref, c_ref, h_ref, w_ref, dy_ref, db_ref, dc_ref, dh_ref, dw_ref, xp, dp):
        xp[0:PAD_A, :] = jnp.zeros((PAD_A, 128), F32)
        xp[PAD_A:, :] = c_ref[...] * h_ref[...]
        dp[S:, :] = jnp.zeros((PAD_A, 128), F32)
        dw_ref[...] = jnp.zeros_like(dw_ref)

        def chunk(i, carry):
            base = pl.multiple_of(i * ch, ch)
            win = xp[pl.ds(base, ch + PAD_A), :]
            dya = dy_ref[pl.ds(base, ch), :]
            dcv = dya * b_ref[pl.ds(base, ch), :]
            dp[pl.ds(base, ch), :] = dcv
            acc = None
            for j in range(K_A):
                xs = _delayed(win, j, PAD_A)
                k = K_A - 1 - j
                term = xs * w_ref[k:k + 1, :]
                acc = term if acc is None else acc + term
                dw_ref[k:k + 1, :] += jnp.sum(dcv * xs, axis=0, keepdims=True)
            db_ref[pl.ds(base, ch), :] = (dya * acc).astype(BF16)
            return carry

        lax.fori_loop(0, S // ch, chunk, 0)

        def chunk2(i, carry):
            base = pl.multiple_of(i * ch, ch)
            win = dp[pl.ds(base, ch + PAD_A), :]
            acc = None
            for j in range(K_A):
                term = _advanced(win, j, ch) * w_ref[K_A - 1 - j:K_A - j, :]
                acc = term if acc is None else acc + term
            dc_ref[pl.ds(base, ch), :] = (acc * h_ref[pl.ds(base, ch), :]).astype(BF16)
            dh_ref[pl.ds(base, ch), :] = (acc * c_ref[pl.ds(base, ch), :]).astype(BF16)
            return carry

        lax.fori_loop(0, S // ch, chunk2, 0)

    def col(g):
        return pl.BlockSpec((S, 128), lambda j: (0, g * nb + j))

    out_col = pl.BlockSpec((S, 128), lambda j: (0, j))
    wspec = pl.BlockSpec((K_A, 128), lambda j: (0, j))
    return pl.pallas_call(
        body, name=name, grid=(nb,),
        in_specs=[col(0), col(1), col(2), wspec, out_col],
        out_specs=[out_col, out_col, out_col, wspec],
        out_shape=[jax.ShapeDtypeStruct((S, C), BF16)] * 3 + [jax.ShapeDtypeStruct((K_A, C), F32)],
        scratch_shapes=[pltpu.VMEM((PAD_A + S, 128), F32), pltpu.VMEM((S + PAD_A, 128), F32)],
        compiler_params=_params(("parallel",)),
    )(z, z, z, w, dy)


def _mixer_b_fwd(z, w, bias, *, name):
    S = z.shape[0]
    C = w.shape[1]
    nb = C // 128
    ch = min(CONV_CHUNK, S)

    def body(v_ref, g_ref, w_ref, bias_ref, cb_ref, xp):
        xp[0:PAD_B, :] = jnp.zeros((PAD_B, 128), F32)
        xp[PAD_B:, :] = v_ref[...] * _sigmoid(g_ref[...])

        def chunk(i, carry):
            base = pl.multiple_of(i * ch, ch)
            win = xp[pl.ds(base, ch + PAD_B), :]
            acc = None
            for j in range(K_B):
                term = _delayed(win, j, PAD_B) * w_ref[K_B - 1 - j:K_B - j, :]
                acc = term if acc is None else acc + term
            cb_ref[pl.ds(base, ch), :] = acc + bias_ref[...]
            return carry

        lax.fori_loop(0, S // ch, chunk, 0)

    def col(g):
        return pl.BlockSpec((S, 128), lambda j: (0, g * nb + j))

    return pl.pallas_call(
        body, name=name, grid=(nb,),
        in_specs=[col(3), col(4), pl.BlockSpec((K_B, 128), lambda j: (0, j)),
                  pl.BlockSpec((1, 128), lambda j: (0, j))],
        out_specs=pl.BlockSpec((S, 128), lambda j: (0, j)),
        out_shape=jax.ShapeDtypeStruct((S, C), F32),
        scratch_shapes=[pltpu.VMEM((PAD_B + S, 128), F32)],
        compiler_params=_params(("parallel",)),
    )(z, z, w, bias)


def _mixer_b_bwd(z, w, dcb, *, name):
    S = z.shape[0]
    C = w.shape[1]
    nb = C // 128
    ch = min(CONV_CHUNK, S)

    def body(v_ref, g_ref, w_ref, dcb_ref, dv_ref, dg_ref, dw_ref, dbias_ref, xp, dp):
        xp[0:PAD_B, :] = jnp.zeros((PAD_B, 128), F32)
        xp[PAD_B:, :] = v_ref[...] * _sigmoid(g_ref[...])
        dp[0:S, :] = dcb_ref[...]
        dp[S:, :] = jnp.zeros((PAD_B, 128), F32)
        dw_ref[...] = jnp.zeros_like(dw_ref)
        dbias_ref[...] = jnp.sum(dcb_ref[...], axis=0, keepdims=True)

        def chunk(i, carry):
            base = pl.multiple_of(i * ch, ch)
            win = xp[pl.ds(base, ch + PAD_B), :]
            d = dcb_ref[pl.ds(base, ch), :]
            for j in range(K_B):
                k = K_B - 1 - j
                dw_ref[k:k + 1, :] += jnp.sum(d * _delayed(win, j, PAD_B), axis=0, keepdims=True)
            return carry

        lax.fori_loop(0, S // ch, chunk, 0)

        def chunk2(i, carry):
            base = pl.multiple_of(i * ch, ch)
            win = dp[pl.ds(base, ch + PAD_B), :]
            acc = None
            for j in range(K_B):
                term = _advanced(win, j, ch) * w_ref[K_B - 1 - j:K_B - j, :]
                acc = term if acc is None else acc + term
            sg = _sigmoid(g_ref[pl.ds(base, ch), :])
            vv = v_ref[pl.ds(base, ch), :]
            dv_ref[pl.ds(base, ch), :] = (acc * sg).astype(BF16)
            dg_ref[pl.ds(base, ch), :] = (acc * vv * sg * (1.0 - sg)).astype(BF16)
            return carry

        lax.fori_loop(0, S // ch, chunk2, 0)

    def col(g):
        return pl.BlockSpec((S, 128), lambda j: (0, g * nb + j))

    out_col = pl.BlockSpec((S, 128), lambda j: (0, j))
    wspec = pl.BlockSpec((K_B, 128), lambda j: (0, j))
    bspec = pl.BlockSpec((1, 128), lambda j: (0, j))
    return pl.pallas_call(
        body, name=name, grid=(nb,),
        in_specs=[col(3), col(4), wspec, out_col],
        out_specs=[out_col, out_col, wspec, bspec],
        out_shape=[jax.ShapeDtypeStruct((S, C), BF16)] * 2
        + [jax.ShapeDtypeStruct((K_B, C), F32), jax.ShapeDtypeStruct((1, C), F32)],
        scratch_shapes=[pltpu.VMEM((PAD_B + S, 128), F32), pltpu.VMEM((S + PAD_B, 128), F32)],
        compiler_params=_params(("parallel",)),
    )(z, z, w, dcb)


def _ew_tile(R):
    for t in (512, 256, 128, 64, 32, 16, 8):
        if R % t == 0:
            return t
    return R


def _pair_sum(g, r, meta, *, name):
    n, a, b = g.shape
    ah = a // 2
    tr = _pick((256, 128, 64, 32, 16), ah)
    nh = ah // tr

    def body(meta_ref, g_ref, r_ref, o_ref):
        o_ref[...] = (g_ref[...] + r_ref[...]).astype(BF16)

    half = pl.BlockSpec((None, tr, b), lambda j, i, meta_ref: (j, i, 0))
    return pl.pallas_call(
        body, name=name,
        grid_spec=pltpu.PrefetchScalarGridSpec(
            num_scalar_prefetch=1, grid=(n, nh),
            in_specs=[pl.BlockSpec((None, tr, b), lambda j, i, meta_ref: (j, meta_ref[0] * nh + i, 0)), half],
            out_specs=half),
        out_shape=jax.ShapeDtypeStruct((n, ah, b), BF16), compiler_params=_params(("parallel", "parallel")),
    )(meta, g, r)


def _chip_sum(p, q, meta, *, name):
    n, ah, b = p.shape
    tr = _pick((256, 128, 64, 32, 16), ah)
    nh = ah // tr

    def body(meta_ref, p_ref, q1_ref, q2_ref, q3_ref, o_ref):
        o_ref[...] = ((p_ref[...].astype(F32) + q1_ref[...].astype(F32)) + q2_ref[...].astype(F32)
                      ) + q3_ref[...].astype(F32)

    def piece(mask):
        return pl.BlockSpec((None, tr, b), lambda i, meta_ref: (meta_ref[1] ^ mask, i, 0))

    return pl.pallas_call(
        body, name=name,
        grid_spec=pltpu.PrefetchScalarGridSpec(
            num_scalar_prefetch=1, grid=(nh,),
            in_specs=[piece(0), piece(1), piece(2), piece(3)],
            out_specs=pl.BlockSpec((tr, b), lambda i, meta_ref: (meta_ref[0] * nh + i, 0))),
        out_shape=jax.ShapeDtypeStruct((2 * ah, b), F32), compiler_params=_params(("parallel",)),
    )(meta, p, q, q, q)


def _sum_leading(x, *, name):
    n, R, C = x.shape
    tr = _ew_tile(R)

    def body(x_ref, o_ref):
        acc = x_ref[0].astype(F32)
        for k in range(1, n):
            acc = acc + x_ref[k].astype(F32)
        o_ref[...] = acc

    return pl.pallas_call(
        body, name=name, grid=(R // tr,),
        in_specs=[pl.BlockSpec((n, tr, C), lambda i: (0, i, 0))],
        out_specs=pl.BlockSpec((tr, C), lambda i: (i, 0)),
        out_shape=jax.ShapeDtypeStruct((R, C), F32), compiler_params=_params(("parallel",)),
    )(x)


def _adamw(w, g, m, v, *, name):
    R, C = w.shape
    tr = _ew_tile(R)

    def body(w_ref, g_ref, m_ref, v_ref, d_ref, nm_ref, nv_ref):
        gv = g_ref[...]
        nm = ADAM_B1 * m_ref[...] + (1.0 - ADAM_B1) * gv
        nv = ADAM_B2 * v_ref[...] + (1.0 - ADAM_B2) * (gv * gv)
        m_hat = nm / (1.0 - ADAM_B1 ** ADAM_STEP)
        v_hat = nv / (1.0 - ADAM_B2 ** ADAM_STEP)
        d_ref[...] = -ADAM_LR * (m_hat / (jnp.sqrt(v_hat) + ADAM_EPS) + ADAM_WD * w_ref[...])
        nm_ref[...] = nm
        nv_ref[...] = nv

    row = pl.BlockSpec((tr, C), lambda i: (i, 0))
    return pl.pallas_call(
        body, name=name, grid=(R // tr,), in_specs=[row] * 4, out_specs=[row] * 3,
        out_shape=[jax.ShapeDtypeStruct((R, C), F32)] * 3, compiler_params=_params(("parallel",)),
    )(w, g, m, v)


def _adamw_layers(w, g0, g1, m, v, *, name):
    _, a, b = w.shape
    tr = _pick((256, 128, 64, 32, 16, 8), a)

    def body(w_ref, g0_ref, g1_ref, m_ref, v_ref, g_ref, d_ref, nm_ref, nv_ref):
        gv = jnp.where(pl.program_id(1) == 0, g0_ref[...], g1_ref[...])
        nm = ADAM_B1 * m_ref[...] + (1.0 - ADAM_B1) * gv
        nv = ADAM_B2 * v_ref[...] + (1.0 - ADAM_B2) * (gv * gv)
        m_hat = nm / (1.0 - ADAM_B1 ** ADAM_STEP)
        v_hat = nv / (1.0 - ADAM_B2 ** ADAM_STEP)
        g_ref[...] = gv
        d_ref[...] = -ADAM_LR * (m_hat / (jnp.sqrt(v_hat) + ADAM_EPS) + ADAM_WD * w_ref[...])
        nm_ref[...] = nm
        nv_ref[...] = nv

    lay = pl.BlockSpec((None, tr, b), lambda i, l: (l, i, 0))
    row = pl.BlockSpec((tr, b), lambda i, l: (i, 0))
    return pl.pallas_call(
        body, name=name, grid=(a // tr, 2), in_specs=[lay, row, row, lay, lay], out_specs=[lay] * 4,
        out_shape=[jax.ShapeDtypeStruct(w.shape, F32)] * 4, compiler_params=_params(("parallel", "arbitrary")),
    )(w, g0, g1, m, v)


ANY = pl.BlockSpec(memory_space=pl.ANY)


def _place():
    x, y, c = lax.axis_index("x"), lax.axis_index("y"), lax.axis_index("c")
    return x, y, c, 2 * x + y


def _other_chip(x, y, mask):
    px = 1 - x if mask & 2 else x
    py = 1 - y if mask & 1 else y
    return px, py, 2 * px + py


MASKS = (1, 2, 3)


def _half(ref, c, lead=()):
    ah = ref.shape[-2] // 2
    return ref.at[(*lead, pl.ds(c * ah, ah), slice(None))]


def _gather_weights(owns, *, name):
    n = len(owns)

    def body(*refs):
        own, out = refs[:n], refs[n:2 * n]
        send_sems, recv_sems = refs[2 * n], refs[2 * n + 1]
        x, y, c, chip = _place()
        sends = []
        for i in range(n):
            for k, mask in enumerate(MASKS):
                px, py, _ = _other_chip(x, y, mask)
                cp = pltpu.make_async_remote_copy(_half(own[i], c), _half(out[i], c, (chip,)),
                                                  send_sems.at[i, k], recv_sems.at[i, k],
                                                  device_id=(px, py, c), device_id_type=MESH)
                cp.start()
                sends.append(cp)
        for i in range(n):
            for k, mask in enumerate(MASKS):
                px, py, pchip = _other_chip(x, y, mask)
                got = _half(out[i], c, (pchip,))
                pltpu.make_async_remote_copy(got, got, send_sems.at[i, k], recv_sems.at[i, k],
                                             device_id=(px, py, c), device_id_type=MESH).wait_recv()
                cp = pltpu.make_async_remote_copy(got, got, send_sems.at[i, 3 + k], recv_sems.at[i, 3 + k],
                                                  device_id=(x, y, 1 - c), device_id_type=MESH)
                cp.start()
                sends.append(cp)
        for i in range(n):
            for k, mask in enumerate(MASKS):
                _, _, pchip = _other_chip(x, y, mask)
                theirs = _half(out[i], 1 - c, (pchip,))
                pltpu.make_async_remote_copy(theirs, theirs, send_sems.at[i, 3 + k], recv_sems.at[i, 3 + k],
                                             device_id=(x, y, 1 - c), device_id_type=MESH).wait_recv()
        for cp in sends:
            cp.wait_send()

    return pl.pallas_call(
        body, name=name, in_specs=[ANY] * n, out_specs=[ANY] * n,
        out_shape=[jax.ShapeDtypeStruct((N_CHIPS, *o.shape), o.dtype) for o in owns],
        scratch_shapes=[pltpu.SemaphoreType.DMA((n, 6)), pltpu.SemaphoreType.DMA((n, 6))],
    )(*owns)


def _swap_halves(gs, *, name):
    n = len(gs)

    def body(*refs):
        g, out = refs[:n], refs[n:2 * n]
        send_sems, recv_sems = refs[2 * n], refs[2 * n + 1]
        x, y, c, _ = _place()
        cps = []
        for i in range(n):
            ah = g[i].shape[1] // 2
            cp = pltpu.make_async_remote_copy(g[i].at[:, pl.ds((1 - c) * ah, ah), :], out[i],
                                              send_sems.at[i], recv_sems.at[i],
                                              device_id=(x, y, 1 - c), device_id_type=MESH)
            cp.start()
            cps.append(cp)
        for cp in cps:
            cp.wait()

    return pl.pallas_call(
        body, name=name, in_specs=[ANY] * n, out_specs=[ANY] * n,
        out_shape=[jax.ShapeDtypeStruct((g.shape[0], g.shape[1] // 2, g.shape[2]), g.dtype) for g in gs],
        scratch_shapes=[pltpu.SemaphoreType.DMA((n,)), pltpu.SemaphoreType.DMA((n,))],
    )(*gs)


def _scatter_to_chips(ps, *, name):
    n = len(ps)

    def body(*refs):
        p, out = refs[:n], refs[n:2 * n]
        send_sems, recv_sems = refs[2 * n], refs[2 * n + 1]
        x, y, c, chip = _place()
        sends = []
        for i in range(n):
            for k, mask in enumerate(MASKS):
                px, py, pchip = _other_chip(x, y, mask)
                cp = pltpu.make_async_remote_copy(p[i].at[pchip], out[i].at[chip], send_sems.at[i, k],
                                                  recv_sems.at[i, k], device_id=(px, py, c), device_id_type=MESH)
                cp.start()
                sends.append(cp)
        for i in range(n):
            for k, mask in enumerate(MASKS):
                px, py, pchip = _other_chip(x, y, mask)
                pltpu.make_async_remote_copy(p[i].at[pchip], out[i].at[pchip], send_sems.at[i, k],
                                             recv_sems.at[i, k], device_id=(px, py, c),
                                             device_id_type=MESH).wait_recv()
        for cp in sends:
            cp.wait_send()

    return pl.pallas_call(
        body, name=name, in_specs=[ANY] * n, out_specs=[ANY] * n,
        out_shape=[jax.ShapeDtypeStruct(p.shape, p.dtype) for p in ps],
        scratch_shapes=[pltpu.SemaphoreType.DMA((n, 3)), pltpu.SemaphoreType.DMA((n, 3))],
    )(*ps)


HBM_SPEC = pl.BlockSpec(memory_space=pltpu.HBM)
SEM_SPEC = pl.BlockSpec(memory_space=pltpu.SEMAPHORE)
EFFECT = pltpu.SideEffectType.DATAFLOW_SIDE_EFFECTING


def _ici_ends(src, land, gather, x, y, c, chip, mask):
    px, py, pchip = _other_chip(x, y, mask)
    if gather:
        return _half(src, c), _half(land, c, (chip,)), _half(land, c, (pchip,)), (px, py, c)
    return src.at[pchip], land.at[chip], land.at[pchip], (px, py, c)


def _ici_start(srcs, land_shapes, gather, *, name):
    n = len(srcs)

    def body(*refs):
        src, land = refs[:n], refs[n:2 * n]
        send_sems, recv_sems = refs[2 * n], refs[2 * n + 1]
        token = refs[-1]
        x, y, c, chip = _place()
        for i in range(n):
            for k, mask in enumerate(MASKS):
                s, d, _, peer = _ici_ends(src[i], land[i], gather, x, y, c, chip, mask)
                pltpu.make_async_remote_copy(s, d, send_sems.at[3 * i + k], recv_sems.at[3 * i + k],
                                             device_id=peer, device_id_type=MESH).start()
        token[...] = jnp.zeros_like(token)

    lands = [pltpu.with_memory_space_constraint(lax.empty(s.shape, s.dtype), pltpu.HBM) for s in land_shapes]
    srcs = [pltpu.with_memory_space_constraint(s, pltpu.HBM) for s in srcs]
    out = pl.pallas_call(
        body, name=name,
        out_shape=(pltpu.SemaphoreType.DMA((3 * n,)), pltpu.SemaphoreType.DMA((3 * n,)),
                   *[pltpu.HBM(s.shape, s.dtype) for s in srcs], *[pltpu.HBM(s.shape, s.dtype) for s in lands],
                   jax.ShapeDtypeStruct((8, 128), F32)),
        in_specs=[HBM_SPEC] * (2 * n),
        out_specs=(SEM_SPEC, SEM_SPEC, *[HBM_SPEC] * (2 * n), pl.BlockSpec(memory_space=pltpu.VMEM)),
        input_output_aliases={i: 2 + i for i in range(2 * n)},
        compiler_params=pltpu.CompilerParams(has_side_effects=EFFECT),
    )(*srcs, *lands)
    return out[0], out[1], list(out[2:2 + n]), list(out[2 + n:2 + 2 * n]), out[-1]


def _ici_wait(send_sems, recv_sems, srcs, lands, gather, after, *, name):
    n = len(srcs)

    def body(*refs):
        src, land = refs[:n], refs[n:2 * n]
        send_sems, recv_sems = refs[2 * n], refs[2 * n + 1]
        x, y, c, chip = _place()
        for i in range(n):
            for k, mask in enumerate(MASKS):
                s, d, got, peer = _ici_ends(src[i], land[i], gather, x, y, c, chip, mask)
                pltpu.make_async_remote_copy(s, d, send_sems.at[3 * i + k], recv_sems.at[3 * i + k],
                                             device_id=peer, device_id_type=MESH).wait_send()
                pltpu.make_async_remote_copy(s, got, send_sems.at[3 * i + k], recv_sems.at[3 * i + k],
                                             device_id=peer, device_id_type=MESH).wait_recv()

    out = pl.pallas_call(
        body, name=name,
        out_shape=tuple(pltpu.HBM(s.shape, s.dtype) for s in (*srcs, *lands)),
        in_specs=[HBM_SPEC] * (2 * n) + [SEM_SPEC, SEM_SPEC, ANY],
        out_specs=tuple([HBM_SPEC] * (2 * n)),
        input_output_aliases={i: i for i in range(2 * n)},
        compiler_params=pltpu.CompilerParams(has_side_effects=EFFECT),
    )(*srcs, *lands, send_sems, recv_sems, after)
    return list(out[:n]), list(out[n:])


def _relay_halves(gs, *, name):
    n = len(gs)

    def body(*refs):
        out = refs[n:2 * n]
        send_sems, recv_sems = refs[2 * n], refs[2 * n + 1]
        x, y, c, _ = _place()
        cps = []
        for i in range(n):
            for k, mask in enumerate(MASKS):
                _, _, pchip = _other_chip(x, y, mask)
                got = _half(out[i], c, (pchip,))
                cp = pltpu.make_async_remote_copy(got, got, send_sems.at[i, k], recv_sems.at[i, k],
                                                  device_id=(x, y, 1 - c), device_id_type=MESH)
                cp.start()
                cps.append(cp)
        for i in range(n):
            for k, mask in enumerate(MASKS):
                _, _, pchip = _other_chip(x, y, mask)
                theirs = _half(out[i], 1 - c, (pchip,))
                pltpu.make_async_remote_copy(theirs, theirs, send_sems.at[i, k], recv_sems.at[i, k],
                                             device_id=(x, y, 1 - c), device_id_type=MESH).wait_recv()
        for cp in cps:
            cp.wait_send()

    return pl.pallas_call(
        body, name=name, in_specs=[ANY] * n, out_specs=[ANY] * n,
        out_shape=[jax.ShapeDtypeStruct(g.shape, g.dtype) for g in gs],
        input_output_aliases={i: i for i in range(n)},
        scratch_shapes=[pltpu.SemaphoreType.DMA((n, 3)), pltpu.SemaphoreType.DMA((n, 3))],
    )(*gs)


def _share_halves(gs, *, name):
    n = len(gs)

    def body(*refs):
        out = refs[n:2 * n]
        send_sems, recv_sems = refs[2 * n], refs[2 * n + 1]
        x, y, c, _ = _place()
        cps = []
        for i in range(n):
            cp = pltpu.make_async_remote_copy(_half(out[i], c), _half(out[i], c), send_sems.at[i], recv_sems.at[i],
                                              device_id=(x, y, 1 - c), device_id_type=MESH)
            cp.start()
            cps.append(cp)
        for i in range(n):
            theirs = _half(out[i], 1 - c)
            pltpu.make_async_remote_copy(theirs, theirs, send_sems.at[i], recv_sems.at[i],
                                         device_id=(x, y, 1 - c), device_id_type=MESH).wait_recv()
        for cp in cps:
            cp.wait_send()

    return pl.pallas_call(
        body, name=name, in_specs=[ANY] * n, out_specs=[ANY] * n,
        out_shape=[jax.ShapeDtypeStruct(g.shape, g.dtype) for g in gs],
        input_output_aliases={i: i for i in range(n)},
        scratch_shapes=[pltpu.SemaphoreType.DMA((n,)), pltpu.SemaphoreType.DMA((n,))],
    )(*gs)


def _gather_all(buf, *, name):
    r, L = buf.shape
    vmem = pl.BlockSpec(memory_space=pltpu.VMEM)
    masks = tuple(range(1, N_DEV))

    def body(buf_ref, out_ref, send_sems, recv_sems):
        x, y, c, _ = _place()
        me = 4 * x + 2 * y + c
        out_ref[me] = buf_ref[...]
        sends = []
        for k, mask in enumerate(masks):
            px = 1 - x if mask & 4 else x
            py = 1 - y if mask & 2 else y
            pc = 1 - c if mask & 1 else c
            cp = pltpu.make_async_remote_copy(buf_ref, out_ref.at[me], send_sems.at[k], recv_sems.at[k],
                                              device_id=(px, py, pc), device_id_type=MESH)
            cp.start()
            sends.append(cp)
        for k, mask in enumerate(masks):
            px = 1 - x if mask & 4 else x
            py = 1 - y if mask & 2 else y
            pc = 1 - c if mask & 1 else c
            pltpu.make_async_remote_copy(buf_ref, out_ref.at[4 * px + 2 * py + pc], send_sems.at[k],
                                         recv_sems.at[k], device_id=(px, py, pc), device_id_type=MESH).wait_recv()
        for cp in sends:
            cp.wait_send()

    return pl.pallas_call(
        body, name=name, in_specs=[vmem], out_specs=vmem,
        out_shape=jax.ShapeDtypeStruct((N_DEV, r, L), buf.dtype),
        scratch_shapes=[pltpu.SemaphoreType.DMA((N_DEV - 1,)), pltpu.SemaphoreType.DMA((N_DEV - 1,))],
    )(buf)


def _pack(arrs, lanes, row_mult=8):
    flat = jnp.concatenate([a.reshape(-1) for a in arrs])
    rows = -(-flat.shape[0] // lanes)
    rows = -(-rows // row_mult) * row_mult
    flat = jnp.pad(flat, (0, rows * lanes - flat.shape[0]))
    return flat.reshape(rows, lanes)


def _unpack(buf, shapes):
    flat = buf.reshape(-1)
    out, pos = [], 0
    for s in shapes:
        n = 1
        for d in s:
            n *= d
        out.append(flat[pos:pos + n].reshape(s))
        pos += n
    return out


def kernel(x, mem, norm_mix_g, w_in, conv_a_w, conv_b_w, conv_b_bias, ln_b_g, ln_b_b, w_out, norm_x_g, norm_mem_g, w_q, w_kv, w_xo, norm_ffn_g, w_up, w_down, final_g, loss_target, m_norm_mix_g, m_w_in, m_conv_a_w, m_conv_b_w, m_conv_b_bias, m_ln_b_g, m_ln_b_b, m_w_out, m_norm_x_g, m_norm_mem_g, m_w_q, m_w_kv, m_w_xo, m_norm_ffn_g, m_w_up, m_w_down, m_final_g, v_norm_mix_g, v_w_in, v_conv_a_w, v_conv_b_w, v_conv_b_bias, v_ln_b_g, v_ln_b_b, v_w_out, v_norm_x_g, v_norm_mem_g, v_w_q, v_w_kv, v_w_xo, v_norm_ffn_g, v_w_up, v_w_down, v_final_g):
    W = dict(norm_mix_g=norm_mix_g, w_in=w_in, conv_a_w=conv_a_w, conv_b_w=conv_b_w, conv_b_bias=conv_b_bias,
             ln_b_g=ln_b_g, ln_b_b=ln_b_b, w_out=w_out, norm_x_g=norm_x_g, norm_mem_g=norm_mem_g, w_q=w_q,
             w_kv=w_kv, w_xo=w_xo, norm_ffn_g=norm_ffn_g, w_up=w_up, w_down=w_down, final_g=final_g)
    MO = dict(norm_mix_g=m_norm_mix_g, w_in=m_w_in, conv_a_w=m_conv_a_w, conv_b_w=m_conv_b_w,
              conv_b_bias=m_conv_b_bias, ln_b_g=m_ln_b_g, ln_b_b=m_ln_b_b, w_out=m_w_out, norm_x_g=m_norm_x_g,
              norm_mem_g=m_norm_mem_g, w_q=m_w_q, w_kv=m_w_kv, w_xo=m_w_xo, norm_ffn_g=m_norm_ffn_g,
              w_up=m_w_up, w_down=m_w_down, final_g=m_final_g)
    VO = dict(norm_mix_g=v_norm_mix_g, w_in=v_w_in, conv_a_w=v_conv_a_w, conv_b_w=v_conv_b_w,
              conv_b_bias=v_conv_b_bias, ln_b_g=v_ln_b_g, ln_b_b=v_ln_b_b, w_out=v_w_out, norm_x_g=v_norm_x_g,
              norm_mem_g=v_norm_mem_g, w_q=v_w_q, w_kv=v_w_kv, w_xo=v_w_xo, norm_ffn_g=v_norm_ffn_g,
              w_up=v_w_up, w_down=v_w_down, final_g=v_final_g)
    names = list(W.keys())
    depth = norm_mix_g.shape[0]
    assert depth == 2, "the exchange splits the weights into one layer per core of a chip"
    c_idx = lax.axis_index("c")
    chip_idx = 2 * lax.axis_index("x") + lax.axis_index("y")

    xs = x[0]
    ms = mem[0]
    tgt = loss_target[0]
    S, D = xs.shape
    c_a = conv_a_w.shape[-1] * N_CHIPS
    c_loc = conv_a_w.shape[-1]

    meta = jnp.stack([c_idx, chip_idx]).astype(jnp.int32)
    owns = [W[n][l].astype(BF16) for l in range(depth) for n, _ in BIG]
    nw = len(BIG)

    def layer_weights(gathered, own):
        wl = {}
        for (n, ax), g, o in zip(BIG, gathered, own):
            g = lax.dynamic_update_slice(g, o[None], (chip_idx, 0, 0))
            wl[n] = g.reshape(-1, g.shape[-1]) if ax == 0 else g
        return wl

    conv_local = _pack([conv_a_w, conv_b_w], 128)
    conv_all = _gather_all(conv_local, name="gather_conv_weights")
    Wb = [layer_weights(_gather_weights(owns[:nw], name="gather_weights_l0"), owns[:nw]), None]
    w1_sems = _ici_start(owns[nw:], [jax.ShapeDtypeStruct((N_CHIPS, *o.shape), o.dtype) for o in owns[nw:]], True,
                         name="gather_weights_l1_start")
    na = depth * K_A * c_loc
    nbw = depth * K_B * c_loc
    ca_parts, cb_parts = [], []
    for j in range(N_CHIPS):
        fl = conv_all[2 * j].reshape(-1)
        ca_parts.append(fl[:na].reshape(depth, K_A, c_loc))
        cb_parts.append(fl[na:na + nbw].reshape(depth, K_B, c_loc))
    conv_a_full = jnp.concatenate(ca_parts, axis=-1)
    conv_b_full = jnp.concatenate(cb_parts, axis=-1)

    saved = []
    h = xs
    for l in range(depth):
        if l == 1:
            send_sems, recv_sems, srcs, lands, _ = w1_sems
            srcs, lands = _ici_wait(send_sems, recv_sems, srcs, lands, True, h, name="gather_weights_l1_wait")
            Wb[1] = layer_weights(_relay_halves(lands, name="gather_weights_l1_relay"), srcs)
        wl = Wb[l]
        t = f"l{l}_"
        u = _rms_fwd(h, norm_mix_g[l:l + 1], name=t + "rms_mix", after=w1_sems[4] if l == 0 else None)
        z = _mm(u, wl["w_in"], b_stack=True, name=t + "mm_in")
        y_a = _mixer_a_fwd(z, conv_a_full[l], name=t + "mixer_a")
        cb = _mixer_b_fwd(z, conv_b_full[l], conv_b_bias[l:l + 1], name=t + "mixer_b")
        y_b = _ln_silu_fwd(cb, ln_b_g[l:l + 1], ln_b_b[l:l + 1], name=t + "ln_silu")
        yy = jnp.concatenate([y_a, y_b], axis=1)
        h2 = _mm(yy, wl["w_out"], res=h, name=t + "mm_out")
        q_in = _rms_fwd(h2, norm_x_g[l:l + 1], name=t + "rms_x")
        q = _mm(q_in, wl["w_q"], out_dtype=BF16, name=t + "mm_q")
        mn = _rms_fwd(ms, norm_mem_g[l:l + 1], name=t + "rms_mem")
        kv = _mm(mn, wl["w_kv"], b_stack=True, out_dtype=BF16, name=t + "mm_kv")
        o = _attn_fwd(q, kv, name=t + "attn")
        h3 = _mm(o, wl["w_xo"], res=h2, name=t + "mm_xo")
        u3 = _rms_fwd(h3, norm_ffn_g[l:l + 1], name=t + "rms_ffn")
        a_pre, hh = _mm(u3, wl["w_up"], b_stack=True, out_dtype=BF16, epi="sqrelu", name=t + "mm_up")
        h4 = _mm(hh, wl["w_down"], res=h3, name=t + "mm_down", **(dict(bn=1024) if l == 1 else {}))
        saved.append(dict(h=h, u=u, z=z, cb=cb, yy=yy, h2=h2, q_in=q_in, q=q, mn=mn, kv=kv, o=o, h3=h3,
                          u3=u3, a_pre=a_pre, hh=hh))
        h = h4

    loss_vec, dh, dhb, d_final = _loss_head(h, final_g.reshape(1, D), tgt, name="loss_head")
    loss = lax.psum(loss_vec[0, 0], ("x", "y", "c"))

    GW = [dict() for _ in range(depth)]
    GS = [dict() for _ in range(depth)]
    pairs = [None] * depth
    for l in reversed(range(depth)):
        wl, sv = Wb[l], saved[l]
        t = f"l{l}_b_"
        wide = dict(bn=1024) if l == 1 else {}
        GW[l]["w_down"] = _mm(sv["hh"], dhb, ta=True, name=t + "dw_down", **wide)
        da = _mm(dhb, wl["w_down"], tb=True, out_dtype=BF16, epi="dsqrelu", aux=sv["a_pre"], name=t + "d_hidden")
        GW[l]["w_up"] = _mm(sv["u3"], da, ta=True, o_stack=N_CHIPS, name=t + "dw_up", **wide)
        du3 = _mm(da, wl["w_up"], tb=True, b_stack=True, name=t + "d_u3", **wide)
        dh, dhb, GS[l]["norm_ffn_g"] = _rms_bwd(sv["h3"], norm_ffn_g[l:l + 1], du3, dh, name=t + "rms_ffn")
        GW[l]["w_xo"] = _mm(sv["o"], dhb, ta=True, name=t + "dw_xo", **wide)
        d_o = _mm(dhb, wl["w_xo"], tb=True, out_dtype=BF16, name=t + "d_o")
        dq, dkv = _attn_bwd(sv["q"], sv["kv"], d_o, name=t + "attn")
        GW[l]["w_q"] = _mm(sv["q_in"], dq, ta=True, name=t + "dw_q", **wide)
        dq_in = _mm(dq, wl["w_q"], tb=True, name=t + "d_q_in")
        dkvb = dkv.astype(BF16)
        GW[l]["w_kv"] = _mm(sv["mn"], dkvb, ta=True, o_stack=N_CHIPS, name=t + "dw_kv")
        dmn = _mm(dkvb, wl["w_kv"], tb=True, b_stack=True, name=t + "d_mem")
        _, _, GS[l]["norm_mem_g"] = _rms_bwd(ms, norm_mem_g[l:l + 1], dmn, None, name=t + "rms_mem")
        dh, dhb, GS[l]["norm_x_g"] = _rms_bwd(sv["h2"], norm_x_g[l:l + 1], dq_in, dh, name=t + "rms_x")
        GW[l]["w_out"] = _mm(sv["yy"], dhb, ta=True, name=t + "dw_out", **wide)
        dyy = _mm(dhb, wl["w_out"], tb=True, name=t + "d_y")
        dcb, GS[l]["ln_b_g"], GS[l]["ln_b_b"] = _ln_silu_bwd(sv["cb"], ln_b_g[l:l + 1], ln_b_b[l:l + 1], dyy, 1,
                                                             name=t + "ln_silu")
        db_, dc_, dh_, GS[l]["conv_a_w"] = _mixer_a_bwd(sv["z"], conv_a_full[l], dyy, name=t + "mixer_a")
        dv_, dg_, GS[l]["conv_b_w"], GS[l]["conv_b_bias"] = _mixer_b_bwd(sv["z"], conv_b_full[l], dcb,
                                                                         name=t + "mixer_b")
        dz = jnp.concatenate([db_, dc_, dh_, dv_, dg_], axis=1)
        GW[l]["w_in"] = _mm(sv["u"], dz, ta=True, o_stack=N_CHIPS, name=t + "dw_in")
        du = _mm(dz, wl["w_in"], tb=True, b_stack=True, name=t + "d_u")
        gs = [GW[l][n] if GW[l][n].ndim == 3 else GW[l][n].reshape(N_CHIPS, *W[n].shape[1:]) for n, _ in BIG]
        from_sibling = _swap_halves(gs, name=f"grad_swap_sibling_l{l}")
        pairs[l] = [_pair_sum(g, r, meta, name=f"grad_pair_sum_l{l}_{n}")
                    for g, r, (n, _) in zip(gs, from_sibling, BIG)]
        after = None
        if l == 1:
            g1_sems = _ici_start(pairs[1], pairs[1], False, name="grad_scatter_chips_l1_start")
            after = g1_sems[4]
        dh, dhb, GS[l]["norm_mix_g"] = _rms_bwd(sv["h"], norm_mix_g[l:l + 1], du, dh, name=t + "rms_mix",
                                                after=after)
    grad_x = dh[None]

    send_sems, recv_sems, srcs, lands, _ = g1_sems
    pairs[1], pieces1 = _ici_wait(send_sems, recv_sems, srcs, lands, False, GS[0]["norm_mix_g"],
                                  name="grad_scatter_chips_l1_wait")
    pieces = [_scatter_to_chips(pairs[0], name="grad_scatter_chips_l0"), pieces1]
    keys = [(l, n) for l in range(depth) for n, _ in BIG]
    halves = [_chip_sum(p, q, meta, name=f"grad_chip_sum_l{l}_{n}")
              for (l, n), p, q in zip(keys, list(pairs[0]) + list(pairs[1]), list(pieces[0]) + list(pieces[1]))]
    reduced = dict(zip(keys, _share_halves(halves, name="grad_share_sibling")))

    grads, deltas, new_m, new_v = {}, {}, {}, {}
    for n, _ in BIG:
        grads[n], deltas[n], new_m[n], new_v[n] = _adamw_layers(W[n], reduced[(0, n)], reduced[(1, n)], MO[n], VO[n],
                                                               name="adamw_" + n)

    small = [n for n in names if n not in dict(BIG)]
    full_shapes = {n: ((depth, W[n].shape[1], c_a) if n in ("conv_a_w", "conv_b_w") else W[n].shape)
                   for n in small}

    def small_grad(n):
        if n == "final_g":
            return d_final.reshape(W[n].shape)
        return jnp.stack([GS[l][n].reshape(full_shapes[n][1:]) for l in range(depth)])

    part = _pack([small_grad(n) for n in small], LANES)
    everyone = _gather_all(part, name="gather_small_grads")
    total = _sum_leading(everyone, name="small_grad_sum")
    full_grads = dict(zip(small, _unpack(total, [full_shapes[n] for n in small])))
    for n in ("conv_a_w", "conv_b_w"):
        full_grads[n] = lax.dynamic_slice_in_dim(full_grads[n], chip_idx * c_loc, c_loc, axis=2)
    shapes = [W[n].shape for n in small]
    d_s, m_s, v_s = _adamw(_pack([W[n] for n in small], 128), _pack([full_grads[n] for n in small], 128),
                           _pack([MO[n] for n in small], 128), _pack([VO[n] for n in small], 128),
                           name="adamw_small")
    for n, d, nm, nv in zip(small, _unpack(d_s, shapes), _unpack(m_s, shapes), _unpack(v_s, shapes)):
        grads[n], deltas[n], new_m[n], new_v[n] = full_grads[n], d, nm, nv

    return (loss, grad_x, *[grads[n] for n in names], *[deltas[n] for n in names],
            *[new_m[n] for n in names], *[new_v[n] for n in names])
```

```python
import jax
import jax.numpy as jnp
from jax import lax
from jax.experimental import pallas as pl
from jax.experimental.pallas import tpu as pltpu

F32 = jnp.float32
BF16 = jnp.bfloat16
MESH = pl.DeviceIdType.MESH

EPS = 1e-6
N_XHEADS = 4
K_A = 3
K_B = 31
PAD_A = 8
PAD_B = 32
CONV_CHUNK = 256
ROW_TILE = 512
LANES = 1024
VMEM_LIMIT_BYTES = 48 * 1024 * 1024

ADAM_LR = 0.001
ADAM_B1 = 0.9
ADAM_B2 = 0.999
ADAM_EPS = 1e-08
ADAM_WD = 0.01
ADAM_STEP = 10

BIG = (("w_in", 1), ("w_out", 0), ("w_q", 0), ("w_kv", 1), ("w_xo", 0), ("w_up", 1), ("w_down", 0))
FWD_GROUPS = (("w_in", "w_out"), ("w_q", "w_kv", "w_xo"), ("w_up", "w_down"))
BWD_GROUPS = (("w_down", "w_up"), ("w_xo", "w_q", "w_kv"), ("w_out", "w_in"))
N_CHIPS = 4
N_DEV = 8


def _params(sem=None):
    return pltpu.CompilerParams(dimension_semantics=sem, vmem_limit_bytes=VMEM_LIMIT_BYTES)


def _pick(cands, n):
    for c in cands:
        if c <= n and n % c == 0:
            return c
    return n


def _mm(a, b, *, name, ta=False, tb=False, out_dtype=F32, res=None, epi=None, aux=None,
        b_stack=False, o_stack=0, bm=1024, bn=512, bk=1024):
    if ta:
        K, M = a.shape
    else:
        M, K = a.shape
    if b_stack:
        n_st, d1, d2 = b.shape
        N, kb = (d1, d2) if tb else (n_st * d2, d1)
        assert K == (n_st * d2 if tb else d1), (name, a.shape, b.shape)
    else:
        N = b.shape[0] if tb else b.shape[1]
    n_unit = b.shape[2] if (b_stack and not tb) else (N // o_stack if o_stack else N)
    k_unit = b.shape[2] if (b_stack and tb) else K
    bm = _pick((bm, 512, 256, 128), M)
    bn = _pick((bn, 640, 256, 384, 128), n_unit)
    bk = _pick((bk, 640, 512, 256, 128), k_unit)
    assert M % bm == 0 and N % bn == 0 and K % bk == 0, (name, M, N, K)
    nk = K // bk
    per_n = n_unit // bn
    per_k = k_unit // bk
    a_spec = (pl.BlockSpec((bk, bm), lambda i, j, k: (k, i)) if ta
              else pl.BlockSpec((bm, bk), lambda i, j, k: (i, k)))
    if b_stack and tb:
        b_spec = pl.BlockSpec((None, bn, bk), lambda i, j, k: (k // per_k, j, k % per_k))
    elif b_stack:
        b_spec = pl.BlockSpec((None, bk, bn), lambda i, j, k: (j // per_n, k, j % per_n))
    elif tb:
        b_spec = pl.BlockSpec((bn, bk), lambda i, j, k: (j, k))
    else:
        b_spec = pl.BlockSpec((bk, bn), lambda i, j, k: (k, j))
    o_spec = pl.BlockSpec((bm, bn), lambda i, j, k: (i, j))
    dims = (((0 if ta else 1,), (1 if tb else 0,)), ((), ()))
    ins, in_specs = [a, b], [a_spec, b_spec]
    if res is not None:
        ins.append(res)
        in_specs.append(o_spec)
    if aux is not None:
        ins.append(aux)
        in_specs.append(o_spec)
    n_out = 2 if epi == "sqrelu" else 1
    out_shape = [jax.ShapeDtypeStruct((M, N), out_dtype)] * n_out
    out_specs = [o_spec] * n_out
    if o_stack:
        assert n_out == 1 and res is None and aux is None
        out_shape = [jax.ShapeDtypeStruct((o_stack, M, N // o_stack), out_dtype)]
        out_specs = [pl.BlockSpec((None, bm, bn), lambda i, j, k: (j // per_n, i, j % per_n))]

    def body(*refs):
        a_ref, b_ref = refs[0], refs[1]
        pos = 2
        res_ref = aux_ref = None
        if res is not None:
            res_ref = refs[pos]
            pos += 1
        if aux is not None:
            aux_ref = refs[pos]
            pos += 1
        outs = refs[pos:pos + n_out]

        def product():
            return lax.dot_general(a_ref[...], b_ref[...], dims, preferred_element_type=F32)

        def finish(r):
            if res_ref is not None:
                r = r + res_ref[...]
            if epi == "sqrelu":
                outs[0][...] = r.astype(out_dtype)
                rl = jnp.maximum(r, 0.0)
                outs[1][...] = (rl * rl).astype(out_dtype)
            elif epi == "dsqrelu":
                outs[0][...] = (r * (2.0 * jnp.maximum(aux_ref[...].astype(F32), 0.0))).astype(out_dtype)
            else:
                outs[0][...] = r.astype(out_dtype)

        if nk == 1:
            finish(product())
            return
        acc = refs[pos + n_out]
        k = pl.program_id(2)

        @pl.when(k == 0)
        def _():
            acc[...] = product()

        @pl.when(jnp.logical_and(k > 0, k < nk - 1))
        def _():
            acc[...] += product()

        @pl.when(k == nk - 1)
        def _():
            finish(acc[...] + product())

    out = pl.pallas_call(
        body, name=name, grid=(M // bm, N // bn, nk),
        in_specs=in_specs, out_specs=out_specs, out_shape=out_shape,
        scratch_shapes=[pltpu.VMEM((bm, bn), F32)] if nk > 1 else [],
        compiler_params=_params(("parallel", "parallel", "arbitrary")),
    )(*ins)
    return out if n_out == 2 else out[0]


def _row_tile(rows):
    return min(ROW_TILE, rows)


def _rms_fwd(x, g, *, name, after=None):
    S, D = x.shape
    tr = _row_tile(S)

    def body(x_ref, g_ref, *rest):
        o_ref = rest[-1]
        xv = x_ref[...]
        r = lax.rsqrt(jnp.mean(xv * xv, axis=-1, keepdims=True) + EPS)
        o_ref[...] = (xv * r * g_ref[...]).astype(BF16)

    deps = [] if after is None else [after]
    return pl.pallas_call(
        body, name=name, grid=(S // tr,),
        in_specs=[pl.BlockSpec((tr, D), lambda i: (i, 0)), pl.BlockSpec((1, D), lambda i: (0, 0))]
        + [ANY] * len(deps),
        out_specs=pl.BlockSpec((tr, D), lambda i: (i, 0)),
        out_shape=jax.ShapeDtypeStruct((S, D), BF16),
        compiler_params=_params(("parallel",)),
    )(x, g, *deps)


def _rms_bwd(x, g, du, dres, *, name, after=None):
    S, D = x.shape
    tr = _row_tile(S)
    has_res = dres is not None
    deps = [] if after is None else [after]

    def body(*refs):
        dx_ref, dxb_ref, dg_ref = refs[-3:]
        if has_res:
            x_ref, g_ref, du_ref, dres_ref = refs[:4]
        else:
            x_ref, g_ref, du_ref = refs[:3]
        xv = x_ref[...]
        r = lax.rsqrt(jnp.mean(xv * xv, axis=-1, keepdims=True) + EPS)
        xh = xv * r
        dy = du_ref[...]
        dxh = dy * g_ref[...]
        dx = r * (dxh - xh * jnp.mean(dxh * xh, axis=-1, keepdims=True))
        if has_res:
            dx = dx + dres_ref[...]
        dx_ref[...] = dx
        dxb_ref[...] = dx.astype(BF16)

        @pl.when(pl.program_id(0) == 0)
        def _():
            dg_ref[...] = jnp.zeros_like(dg_ref)

        dg_ref[...] += jnp.sum(dy * xh, axis=0, keepdims=True)

    row = pl.BlockSpec((tr, D), lambda i: (i, 0))
    vec = pl.BlockSpec((1, D), lambda i: (0, 0))
    ins = [x, g, du] + ([dres] if has_res else []) + deps
    in_specs = [row, vec, row] + ([row] if has_res else []) + [ANY] * len(deps)
    return pl.pallas_call(
        body, name=name, grid=(S // tr,),
        in_specs=in_specs, out_specs=[row, row, vec],
        out_shape=[jax.ShapeDtypeStruct((S, D), F32), jax.ShapeDtypeStruct((S, D), BF16),
                   jax.ShapeDtypeStruct((1, D), F32)],
        compiler_params=_params(("arbitrary",)),
    )(*ins)


def _loss_head(h, g, target, *, name):
    S, D = h.shape
    tr = _row_tile(S)

    def body(x_ref, g_ref, t_ref, loss_ref, dx_ref, dxb_ref, dg_ref):
        xv = x_ref[...]
        r = lax.rsqrt(jnp.mean(xv * xv, axis=-1, keepdims=True) + EPS)
        xh = xv * r
        gv = g_ref[...]
        err = xh * gv - t_ref[...]
        part = 0.5 * jnp.sum(jnp.mean(err * err, axis=-1, keepdims=True), axis=0, keepdims=True)
        dy = err * (1.0 / D)
        dxh = dy * gv
        dx = r * (dxh - xh * jnp.mean(dxh * xh, axis=-1, keepdims=True))
        dx_ref[...] = dx
        dxb_ref[...] = dx.astype(BF16)

        @pl.when(pl.program_id(0) == 0)
        def _():
            dg_ref[...] = jnp.zeros_like(dg_ref)
            loss_ref[...] = jnp.zeros_like(loss_ref)

        dg_ref[...] += jnp.sum(dy * xh, axis=0, keepdims=True)
        loss_ref[...] += jnp.broadcast_to(part, loss_ref.shape)

    row = pl.BlockSpec((tr, D), lambda i: (i, 0))
    vec = pl.BlockSpec((1, D), lambda i: (0, 0))
    return pl.pallas_call(
        body, name=name, grid=(S // tr,),
        in_specs=[row, vec, row],
        out_specs=[pl.BlockSpec((1, 128), lambda i: (0, 0)), row, row, vec],
        out_shape=[jax.ShapeDtypeStruct((1, 128), F32), jax.ShapeDtypeStruct((S, D), F32),
                   jax.ShapeDtypeStruct((S, D), BF16), jax.ShapeDtypeStruct((1, D), F32)],
        compiler_params=_params(("arbitrary",)),
    )(h, g, target)


def _sigmoid(x):
    return 1.0 / (1.0 + jnp.exp(-x))


def _ln_silu_fwd(cb, g, b, *, name):
    S, C = cb.shape
    tr = _row_tile(S)

    def body(x_ref, g_ref, b_ref, o_ref):
        xv = x_ref[...]
        mu = jnp.mean(xv, axis=-1, keepdims=True)
        xc = xv - mu
        rs = lax.rsqrt(jnp.mean(xc * xc, axis=-1, keepdims=True) + EPS)
        l = xc * rs * g_ref[...] + b_ref[...]
        o_ref[...] = (l * _sigmoid(l)).astype(BF16)

    row = pl.BlockSpec((tr, C), lambda i: (i, 0))
    vec = pl.BlockSpec((1, C), lambda i: (0, 0))
    return pl.pallas_call(
        body, name=name, grid=(S // tr,), in_specs=[row, vec, vec], out_specs=row,
        out_shape=jax.ShapeDtypeStruct((S, C), BF16), compiler_params=_params(("parallel",)),
    )(cb, g, b)


def _ln_silu_bwd(cb, g, b, dy, col_block, *, name):
    S, C = cb.shape
    tr = _row_tile(S)

    def body(x_ref, g_ref, b_ref, dy_ref, dx_ref, dg_ref, db_ref):
        xv = x_ref[...]
        mu = jnp.mean(xv, axis=-1, keepdims=True)
        xc = xv - mu
        rs = lax.rsqrt(jnp.mean(xc * xc, axis=-1, keepdims=True) + EPS)
        xh = xc * rs
        gv = g_ref[...]
        l = xh * gv + b_ref[...]
        sg = _sigmoid(l)
        dl = dy_ref[...] * (sg + l * sg * (1.0 - sg))
        dxh = dl * gv
        dx_ref[...] = rs * (dxh - jnp.mean(dxh, axis=-1, keepdims=True)
                            - xh * jnp.mean(dxh * xh, axis=-1, keepdims=True))

        @pl.when(pl.program_id(0) == 0)
        def _():
            dg_ref[...] = jnp.zeros_like(dg_ref)
            db_ref[...] = jnp.zeros_like(db_ref)

        dg_ref[...] += jnp.sum(dl * xh, axis=0, keepdims=True)
        db_ref[...] += jnp.sum(dl, axis=0, keepdims=True)

    row = pl.BlockSpec((tr, C), lambda i: (i, 0))
    vec = pl.BlockSpec((1, C), lambda i: (0, 0))
    return pl.pallas_call(
        body, name=name, grid=(S // tr,),
        in_specs=[row, vec, vec, pl.BlockSpec((tr, C), lambda i: (i, col_block))],
        out_specs=[row, vec, vec],
        out_shape=[jax.ShapeDtypeStruct((S, C), F32), jax.ShapeDtypeStruct((1, C), F32),
                   jax.ShapeDtypeStruct((1, C), F32)],
        compiler_params=_params(("arbitrary",)),
    )(cb, g, b, dy)


def _attn_fwd(q, kv, *, name):
    S, D = q.shape
    M = kv.shape[0]
    hd = D // N_XHEADS
    scale = 1.0 / float(hd) ** 0.5
    tq = _row_tile(S)

    def body(q_ref, k_ref, v_ref, o_ref):
        for h in range(N_XHEADS):
            cols = slice(h * hd, (h + 1) * hd)
            s = lax.dot_general(q_ref[:, cols], k_ref[:, cols], (((1,), (1,)), ((), ())),
                                preferred_element_type=F32) * scale
            e = jnp.exp(s - jnp.max(s, axis=-1, keepdims=True))
            p = e / jnp.sum(e, axis=-1, keepdims=True)
            o = jnp.dot(p.astype(BF16), v_ref[:, cols], preferred_element_type=F32)
            o_ref[:, cols] = o.astype(BF16)

    return pl.pallas_call(
        body, name=name, grid=(S // tq,),
        in_specs=[pl.BlockSpec((tq, D), lambda i: (i, 0)), pl.BlockSpec((M, D), lambda i: (0, 0)),
                  pl.BlockSpec((M, D), lambda i: (0, 1))],
        out_specs=pl.BlockSpec((tq, D), lambda i: (i, 0)),
        out_shape=jax.ShapeDtypeStruct((S, D), BF16), compiler_params=_params(("parallel",)),
    )(q, kv, kv)


def _attn_bwd(q, kv, do, *, name):
    S, D = q.shape
    M = kv.shape[0]
    hd = D // N_XHEADS
    scale = 1.0 / float(hd) ** 0.5
    tq = _row_tile(S)

    def body(q_ref, k_ref, v_ref, do_ref, dq_ref, dkv_ref):
        @pl.when(pl.program_id(0) == 0)
        def _():
            dkv_ref[...] = jnp.zeros_like(dkv_ref)

        for h in range(N_XHEADS):
            cols = slice(h * hd, (h + 1) * hd)
            vcols = slice(D + h * hd, D + (h + 1) * hd)
            qh, kh, vh, doh = q_ref[:, cols], k_ref[:, cols], v_ref[:, cols], do_ref[:, cols]
            s = lax.dot_general(qh, kh, (((1,), (1,)), ((), ())), preferred_element_type=F32) * scale
            e = jnp.exp(s - jnp.max(s, axis=-1, keepdims=True))
            p = e / jnp.sum(e, axis=-1, keepdims=True)
            pb = p.astype(BF16)
            dp = lax.dot_general(doh, vh, (((1,), (1,)), ((), ())), preferred_element_type=F32)
            ds = (p * (dp - jnp.sum(dp * p, axis=-1, keepdims=True)) * scale).astype(BF16)
            dq_ref[:, cols] = jnp.dot(ds, kh, preferred_element_type=F32).astype(BF16)
            dkv_ref[:, cols] += lax.dot_general(ds, qh, (((0,), (0,)), ((), ())), preferred_element_type=F32)
            dkv_ref[:, vcols] += lax.dot_general(pb, doh, (((0,), (0,)), ((), ())), preferred_element_type=F32)

    row = pl.BlockSpec((tq, D), lambda i: (i, 0))
    return pl.pallas_call(
        body, name=name, grid=(S // tq,),
        in_specs=[row, pl.BlockSpec((M, D), lambda i: (0, 0)), pl.BlockSpec((M, D), lambda i: (0, 1)), row],
        out_specs=[row, pl.BlockSpec((M, 2 * D), lambda i: (0, 0))],
        out_shape=[jax.ShapeDtypeStruct((S, D), BF16), jax.ShapeDtypeStruct((M, 2 * D), F32)],
        compiler_params=_params(("arbitrary",)),
    )(q, kv, kv, do)


def _delayed(win, j, pad):
    return (win if j == 0 else pltpu.roll(win, j, 0))[pad:, :]


def _advanced(win, j, ch):
    return (win if j == 0 else pltpu.roll(win, win.shape[0] - j, 0))[:ch, :]


def _mixer_a_fwd(z, w, *, name):
    S = z.shape[0]
    C = w.shape[1]
    nb = C // 128
    ch = min(CONV_CHUNK, S)

    def body(b_ref, c_ref, h_ref, w_ref, y_ref, xp):
        xp[0:PAD_A, :] = jnp.zeros((PAD_A, 128), F32)
        xp[PAD_A:, :] = c_ref[...] * h_ref[...]

        def chunk(i, carry):
            base = pl.multiple_of(i * ch, ch)
            win = xp[pl.ds(base, ch + PAD_A), :]
            acc = _delayed(win, 0, PAD_A) * w_ref[K_A - 1:K_A, :]
            for j in range(1, K_A):
                acc = acc + _delayed(win, j, PAD_A) * w_ref[K_A - 1 - j:K_A - j, :]
            y_ref[pl.ds(base, ch), :] = (b_ref[pl.ds(base, ch), :] * acc).astype(BF16)
            return carry

        lax.fori_loop(0, S // ch, chunk, 0)

    def col(g):
        return pl.BlockSpec((S, 128), lambda j: (0, g * nb + j))

    return pl.pallas_call(
        body, name=name, grid=(nb,),
        in_specs=[col(0), col(1), col(2), pl.BlockSpec((K_A, 128), lambda j: (0, j))],
        out_specs=pl.BlockSpec((S, 128), lambda j: (0, j)),
        out_shape=jax.ShapeDtypeStruct((S, C), BF16),
        scratch_shapes=[pltpu.VMEM((PAD_A + S, 128), F32)],
        compiler_params=_params(("parallel",)),
    )(z, z, z, w)


def _mixer_a_bwd(z, w, dy, *, name):
    S = z.shape[0]
    C = w.shape[1]
    nb = C // 128
    ch = min(CONV_CHUNK, S)

    def body(b_ref, c_ref, h_ref, w_ref, dy_ref, db_ref, dc_ref, dh_ref, dw_ref, xp, dp):
        xp[0:PAD_A, :] = jnp.zeros((PAD_A, 128), F32)
        xp[PAD_A:, :] = c_ref[...] * h_ref[...]
        dp[S:, :] = jnp.zeros((PAD_A, 128), F32)
        dw_ref[...] = jnp.zeros_like(dw_ref)

        def chunk(i, carry):
            base = pl.multiple_of(i * ch, ch)
            win = xp[pl.ds(base, ch + PAD_A), :]
            dya = dy_ref[pl.ds(base, ch), :]
            dcv = dya * b_ref[pl.ds(base, ch), :]
            dp[pl.ds(base, ch), :] = dcv
            acc = None
            for j in range(K_A):
                xs = _delayed(win, j, PAD_A)
                k = K_A - 1 - j
                term = xs * w_ref[k:k + 1, :]
                acc = term if acc is None else acc + term
                dw_ref[k:k + 1, :] += jnp.sum(dcv * xs, axis=0, keepdims=True)
            db_ref[pl.ds(base, ch), :] = (dya * acc).astype(BF16)
            return carry

        lax.fori_loop(0, S // ch, chunk, 0)

        def chunk2(i, carry):
            base = pl.multiple_of(i * ch, ch)
            win = dp[pl.ds(base, ch + PAD_A), :]
            acc = None
            for j in range(K_A):
                term = _advanced(win, j, ch) * w_ref[K_A - 1 - j:K_A - j, :]
                acc = term if acc is None else acc + term
            dc_ref[pl.ds(base, ch), :] = (acc * h_ref[pl.ds(base, ch), :]).astype(BF16)
            dh_ref[pl.ds(base, ch), :] = (acc * c_ref[pl.ds(base, ch), :]).astype(BF16)
            return carry

        lax.fori_loop(0, S // ch, chunk2, 0)

    def col(g):
        return pl.BlockSpec((S, 128), lambda j: (0, g * nb + j))

    out_col = pl.BlockSpec((S, 128), lambda j: (0, j))
    wspec = pl.BlockSpec((K_A, 128), lambda j: (0, j))
    return pl.pallas_call(
        body, name=name, grid=(nb,),
        in_specs=[col(0), col(1), col(2), wspec, out_col],
        out_specs=[out_col, out_col, out_col, wspec],
        out_shape=[jax.ShapeDtypeStruct((S, C), BF16)] * 3 + [jax.ShapeDtypeStruct((K_A, C), F32)],
        scratch_shapes=[pltpu.VMEM((PAD_A + S, 128), F32), pltpu.VMEM((S + PAD_A, 128), F32)],
        compiler_params=_params(("parallel",)),
    )(z, z, z, w, dy)


def _mixer_b_fwd(z, w, bias, *, name):
    S = z.shape[0]
    C = w.shape[1]
    nb = C // 128
    ch = min(CONV_CHUNK, S)

    def body(v_ref, g_ref, w_ref, bias_ref, cb_ref, xp):
        xp[0:PAD_B, :] = jnp.zeros((PAD_B, 128), F32)
        xp[PAD_B:, :] = v_ref[...] * _sigmoid(g_ref[...])

        def chunk(i, carry):
            base = pl.multiple_of(i * ch, ch)
            win = xp[pl.ds(base, ch + PAD_B), :]
            acc = None
            for j in range(K_B):
                term = _delayed(win, j, PAD_B) * w_ref[K_B - 1 - j:K_B - j, :]
                acc = term if acc is None else acc + term
            cb_ref[pl.ds(base, ch), :] = acc + bias_ref[...]
            return carry

        lax.fori_loop(0, S // ch, chunk, 0)

    def col(g):
        return pl.BlockSpec((S, 128), lambda j: (0, g * nb + j))

    return pl.pallas_call(
        body, name=name, grid=(nb,),
        in_specs=[col(3), col(4), pl.BlockSpec((K_B, 128), lambda j: (0, j)),
                  pl.BlockSpec((1, 128), lambda j: (0, j))],
        out_specs=pl.BlockSpec((S, 128), lambda j: (0, j)),
        out_shape=jax.ShapeDtypeStruct((S, C), F32),
        scratch_shapes=[pltpu.VMEM((PAD_B + S, 128), F32)],
        compiler_params=_params(("parallel",)),
    )(z, z, w, bias)


def _mixer_b_bwd(z, w, dcb, *, name):
    S = z.shape[0]
    C = w.shape[1]
    nb = C // 128
    ch = min(CONV_CHUNK, S)

    def body(v_ref, g_ref, w_ref, dcb_ref, dv_ref, dg_ref, dw_ref, dbias_ref, xp, dp):
        xp[0:PAD_B, :] = jnp.zeros((PAD_B, 128), F32)
        xp[PAD_B:, :] = v_ref[...] * _sigmoid(g_ref[...])
        dp[0:S, :] = dcb_ref[...]
        dp[S:, :] = jnp.zeros((PAD_B, 128), F32)
        dw_ref[...] = jnp.zeros_like(dw_ref)
        dbias_ref[...] = jnp.sum(dcb_ref[...], axis=0, keepdims=True)

        def chunk(i, carry):
            base = pl.multiple_of(i * ch, ch)
            win = xp[pl.ds(base, ch + PAD_B), :]
            d = dcb_ref[pl.ds(base, ch), :]
            for j in range(K_B):
                k = K_B - 1 - j
                dw_ref[k:k + 1, :] += jnp.sum(d * _delayed(win, j, PAD_B), axis=0, keepdims=True)
            return carry

        lax.fori_loop(0, S // ch, chunk, 0)

        def chunk2(i, carry):
            base = pl.multiple_of(i * ch, ch)
            win = dp[pl.ds(base, ch + PAD_B), :]
            acc = None
            for j in range(K_B):
                term = _advanced(win, j, ch) * w_ref[K_B - 1 - j:K_B - j, :]
                acc = term if acc is None else acc + term
            sg = _sigmoid(g_ref[pl.ds(base, ch), :])
            vv = v_ref[pl.ds(base, ch), :]
            dv_ref[pl.ds(base, ch), :] = (acc * sg).astype(BF16)
            dg_ref[pl.ds(base, ch), :] = (acc * vv * sg * (1.0 - sg)).astype(BF16)
            return carry

        lax.fori_loop(0, S // ch, chunk2, 0)

    def col(g):
        return pl.BlockSpec((S, 128), lambda j: (0, g * nb + j))

    out_col = pl.BlockSpec((S, 128), lambda j: (0, j))
    wspec = pl.BlockSpec((K_B, 128), lambda j: (0, j))
    bspec = pl.BlockSpec((1, 128), lambda j: (0, j))
    return pl.pallas_call(
        body, name=name, grid=(nb,),
        in_specs=[col(3), col(4), wspec, out_col],
        out_specs=[out_col, out_col, wspec, bspec],
        out_shape=[jax.ShapeDtypeStruct((S, C), BF16)] * 2
        + [jax.ShapeDtypeStruct((K_B, C), F32), jax.ShapeDtypeStruct((1, C), F32)],
        scratch_shapes=[pltpu.VMEM((PAD_B + S, 128), F32), pltpu.VMEM((S + PAD_B, 128), F32)],
        compiler_params=_params(("parallel",)),
    )(z, z, w, dcb)


def _ew_tile(R):
    for t in (512, 256, 128, 64, 32, 16, 8):
        if R % t == 0:
            return t
    return R


def _pair_sum(g, r, meta, *, name):
    n, a, b = g.shape
    ah = a // 2
    tr = _pick((256, 128, 64, 32, 16), ah)
    nh = ah // tr

    def body(meta_ref, g_ref, r_ref, o_ref):
        o_ref[...] = (g_ref[...] + r_ref[...]).astype(BF16)

    half = pl.BlockSpec((None, tr, b), lambda j, i, meta_ref: (j, i, 0))
    return pl.pallas_call(
        body, name=name,
        grid_spec=pltpu.PrefetchScalarGridSpec(
            num_scalar_prefetch=1, grid=(n, nh),
            in_specs=[pl.BlockSpec((None, tr, b), lambda j, i, meta_ref: (j, meta_ref[0] * nh + i, 0)), half],
            out_specs=half),
        out_shape=jax.ShapeDtypeStruct((n, ah, b), BF16), compiler_params=_params(("parallel", "parallel")),
    )(meta, g, r)


def _chip_sum(p, q, meta, *, name):
    n, ah, b = p.shape
    tr = _pick((256, 128, 64, 32, 16), ah)
    nh = ah // tr

    def body(meta_ref, p_ref, q1_ref, q2_ref, q3_ref, o_ref):
        o_ref[...] = ((p_ref[...].astype(F32) + q1_ref[...].astype(F32)) + q2_ref[...].astype(F32)
                      ) + q3_ref[...].astype(F32)

    def piece(mask):
        return pl.BlockSpec((None, tr, b), lambda i, meta_ref: (meta_ref[1] ^ mask, i, 0))

    return pl.pallas_call(
        body, name=name,
        grid_spec=pltpu.PrefetchScalarGridSpec(
            num_scalar_prefetch=1, grid=(nh,),
            in_specs=[piece(0), piece(1), piece(2), piece(3)],
            out_specs=pl.BlockSpec((tr, b), lambda i, meta_ref: (meta_ref[0] * nh + i, 0))),
        out_shape=jax.ShapeDtypeStruct((2 * ah, b), F32), compiler_params=_params(("parallel",)),
    )(meta, p, q, q, q)


def _sum_leading(x, *, name):
    n, R, C = x.shape
    tr = _ew_tile(R)

    def body(x_ref, o_ref):
        acc = x_ref[0].astype(F32)
        for k in range(1, n):
            acc = acc + x_ref[k].astype(F32)
        o_ref[...] = acc

    return pl.pallas_call(
        body, name=name, grid=(R // tr,),
        in_specs=[pl.BlockSpec((n, tr, C), lambda i: (0, i, 0))],
        out_specs=pl.BlockSpec((tr, C), lambda i: (i, 0)),
        out_shape=jax.ShapeDtypeStruct((R, C), F32), compiler_params=_params(("parallel",)),
    )(x)


def _adamw(w, g, m, v, *, name):
    R, C = w.shape
    tr = _ew_tile(R)

    def body(w_ref, g_ref, m_ref, v_ref, d_ref, nm_ref, nv_ref):
        gv = g_ref[...]
        nm = ADAM_B1 * m_ref[...] + (1.0 - ADAM_B1) * gv
        nv = ADAM_B2 * v_ref[...] + (1.0 - ADAM_B2) * (gv * gv)
        m_hat = nm / (1.0 - ADAM_B1 ** ADAM_STEP)
        v_hat = nv / (1.0 - ADAM_B2 ** ADAM_STEP)
        d_ref[...] = -ADAM_LR * (m_hat / (jnp.sqrt(v_hat) + ADAM_EPS) + ADAM_WD * w_ref[...])
        nm_ref[...] = nm
        nv_ref[...] = nv

    row = pl.BlockSpec((tr, C), lambda i: (i, 0))
    return pl.pallas_call(
        body, name=name, grid=(R // tr,), in_specs=[row] * 4, out_specs=[row] * 3,
        out_shape=[jax.ShapeDtypeStruct((R, C), F32)] * 3, compiler_params=_params(("parallel",)),
    )(w, g, m, v)


def _adamw_layers(w, g0, g1, m, v, *, name):
    _, a, b = w.shape
    tr = _pick((256, 128, 64, 32, 16, 8), a)

    def body(w_ref, g0_ref, g1_ref, m_ref, v_ref, g_ref, d_ref, nm_ref, nv_ref):
        gv = jnp.where(pl.program_id(1) == 0, g0_ref[...], g1_ref[...])
        nm = ADAM_B1 * m_ref[...] + (1.0 - ADAM_B1) * gv
        nv = ADAM_B2 * v_ref[...] + (1.0 - ADAM_B2) * (gv * gv)
        m_hat = nm / (1.0 - ADAM_B1 ** ADAM_STEP)
        v_hat = nv / (1.0 - ADAM_B2 ** ADAM_STEP)
        g_ref[...] = gv
        d_ref[...] = -ADAM_LR * (m_hat / (jnp.sqrt(v_hat) + ADAM_EPS) + ADAM_WD * w_ref[...])
        nm_ref[...] = nm
        nv_ref[...] = nv

    lay = pl.BlockSpec((None, tr, b), lambda i, l: (l, i, 0))
    row = pl.BlockSpec((tr, b), lambda i, l: (i, 0))
    return pl.pallas_call(
        body, name=name, grid=(a // tr, 2), in_specs=[lay, row, row, lay, lay], out_specs=[lay] * 4,
        out_shape=[jax.ShapeDtypeStruct(w.shape, F32)] * 4, compiler_params=_params(("parallel", "arbitrary")),
    )(w, g0, g1, m, v)


ANY = pl.BlockSpec(memory_space=pl.ANY)


def _place():
    x, y, c = lax.axis_index("x"), lax.axis_index("y"), lax.axis_index("c")
    return x, y, c, 2 * x + y


def _other_chip(x, y, mask):
    px = 1 - x if mask & 2 else x
    py = 1 - y if mask & 1 else y
    return px, py, 2 * px + py


MASKS = (1, 2, 3)


def _half(ref, c, lead=()):
    ah = ref.shape[-2] // 2
    return ref.at[(*lead, pl.ds(c * ah, ah), slice(None))]


def _gather_weights(owns, *, name):
    n = len(owns)

    def body(*refs):
        own, out = refs[:n], refs[n:2 * n]
        send_sems, recv_sems = refs[2 * n], refs[2 * n + 1]
        x, y, c, chip = _place()
        sends = []
        for i in range(n):
            for k, mask in enumerate(MASKS):
                px, py, _ = _other_chip(x, y, mask)
                cp = pltpu.make_async_remote_copy(_half(own[i], c), _half(out[i], c, (chip,)),
                                                  send_sems.at[i, k], recv_sems.at[i, k],
                                                  device_id=(px, py, c), device_id_type=MESH)
                cp.start()
                sends.append(cp)
        for i in range(n):
            for k, mask in enumerate(MASKS):
                px, py, pchip = _other_chip(x, y, mask)
                got = _half(out[i], c, (pchip,))
                pltpu.make_async_remote_copy(got, got, send_sems.at[i, k], recv_sems.at[i, k],
                                             device_id=(px, py, c), device_id_type=MESH).wait_recv()
                cp = pltpu.make_async_remote_copy(got, got, send_sems.at[i, 3 + k], recv_sems.at[i, 3 + k],
                                                  device_id=(x, y, 1 - c), device_id_type=MESH)
                cp.start()
                sends.append(cp)
        for i in range(n):
            for k, mask in enumerate(MASKS):
                _, _, pchip = _other_chip(x, y, mask)
                theirs = _half(out[i], 1 - c, (pchip,))
                pltpu.make_async_remote_copy(theirs, theirs, send_sems.at[i, 3 + k], recv_sems.at[i, 3 + k],
                                             device_id=(x, y, 1 - c), device_id_type=MESH).wait_recv()
        for cp in sends:
            cp.wait_send()

    return pl.pallas_call(
        body, name=name, in_specs=[ANY] * n, out_specs=[ANY] * n,
        out_shape=[jax.ShapeDtypeStruct((N_CHIPS, *o.shape), o.dtype) for o in owns],
        scratch_shapes=[pltpu.SemaphoreType.DMA((n, 6)), pltpu.SemaphoreType.DMA((n, 6))],
    )(*owns)


SIBLING_ID = 0


def _sibling_barrier(x, y, c):
    sem = pltpu.get_barrier_semaphore()
    pl.semaphore_signal(sem, inc=1, device_id=(x, y, 1 - c), device_id_type=MESH)
    pl.semaphore_wait(sem, 1)


def _swap_halves(gs, *, name):
    n = len(gs)

    def body(*refs):
        g, out = refs[:n], refs[n:2 * n]
        send_sems, recv_sems = refs[2 * n], refs[2 * n + 1]
        x, y, c, _ = _place()
        _sibling_barrier(x, y, c)
        cps = []
        for i in range(n):
            ah = g[i].shape[1] // 2
            cp = pltpu.make_async_remote_copy(g[i].at[:, pl.ds((1 - c) * ah, ah), :], out[i],
                                              send_sems.at[i], recv_sems.at[i],
                                              device_id=(x, y, 1 - c), device_id_type=MESH)
            cp.start()
            cps.append(cp)
        for cp in cps:
            cp.wait()

    return pl.pallas_call(
        body, name=name, in_specs=[ANY] * n, out_specs=[ANY] * n,
        out_shape=[jax.ShapeDtypeStruct((g.shape[0], g.shape[1] // 2, g.shape[2]), g.dtype) for g in gs],
        scratch_shapes=[pltpu.SemaphoreType.DMA((n,)), pltpu.SemaphoreType.DMA((n,))],
        compiler_params=pltpu.CompilerParams(collective_id=SIBLING_ID),
    )(*gs)


def _scatter_to_chips(ps, *, name):
    n = len(ps)

    def body(*refs):
        p, out = refs[:n], refs[n:2 * n]
        send_sems, recv_sems = refs[2 * n], refs[2 * n + 1]
        x, y, c, chip = _place()
        sends = []
        for i in range(n):
            for k, mask in enumerate(MASKS):
                px, py, pchip = _other_chip(x, y, mask)
                cp = pltpu.make_async_remote_copy(p[i].at[pchip], out[i].at[chip], send_sems.at[i, k],
                                                  recv_sems.at[i, k], device_id=(px, py, c), device_id_type=MESH)
                cp.start()
                sends.append(cp)
        for i in range(n):
            for k, mask in enumerate(MASKS):
                px, py, pchip = _other_chip(x, y, mask)
                pltpu.make_async_remote_copy(p[i].at[pchip], out[i].at[pchip], send_sems.at[i, k],
                                             recv_sems.at[i, k], device_id=(px, py, c),
                                             device_id_type=MESH).wait_recv()
        for cp in sends:
            cp.wait_send()

    return pl.pallas_call(
        body, name=name, in_specs=[ANY] * n, out_specs=[ANY] * n,
        out_shape=[jax.ShapeDtypeStruct(p.shape, p.dtype) for p in ps],
        scratch_shapes=[pltpu.SemaphoreType.DMA((n, 3)), pltpu.SemaphoreType.DMA((n, 3))],
    )(*ps)


HBM_SPEC = pl.BlockSpec(memory_space=pltpu.HBM)
SEM_SPEC = pl.BlockSpec(memory_space=pltpu.SEMAPHORE)
EFFECT = pltpu.SideEffectType.DATAFLOW_SIDE_EFFECTING


def _ici_ends(src, land, gather, x, y, c, chip, mask):
    px, py, pchip = _other_chip(x, y, mask)
    if gather:
        return _half(src, c), _half(land, c, (chip,)), _half(land, c, (pchip,)), (px, py, c)
    return src.at[pchip], land.at[chip], land.at[pchip], (px, py, c)


def _ici_start(groups, land_groups, gather, *, name, after=None):
    sizes = [len(g) for g in groups]
    n = sum(sizes)
    ng = len(groups)
    deps = [] if after is None else [after]

    def body(*refs):
        src, land = refs[:n], refs[n:2 * n]
        sems = refs[2 * n + len(deps):2 * n + len(deps) + 2 * ng]
        token = refs[-1]
        x, y, c, chip = _place()
        i = 0
        for g in range(ng):
            for j in range(sizes[g]):
                for k, mask in enumerate(MASKS):
                    s, d, _, peer = _ici_ends(src[i], land[i], gather, x, y, c, chip, mask)
                    pltpu.make_async_remote_copy(s, d, sems[2 * g].at[3 * j + k], sems[2 * g + 1].at[3 * j + k],
                                                 device_id=peer, device_id_type=MESH).start()
                i += 1
        token[...] = jnp.zeros_like(token)

    lands = [pltpu.with_memory_space_constraint(lax.empty(s.shape, s.dtype), pltpu.HBM)
             for g in land_groups for s in g]
    srcs = [pltpu.with_memory_space_constraint(s, pltpu.HBM) for g in groups for s in g]
    sem_shapes = [pltpu.SemaphoreType.DMA((3 * m,)) for m in sizes for _ in range(2)]
    out = pl.pallas_call(
        body, name=name,
        out_shape=(*sem_shapes, *[pltpu.HBM(s.shape, s.dtype) for s in srcs],
                   *[pltpu.HBM(s.shape, s.dtype) for s in lands], jax.ShapeDtypeStruct((8, 128), F32)),
        in_specs=[HBM_SPEC] * (2 * n) + [ANY] * len(deps),
        out_specs=(*[SEM_SPEC] * (2 * ng), *[HBM_SPEC] * (2 * n), pl.BlockSpec(memory_space=pltpu.VMEM)),
        input_output_aliases={i: 2 * ng + i for i in range(2 * n)},
        compiler_params=pltpu.CompilerParams(has_side_effects=EFFECT),
    )(*srcs, *lands, *deps)
    res, pos = [], 0
    for g in range(ng):
        res.append((out[2 * g], out[2 * g + 1], list(out[2 * ng + pos:2 * ng + pos + sizes[g]]),
                    list(out[2 * ng + n + pos:2 * ng + n + pos + sizes[g]])))
        pos += sizes[g]
    return res, out[-1]


def _ici_wait(send_sems, recv_sems, srcs, lands, gather, after, *, name):
    n = len(srcs)

    def body(*refs):
        src, land = refs[:n], refs[n:2 * n]
        send_sems, recv_sems = refs[2 * n], refs[2 * n + 1]
        x, y, c, chip = _place()
        for i in range(n):
            for k, mask in enumerate(MASKS):
                s, d, got, peer = _ici_ends(src[i], land[i], gather, x, y, c, chip, mask)
                pltpu.make_async_remote_copy(s, d, send_sems.at[3 * i + k], recv_sems.at[3 * i + k],
                                             device_id=peer, device_id_type=MESH).wait_send()
                pltpu.make_async_remote_copy(s, got, send_sems.at[3 * i + k], recv_sems.at[3 * i + k],
                                             device_id=peer, device_id_type=MESH).wait_recv()

    out = pl.pallas_call(
        body, name=name,
        out_shape=tuple(pltpu.HBM(s.shape, s.dtype) for s in (*srcs, *lands)),
        in_specs=[HBM_SPEC] * (2 * n) + [SEM_SPEC, SEM_SPEC, ANY],
        out_specs=tuple([HBM_SPEC] * (2 * n)),
        input_output_aliases={i: i for i in range(2 * n)},
        compiler_params=pltpu.CompilerParams(has_side_effects=EFFECT),
    )(*srcs, *lands, send_sems, recv_sems, after)
    return list(out[:n]), list(out[n:])


def _relay_halves(gs, *, name):
    n = len(gs)

    def body(*refs):
        out = refs[n:2 * n]
        send_sems, recv_sems = refs[2 * n], refs[2 * n + 1]
        x, y, c, _ = _place()
        _sibling_barrier(x, y, c)
        cps = []
        for i in range(n):
            for k, mask in enumerate(MASKS):
                _, _, pchip = _other_chip(x, y, mask)
                got = _half(out[i], c, (pchip,))
                cp = pltpu.make_async_remote_copy(got, got, send_sems.at[i, k], recv_sems.at[i, k],
                                                  device_id=(x, y, 1 - c), device_id_type=MESH)
                cp.start()
                cps.append(cp)
        for i in range(n):
            for k, mask in enumerate(MASKS):
                _, _, pchip = _other_chip(x, y, mask)
                theirs = _half(out[i], 1 - c, (pchip,))
                pltpu.make_async_remote_copy(theirs, theirs, send_sems.at[i, k], recv_sems.at[i, k],
                                             device_id=(x, y, 1 - c), device_id_type=MESH).wait_recv()
        for cp in cps:
            cp.wait_send()

    return pl.pallas_call(
        body, name=name, in_specs=[ANY] * n, out_specs=[ANY] * n,
        out_shape=[jax.ShapeDtypeStruct(g.shape, g.dtype) for g in gs],
        input_output_aliases={i: i for i in range(n)},
        scratch_shapes=[pltpu.SemaphoreType.DMA((n, 3)), pltpu.SemaphoreType.DMA((n, 3))],
        compiler_params=pltpu.CompilerParams(collective_id=SIBLING_ID),
    )(*gs)


def _share_halves(gs, *, name):
    n = len(gs)

    def body(*refs):
        out = refs[n:2 * n]
        send_sems, recv_sems = refs[2 * n], refs[2 * n + 1]
        x, y, c, _ = _place()
        _sibling_barrier(x, y, c)
        cps = []
        for i in range(n):
            cp = pltpu.make_async_remote_copy(_half(out[i], c), _half(out[i], c), send_sems.at[i], recv_sems.at[i],
                                              device_id=(x, y, 1 - c), device_id_type=MESH)
            cp.start()
            cps.append(cp)
        for i in range(n):
            theirs = _half(out[i], 1 - c)
            pltpu.make_async_remote_copy(theirs, theirs, send_sems.at[i], recv_sems.at[i],
                                         device_id=(x, y, 1 - c), device_id_type=MESH).wait_recv()
        for cp in cps:
            cp.wait_send()

    return pl.pallas_call(
        body, name=name, in_specs=[ANY] * n, out_specs=[ANY] * n,
        out_shape=[jax.ShapeDtypeStruct(g.shape, g.dtype) for g in gs],
        input_output_aliases={i: i for i in range(n)},
        scratch_shapes=[pltpu.SemaphoreType.DMA((n,)), pltpu.SemaphoreType.DMA((n,))],
        compiler_params=pltpu.CompilerParams(collective_id=SIBLING_ID),
    )(*gs)


def _gather_all(buf, *, name):
    r, L = buf.shape
    vmem = pl.BlockSpec(memory_space=pltpu.VMEM)
    masks = tuple(range(1, N_DEV))

    def body(buf_ref, out_ref, send_sems, recv_sems):
        x, y, c, _ = _place()
        me = 4 * x + 2 * y + c
        out_ref[me] = buf_ref[...]
        sends = []
        for k, mask in enumerate(masks):
            px = 1 - x if mask & 4 else x
            py = 1 - y if mask & 2 else y
            pc = 1 - c if mask & 1 else c
            cp = pltpu.make_async_remote_copy(buf_ref, out_ref.at[me], send_sems.at[k], recv_sems.at[k],
                                              device_id=(px, py, pc), device_id_type=MESH)
            cp.start()
            sends.append(cp)
        for k, mask in enumerate(masks):
            px = 1 - x if mask & 4 else x
            py = 1 - y if mask & 2 else y
            pc = 1 - c if mask & 1 else c
            pltpu.make_async_remote_copy(buf_ref, out_ref.at[4 * px + 2 * py + pc], send_sems.at[k],
                                         recv_sems.at[k], device_id=(px, py, pc), device_id_type=MESH).wait_recv()
        for cp in sends:
            cp.wait_send()

    return pl.pallas_call(
        body, name=name, in_specs=[vmem], out_specs=vmem,
        out_shape=jax.ShapeDtypeStruct((N_DEV, r, L), buf.dtype),
        scratch_shapes=[pltpu.SemaphoreType.DMA((N_DEV - 1,)), pltpu.SemaphoreType.DMA((N_DEV - 1,))],
    )(buf)


def _pack(arrs, lanes, row_mult=8):
    flat = jnp.concatenate([a.reshape(-1) for a in arrs])
    rows = -(-flat.shape[0] // lanes)
    rows = -(-rows // row_mult) * row_mult
    flat = jnp.pad(flat, (0, rows * lanes - flat.shape[0]))
    return flat.reshape(rows, lanes)


def _unpack(buf, shapes):
    flat = buf.reshape(-1)
    out, pos = [], 0
    for s in shapes:
        n = 1
        for d in s:
            n *= d
        out.append(flat[pos:pos + n].reshape(s))
        pos += n
    return out


def kernel(x, mem, norm_mix_g, w_in, conv_a_w, conv_b_w, conv_b_bias, ln_b_g, ln_b_b, w_out, norm_x_g, norm_mem_g, w_q, w_kv, w_xo, norm_ffn_g, w_up, w_down, final_g, loss_target, m_norm_mix_g, m_w_in, m_conv_a_w, m_conv_b_w, m_conv_b_bias, m_ln_b_g, m_ln_b_b, m_w_out, m_norm_x_g, m_norm_mem_g, m_w_q, m_w_kv, m_w_xo, m_norm_ffn_g, m_w_up, m_w_down, m_final_g, v_norm_mix_g, v_w_in, v_conv_a_w, v_conv_b_w, v_conv_b_bias, v_ln_b_g, v_ln_b_b, v_w_out, v_norm_x_g, v_norm_mem_g, v_w_q, v_w_kv, v_w_xo, v_norm_ffn_g, v_w_up, v_w_down, v_final_g):
    W = dict(norm_mix_g=norm_mix_g, w_in=w_in, conv_a_w=conv_a_w, conv_b_w=conv_b_w, conv_b_bias=conv_b_bias,
             ln_b_g=ln_b_g, ln_b_b=ln_b_b, w_out=w_out, norm_x_g=norm_x_g, norm_mem_g=norm_mem_g, w_q=w_q,
             w_kv=w_kv, w_xo=w_xo, norm_ffn_g=norm_ffn_g, w_up=w_up, w_down=w_down, final_g=final_g)
    MO = dict(norm_mix_g=m_norm_mix_g, w_in=m_w_in, conv_a_w=m_conv_a_w, conv_b_w=m_conv_b_w,
              conv_b_bias=m_conv_b_bias, ln_b_g=m_ln_b_g, ln_b_b=m_ln_b_b, w_out=m_w_out, norm_x_g=m_norm_x_g,
              norm_mem_g=m_norm_mem_g, w_q=m_w_q, w_kv=m_w_kv, w_xo=m_w_xo, norm_ffn_g=m_norm_ffn_g,
              w_up=m_w_up, w_down=m_w_down, final_g=m_final_g)
    VO = dict(norm_mix_g=v_norm_mix_g, w_in=v_w_in, conv_a_w=v_conv_a_w, conv_b_w=v_conv_b_w,
              conv_b_bias=v_conv_b_bias, ln_b_g=v_ln_b_g, ln_b_b=v_ln_b_b, w_out=v_w_out, norm_x_g=v_norm_x_g,
              norm_mem_g=v_norm_mem_g, w_q=v_w_q, w_kv=v_w_kv, w_xo=v_w_xo, norm_ffn_g=v_norm_ffn_g,
              w_up=v_w_up, w_down=v_w_down, final_g=v_final_g)
    names = list(W.keys())
    depth = norm_mix_g.shape[0]
    assert depth == 2, "the exchange splits the weights into one layer per core of a chip"
    c_idx = lax.axis_index("c")
    chip_idx = 2 * lax.axis_index("x") + lax.axis_index("y")

    xs = x[0]
    ms = mem[0]
    tgt = loss_target[0]
    S, D = xs.shape
    c_a = conv_a_w.shape[-1] * N_CHIPS
    c_loc = conv_a_w.shape[-1]

    meta = jnp.stack([c_idx, chip_idx]).astype(jnp.int32)
    shard_axis = dict(BIG)
    own = {(l, n): W[n][l].astype(BF16) for l in range(depth) for n, _ in BIG}

    conv_local = _pack([conv_a_w, conv_b_w], 128)
    conv_all = _gather_all(conv_local, name="gather_conv_weights")
    order = [(l, gi) for l in range(depth) for gi in range(len(FWD_GROUPS))]
    src_groups = [[own[(l, n)] for n in FWD_GROUPS[gi]] for l, gi in order]
    started, gather_token = _ici_start(
        src_groups, [[jax.ShapeDtypeStruct((N_CHIPS, *s.shape), s.dtype) for s in g] for g in src_groups], True,
        name="gather_weights_start", after=conv_all)
    started = dict(zip(order, started))
    Wb = [dict() for _ in range(depth)]

    def weights_ready(l, gi, after):
        send_sems, recv_sems, srcs, lands = started[(l, gi)]
        srcs, lands = _ici_wait(send_sems, recv_sems, srcs, lands, True, after, name=f"gather_weights_l{l}_g{gi}_wait")
        full = _relay_halves(lands, name=f"gather_weights_l{l}_g{gi}_relay")
        for n, g, o in zip(FWD_GROUPS[gi], full, srcs):
            g = lax.dynamic_update_slice(g, o[None], (chip_idx, 0, 0))
            Wb[l][n] = g.reshape(-1, g.shape[-1]) if shard_axis[n] == 0 else g
    na = depth * K_A * c_loc
    nbw = depth * K_B * c_loc
    ca_parts, cb_parts = [], []
    for j in range(N_CHIPS):
        fl = conv_all[2 * j].reshape(-1)
        ca_parts.append(fl[:na].reshape(depth, K_A, c_loc))
        cb_parts.append(fl[na:na + nbw].reshape(depth, K_B, c_loc))
    conv_a_full = jnp.concatenate(ca_parts, axis=-1)
    conv_b_full = jnp.concatenate(cb_parts, axis=-1)

    saved = []
    h = xs
    for l in range(depth):
        wl = Wb[l]
        t = f"l{l}_"
        trial = dict(bn=1024) if l == 1 else {}
        weights_ready(l, 0, gather_token if l == 0 else h)
        u = _rms_fwd(h, norm_mix_g[l:l + 1], name=t + "rms_mix")
        z = _mm(u, wl["w_in"], b_stack=True, name=t + "mm_in", **(dict(bm=2048) if l == 1 else {}))
        y_a = _mixer_a_fwd(z, conv_a_full[l], name=t + "mixer_a")
        cb = _mixer_b_fwd(z, conv_b_full[l], conv_b_bias[l:l + 1], name=t + "mixer_b")
        y_b = _ln_silu_fwd(cb, ln_b_g[l:l + 1], ln_b_b[l:l + 1], name=t + "ln_silu")
        yy = jnp.concatenate([y_a, y_b], axis=1)
        h2 = _mm(yy, wl["w_out"], res=h, name=t + "mm_out", **trial)
        weights_ready(l, 1, h2)
        q_in = _rms_fwd(h2, norm_x_g[l:l + 1], name=t + "rms_x")
        q = _mm(q_in, wl["w_q"], out_dtype=BF16, name=t + "mm_q", **trial)
        mn = _rms_fwd(ms, norm_mem_g[l:l + 1], name=t + "rms_mem")
        kv = _mm(mn, wl["w_kv"], b_stack=True, out_dtype=BF16, name=t + "mm_kv")
        o = _attn_fwd(q, kv, name=t + "attn")
        h3 = _mm(o, wl["w_xo"], res=h2, name=t + "mm_xo", **trial)
        weights_ready(l, 2, h3)
        u3 = _rms_fwd(h3, norm_ffn_g[l:l + 1], name=t + "rms_ffn")
        a_pre, hh = _mm(u3, wl["w_up"], b_stack=True, out_dtype=BF16, epi="sqrelu", name=t + "mm_up", **trial)
        h4 = _mm(hh, wl["w_down"], res=h3, name=t + "mm_down", bn=1024)
        saved.append(dict(h=h, u=u, z=z, cb=cb, yy=yy, h2=h2, q_in=q_in, q=q, mn=mn, kv=kv, o=o, h3=h3,
                          u3=u3, a_pre=a_pre, hh=hh))
        h = h4

    loss_vec, dh, dhb, d_final = _loss_head(h, final_g.reshape(1, D), tgt, name="loss_head")
    loss = lax.psum(loss_vec[0, 0], ("x", "y", "c"))

    GW = [dict() for _ in range(depth)]
    GS = [dict() for _ in range(depth)]
    pending = []

    def reduce_start(l, gi):
        group = BWD_GROUPS[gi]
        gs = [GW[l][n] if GW[l][n].ndim == 3 else GW[l][n].reshape(N_CHIPS, *W[n].shape[1:]) for n in group]
        from_sibling = _swap_halves(gs, name=f"grad_swap_sibling_l{l}_g{gi}")
        prs = [_pair_sum(g, r, meta, name=f"grad_pair_sum_l{l}_{n}") for g, r, n in zip(gs, from_sibling, group)]
        (st,), token = _ici_start([prs], [prs], False, name=f"grad_scatter_chips_l{l}_g{gi}_start")
        pending.append((l, group, st))
        return token

    for l in reversed(range(depth)):
        wl, sv = Wb[l], saved[l]
        t = f"l{l}_b_"
        wide = dict(bn=1024)
        trial = dict(bn=1024) if l == 1 else {}
        GW[l]["w_down"] = _mm(sv["hh"], dhb, ta=True, name=t + "dw_down", **wide)
        da = _mm(dhb, wl["w_down"], tb=True, out_dtype=BF16, epi="dsqrelu", aux=sv["a_pre"], name=t + "d_hidden",
                 **trial)
        GW[l]["w_up"] = _mm(sv["u3"], da, ta=True, o_stack=N_CHIPS, name=t + "dw_up", **wide)
        du3 = _mm(da, wl["w_up"], tb=True, b_stack=True, name=t + "d_u3", **wide)
        dh, dhb, GS[l]["norm_ffn_g"] = _rms_bwd(sv["h3"], norm_ffn_g[l:l + 1], du3, dh, name=t + "rms_ffn",
                                                after=reduce_start(l, 0))
        GW[l]["w_xo"] = _mm(sv["o"], dhb, ta=True, name=t + "dw_xo", **wide)
        d_o = _mm(dhb, wl["w_xo"], tb=True, out_dtype=BF16, name=t + "d_o", **trial)
        dq, dkv = _attn_bwd(sv["q"], sv["kv"], d_o, name=t + "attn")
        GW[l]["w_q"] = _mm(sv["q_in"], dq, ta=True, name=t + "dw_q", **wide)
        dq_in = _mm(dq, wl["w_q"], tb=True, name=t + "d_q_in", **trial)
        dkvb = dkv.astype(BF16)
        GW[l]["w_kv"] = _mm(sv["mn"], dkvb, ta=True, o_stack=N_CHIPS, name=t + "dw_kv")
        dmn = _mm(dkvb, wl["w_kv"], tb=True, b_stack=True, name=t + "d_mem")
        _, _, GS[l]["norm_mem_g"] = _rms_bwd(ms, norm_mem_g[l:l + 1], dmn, None, name=t + "rms_mem")
        dh, dhb, GS[l]["norm_x_g"] = _rms_bwd(sv["h2"], norm_x_g[l:l + 1], dq_in, dh, name=t + "rms_x",
                                              after=reduce_start(l, 1))
        GW[l]["w_out"] = _mm(sv["yy"], dhb, ta=True, name=t + "dw_out", **wide)
        dyy = _mm(dhb, wl["w_out"], tb=True, name=t + "d_y", **trial)
        dcb, GS[l]["ln_b_g"], GS[l]["ln_b_b"] = _ln_silu_bwd(sv["cb"], ln_b_g[l:l + 1], ln_b_b[l:l + 1], dyy, 1,
                                                             name=t + "ln_silu")
        db_, dc_, dh_, GS[l]["conv_a_w"] = _mixer_a_bwd(sv["z"], conv_a_full[l], dyy, name=t + "mixer_a")
        dv_, dg_, GS[l]["conv_b_w"], GS[l]["conv_b_bias"] = _mixer_b_bwd(sv["z"], conv_b_full[l], dcb,
                                                                         name=t + "mixer_b")
        dz = jnp.concatenate([db_, dc_, dh_, dv_, dg_], axis=1)
        GW[l]["w_in"] = _mm(sv["u"], dz, ta=True, o_stack=N_CHIPS, name=t + "dw_in")
        du = _mm(dz, wl["w_in"], tb=True, b_stack=True, name=t + "d_u", **wide)
        dh, dhb, GS[l]["norm_mix_g"] = _rms_bwd(sv["h"], norm_mix_g[l:l + 1], du, dh, name=t + "rms_mix",
                                                after=reduce_start(l, 2))
    grad_x = dh[None]

    after = GS[0]["norm_mix_g"]
    keys, halves = [], []
    for l, group, (send_sems, recv_sems, srcs, lands) in pending:
        prs, pieces = _ici_wait(send_sems, recv_sems, srcs, lands, False, after,
                                name=f"grad_scatter_chips_l{l}_{group[0]}_wait")
        for n, p, q in zip(group, prs, pieces):
            keys.append((l, n))
            halves.append(_chip_sum(p, q, meta, name=f"grad_chip_sum_l{l}_{n}"))
        after = halves[-1]
    reduced = dict(zip(keys, _share_halves(halves, name="grad_share_sibling")))

    grads, deltas, new_m, new_v = {}, {}, {}, {}
    for n, _ in BIG:
        grads[n], deltas[n], new_m[n], new_v[n] = _adamw_layers(W[n], reduced[(0, n)], reduced[(1, n)], MO[n], VO[n],
                                                               name="adamw_" + n)

    small = [n for n in names if n not in dict(BIG)]
    full_shapes = {n: ((depth, W[n].shape[1], c_a) if n in ("conv_a_w", "conv_b_w") else W[n].shape)
                   for n in small}

    def small_grad(n):
        if n == "final_g":
            return d_final.reshape(W[n].shape)
        return jnp.stack([GS[l][n].reshape(full_shapes[n][1:]) for l in range(depth)])

    part = _pack([small_grad(n) for n in small], LANES)
    everyone = _gather_all(part, name="gather_small_grads")
    total = _sum_leading(everyone, name="small_grad_sum")
    full_grads = dict(zip(small, _unpack(total, [full_shapes[n] for n in small])))
    for n in ("conv_a_w", "conv_b_w"):
        full_grads[n] = lax.dynamic_slice_in_dim(full_grads[n], chip_idx * c_loc, c_loc, axis=2)
    shapes = [W[n].shape for n in small]
    d_s, m_s, v_s = _adamw(_pack([W[n] for n in small], 128), _pack([full_grads[n] for n in small], 128),
                           _pack([MO[n] for n in small], 128), _pack([VO[n] for n in small], 128),
                           name="adamw_small")
    for n, d, nm, nv in zip(small, _unpack(d_s, shapes), _unpack(m_s, shapes), _unpack(v_s, shapes)):
        grads[n], deltas[n], new_m[n], new_v[n] = full_grads[n], d, nm, nv

    return (loss, grad_x, *[grads[n] for n in names], *[deltas[n] for n in names],
            *[new_m[n] for n in names], *[new_v[n] for n in names])
```

```python
import jax
import jax.numpy as jnp
from jax import lax
from jax.experimental import pallas as pl
from jax.experimental.pallas import tpu as pltpu

F32 = jnp.float32
BF16 = jnp.bfloat16
MESH = pl.DeviceIdType.MESH

EPS = 1e-6
N_XHEADS = 4
K_A = 3
K_B = 31
PAD_A = 8
PAD_B = 32
CONV_CHUNK = 256
ROW_TILE = 512
LANES = 1024
VMEM_LIMIT_BYTES = 56 * 1024 * 1024

ADAM_LR = 0.001
ADAM_B1 = 0.9
ADAM_B2 = 0.999
ADAM_EPS = 1e-08
ADAM_WD = 0.01
ADAM_STEP = 10

BIG = (("w_in", 1), ("w_out", 0), ("w_q", 0), ("w_kv", 1), ("w_xo", 0), ("w_up", 1), ("w_down", 0))
FWD_GROUPS = (("w_in", "w_out"), ("w_q", "w_kv", "w_xo"), ("w_up", "w_down"))
BWD_GROUPS = (("w_down", "w_up"), ("w_xo", "w_q", "w_kv"), ("w_out", "w_in"))
N_CHIPS = 4
N_DEV = 8


def _params(sem=None):
    return pltpu.CompilerParams(dimension_semantics=sem, vmem_limit_bytes=VMEM_LIMIT_BYTES)


def _pick(cands, n):
    for c in cands:
        if c <= n and n % c == 0:
            return c
    return n


def _mm(a, b, *, name, ta=False, tb=False, out_dtype=F32, res=None, epi=None, aux=None, norm=None, after=None,
        b_stack=False, o_stack=0, bm=1024, bn=1024, bk=1024):
    if ta:
        K, M = a.shape
    else:
        M, K = a.shape
    if b_stack:
        n_st, d1, d2 = b.shape
        N, kb = (d1, d2) if tb else (n_st * d2, d1)
        assert K == (n_st * d2 if tb else d1), (name, a.shape, b.shape)
    else:
        N = b.shape[0] if tb else b.shape[1]
    n_unit = b.shape[2] if (b_stack and not tb) else (N // o_stack if o_stack else N)
    k_unit = b.shape[2] if (b_stack and tb) else K
    bm = _pick((bm, 512, 256, 128), M)
    bn = _pick((bn, 512, 640, 256, 384, 128), n_unit)
    bk = _pick((bk, 640, 512, 256, 128), k_unit)
    assert M % bm == 0 and N % bn == 0 and K % bk == 0, (name, M, N, K)
    nk = K // bk
    per_n = n_unit // bn
    per_k = k_unit // bk
    a_spec = (pl.BlockSpec((bk, bm), lambda i, j, k: (k, i)) if ta
              else pl.BlockSpec((bm, bk), lambda i, j, k: (i, k)))
    if b_stack and tb:
        b_spec = pl.BlockSpec((None, bn, bk), lambda i, j, k: (k // per_k, j, k % per_k))
    elif b_stack:
        b_spec = pl.BlockSpec((None, bk, bn), lambda i, j, k: (j // per_n, k, j % per_n))
    elif tb:
        b_spec = pl.BlockSpec((bn, bk), lambda i, j, k: (j, k))
    else:
        b_spec = pl.BlockSpec((bk, bn), lambda i, j, k: (k, j))
    o_spec = pl.BlockSpec((bm, bn), lambda i, j, k: (i, j))
    dims = (((0 if ta else 1,), (1 if tb else 0,)), ((), ()))
    ins, in_specs = [a, b], [a_spec, b_spec]
    if res is not None:
        ins.append(res)
        in_specs.append(o_spec)
    if aux is not None:
        ins.append(aux)
        in_specs.append(o_spec)
    vec_spec = pl.BlockSpec((1, bn), lambda i, j, k: (0, j))
    if epi == "rms_fwd":
        assert bn == N, (name, bn, N)
        ins.append(norm)
        in_specs.append(vec_spec)
    elif epi == "rms_bwd":
        assert bn == N, (name, bn, N)
        ins += [norm[0], norm[1]]
        in_specs += [o_spec, vec_spec]
    n_norm = {"rms_fwd": 1, "rms_bwd": 2}.get(epi, 0)
    if after is not None:
        ins.append(after)
        in_specs.append(pl.BlockSpec(memory_space=pl.ANY))
    n_out = {"sqrelu": 2, "rms_fwd": 2, "rms_bwd": 3}.get(epi, 1)
    out_shape = [jax.ShapeDtypeStruct((M, N), out_dtype)] * n_out
    out_specs = [o_spec] * n_out
    if epi == "rms_fwd":
        out_shape = [jax.ShapeDtypeStruct((M, N), F32), jax.ShapeDtypeStruct((M, N), BF16)]
    elif epi == "rms_bwd":
        out_shape = [jax.ShapeDtypeStruct((M, N), F32), jax.ShapeDtypeStruct((M, N), BF16),
                     jax.ShapeDtypeStruct((1, N), F32)]
        out_specs = [o_spec, o_spec, vec_spec]
    if o_stack:
        assert n_out == 1 and res is None and aux is None
        out_shape = [jax.ShapeDtypeStruct((o_stack, M, N // o_stack), out_dtype)]
        out_specs = [pl.BlockSpec((None, bm, bn), lambda i, j, k: (j // per_n, i, j % per_n))]

    def body(*refs):
        a_ref, b_ref = refs[0], refs[1]
        pos = 2
        res_ref = aux_ref = None
        if res is not None:
            res_ref = refs[pos]
            pos += 1
        if aux is not None:
            aux_ref = refs[pos]
            pos += 1
        norm_refs = refs[pos:pos + n_norm]
        pos += n_norm + (after is not None)
        outs = refs[pos:pos + n_out]

        def product():
            return lax.dot_general(a_ref[...], b_ref[...], dims, preferred_element_type=F32)

        def finish(r):
            if epi == "rms_bwd":
                x_ref, g_ref = norm_refs
                xv = x_ref[...]
                rs = lax.rsqrt(jnp.mean(xv * xv, axis=-1, keepdims=True) + EPS)
                xh = xv * rs
                dxh = r * g_ref[...]
                dx = rs * (dxh - xh * jnp.mean(dxh * xh, axis=-1, keepdims=True))
                if res_ref is not None:
                    dx = dx + res_ref[...]
                outs[0][...] = dx
                outs[1][...] = dx.astype(BF16)

                @pl.when(pl.program_id(0) == 0)
                def _():
                    outs[2][...] = jnp.zeros_like(outs[2])

                outs[2][...] += jnp.sum(r * xh, axis=0, keepdims=True)
                return
            if res_ref is not None:
                r = r + res_ref[...]
            if epi == "rms_fwd":
                outs[0][...] = r
                rs = lax.rsqrt(jnp.mean(r * r, axis=-1, keepdims=True) + EPS)
                outs[1][...] = (r * rs * norm_refs[0][...]).astype(BF16)
            elif epi == "sqrelu":
                outs[0][...] = r.astype(out_dtype)
                rl = jnp.maximum(r, 0.0)
                outs[1][...] = (rl * rl).astype(out_dtype)
            elif epi == "dsqrelu":
                outs[0][...] = (r * (2.0 * jnp.maximum(aux_ref[...].astype(F32), 0.0))).astype(out_dtype)
            else:
                outs[0][...] = r.astype(out_dtype)

        if nk == 1:
            finish(product())
            return
        acc = refs[pos + n_out]
        k = pl.program_id(2)

        @pl.when(k == 0)
        def _():
            acc[...] = product()

        @pl.when(jnp.logical_and(k > 0, k < nk - 1))
        def _():
            acc[...] += product()

        @pl.when(k == nk - 1)
        def _():
            finish(acc[...] + product())

    out = pl.pallas_call(
        body, name=name, grid=(M // bm, N // bn, nk),
        in_specs=in_specs, out_specs=out_specs, out_shape=out_shape,
        scratch_shapes=[pltpu.VMEM((bm, bn), F32)] if nk > 1 else [],
        compiler_params=_params(("arbitrary",) * 3 if epi == "rms_bwd" else ("parallel", "parallel", "arbitrary")),
    )(*ins)
    return out if n_out > 1 else out[0]


def _row_tile(rows):
    return min(ROW_TILE, rows)


def _rms_fwd(x, g, *, name, after=None):
    S, D = x.shape
    tr = _row_tile(S)

    def body(x_ref, g_ref, *rest):
        o_ref = rest[-1]
        xv = x_ref[...]
        r = lax.rsqrt(jnp.mean(xv * xv, axis=-1, keepdims=True) + EPS)
        o_ref[...] = (xv * r * g_ref[...]).astype(BF16)

    deps = [] if after is None else [after]
    return pl.pallas_call(
        body, name=name, grid=(S // tr,),
        in_specs=[pl.BlockSpec((tr, D), lambda i: (i, 0)), pl.BlockSpec((1, D), lambda i: (0, 0))]
        + [ANY] * len(deps),
        out_specs=pl.BlockSpec((tr, D), lambda i: (i, 0)),
        out_shape=jax.ShapeDtypeStruct((S, D), BF16),
        compiler_params=_params(("parallel",)),
    )(x, g, *deps)


def _rms_bwd(x, g, du, dres, *, name, after=None):
    S, D = x.shape
    tr = _row_tile(S)
    has_res = dres is not None
    deps = [] if after is None else [after]

    def body(*refs):
        dx_ref, dxb_ref, dg_ref = refs[-3:]
        if has_res:
            x_ref, g_ref, du_ref, dres_ref = refs[:4]
        else:
            x_ref, g_ref, du_ref = refs[:3]
        xv = x_ref[...]
        r = lax.rsqrt(jnp.mean(xv * xv, axis=-1, keepdims=True) + EPS)
        xh = xv * r
        dy = du_ref[...]
        dxh = dy * g_ref[...]
        dx = r * (dxh - xh * jnp.mean(dxh * xh, axis=-1, keepdims=True))
        if has_res:
            dx = dx + dres_ref[...]
        dx_ref[...] = dx
        dxb_ref[...] = dx.astype(BF16)

        @pl.when(pl.program_id(0) == 0)
        def _():
            dg_ref[...] = jnp.zeros_like(dg_ref)

        dg_ref[...] += jnp.sum(dy * xh, axis=0, keepdims=True)

    row = pl.BlockSpec((tr, D), lambda i: (i, 0))
    vec = pl.BlockSpec((1, D), lambda i: (0, 0))
    ins = [x, g, du] + ([dres] if has_res else []) + deps
    in_specs = [row, vec, row] + ([row] if has_res else []) + [ANY] * len(deps)
    return pl.pallas_call(
        body, name=name, grid=(S // tr,),
        in_specs=in_specs, out_specs=[row, row, vec],
        out_shape=[jax.ShapeDtypeStruct((S, D), F32), jax.ShapeDtypeStruct((S, D), BF16),
                   jax.ShapeDtypeStruct((1, D), F32)],
        compiler_params=_params(("arbitrary",)),
    )(*ins)


def _loss_head(h, g, target, *, name):
    S, D = h.shape
    tr = _row_tile(S)

    def body(x_ref, g_ref, t_ref, loss_ref, dx_ref, dxb_ref, dg_ref):
        xv = x_ref[...]
        r = lax.rsqrt(jnp.mean(xv * xv, axis=-1, keepdims=True) + EPS)
        xh = xv * r
        gv = g_ref[...]
        err = xh * gv - t_ref[...]
        part = 0.5 * jnp.sum(jnp.mean(err * err, axis=-1, keepdims=True), axis=0, keepdims=True)
        dy = err * (1.0 / D)
        dxh = dy * gv
        dx = r * (dxh - xh * jnp.mean(dxh * xh, axis=-1, keepdims=True))
        dx_ref[...] = dx
        dxb_ref[...] = dx.astype(BF16)

        @pl.when(pl.program_id(0) == 0)
        def _():
            dg_ref[...] = jnp.zeros_like(dg_ref)
            loss_ref[...] = jnp.zeros_like(loss_ref)

        dg_ref[...] += jnp.sum(dy * xh, axis=0, keepdims=True)
        loss_ref[...] += jnp.broadcast_to(part, loss_ref.shape)

    row = pl.BlockSpec((tr, D), lambda i: (i, 0))
    vec = pl.BlockSpec((1, D), lambda i: (0, 0))
    return pl.pallas_call(
        body, name=name, grid=(S // tr,),
        in_specs=[row, vec, row],
        out_specs=[pl.BlockSpec((1, 128), lambda i: (0, 0)), row, row, vec],
        out_shape=[jax.ShapeDtypeStruct((1, 128), F32), jax.ShapeDtypeStruct((S, D), F32),
                   jax.ShapeDtypeStruct((S, D), BF16), jax.ShapeDtypeStruct((1, D), F32)],
        compiler_params=_params(("arbitrary",)),
    )(h, g, target)


def _sigmoid(x):
    return 1.0 / (1.0 + jnp.exp(-x))


def _ln_silu_fwd(cb, g, b, *, name):
    S, C = cb.shape
    tr = _row_tile(S)

    def body(x_ref, g_ref, b_ref, o_ref):
        xv = x_ref[...]
        mu = jnp.mean(xv, axis=-1, keepdims=True)
        xc = xv - mu
        rs = lax.rsqrt(jnp.mean(xc * xc, axis=-1, keepdims=True) + EPS)
        l = xc * rs * g_ref[...] + b_ref[...]
        o_ref[...] = (l * _sigmoid(l)).astype(BF16)

    row = pl.BlockSpec((tr, C), lambda i: (i, 0))
    vec = pl.BlockSpec((1, C), lambda i: (0, 0))
    return pl.pallas_call(
        body, name=name, grid=(S // tr,), in_specs=[row, vec, vec], out_specs=row,
        out_shape=jax.ShapeDtypeStruct((S, C), BF16), compiler_params=_params(("parallel",)),
    )(cb, g, b)


def _ln_silu_bwd(cb, g, b, dy, col_block, *, name):
    S, C = cb.shape
    tr = _row_tile(S)

    def body(x_ref, g_ref, b_ref, dy_ref, dx_ref, dg_ref, db_ref):
        xv = x_ref[...]
        mu = jnp.mean(xv, axis=-1, keepdims=True)
        xc = xv - mu
        rs = lax.rsqrt(jnp.mean(xc * xc, axis=-1, keepdims=True) + EPS)
        xh = xc * rs
        gv = g_ref[...]
        l = xh * gv + b_ref[...]
        sg = _sigmoid(l)
        dl = dy_ref[...] * (sg + l * sg * (1.0 - sg))
        dxh = dl * gv
        dx_ref[...] = rs * (dxh - jnp.mean(dxh, axis=-1, keepdims=True)
                            - xh * jnp.mean(dxh * xh, axis=-1, keepdims=True))

        @pl.when(pl.program_id(0) == 0)
        def _():
            dg_ref[...] = jnp.zeros_like(dg_ref)
            db_ref[...] = jnp.zeros_like(db_ref)

        dg_ref[...] += jnp.sum(dl * xh, axis=0, keepdims=True)
        db_ref[...] += jnp.sum(dl, axis=0, keepdims=True)

    row = pl.BlockSpec((tr, C), lambda i: (i, 0))
    vec = pl.BlockSpec((1, C), lambda i: (0, 0))
    return pl.pallas_call(
        body, name=name, grid=(S // tr,),
        in_specs=[row, vec, vec, pl.BlockSpec((tr, C), lambda i: (i, col_block))],
        out_specs=[row, vec, vec],
        out_shape=[jax.ShapeDtypeStruct((S, C), F32), jax.ShapeDtypeStruct((1, C), F32),
                   jax.ShapeDtypeStruct((1, C), F32)],
        compiler_params=_params(("arbitrary",)),
    )(cb, g, b, dy)


def _attn_fwd(q, kv, *, name):
    S, D = q.shape
    M = kv.shape[0]
    hd = D // N_XHEADS
    scale = 1.0 / float(hd) ** 0.5
    tq = _row_tile(S)

    def body(q_ref, k_ref, v_ref, o_ref):
        for h in range(N_XHEADS):
            cols = slice(h * hd, (h + 1) * hd)
            s = lax.dot_general(q_ref[:, cols], k_ref[:, cols], (((1,), (1,)), ((), ())),
                                preferred_element_type=F32) * scale
            e = jnp.exp(s - jnp.max(s, axis=-1, keepdims=True))
            p = e / jnp.sum(e, axis=-1, keepdims=True)
            o = jnp.dot(p.astype(BF16), v_ref[:, cols], preferred_element_type=F32)
            o_ref[:, cols] = o.astype(BF16)

    return pl.pallas_call(
        body, name=name, grid=(S // tq,),
        in_specs=[pl.BlockSpec((tq, D), lambda i: (i, 0)), pl.BlockSpec((M, D), lambda i: (0, 0)),
                  pl.BlockSpec((M, D), lambda i: (0, 1))],
        out_specs=pl.BlockSpec((tq, D), lambda i: (i, 0)),
        out_shape=jax.ShapeDtypeStruct((S, D), BF16), compiler_params=_params(("parallel",)),
    )(q, kv, kv)


def _attn_bwd(q, kv, do, *, name):
    S, D = q.shape
    M = kv.shape[0]
    hd = D // N_XHEADS
    scale = 1.0 / float(hd) ** 0.5
    tq = _row_tile(S)

    def body(q_ref, k_ref, v_ref, do_ref, dq_ref, dkv_ref):
        @pl.when(pl.program_id(0) == 0)
        def _():
            dkv_ref[...] = jnp.zeros_like(dkv_ref)

        for h in range(N_XHEADS):
            cols = slice(h * hd, (h + 1) * hd)
            vcols = slice(D + h * hd, D + (h + 1) * hd)
            qh, kh, vh, doh = q_ref[:, cols], k_ref[:, cols], v_ref[:, cols], do_ref[:, cols]
            s = lax.dot_general(qh, kh, (((1,), (1,)), ((), ())), preferred_element_type=F32) * scale
            e = jnp.exp(s - jnp.max(s, axis=-1, keepdims=True))
            p = e / jnp.sum(e, axis=-1, keepdims=True)
            pb = p.astype(BF16)
            dp = lax.dot_general(doh, vh, (((1,), (1,)), ((), ())), preferred_element_type=F32)
            ds = (p * (dp - jnp.sum(dp * p, axis=-1, keepdims=True)) * scale).astype(BF16)
            dq_ref[:, cols] = jnp.dot(ds, kh, preferred_element_type=F32).astype(BF16)
            dkv_ref[:, cols] += lax.dot_general(ds, qh, (((0,), (0,)), ((), ())), preferred_element_type=F32)
            dkv_ref[:, vcols] += lax.dot_general(pb, doh, (((0,), (0,)), ((), ())), preferred_element_type=F32)

    row = pl.BlockSpec((tq, D), lambda i: (i, 0))
    return pl.pallas_call(
        body, name=name, grid=(S // tq,),
        in_specs=[row, pl.BlockSpec((M, D), lambda i: (0, 0)), pl.BlockSpec((M, D), lambda i: (0, 1)), row],
        out_specs=[row, pl.BlockSpec((M, 2 * D), lambda i: (0, 0))],
        out_shape=[jax.ShapeDtypeStruct((S, D), BF16), jax.ShapeDtypeStruct((M, 2 * D), F32)],
        compiler_params=_params(("arbitrary",)),
    )(q, kv, kv, do)


def _delayed(win, j, pad):
    return (win if j == 0 else pltpu.roll(win, j, 0))[pad:, :]


def _advanced(win, j, ch):
    return (win if j == 0 else pltpu.roll(win, win.shape[0] - j, 0))[:ch, :]


def _mixer_a_fwd(z, w, *, name):
    S = z.shape[0]
    C = w.shape[1]
    nb = C // 128
    ch = min(CONV_CHUNK, S)

    def body(b_ref, c_ref, h_ref, w_ref, y_ref, xp):
        xp[0:PAD_A, :] = jnp.zeros((PAD_A, 128), F32)
        xp[PAD_A:, :] = c_ref[...] * h_ref[...]

        def chunk(i, carry):
            base = pl.multiple_of(i * ch, ch)
            win = xp[pl.ds(base, ch + PAD_A), :]
            acc = _delayed(win, 0, PAD_A) * w_ref[K_A - 1:K_A, :]
            for j in range(1, K_A):
                acc = acc + _delayed(win, j, PAD_A) * w_ref[K_A - 1 - j:K_A - j, :]
            y_ref[pl.ds(base, ch), :] = (b_ref[pl.ds(base, ch), :] * acc).astype(BF16)
            return carry

        lax.fori_loop(0, S // ch, chunk, 0)

    def col(g):
        return pl.BlockSpec((S, 128), lambda j: (0, g * nb + j))

    return pl.pallas_call(
        body, name=name, grid=(nb,),
        in_specs=[col(0), col(1), col(2), pl.BlockSpec((K_A, 128), lambda j: (0, j))],
        out_specs=pl.BlockSpec((S, 128), lambda j: (0, j)),
        out_shape=jax.ShapeDtypeStruct((S, C), BF16),
        scratch_shapes=[pltpu.VMEM((PAD_A + S, 128), F32)],
        compiler_params=_params(("parallel",)),
    )(z, z, z, w)


def _mixer_a_bwd(z, w, dy, *, name):
    S = z.shape[0]
    C = w.shape[1]
    nb = C // 128
    ch = min(CONV_CHUNK, S)

    def body(b_ref, c_ref, h_ref, w_ref, dy_ref, db_ref, dc_ref, dh_ref, dw_ref, xp, dp):
        xp[0:PAD_A, :] = jnp.zeros((PAD_A, 128), F32)
        xp[PAD_A:, :] = c_ref[...] * h_ref[...]
        dp[S:, :] = jnp.zeros((PAD_A, 128), F32)
        dw_ref[...] = jnp.zeros_like(dw_ref)

        def chunk(i, carry):
            base = pl.multiple_of(i * ch, ch)
            win = xp[pl.ds(base, ch + PAD_A), :]
            dya = dy_ref[pl.ds(base, ch), :]
            dcv = dya * b_ref[pl.ds(base, ch), :]
            dp[pl.ds(base, ch), :] = dcv
            acc = None
            for j in range(K_A):
                xs = _delayed(win, j, PAD_A)
                k = K_A - 1 - j
                term = xs * w_ref[k:k + 1, :]
                acc = term if acc is None else acc + term
                dw_ref[k:k + 1, :] += jnp.sum(dcv * xs, axis=0, keepdims=True)
            db_ref[pl.ds(base, ch), :] = (dya * acc).astype(BF16)
            return carry

        lax.fori_loop(0, S // ch, chunk, 0)

        def chunk2(i, carry):
            base = pl.multiple_of(i * ch, ch)
            win = dp[pl.ds(base, ch + PAD_A), :]
            acc = None
            for j in range(K_A):
                term = _advanced(win, j, ch) * w_ref[K_A - 1 - j:K_A - j, :]
                acc = term if acc is None else acc + term
            dc_ref[pl.ds(base, ch), :] = (acc * h_ref[pl.ds(base, ch), :]).astype(BF16)
            dh_ref[pl.ds(base, ch), :] = (acc * c_ref[pl.ds(base, ch), :]).astype(BF16)
            return carry

        lax.fori_loop(0, S // ch, chunk2, 0)

    def col(g):
        return pl.BlockSpec((S, 128), lambda j: (0, g * nb + j))

    out_col = pl.BlockSpec((S, 128), lambda j: (0, j))
    wspec = pl.BlockSpec((K_A, 128), lambda j: (0, j))
    return pl.pallas_call(
        body, name=name, grid=(nb,),
        in_specs=[col(0), col(1), col(2), wspec, out_col],
        out_specs=[out_col, out_col, out_col, wspec],
        out_shape=[jax.ShapeDtypeStruct((S, C), BF16)] * 3 + [jax.ShapeDtypeStruct((K_A, C), F32)],
        scratch_shapes=[pltpu.VMEM((PAD_A + S, 128), F32), pltpu.VMEM((S + PAD_A, 128), F32)],
        compiler_params=_params(("parallel",)),
    )(z, z, z, w, dy)


def _mixer_b_fwd(z, w, bias, *, name):
    S = z.shape[0]
    C = w.shape[1]
    nb = C // 128
    ch = min(CONV_CHUNK, S)

    def body(v_ref, g_ref, w_ref, bias_ref, cb_ref, xp):
        xp[0:PAD_B, :] = jnp.zeros((PAD_B, 128), F32)
        xp[PAD_B:, :] = v_ref[...] * _sigmoid(g_ref[...])

        def chunk(i, carry):
            base = pl.multiple_of(i * ch, ch)
            win = xp[pl.ds(base, ch + PAD_B), :]
            acc = None
            for j in range(K_B):
                term = _delayed(win, j, PAD_B) * w_ref[K_B - 1 - j:K_B - j, :]
                acc = term if acc is None else acc + term
            cb_ref[pl.ds(base, ch), :] = acc + bias_ref[...]
            return carry

        lax.fori_loop(0, S // ch, chunk, 0)

    def col(g):
        return pl.BlockSpec((S, 128), lambda j: (0, g * nb + j))

    return pl.pallas_call(
        body, name=name, grid=(nb,),
        in_specs=[col(3), col(4), pl.BlockSpec((K_B, 128), lambda j: (0, j)),
                  pl.BlockSpec((1, 128), lambda j: (0, j))],
        out_specs=pl.BlockSpec((S, 128), lambda j: (0, j)),
        out_shape=jax.ShapeDtypeStruct((S, C), F32),
        scratch_shapes=[pltpu.VMEM((PAD_B + S, 128), F32)],
        compiler_params=_params(("parallel",)),
    )(z, z, w, bias)


def _mixer_b_bwd(z, w, dcb, *, name):
    S = z.shape[0]
    C = w.shape[1]
    nb = C // 128
    ch = min(CONV_CHUNK, S)

    def body(v_ref, g_ref, w_ref, dcb_ref, dv_ref, dg_ref, dw_ref, dbias_ref, xp, dp):
        xp[0:PAD_B, :] = jnp.zeros((PAD_B, 128), F32)
        xp[PAD_B:, :] = v_ref[...] * _sigmoid(g_ref[...])
        dp[0:S, :] = dcb_ref[...]
        dp[S:, :] = jnp.zeros((PAD_B, 128), F32)
        dw_ref[...] = jnp.zeros_like(dw_ref)
        dbias_ref[...] = jnp.sum(dcb_ref[...], axis=0, keepdims=True)

        def chunk(i, carry):
            base = pl.multiple_of(i * ch, ch)
            win = xp[pl.ds(base, ch + PAD_B), :]
            d = dcb_ref[pl.ds(base, ch), :]
            for j in range(K_B):
                k = K_B - 1 - j
                dw_ref[k:k + 1, :] += jnp.sum(d * _delayed(win, j, PAD_B), axis=0, keepdims=True)
            return carry

        lax.fori_loop(0, S // ch, chunk, 0)

        def chunk2(i, carry):
            base = pl.multiple_of(i * ch, ch)
            win = dp[pl.ds(base, ch + PAD_B), :]
            acc = None
            for j in range(K_B):
                term = _advanced(win, j, ch) * w_ref[K_B - 1 - j:K_B - j, :]
                acc = term if acc is None else acc + term
            sg = _sigmoid(g_ref[pl.ds(base, ch), :])
            vv = v_ref[pl.ds(base, ch), :]
            dv_ref[pl.ds(base, ch), :] = (acc * sg).astype(BF16)
            dg_ref[pl.ds(base, ch), :] = (acc * vv * sg * (1.0 - sg)).astype(BF16)
            return carry

        lax.fori_loop(0, S // ch, chunk2, 0)

    def col(g):
        return pl.BlockSpec((S, 128), lambda j: (0, g * nb + j))

    out_col = pl.BlockSpec((S, 128), lambda j: (0, j))
    wspec = pl.BlockSpec((K_B, 128), lambda j: (0, j))
    bspec = pl.BlockSpec((1, 128), lambda j: (0, j))
    return pl.pallas_call(
        body, name=name, grid=(nb,),
        in_specs=[col(3), col(4), wspec, out_col],
        out_specs=[out_col, out_col, wspec, bspec],
        out_shape=[jax.ShapeDtypeStruct((S, C), BF16)] * 2
        + [jax.ShapeDtypeStruct((K_B, C), F32), jax.ShapeDtypeStruct((1, C), F32)],
        scratch_shapes=[pltpu.VMEM((PAD_B + S, 128), F32), pltpu.VMEM((S + PAD_B, 128), F32)],
        compiler_params=_params(("parallel",)),
    )(z, z, w, dcb)


def _ew_tile(R):
    for t in (512, 256, 128, 64, 32, 16, 8):
        if R % t == 0:
            return t
    return R


def _pair_sum(g, r, meta, *, name):
    n, a, b = g.shape
    ah = a // 2
    tr = _pick((256, 128, 64, 32, 16), ah)
    nh = ah // tr

    def body(meta_ref, g_ref, r_ref, o_ref):
        o_ref[...] = (g_ref[...] + r_ref[...]).astype(BF16)

    half = pl.BlockSpec((None, tr, b), lambda j, i, meta_ref: (j, i, 0))
    return pl.pallas_call(
        body, name=name,
        grid_spec=pltpu.PrefetchScalarGridSpec(
            num_scalar_prefetch=1, grid=(n, nh),
            in_specs=[pl.BlockSpec((None, tr, b), lambda j, i, meta_ref: (j, meta_ref[0] * nh + i, 0)), half],
            out_specs=half),
        out_shape=jax.ShapeDtypeStruct((n, ah, b), BF16), compiler_params=_params(("parallel", "parallel")),
    )(meta, g, r)


def _chip_sum(p, q, meta, *, name):
    n, ah, b = p.shape
    tr = _pick((256, 128, 64, 32, 16), ah)
    nh = ah // tr

    def body(meta_ref, p_ref, q1_ref, q2_ref, q3_ref, o_ref):
        o_ref[...] = ((p_ref[...].astype(F32) + q1_ref[...].astype(F32)) + q2_ref[...].astype(F32)
                      ) + q3_ref[...].astype(F32)

    def piece(mask):
        return pl.BlockSpec((None, tr, b), lambda i, meta_ref: (meta_ref[1] ^ mask, i, 0))

    return pl.pallas_call(
        body, name=name,
        grid_spec=pltpu.PrefetchScalarGridSpec(
            num_scalar_prefetch=1, grid=(nh,),
            in_specs=[piece(0), piece(1), piece(2), piece(3)],
            out_specs=pl.BlockSpec((tr, b), lambda i, meta_ref: (meta_ref[0] * nh + i, 0))),
        out_shape=jax.ShapeDtypeStruct((2 * ah, b), F32), compiler_params=_params(("parallel",)),
    )(meta, p, q, q, q)


def _sum_leading(x, *, name):
    n, R, C = x.shape
    tr = _ew_tile(R)

    def body(x_ref, o_ref):
        acc = x_ref[0].astype(F32)
        for k in range(1, n):
            acc = acc + x_ref[k].astype(F32)
        o_ref[...] = acc

    return pl.pallas_call(
        body, name=name, grid=(R // tr,),
        in_specs=[pl.BlockSpec((n, tr, C), lambda i: (0, i, 0))],
        out_specs=pl.BlockSpec((tr, C), lambda i: (i, 0)),
        out_shape=jax.ShapeDtypeStruct((R, C), F32), compiler_params=_params(("parallel",)),
    )(x)


def _adamw(w, g, m, v, *, name):
    R, C = w.shape
    tr = _ew_tile(R)

    def body(w_ref, g_ref, m_ref, v_ref, d_ref, nm_ref, nv_ref):
        gv = g_ref[...]
        nm = ADAM_B1 * m_ref[...] + (1.0 - ADAM_B1) * gv
        nv = ADAM_B2 * v_ref[...] + (1.0 - ADAM_B2) * (gv * gv)
        m_hat = nm / (1.0 - ADAM_B1 ** ADAM_STEP)
        v_hat = nv / (1.0 - ADAM_B2 ** ADAM_STEP)
        d_ref[...] = -ADAM_LR * (m_hat / (jnp.sqrt(v_hat) + ADAM_EPS) + ADAM_WD * w_ref[...])
        nm_ref[...] = nm
        nv_ref[...] = nv

    row = pl.BlockSpec((tr, C), lambda i: (i, 0))
    return pl.pallas_call(
        body, name=name, grid=(R // tr,), in_specs=[row] * 4, out_specs=[row] * 3,
        out_shape=[jax.ShapeDtypeStruct((R, C), F32)] * 3, compiler_params=_params(("parallel",)),
    )(w, g, m, v)


def _adamw_layers(w, g0, g1, m, v, *, name):
    _, a, b = w.shape
    tr = _pick((256, 128, 64, 32, 16, 8), a)

    def body(w_ref, g0_ref, g1_ref, m_ref, v_ref, g_ref, d_ref, nm_ref, nv_ref):
        gv = jnp.where(pl.program_id(1) == 0, g0_ref[...], g1_ref[...])
        nm = ADAM_B1 * m_ref[...] + (1.0 - ADAM_B1) * gv
        nv = ADAM_B2 * v_ref[...] + (1.0 - ADAM_B2) * (gv * gv)
        m_hat = nm / (1.0 - ADAM_B1 ** ADAM_STEP)
        v_hat = nv / (1.0 - ADAM_B2 ** ADAM_STEP)
        g_ref[...] = gv
        d_ref[...] = -ADAM_LR * (m_hat / (jnp.sqrt(v_hat) + ADAM_EPS) + ADAM_WD * w_ref[...])
        nm_ref[...] = nm
        nv_ref[...] = nv

    lay = pl.BlockSpec((None, tr, b), lambda i, l: (l, i, 0))
    row = pl.BlockSpec((tr, b), lambda i, l: (i, 0))
    return pl.pallas_call(
        body, name=name, grid=(a // tr, 2), in_specs=[lay, row, row, lay, lay], out_specs=[lay] * 4,
        out_shape=[jax.ShapeDtypeStruct(w.shape, F32)] * 4, compiler_params=_params(("parallel", "arbitrary")),
    )(w, g0, g1, m, v)


ANY = pl.BlockSpec(memory_space=pl.ANY)


def _place():
    x, y, c = lax.axis_index("x"), lax.axis_index("y"), lax.axis_index("c")
    return x, y, c, 2 * x + y


def _other_chip(x, y, mask):
    px = 1 - x if mask & 2 else x
    py = 1 - y if mask & 1 else y
    return px, py, 2 * px + py


MASKS = (1, 2, 3)


def _half(ref, c, lead=()):
    ah = ref.shape[-2] // 2
    return ref.at[(*lead, pl.ds(c * ah, ah), slice(None))]


def _gather_weights(owns, *, name):
    n = len(owns)

    def body(*refs):
        own, out = refs[:n], refs[n:2 * n]
        send_sems, recv_sems = refs[2 * n], refs[2 * n + 1]
        x, y, c, chip = _place()
        sends = []
        for i in range(n):
            for k, mask in enumerate(MASKS):
                px, py, _ = _other_chip(x, y, mask)
                cp = pltpu.make_async_remote_copy(_half(own[i], c), _half(out[i], c, (chip,)),
                                                  send_sems.at[i, k], recv_sems.at[i, k],
                                                  device_id=(px, py, c), device_id_type=MESH)
                cp.start()
                sends.append(cp)
        for i in range(n):
            for k, mask in enumerate(MASKS):
                px, py, pchip = _other_chip(x, y, mask)
                got = _half(out[i], c, (pchip,))
                pltpu.make_async_remote_copy(got, got, send_sems.at[i, k], recv_sems.at[i, k],
                                             device_id=(px, py, c), device_id_type=MESH).wait_recv()
                cp = pltpu.make_async_remote_copy(got, got, send_sems.at[i, 3 + k], recv_sems.at[i, 3 + k],
                                                  device_id=(x, y, 1 - c), device_id_type=MESH)
                cp.start()
                sends.append(cp)
        for i in range(n):
            for k, mask in enumerate(MASKS):
                _, _, pchip = _other_chip(x, y, mask)
                theirs = _half(out[i], 1 - c, (pchip,))
                pltpu.make_async_remote_copy(theirs, theirs, send_sems.at[i, 3 + k], recv_sems.at[i, 3 + k],
                                             device_id=(x, y, 1 - c), device_id_type=MESH).wait_recv()
        for cp in sends:
            cp.wait_send()

    return pl.pallas_call(
        body, name=name, in_specs=[ANY] * n, out_specs=[ANY] * n,
        out_shape=[jax.ShapeDtypeStruct((N_CHIPS, *o.shape), o.dtype) for o in owns],
        scratch_shapes=[pltpu.SemaphoreType.DMA((n, 6)), pltpu.SemaphoreType.DMA((n, 6))],
    )(*owns)


SIBLING_ID = 0


def _sibling_barrier(x, y, c):
    sem = pltpu.get_barrier_semaphore()
    pl.semaphore_signal(sem, inc=1, device_id=(x, y, 1 - c), device_id_type=MESH)
    pl.semaphore_wait(sem, 1)


def _swap_halves(gs, *, name):
    n = len(gs)

    def body(*refs):
        g, out = refs[:n], refs[n:2 * n]
        send_sems, recv_sems = refs[2 * n], refs[2 * n + 1]
        x, y, c, _ = _place()
        _sibling_barrier(x, y, c)
        cps = []
        for i in range(n):
            ah = g[i].shape[1] // 2
            cp = pltpu.make_async_remote_copy(g[i].at[:, pl.ds((1 - c) * ah, ah), :], out[i],
                                              send_sems.at[i], recv_sems.at[i],
                                              device_id=(x, y, 1 - c), device_id_type=MESH)
            cp.start()
            cps.append(cp)
        for cp in cps:
            cp.wait()

    return pl.pallas_call(
        body, name=name, in_specs=[ANY] * n, out_specs=[ANY] * n,
        out_shape=[jax.ShapeDtypeStruct((g.shape[0], g.shape[1] // 2, g.shape[2]), g.dtype) for g in gs],
        scratch_shapes=[pltpu.SemaphoreType.DMA((n,)), pltpu.SemaphoreType.DMA((n,))],
        compiler_params=pltpu.CompilerParams(collective_id=SIBLING_ID),
    )(*gs)


def _scatter_to_chips(ps, *, name):
    n = len(ps)

    def body(*refs):
        p, out = refs[:n], refs[n:2 * n]
        send_sems, recv_sems = refs[2 * n], refs[2 * n + 1]
        x, y, c, chip = _place()
        sends = []
        for i in range(n):
            for k, mask in enumerate(MASKS):
                px, py, pchip = _other_chip(x, y, mask)
                cp = pltpu.make_async_remote_copy(p[i].at[pchip], out[i].at[chip], send_sems.at[i, k],
                                                  recv_sems.at[i, k], device_id=(px, py, c), device_id_type=MESH)
                cp.start()
                sends.append(cp)
        for i in range(n):
            for k, mask in enumerate(MASKS):
                px, py, pchip = _other_chip(x, y, mask)
                pltpu.make_async_remote_copy(p[i].at[pchip], out[i].at[pchip], send_sems.at[i, k],
                                             recv_sems.at[i, k], device_id=(px, py, c),
                                             device_id_type=MESH).wait_recv()
        for cp in sends:
            cp.wait_send()

    return pl.pallas_call(
        body, name=name, in_specs=[ANY] * n, out_specs=[ANY] * n,
        out_shape=[jax.ShapeDtypeStruct(p.shape, p.dtype) for p in ps],
        scratch_shapes=[pltpu.SemaphoreType.DMA((n, 3)), pltpu.SemaphoreType.DMA((n, 3))],
    )(*ps)


HBM_SPEC = pl.BlockSpec(memory_space=pltpu.HBM)
SEM_SPEC = pl.BlockSpec(memory_space=pltpu.SEMAPHORE)
EFFECT = pltpu.SideEffectType.DATAFLOW_SIDE_EFFECTING


def _ici_ends(src, land, gather, x, y, c, chip, mask):
    px, py, pchip = _other_chip(x, y, mask)
    if gather:
        return _half(src, c), _half(land, c, (chip,)), _half(land, c, (pchip,)), (px, py, c)
    return src.at[pchip], land.at[chip], land.at[pchip], (px, py, c)


def _ici_start(groups, land_groups, gather, *, name, after=None):
    sizes = [len(g) for g in groups]
    n = sum(sizes)
    ng = len(groups)
    deps = [] if after is None else [after]

    def body(*refs):
        src, land = refs[:n], refs[n:2 * n]
        sems = refs[2 * n + len(deps):2 * n + len(deps) + 2 * ng]
        token = refs[-1]
        x, y, c, chip = _place()
        i = 0
        for g in range(ng):
            for j in range(sizes[g]):
                for k, mask in enumerate(MASKS):
                    s, d, _, peer = _ici_ends(src[i], land[i], gather, x, y, c, chip, mask)
                    pltpu.make_async_remote_copy(s, d, sems[2 * g].at[3 * j + k], sems[2 * g + 1].at[3 * j + k],
                                                 device_id=peer, device_id_type=MESH).start()
                i += 1
        token[...] = jnp.zeros_like(token)

    lands = [pltpu.with_memory_space_constraint(lax.empty(s.shape, s.dtype), pltpu.HBM)
             for g in land_groups for s in g]
    srcs = [pltpu.with_memory_space_constraint(s, pltpu.HBM) for g in groups for s in g]
    sem_shapes = [pltpu.SemaphoreType.DMA((3 * m,)) for m in sizes for _ in range(2)]
    out = pl.pallas_call(
        body, name=name,
        out_shape=(*sem_shapes, *[pltpu.HBM(s.shape, s.dtype) for s in srcs],
                   *[pltpu.HBM(s.shape, s.dtype) for s in lands], jax.ShapeDtypeStruct((8, 128), F32)),
        in_specs=[HBM_SPEC] * (2 * n) + [ANY] * len(deps),
        out_specs=(*[SEM_SPEC] * (2 * ng), *[HBM_SPEC] * (2 * n), pl.BlockSpec(memory_space=pltpu.VMEM)),
        input_output_aliases={i: 2 * ng + i for i in range(2 * n)},
        compiler_params=pltpu.CompilerParams(has_side_effects=EFFECT),
    )(*srcs, *lands, *deps)
    res, pos = [], 0
    for g in range(ng):
        res.append((out[2 * g], out[2 * g + 1], list(out[2 * ng + pos:2 * ng + pos + sizes[g]]),
                    list(out[2 * ng + n + pos:2 * ng + n + pos + sizes[g]])))
        pos += sizes[g]
    return res, out[-1]


def _ici_wait(send_sems, recv_sems, srcs, lands, gather, after, *, name):
    n = len(srcs)

    def body(*refs):
        src, land = refs[:n], refs[n:2 * n]
        send_sems, recv_sems = refs[2 * n], refs[2 * n + 1]
        x, y, c, chip = _place()
        for i in range(n):
            for k, mask in enumerate(MASKS):
                s, d, got, peer = _ici_ends(src[i], land[i], gather, x, y, c, chip, mask)
                pltpu.make_async_remote_copy(s, d, send_sems.at[3 * i + k], recv_sems.at[3 * i + k],
                                             device_id=peer, device_id_type=MESH).wait_send()
                pltpu.make_async_remote_copy(s, got, send_sems.at[3 * i + k], recv_sems.at[3 * i + k],
                                             device_id=peer, device_id_type=MESH).wait_recv()

    out = pl.pallas_call(
        body, name=name,
        out_shape=tuple(pltpu.HBM(s.shape, s.dtype) for s in (*srcs, *lands)),
        in_specs=[HBM_SPEC] * (2 * n) + [SEM_SPEC, SEM_SPEC, ANY],
        out_specs=tuple([HBM_SPEC] * (2 * n)),
        input_output_aliases={i: i for i in range(2 * n)},
        compiler_params=pltpu.CompilerParams(has_side_effects=EFFECT),
    )(*srcs, *lands, send_sems, recv_sems, after)
    return list(out[:n]), list(out[n:])


def _relay_halves(gs, *, name):
    n = len(gs)

    def body(*refs):
        out = refs[n:2 * n]
        send_sems, recv_sems = refs[2 * n], refs[2 * n + 1]
        x, y, c, _ = _place()
        _sibling_barrier(x, y, c)
        cps = []
        for i in range(n):
            for k, mask in enumerate(MASKS):
                _, _, pchip = _other_chip(x, y, mask)
                got = _half(out[i], c, (pchip,))
                cp = pltpu.make_async_remote_copy(got, got, send_sems.at[i, k], recv_sems.at[i, k],
                                                  device_id=(x, y, 1 - c), device_id_type=MESH)
                cp.start()
                cps.append(cp)
        for i in range(n):
            for k, mask in enumerate(MASKS):
                _, _, pchip = _other_chip(x, y, mask)
                theirs = _half(out[i], 1 - c, (pchip,))
                pltpu.make_async_remote_copy(theirs, theirs, send_sems.at[i, k], recv_sems.at[i, k],
                                             device_id=(x, y, 1 - c), device_id_type=MESH).wait_recv()
        for cp in cps:
            cp.wait_send()

    return pl.pallas_call(
        body, name=name, in_specs=[ANY] * n, out_specs=[ANY] * n,
        out_shape=[jax.ShapeDtypeStruct(g.shape, g.dtype) for g in gs],
        input_output_aliases={i: i for i in range(n)},
        scratch_shapes=[pltpu.SemaphoreType.DMA((n, 3)), pltpu.SemaphoreType.DMA((n, 3))],
        compiler_params=pltpu.CompilerParams(collective_id=SIBLING_ID),
    )(*gs)


def _share_halves(gs, *, name):
    n = len(gs)

    def body(*refs):
        out = refs[n:2 * n]
        send_sems, recv_sems = refs[2 * n], refs[2 * n + 1]
        x, y, c, _ = _place()
        _sibling_barrier(x, y, c)
        cps = []
        for i in range(n):
            cp = pltpu.make_async_remote_copy(_half(out[i], c), _half(out[i], c), send_sems.at[i], recv_sems.at[i],
                                              device_id=(x, y, 1 - c), device_id_type=MESH)
            cp.start()
            cps.append(cp)
        for i in range(n):
            theirs = _half(out[i], 1 - c)
            pltpu.make_async_remote_copy(theirs, theirs, send_sems.at[i], recv_sems.at[i],
                                         device_id=(x, y, 1 - c), device_id_type=MESH).wait_recv()
        for cp in cps:
            cp.wait_send()

    return pl.pallas_call(
        body, name=name, in_specs=[ANY] * n, out_specs=[ANY] * n,
        out_shape=[jax.ShapeDtypeStruct(g.shape, g.dtype) for g in gs],
        input_output_aliases={i: i for i in range(n)},
        scratch_shapes=[pltpu.SemaphoreType.DMA((n,)), pltpu.SemaphoreType.DMA((n,))],
        compiler_params=pltpu.CompilerParams(collective_id=SIBLING_ID),
    )(*gs)


def _gather_all(buf, *, name):
    r, L = buf.shape
    vmem = pl.BlockSpec(memory_space=pltpu.VMEM)
    masks = tuple(range(1, N_DEV))

    def body(buf_ref, out_ref, send_sems, recv_sems):
        x, y, c, _ = _place()
        me = 4 * x + 2 * y + c
        out_ref[me] = buf_ref[...]
        sends = []
        for k, mask in enumerate(masks):
            px = 1 - x if mask & 4 else x
            py = 1 - y if mask & 2 else y
            pc = 1 - c if mask & 1 else c
            cp = pltpu.make_async_remote_copy(buf_ref, out_ref.at[me], send_sems.at[k], recv_sems.at[k],
                                              device_id=(px, py, pc), device_id_type=MESH)
            cp.start()
            sends.append(cp)
        for k, mask in enumerate(masks):
            px = 1 - x if mask & 4 else x
            py = 1 - y if mask & 2 else y
            pc = 1 - c if mask & 1 else c
            pltpu.make_async_remote_copy(buf_ref, out_ref.at[4 * px + 2 * py + pc], send_sems.at[k],
                                         recv_sems.at[k], device_id=(px, py, pc), device_id_type=MESH).wait_recv()
        for cp in sends:
            cp.wait_send()

    return pl.pallas_call(
        body, name=name, in_specs=[vmem], out_specs=vmem,
        out_shape=jax.ShapeDtypeStruct((N_DEV, r, L), buf.dtype),
        scratch_shapes=[pltpu.SemaphoreType.DMA((N_DEV - 1,)), pltpu.SemaphoreType.DMA((N_DEV - 1,))],
    )(buf)


def _pack(arrs, lanes, row_mult=8):
    flat = jnp.concatenate([a.reshape(-1) for a in arrs])
    rows = -(-flat.shape[0] // lanes)
    rows = -(-rows // row_mult) * row_mult
    flat = jnp.pad(flat, (0, rows * lanes - flat.shape[0]))
    return flat.reshape(rows, lanes)


def _unpack(buf, shapes):
    flat = buf.reshape(-1)
    out, pos = [], 0
    for s in shapes:
        n = 1
        for d in s:
            n *= d
        out.append(flat[pos:pos + n].reshape(s))
        pos += n
    return out


def kernel(x, mem, norm_mix_g, w_in, conv_a_w, conv_b_w, conv_b_bias, ln_b_g, ln_b_b, w_out, norm_x_g, norm_mem_g, w_q, w_kv, w_xo, norm_ffn_g, w_up, w_down, final_g, loss_target, m_norm_mix_g, m_w_in, m_conv_a_w, m_conv_b_w, m_conv_b_bias, m_ln_b_g, m_ln_b_b, m_w_out, m_norm_x_g, m_norm_mem_g, m_w_q, m_w_kv, m_w_xo, m_norm_ffn_g, m_w_up, m_w_down, m_final_g, v_norm_mix_g, v_w_in, v_conv_a_w, v_conv_b_w, v_conv_b_bias, v_ln_b_g, v_ln_b_b, v_w_out, v_norm_x_g, v_norm_mem_g, v_w_q, v_w_kv, v_w_xo, v_norm_ffn_g, v_w_up, v_w_down, v_final_g):
    W = dict(norm_mix_g=norm_mix_g, w_in=w_in, conv_a_w=conv_a_w, conv_b_w=conv_b_w, conv_b_bias=conv_b_bias,
             ln_b_g=ln_b_g, ln_b_b=ln_b_b, w_out=w_out, norm_x_g=norm_x_g, norm_mem_g=norm_mem_g, w_q=w_q,
             w_kv=w_kv, w_xo=w_xo, norm_ffn_g=norm_ffn_g, w_up=w_up, w_down=w_down, final_g=final_g)
    MO = dict(norm_mix_g=m_norm_mix_g, w_in=m_w_in, conv_a_w=m_conv_a_w, conv_b_w=m_conv_b_w,
              conv_b_bias=m_conv_b_bias, ln_b_g=m_ln_b_g, ln_b_b=m_ln_b_b, w_out=m_w_out, norm_x_g=m_norm_x_g,
              norm_mem_g=m_norm_mem_g, w_q=m_w_q, w_kv=m_w_kv, w_xo=m_w_xo, norm_ffn_g=m_norm_ffn_g,
              w_up=m_w_up, w_down=m_w_down, final_g=m_final_g)
    VO = dict(norm_mix_g=v_norm_mix_g, w_in=v_w_in, conv_a_w=v_conv_a_w, conv_b_w=v_conv_b_w,
              conv_b_bias=v_conv_b_bias, ln_b_g=v_ln_b_g, ln_b_b=v_ln_b_b, w_out=v_w_out, norm_x_g=v_norm_x_g,
              norm_mem_g=v_norm_mem_g, w_q=v_w_q, w_kv=v_w_kv, w_xo=v_w_xo, norm_ffn_g=v_norm_ffn_g,
              w_up=v_w_up, w_down=v_w_down, final_g=v_final_g)
    names = list(W.keys())
    depth = norm_mix_g.shape[0]
    assert depth == 2, "the exchange splits the weights into one layer per core of a chip"
    c_idx = lax.axis_index("c")
    chip_idx = 2 * lax.axis_index("x") + lax.axis_index("y")

    xs = x[0]
    ms = mem[0]
    tgt = loss_target[0]
    S, D = xs.shape
    c_a = conv_a_w.shape[-1] * N_CHIPS
    c_loc = conv_a_w.shape[-1]

    meta = jnp.stack([c_idx, chip_idx]).astype(jnp.int32)
    shard_axis = dict(BIG)
    own = {(l, n): W[n][l].astype(BF16) for l in range(depth) for n, _ in BIG}

    conv_local = _pack([conv_a_w, conv_b_w], 128)
    conv_all = _gather_all(conv_local, name="gather_conv_weights")
    order = [(l, gi) for l in range(depth) for gi in range(len(FWD_GROUPS))]
    src_groups = [[own[(l, n)] for n in FWD_GROUPS[gi]] for l, gi in order]
    started, gather_token = _ici_start(
        src_groups, [[jax.ShapeDtypeStruct((N_CHIPS, *s.shape), s.dtype) for s in g] for g in src_groups], True,
        name="gather_weights_start", after=conv_all)
    started = dict(zip(order, started))
    Wb = [dict() for _ in range(depth)]

    def weights_ready(l, gi, after):
        send_sems, recv_sems, srcs, lands = started[(l, gi)]
        srcs, lands = _ici_wait(send_sems, recv_sems, srcs, lands, True, after, name=f"gather_weights_l{l}_g{gi}_wait")
        full = _relay_halves(lands, name=f"gather_weights_l{l}_g{gi}_relay")
        for n, g, o in zip(FWD_GROUPS[gi], full, srcs):
            g = lax.dynamic_update_slice(g, o[None], (chip_idx, 0, 0))
            Wb[l][n] = g.reshape(-1, g.shape[-1]) if shard_axis[n] == 0 else g
    na = depth * K_A * c_loc
    nbw = depth * K_B * c_loc
    ca_parts, cb_parts = [], []
    for j in range(N_CHIPS):
        fl = conv_all[2 * j].reshape(-1)
        ca_parts.append(fl[:na].reshape(depth, K_A, c_loc))
        cb_parts.append(fl[na:na + nbw].reshape(depth, K_B, c_loc))
    conv_a_full = jnp.concatenate(ca_parts, axis=-1)
    conv_b_full = jnp.concatenate(cb_parts, axis=-1)

    saved = []
    h = xs
    for l in range(depth):
        wl = Wb[l]
        t = f"l{l}_"
        weights_ready(l, 0, gather_token if l == 0 else h)
        if l == 0:
            u = _rms_fwd(h, norm_mix_g[l:l + 1], name=t + "rms_mix")
        z = _mm(u, wl["w_in"], b_stack=True, name=t + "mm_in", bm=2048)
        y_a = _mixer_a_fwd(z, conv_a_full[l], name=t + "mixer_a")
        cb = _mixer_b_fwd(z, conv_b_full[l], conv_b_bias[l:l + 1], name=t + "mixer_b")
        y_b = _ln_silu_fwd(cb, ln_b_g[l:l + 1], ln_b_b[l:l + 1], name=t + "ln_silu")
        yy = jnp.concatenate([y_a, y_b], axis=1)
        h2, q_in = _mm(yy, wl["w_out"], res=h, epi="rms_fwd", norm=norm_x_g[l:l + 1], name=t + "mm_out")
        weights_ready(l, 1, h2)
        q = _mm(q_in, wl["w_q"], out_dtype=BF16, name=t + "mm_q")
        mn = _rms_fwd(ms, norm_mem_g[l:l + 1], name=t + "rms_mem")
        kv = _mm(mn, wl["w_kv"], b_stack=True, out_dtype=BF16, name=t + "mm_kv")
        o = _attn_fwd(q, kv, name=t + "attn")
        h3, u3 = _mm(o, wl["w_xo"], res=h2, epi="rms_fwd", norm=norm_ffn_g[l:l + 1], name=t + "mm_xo")
        weights_ready(l, 2, h3)
        a_pre, hh = _mm(u3, wl["w_up"], b_stack=True, out_dtype=BF16, epi="sqrelu", name=t + "mm_up")
        saved.append(dict(h=h, u=u, z=z, cb=cb, yy=yy, h2=h2, q_in=q_in, q=q, mn=mn, kv=kv, o=o, h3=h3,
                          u3=u3, a_pre=a_pre, hh=hh))
        if l + 1 < depth:
            h, u = _mm(hh, wl["w_down"], res=h3, epi="rms_fwd", norm=norm_mix_g[l + 1:l + 2], name=t + "mm_down")
        else:
            h = _mm(hh, wl["w_down"], res=h3, name=t + "mm_down")

    loss_vec, dh, dhb, d_final = _loss_head(h, final_g.reshape(1, D), tgt, name="loss_head")
    loss = lax.psum(loss_vec[0, 0], ("x", "y", "c"))

    GW = [dict() for _ in range(depth)]
    GS = [dict() for _ in range(depth)]
    pending = []

    def reduce_start(l, gi):
        group = BWD_GROUPS[gi]
        gs = [GW[l][n] if GW[l][n].ndim == 3 else GW[l][n].reshape(N_CHIPS, *W[n].shape[1:]) for n in group]
        from_sibling = _swap_halves(gs, name=f"grad_swap_sibling_l{l}_g{gi}")
        prs = [_pair_sum(g, r, meta, name=f"grad_pair_sum_l{l}_{n}") for g, r, n in zip(gs, from_sibling, group)]
        (st,), token = _ici_start([prs], [prs], False, name=f"grad_scatter_chips_l{l}_g{gi}_start")
        pending.append((l, group, st))
        return token

    for l in reversed(range(depth)):
        wl, sv = Wb[l], saved[l]
        t = f"l{l}_b_"
        fused = dict(epi="rms_bwd", bm=512)
        GW[l]["w_down"] = _mm(sv["hh"], dhb, ta=True, name=t + "dw_down")
        da = _mm(dhb, wl["w_down"], tb=True, out_dtype=BF16, epi="dsqrelu", aux=sv["a_pre"], name=t + "d_hidden")
        GW[l]["w_up"] = _mm(sv["u3"], da, ta=True, o_stack=N_CHIPS, name=t + "dw_up")
        dh, dhb, GS[l]["norm_ffn_g"] = _mm(da, wl["w_up"], tb=True, b_stack=True, res=dh,
                                           norm=(sv["h3"], norm_ffn_g[l:l + 1]), after=reduce_start(l, 0),
                                           name=t + "d_u3", **fused)
        GW[l]["w_xo"] = _mm(sv["o"], dhb, ta=True, name=t + "dw_xo")
        d_o = _mm(dhb, wl["w_xo"], tb=True, out_dtype=BF16, name=t + "d_o")
        dq, dkv = _attn_bwd(sv["q"], sv["kv"], d_o, name=t + "attn")
        GW[l]["w_q"] = _mm(sv["q_in"], dq, ta=True, name=t + "dw_q")
        dkvb = dkv.astype(BF16)
        GW[l]["w_kv"] = _mm(sv["mn"], dkvb, ta=True, o_stack=N_CHIPS, name=t + "dw_kv")
        dmn = _mm(dkvb, wl["w_kv"], tb=True, b_stack=True, name=t + "d_mem")
        _, _, GS[l]["norm_mem_g"] = _rms_bwd(ms, norm_mem_g[l:l + 1], dmn, None, name=t + "rms_mem")
        dh, dhb, GS[l]["norm_x_g"] = _mm(dq, wl["w_q"], tb=True, res=dh, norm=(sv["h2"], norm_x_g[l:l + 1]),
                                         after=reduce_start(l, 1), name=t + "d_q_in", **fused)
        GW[l]["w_out"] = _mm(sv["yy"], dhb, ta=True, name=t + "dw_out")
        dyy = _mm(dhb, wl["w_out"], tb=True, name=t + "d_y")
        dcb, GS[l]["ln_b_g"], GS[l]["ln_b_b"] = _ln_silu_bwd(sv["cb"], ln_b_g[l:l + 1], ln_b_b[l:l + 1], dyy, 1,
                                                             name=t + "ln_silu")
        db_, dc_, dh_, GS[l]["conv_a_w"] = _mixer_a_bwd(sv["z"], conv_a_full[l], dyy, name=t + "mixer_a")
        dv_, dg_, GS[l]["conv_b_w"], GS[l]["conv_b_bias"] = _mixer_b_bwd(sv["z"], conv_b_full[l], dcb,
                                                                         name=t + "mixer_b")
        dz = jnp.concatenate([db_, dc_, dh_, dv_, dg_], axis=1)
        GW[l]["w_in"] = _mm(sv["u"], dz, ta=True, o_stack=N_CHIPS, name=t + "dw_in")
        dh, dhb, GS[l]["norm_mix_g"] = _mm(dz, wl["w_in"], tb=True, b_stack=True, res=dh,
                                           norm=(sv["h"], norm_mix_g[l:l + 1]), after=reduce_start(l, 2),
                                           name=t + "d_u", **fused)
    grad_x = dh[None]

    after = GS[0]["norm_mix_g"]
    keys, halves = [], []
    for l, group, (send_sems, recv_sems, srcs, lands) in pending:
        prs, pieces = _ici_wait(send_sems, recv_sems, srcs, lands, False, after,
                                name=f"grad_scatter_chips_l{l}_{group[0]}_wait")
        for n, p, q in zip(group, prs, pieces):
            keys.append((l, n))
            halves.append(_chip_sum(p, q, meta, name=f"grad_chip_sum_l{l}_{n}"))
        after = halves[-1]
    reduced = dict(zip(keys, _share_halves(halves, name="grad_share_sibling")))

    grads, deltas, new_m, new_v = {}, {}, {}, {}
    for n, _ in BIG:
        grads[n], deltas[n], new_m[n], new_v[n] = _adamw_layers(W[n], reduced[(0, n)], reduced[(1, n)], MO[n], VO[n],
                                                               name="adamw_" + n)

    small = [n for n in names if n not in dict(BIG)]
    full_shapes = {n: ((depth, W[n].shape[1], c_a) if n in ("conv_a_w", "conv_b_w") else W[n].shape)
                   for n in small}

    def small_grad(n):
        if n == "final_g":
            return d_final.reshape(W[n].shape)
        return jnp.stack([GS[l][n].reshape(full_shapes[n][1:]) for l in range(depth)])

    part = _pack([small_grad(n) for n in small], LANES)
    everyone = _gather_all(part, name="gather_small_grads")
    total = _sum_leading(everyone, name="small_grad_sum")
    full_grads = dict(zip(small, _unpack(total, [full_shapes[n] for n in small])))
    for n in ("conv_a_w", "conv_b_w"):
        full_grads[n] = lax.dynamic_slice_in_dim(full_grads[n], chip_idx * c_loc, c_loc, axis=2)
    shapes = [W[n].shape for n in small]
    d_s, m_s, v_s = _adamw(_pack([W[n] for n in small], 128), _pack([full_grads[n] for n in small], 128),
                           _pack([MO[n] for n in small], 128), _pack([VO[n] for n in small], 128),
                           name="adamw_small")
    for n, d, nm, nv in zip(small, _unpack(d_s, shapes), _unpack(m_s, shapes), _unpack(v_s, shapes)):
        grads[n], deltas[n], new_m[n], new_v[n] = full_grads[n], d, nm, nv

    return (loss, grad_x, *[grads[n] for n in names], *[deltas[n] for n in names],
            *[new_m[n] for n in names], *[new_v[n] for n in names])
```

```python
import jax
import jax.numpy as jnp
from jax import lax
from jax.experimental import pallas as pl
from jax.experimental.pallas import tpu as pltpu

F32 = jnp.float32
BF16 = jnp.bfloat16
MESH = pl.DeviceIdType.MESH

EPS = 1e-6
N_XHEADS = 4
K_A = 3
K_B = 31
PAD_A = 8
PAD_B = 32
CONV_CHUNK = 256
ROW_TILE = 512
LANES = 1024
VMEM_LIMIT_BYTES = 56 * 1024 * 1024

ADAM_LR = 0.001
ADAM_B1 = 0.9
ADAM_B2 = 0.999
ADAM_EPS = 1e-08
ADAM_WD = 0.01
ADAM_STEP = 10

BIG = (("w_in", 1), ("w_out", 0), ("w_q", 0), ("w_kv", 1), ("w_xo", 0), ("w_up", 1), ("w_down", 0))
FWD_GROUPS = (("w_in",), ("w_out", "w_q", "w_kv", "w_xo"), ("w_up", "w_down"))
BWD_GROUPS = (("w_down", "w_up"), ("w_xo", "w_q", "w_kv"), ("w_out", "w_in"))
N_CHIPS = 4
N_DEV = 8


def _params(sem=None):
    return pltpu.CompilerParams(dimension_semantics=sem, vmem_limit_bytes=VMEM_LIMIT_BYTES)


def _pick(cands, n):
    for c in cands:
        if c <= n and n % c == 0:
            return c
    return n


def _mm(a, b, *, name, ta=False, tb=False, out_dtype=F32, res=None, epi=None, aux=None, norm=None, after=None,
        b_stack=False, o_stack=0, bm=1024, bn=1024, bk=1024):
    if ta:
        K, M = a.shape
    else:
        M, K = a.shape
    if b_stack:
        n_st, d1, d2 = b.shape
        N, kb = (d1, d2) if tb else (n_st * d2, d1)
        assert K == (n_st * d2 if tb else d1), (name, a.shape, b.shape)
    else:
        N = b.shape[0] if tb else b.shape[1]
    n_unit = b.shape[2] if (b_stack and not tb) else (N // o_stack if o_stack else N)
    k_unit = b.shape[2] if (b_stack and tb) else K
    bm = _pick((bm, 512, 256, 128), M)
    bn = _pick((bn, 512, 640, 256, 384, 128), n_unit)
    bk = _pick((bk, 640, 512, 256, 128), k_unit)
    assert M % bm == 0 and N % bn == 0 and K % bk == 0, (name, M, N, K)
    nk = K // bk
    per_n = n_unit // bn
    per_k = k_unit // bk
    a_spec = (pl.BlockSpec((bk, bm), lambda i, j, k: (k, i)) if ta
              else pl.BlockSpec((bm, bk), lambda i, j, k: (i, k)))
    if b_stack and tb:
        b_spec = pl.BlockSpec((None, bn, bk), lambda i, j, k: (k // per_k, j, k % per_k))
    elif b_stack:
        b_spec = pl.BlockSpec((None, bk, bn), lambda i, j, k: (j // per_n, k, j % per_n))
    elif tb:
        b_spec = pl.BlockSpec((bn, bk), lambda i, j, k: (j, k))
    else:
        b_spec = pl.BlockSpec((bk, bn), lambda i, j, k: (k, j))
    o_spec = pl.BlockSpec((bm, bn), lambda i, j, k: (i, j))
    dims = (((0 if ta else 1,), (1 if tb else 0,)), ((), ()))
    ins, in_specs = [a, b], [a_spec, b_spec]
    if res is not None:
        ins.append(res)
        in_specs.append(o_spec)
    if aux is not None:
        ins.append(aux)
        in_specs.append(o_spec)
    vec_spec = pl.BlockSpec((1, bn), lambda i, j, k: (0, j))
    if epi == "rms_fwd":
        assert bn == N, (name, bn, N)
        ins.append(norm)
        in_specs.append(vec_spec)
    elif epi == "rms_bwd":
        assert bn == N, (name, bn, N)
        ins += [norm[0], norm[1]]
        in_specs += [o_spec, vec_spec]
    n_norm = {"rms_fwd": 1, "rms_bwd": 2}.get(epi, 0)
    if after is not None:
        ins.append(after)
        in_specs.append(pl.BlockSpec(memory_space=pl.ANY))
    n_out = {"sqrelu": 2, "rms_fwd": 2, "rms_bwd": 3}.get(epi, 1)
    out_shape = [jax.ShapeDtypeStruct((M, N), out_dtype)] * n_out
    out_specs = [o_spec] * n_out
    if epi == "rms_fwd":
        out_shape = [jax.ShapeDtypeStruct((M, N), F32), jax.ShapeDtypeStruct((M, N), BF16)]
    elif epi == "rms_bwd":
        out_shape = [jax.ShapeDtypeStruct((M, N), F32), jax.ShapeDtypeStruct((M, N), BF16),
                     jax.ShapeDtypeStruct((1, N), F32)]
        out_specs = [o_spec, o_spec, vec_spec]
    if o_stack:
        assert n_out == 1 and res is None and aux is None
        out_shape = [jax.ShapeDtypeStruct((o_stack, M, N // o_stack), out_dtype)]
        out_specs = [pl.BlockSpec((None, bm, bn), lambda i, j, k: (j // per_n, i, j % per_n))]

    def body(*refs):
        a_ref, b_ref = refs[0], refs[1]
        pos = 2
        res_ref = aux_ref = None
        if res is not None:
            res_ref = refs[pos]
            pos += 1
        if aux is not None:
            aux_ref = refs[pos]
            pos += 1
        norm_refs = refs[pos:pos + n_norm]
        pos += n_norm + (after is not None)
        outs = refs[pos:pos + n_out]

        def product():
            return lax.dot_general(a_ref[...], b_ref[...], dims, preferred_element_type=F32)

        def finish(r):
            if epi == "rms_bwd":
                x_ref, g_ref = norm_refs
                xv = x_ref[...]
                rs = lax.rsqrt(jnp.mean(xv * xv, axis=-1, keepdims=True) + EPS)
                xh = xv * rs
                dxh = r * g_ref[...]
                dx = rs * (dxh - xh * jnp.mean(dxh * xh, axis=-1, keepdims=True))
                if res_ref is not None:
                    dx = dx + res_ref[...]
                outs[0][...] = dx
                outs[1][...] = dx.astype(BF16)

                @pl.when(pl.program_id(0) == 0)
                def _():
                    outs[2][...] = jnp.zeros_like(outs[2])

                outs[2][...] += jnp.sum(r * xh, axis=0, keepdims=True)
                return
            if res_ref is not None:
                r = r + res_ref[...]
            if epi == "rms_fwd":
                outs[0][...] = r
                rs = lax.rsqrt(jnp.mean(r * r, axis=-1, keepdims=True) + EPS)
                outs[1][...] = (r * rs * norm_refs[0][...]).astype(BF16)
            elif epi == "sqrelu":
                outs[0][...] = r.astype(out_dtype)
                rl = jnp.maximum(r, 0.0)
                outs[1][...] = (rl * rl).astype(out_dtype)
            elif epi == "dsqrelu":
                outs[0][...] = (r * (2.0 * jnp.maximum(aux_ref[...].astype(F32), 0.0))).astype(out_dtype)
            else:
                outs[0][...] = r.astype(out_dtype)

        if nk == 1:
            finish(product())
            return
        acc = refs[pos + n_out]
        k = pl.program_id(2)

        @pl.when(k == 0)
        def _():
            acc[...] = product()

        @pl.when(jnp.logical_and(k > 0, k < nk - 1))
        def _():
            acc[...] += product()

        @pl.when(k == nk - 1)
        def _():
            finish(acc[...] + product())

    out = pl.pallas_call(
        body, name=name, grid=(M // bm, N // bn, nk),
        in_specs=in_specs, out_specs=out_specs, out_shape=out_shape,
        scratch_shapes=[pltpu.VMEM((bm, bn), F32)] if nk > 1 else [],
        compiler_params=_params(("arbitrary",) * 3 if epi == "rms_bwd" else ("parallel", "parallel", "arbitrary")),
    )(*ins)
    return out if n_out > 1 else out[0]


def _row_tile(rows):
    return min(ROW_TILE, rows)


def _rms_fwd(x, g, *, name, after=None):
    S, D = x.shape
    tr = _row_tile(S)

    def body(x_ref, g_ref, *rest):
        o_ref = rest[-1]
        xv = x_ref[...]
        r = lax.rsqrt(jnp.mean(xv * xv, axis=-1, keepdims=True) + EPS)
        o_ref[...] = (xv * r * g_ref[...]).astype(BF16)

    deps = [] if after is None else [after]
    return pl.pallas_call(
        body, name=name, grid=(S // tr,),
        in_specs=[pl.BlockSpec((tr, D), lambda i: (i, 0)), pl.BlockSpec((1, D), lambda i: (0, 0))]
        + [ANY] * len(deps),
        out_specs=pl.BlockSpec((tr, D), lambda i: (i, 0)),
        out_shape=jax.ShapeDtypeStruct((S, D), BF16),
        compiler_params=_params(("parallel",)),
    )(x, g, *deps)


def _rms_bwd(x, g, du, dres, *, name, after=None):
    S, D = x.shape
    tr = _row_tile(S)
    has_res = dres is not None
    deps = [] if after is None else [after]

    def body(*refs):
        dx_ref, dxb_ref, dg_ref = refs[-3:]
        if has_res:
            x_ref, g_ref, du_ref, dres_ref = refs[:4]
        else:
            x_ref, g_ref, du_ref = refs[:3]
        xv = x_ref[...]
        r = lax.rsqrt(jnp.mean(xv * xv, axis=-1, keepdims=True) + EPS)
        xh = xv * r
        dy = du_ref[...]
        dxh = dy * g_ref[...]
        dx = r * (dxh - xh * jnp.mean(dxh * xh, axis=-1, keepdims=True))
        if has_res:
            dx = dx + dres_ref[...]
        dx_ref[...] = dx
        dxb_ref[...] = dx.astype(BF16)

        @pl.when(pl.program_id(0) == 0)
        def _():
            dg_ref[...] = jnp.zeros_like(dg_ref)

        dg_ref[...] += jnp.sum(dy * xh, axis=0, keepdims=True)

    row = pl.BlockSpec((tr, D), lambda i: (i, 0))
    vec = pl.BlockSpec((1, D), lambda i: (0, 0))
    ins = [x, g, du] + ([dres] if has_res else []) + deps
    in_specs = [row, vec, row] + ([row] if has_res else []) + [ANY] * len(deps)
    return pl.pallas_call(
        body, name=name, grid=(S // tr,),
        in_specs=in_specs, out_specs=[row, row, vec],
        out_shape=[jax.ShapeDtypeStruct((S, D), F32), jax.ShapeDtypeStruct((S, D), BF16),
                   jax.ShapeDtypeStruct((1, D), F32)],
        compiler_params=_params(("arbitrary",)),
    )(*ins)


def _loss_head(h, g, target, *, name):
    S, D = h.shape
    tr = _row_tile(S)

    def body(x_ref, g_ref, t_ref, loss_ref, dx_ref, dxb_ref, dg_ref):
        xv = x_ref[...]
        r = lax.rsqrt(jnp.mean(xv * xv, axis=-1, keepdims=True) + EPS)
        xh = xv * r
        gv = g_ref[...]
        err = xh * gv - t_ref[...]
        part = 0.5 * jnp.sum(jnp.mean(err * err, axis=-1, keepdims=True), axis=0, keepdims=True)
        dy = err * (1.0 / D)
        dxh = dy * gv
        dx = r * (dxh - xh * jnp.mean(dxh * xh, axis=-1, keepdims=True))
        dx_ref[...] = dx
        dxb_ref[...] = dx.astype(BF16)

        @pl.when(pl.program_id(0) == 0)
        def _():
            dg_ref[...] = jnp.zeros_like(dg_ref)
            loss_ref[...] = jnp.zeros_like(loss_ref)

        dg_ref[...] += jnp.sum(dy * xh, axis=0, keepdims=True)
        loss_ref[...] += jnp.broadcast_to(part, loss_ref.shape)

    row = pl.BlockSpec((tr, D), lambda i: (i, 0))
    vec = pl.BlockSpec((1, D), lambda i: (0, 0))
    return pl.pallas_call(
        body, name=name, grid=(S // tr,),
        in_specs=[row, vec, row],
        out_specs=[pl.BlockSpec((1, 128), lambda i: (0, 0)), row, row, vec],
        out_shape=[jax.ShapeDtypeStruct((1, 128), F32), jax.ShapeDtypeStruct((S, D), F32),
                   jax.ShapeDtypeStruct((S, D), BF16), jax.ShapeDtypeStruct((1, D), F32)],
        compiler_params=_params(("arbitrary",)),
    )(h, g, target)


def _sigmoid(x):
    return 1.0 / (1.0 + jnp.exp(-x))


def _ln_silu_fwd(cb, g, b, into, *, name):
    S, C = cb.shape
    tr = _row_tile(S)

    def body(x_ref, g_ref, b_ref, into_ref, o_ref):
        xv = x_ref[...]
        mu = jnp.mean(xv, axis=-1, keepdims=True)
        xc = xv - mu
        rs = lax.rsqrt(jnp.mean(xc * xc, axis=-1, keepdims=True) + EPS)
        l = xc * rs * g_ref[...] + b_ref[...]
        o_ref[...] = (l * _sigmoid(l)).astype(BF16)

    row = pl.BlockSpec((tr, C), lambda i: (i, 0))
    vec = pl.BlockSpec((1, C), lambda i: (0, 0))
    return pl.pallas_call(
        body, name=name, grid=(S // tr,), in_specs=[row, vec, vec, pl.BlockSpec(memory_space=pl.ANY)],
        out_specs=pl.BlockSpec((tr, C), lambda i: (i, 1)),
        out_shape=jax.ShapeDtypeStruct(into.shape, BF16), input_output_aliases={3: 0},
        compiler_params=_params(("parallel",)),
    )(cb, g, b, into)


def _ln_silu_bwd(cb, g, b, dy, col_block, *, name):
    S, C = cb.shape
    tr = _row_tile(S)

    def body(x_ref, g_ref, b_ref, dy_ref, dx_ref, dg_ref, db_ref):
        xv = x_ref[...]
        mu = jnp.mean(xv, axis=-1, keepdims=True)
        xc = xv - mu
        rs = lax.rsqrt(jnp.mean(xc * xc, axis=-1, keepdims=True) + EPS)
        xh = xc * rs
        gv = g_ref[...]
        l = xh * gv + b_ref[...]
        sg = _sigmoid(l)
        dl = dy_ref[...] * (sg + l * sg * (1.0 - sg))
        dxh = dl * gv
        dx_ref[...] = rs * (dxh - jnp.mean(dxh, axis=-1, keepdims=True)
                            - xh * jnp.mean(dxh * xh, axis=-1, keepdims=True))

        @pl.when(pl.program_id(0) == 0)
        def _():
            dg_ref[...] = jnp.zeros_like(dg_ref)
            db_ref[...] = jnp.zeros_like(db_ref)

        dg_ref[...] += jnp.sum(dl * xh, axis=0, keepdims=True)
        db_ref[...] += jnp.sum(dl, axis=0, keepdims=True)

    row = pl.BlockSpec((tr, C), lambda i: (i, 0))
    vec = pl.BlockSpec((1, C), lambda i: (0, 0))
    return pl.pallas_call(
        body, name=name, grid=(S // tr,),
        in_specs=[row, vec, vec, pl.BlockSpec((tr, C), lambda i: (i, col_block))],
        out_specs=[row, vec, vec],
        out_shape=[jax.ShapeDtypeStruct((S, C), F32), jax.ShapeDtypeStruct((1, C), F32),
                   jax.ShapeDtypeStruct((1, C), F32)],
        compiler_params=_params(("arbitrary",)),
    )(cb, g, b, dy)


def _attn_fwd(q, kv, *, name):
    S, D = q.shape
    M = kv.shape[0]
    hd = D // N_XHEADS
    scale = 1.0 / float(hd) ** 0.5
    tq = _row_tile(S)

    def body(q_ref, k_ref, v_ref, o_ref):
        for h in range(N_XHEADS):
            cols = slice(h * hd, (h + 1) * hd)
            s = lax.dot_general(q_ref[:, cols], k_ref[:, cols], (((1,), (1,)), ((), ())),
                                preferred_element_type=F32) * scale
            e = jnp.exp(s - jnp.max(s, axis=-1, keepdims=True))
            p = e / jnp.sum(e, axis=-1, keepdims=True)
            o = jnp.dot(p.astype(BF16), v_ref[:, cols], preferred_element_type=F32)
            o_ref[:, cols] = o.astype(BF16)

    return pl.pallas_call(
        body, name=name, grid=(S // tq,),
        in_specs=[pl.BlockSpec((tq, D), lambda i: (i, 0)), pl.BlockSpec((M, D), lambda i: (0, 0)),
                  pl.BlockSpec((M, D), lambda i: (0, 1))],
        out_specs=pl.BlockSpec((tq, D), lambda i: (i, 0)),
        out_shape=jax.ShapeDtypeStruct((S, D), BF16), compiler_params=_params(("parallel",)),
    )(q, kv, kv)


def _attn_bwd(q, kv, do, *, name):
    S, D = q.shape
    M = kv.shape[0]
    hd = D // N_XHEADS
    scale = 1.0 / float(hd) ** 0.5
    tq = _row_tile(S)

    def body(q_ref, k_ref, v_ref, do_ref, dq_ref, dkv_ref):
        @pl.when(pl.program_id(0) == 0)
        def _():
            dkv_ref[...] = jnp.zeros_like(dkv_ref)

        for h in range(N_XHEADS):
            cols = slice(h * hd, (h + 1) * hd)
            vcols = slice(D + h * hd, D + (h + 1) * hd)
            qh, kh, vh, doh = q_ref[:, cols], k_ref[:, cols], v_ref[:, cols], do_ref[:, cols]
            s = lax.dot_general(qh, kh, (((1,), (1,)), ((), ())), preferred_element_type=F32) * scale
            e = jnp.exp(s - jnp.max(s, axis=-1, keepdims=True))
            p = e / jnp.sum(e, axis=-1, keepdims=True)
            pb = p.astype(BF16)
            dp = lax.dot_general(doh, vh, (((1,), (1,)), ((), ())), preferred_element_type=F32)
            ds = (p * (dp - jnp.sum(dp * p, axis=-1, keepdims=True)) * scale).astype(BF16)
            dq_ref[:, cols] = jnp.dot(ds, kh, preferred_element_type=F32).astype(BF16)
            dkv_ref[:, cols] += lax.dot_general(ds, qh, (((0,), (0,)), ((), ())), preferred_element_type=F32)
            dkv_ref[:, vcols] += lax.dot_general(pb, doh, (((0,), (0,)), ((), ())), preferred_element_type=F32)

    row = pl.BlockSpec((tq, D), lambda i: (i, 0))
    return pl.pallas_call(
        body, name=name, grid=(S // tq,),
        in_specs=[row, pl.BlockSpec((M, D), lambda i: (0, 0)), pl.BlockSpec((M, D), lambda i: (0, 1)), row],
        out_specs=[row, pl.BlockSpec((M, 2 * D), lambda i: (0, 0))],
        out_shape=[jax.ShapeDtypeStruct((S, D), BF16), jax.ShapeDtypeStruct((M, 2 * D), F32)],
        compiler_params=_params(("arbitrary",)),
    )(q, kv, kv, do)


def _delayed(win, j, pad):
    return (win if j == 0 else pltpu.roll(win, j, 0))[pad:, :]


def _advanced(win, j, ch):
    return (win if j == 0 else pltpu.roll(win, win.shape[0] - j, 0))[:ch, :]


def _mixer_a_fwd(z, w, *, name):
    S = z.shape[0]
    C = w.shape[1]
    nb = C // 128
    ch = min(CONV_CHUNK, S)

    def body(b_ref, c_ref, h_ref, w_ref, y_ref, xp):
        xp[0:PAD_A, :] = jnp.zeros((PAD_A, 128), F32)
        xp[PAD_A:, :] = c_ref[...] * h_ref[...]

        def chunk(i, carry):
            base = pl.multiple_of(i * ch, ch)
            win = xp[pl.ds(base, ch + PAD_A), :]
            acc = _delayed(win, 0, PAD_A) * w_ref[K_A - 1:K_A, :]
            for j in range(1, K_A):
                acc = acc + _delayed(win, j, PAD_A) * w_ref[K_A - 1 - j:K_A - j, :]
            y_ref[pl.ds(base, ch), :] = (b_ref[pl.ds(base, ch), :] * acc).astype(BF16)
            return carry

        lax.fori_loop(0, S // ch, chunk, 0)

    def col(g):
        return pl.BlockSpec((S, 128), lambda j: (0, g * nb + j))

    return pl.pallas_call(
        body, name=name, grid=(nb,),
        in_specs=[col(0), col(1), col(2), pl.BlockSpec((K_A, 128), lambda j: (0, j))],
        out_specs=pl.BlockSpec((S, 128), lambda j: (0, j)),
        out_shape=jax.ShapeDtypeStruct((S, 2 * C), BF16),
        scratch_shapes=[pltpu.VMEM((PAD_A + S, 128), F32)],
        compiler_params=_params(("parallel",)),
    )(z, z, z, w)


def _mixer_a_bwd(z, w, dy, *, name):
    S = z.shape[0]
    C = w.shape[1]
    nb = C // 128
    ch = min(CONV_CHUNK, S)

    def body(b_ref, c_ref, h_ref, w_ref, dy_ref, db_ref, dc_ref, dh_ref, dw_ref, xp, dp):
        xp[0:PAD_A, :] = jnp.zeros((PAD_A, 128), F32)
        xp[PAD_A:, :] = c_ref[...] * h_ref[...]
        dp[S:, :] = jnp.zeros((PAD_A, 128), F32)
        dw_ref[...] = jnp.zeros_like(dw_ref)

        def chunk(i, carry):
            base = pl.multiple_of(i * ch, ch)
            win = xp[pl.ds(base, ch + PAD_A), :]
            dya = dy_ref[pl.ds(base, ch), :]
            dcv = dya * b_ref[pl.ds(base, ch), :]
            dp[pl.ds(base, ch), :] = dcv
            acc = None
            for j in range(K_A):
                xs = _delayed(win, j, PAD_A)
                k = K_A - 1 - j
                term = xs * w_ref[k:k + 1, :]
                acc = term if acc is None else acc + term
                dw_ref[k:k + 1, :] += jnp.sum(dcv * xs, axis=0, keepdims=True)
            db_ref[pl.ds(base, ch), :] = (dya * acc).astype(BF16)
            return carry

        lax.fori_loop(0, S // ch, chunk, 0)

        def chunk2(i, carry):
            base = pl.multiple_of(i * ch, ch)
            win = dp[pl.ds(base, ch + PAD_A), :]
            acc = None
            for j in range(K_A):
                term = _advanced(win, j, ch) * w_ref[K_A - 1 - j:K_A - j, :]
                acc = term if acc is None else acc + term
            dc_ref[pl.ds(base, ch), :] = (acc * h_ref[pl.ds(base, ch), :]).astype(BF16)
            dh_ref[pl.ds(base, ch), :] = (acc * c_ref[pl.ds(base, ch), :]).astype(BF16)
            return carry

        lax.fori_loop(0, S // ch, chunk2, 0)

    def col(g):
        return pl.BlockSpec((S, 128), lambda j: (0, g * nb + j))

    out_col = pl.BlockSpec((S, 128), lambda j: (0, j))
    wspec = pl.BlockSpec((K_A, 128), lambda j: (0, j))
    return pl.pallas_call(
        body, name=name, grid=(nb,),
        in_specs=[col(0), col(1), col(2), wspec, out_col],
        out_specs=[out_col, out_col, out_col, wspec],
        out_shape=[jax.ShapeDtypeStruct((S, C), BF16)] * 3 + [jax.ShapeDtypeStruct((K_A, C), F32)],
        scratch_shapes=[pltpu.VMEM((PAD_A + S, 128), F32), pltpu.VMEM((S + PAD_A, 128), F32)],
        compiler_params=_params(("parallel",)),
    )(z, z, z, w, dy)


def _mixer_b_fwd(z, w, bias, *, name):
    S = z.shape[0]
    C = w.shape[1]
    nb = C // 128
    ch = min(CONV_CHUNK, S)

    def body(v_ref, g_ref, w_ref, bias_ref, cb_ref, xp):
        xp[0:PAD_B, :] = jnp.zeros((PAD_B, 128), F32)
        xp[PAD_B:, :] = v_ref[...] * _sigmoid(g_ref[...])

        def chunk(i, carry):
            base = pl.multiple_of(i * ch, ch)
            win = xp[pl.ds(base, ch + PAD_B), :]
            acc = None
            for j in range(K_B):
                term = _delayed(win, j, PAD_B) * w_ref[K_B - 1 - j:K_B - j, :]
                acc = term if acc is None else acc + term
            cb_ref[pl.ds(base, ch), :] = acc + bias_ref[...]
            return carry

        lax.fori_loop(0, S // ch, chunk, 0)

    def col(g):
        return pl.BlockSpec((S, 128), lambda j: (0, g * nb + j))

    return pl.pallas_call(
        body, name=name, grid=(nb,),
        in_specs=[col(3), col(4), pl.BlockSpec((K_B, 128), lambda j: (0, j)),
                  pl.BlockSpec((1, 128), lambda j: (0, j))],
        out_specs=pl.BlockSpec((S, 128), lambda j: (0, j)),
        out_shape=jax.ShapeDtypeStruct((S, C), F32),
        scratch_shapes=[pltpu.VMEM((PAD_B + S, 128), F32)],
        compiler_params=_params(("parallel",)),
    )(z, z, w, bias)


def _mixer_b_bwd(z, w, dcb, *, name):
    S = z.shape[0]
    C = w.shape[1]
    nb = C // 128
    ch = min(CONV_CHUNK, S)

    def body(v_ref, g_ref, w_ref, dcb_ref, dv_ref, dg_ref, dw_ref, dbias_ref, xp, dp):
        xp[0:PAD_B, :] = jnp.zeros((PAD_B, 128), F32)
        xp[PAD_B:, :] = v_ref[...] * _sigmoid(g_ref[...])
        dp[0:S, :] = dcb_ref[...]
        dp[S:, :] = jnp.zeros((PAD_B, 128), F32)
        dw_ref[...] = jnp.zeros_like(dw_ref)
        dbias_ref[...] = jnp.sum(dcb_ref[...], axis=0, keepdims=True)

        def chunk(i, carry):
            base = pl.multiple_of(i * ch, ch)
            win = xp[pl.ds(base, ch + PAD_B), :]
            d = dcb_ref[pl.ds(base, ch), :]
            for j in range(K_B):
                k = K_B - 1 - j
                dw_ref[k:k + 1, :] += jnp.sum(d * _delayed(win, j, PAD_B), axis=0, keepdims=True)
            return carry

        lax.fori_loop(0, S // ch, chunk, 0)

        def chunk2(i, carry):
            base = pl.multiple_of(i * ch, ch)
            win = dp[pl.ds(base, ch + PAD_B), :]
            acc = None
            for j in range(K_B):
                term = _advanced(win, j, ch) * w_ref[K_B - 1 - j:K_B - j, :]
                acc = term if acc is None else acc + term
            sg = _sigmoid(g_ref[pl.ds(base, ch), :])
            vv = v_ref[pl.ds(base, ch), :]
            dv_ref[pl.ds(base, ch), :] = (acc * sg).astype(BF16)
            dg_ref[pl.ds(base, ch), :] = (acc * vv * sg * (1.0 - sg)).astype(BF16)
            return carry

        lax.fori_loop(0, S // ch, chunk2, 0)

    def col(g):
        return pl.BlockSpec((S, 128), lambda j: (0, g * nb + j))

    out_col = pl.BlockSpec((S, 128), lambda j: (0, j))
    wspec = pl.BlockSpec((K_B, 128), lambda j: (0, j))
    bspec = pl.BlockSpec((1, 128), lambda j: (0, j))
    return pl.pallas_call(
        body, name=name, grid=(nb,),
        in_specs=[col(3), col(4), wspec, out_col],
        out_specs=[out_col, out_col, wspec, bspec],
        out_shape=[jax.ShapeDtypeStruct((S, C), BF16)] * 2
        + [jax.ShapeDtypeStruct((K_B, C), F32), jax.ShapeDtypeStruct((1, C), F32)],
        scratch_shapes=[pltpu.VMEM((PAD_B + S, 128), F32), pltpu.VMEM((S + PAD_B, 128), F32)],
        compiler_params=_params(("parallel",)),
    )(z, z, w, dcb)


def _ew_tile(R):
    for t in (512, 256, 128, 64, 32, 16, 8):
        if R % t == 0:
            return t
    return R


def _pair_sum(g, r, meta, *, name):
    n, a, b = g.shape
    ah = a // 2
    tr = _pick((1024, 512, 256, 128, 64, 32, 16), ah)
    nh = ah // tr

    def body(meta_ref, g_ref, r_ref, o_ref):
        o_ref[...] = (g_ref[...] + r_ref[...]).astype(BF16)

    half = pl.BlockSpec((None, tr, b), lambda j, i, meta_ref: (j, i, 0))
    return pl.pallas_call(
        body, name=name,
        grid_spec=pltpu.PrefetchScalarGridSpec(
            num_scalar_prefetch=1, grid=(n, nh),
            in_specs=[pl.BlockSpec((None, tr, b), lambda j, i, meta_ref: (j, meta_ref[0] * nh + i, 0)), half],
            out_specs=half),
        out_shape=jax.ShapeDtypeStruct((n, ah, b), BF16), compiler_params=_params(("parallel", "parallel")),
    )(meta, g, r)


def _chip_sum(p, q, meta, *, name):
    n, ah, b = p.shape
    tr = _pick((1024, 512, 256, 128, 64, 32, 16), ah)
    nh = ah // tr

    def body(meta_ref, p_ref, q1_ref, q2_ref, q3_ref, o_ref):
        o_ref[...] = ((p_ref[...].astype(F32) + q1_ref[...].astype(F32)) + q2_ref[...].astype(F32)
                      ) + q3_ref[...].astype(F32)

    def piece(mask):
        return pl.BlockSpec((None, tr, b), lambda i, meta_ref: (meta_ref[1] ^ mask, i, 0))

    return pl.pallas_call(
        body, name=name,
        grid_spec=pltpu.PrefetchScalarGridSpec(
            num_scalar_prefetch=1, grid=(nh,),
            in_specs=[piece(0), piece(1), piece(2), piece(3)],
            out_specs=pl.BlockSpec((tr, b), lambda i, meta_ref: (meta_ref[0] * nh + i, 0))),
        out_shape=jax.ShapeDtypeStruct((2 * ah, b), F32), compiler_params=_params(("parallel",)),
    )(meta, p, q, q, q)


def _sum_leading(x, *, name):
    n, R, C = x.shape
    tr = _ew_tile(R)

    def body(x_ref, o_ref):
        acc = x_ref[0].astype(F32)
        for k in range(1, n):
            acc = acc + x_ref[k].astype(F32)
        o_ref[...] = acc

    return pl.pallas_call(
        body, name=name, grid=(R // tr,),
        in_specs=[pl.BlockSpec((n, tr, C), lambda i: (0, i, 0))],
        out_specs=pl.BlockSpec((tr, C), lambda i: (i, 0)),
        out_shape=jax.ShapeDtypeStruct((R, C), F32), compiler_params=_params(("parallel",)),
    )(x)


def _adamw(w, g, m, v, *, name):
    R, C = w.shape
    tr = _ew_tile(R)

    def body(w_ref, g_ref, m_ref, v_ref, d_ref, nm_ref, nv_ref):
        gv = g_ref[...]
        nm = ADAM_B1 * m_ref[...] + (1.0 - ADAM_B1) * gv
        nv = ADAM_B2 * v_ref[...] + (1.0 - ADAM_B2) * (gv * gv)
        m_hat = nm / (1.0 - ADAM_B1 ** ADAM_STEP)
        v_hat = nv / (1.0 - ADAM_B2 ** ADAM_STEP)
        d_ref[...] = -ADAM_LR * (m_hat / (jnp.sqrt(v_hat) + ADAM_EPS) + ADAM_WD * w_ref[...])
        nm_ref[...] = nm
        nv_ref[...] = nv

    row = pl.BlockSpec((tr, C), lambda i: (i, 0))
    return pl.pallas_call(
        body, name=name, grid=(R // tr,), in_specs=[row] * 4, out_specs=[row] * 3,
        out_shape=[jax.ShapeDtypeStruct((R, C), F32)] * 3, compiler_params=_params(("parallel",)),
    )(w, g, m, v)


def _adamw_layers(w, g0, g1, m, v, *, name):
    _, a, b = w.shape
    tr = _pick((512, 256, 128, 64, 32, 16, 8), a)

    def body(w_ref, g0_ref, g1_ref, m_ref, v_ref, g_ref, d_ref, nm_ref, nv_ref):
        gv = jnp.where(pl.program_id(1) == 0, g0_ref[...], g1_ref[...])
        nm = ADAM_B1 * m_ref[...] + (1.0 - ADAM_B1) * gv
        nv = ADAM_B2 * v_ref[...] + (1.0 - ADAM_B2) * (gv * gv)
        m_hat = nm / (1.0 - ADAM_B1 ** ADAM_STEP)
        v_hat = nv / (1.0 - ADAM_B2 ** ADAM_STEP)
        g_ref[...] = gv
        d_ref[...] = -ADAM_LR * (m_hat / (jnp.sqrt(v_hat) + ADAM_EPS) + ADAM_WD * w_ref[...])
        nm_ref[...] = nm
        nv_ref[...] = nv

    lay = pl.BlockSpec((None, tr, b), lambda i, l: (l, i, 0))
    row = pl.BlockSpec((tr, b), lambda i, l: (i, 0))
    return pl.pallas_call(
        body, name=name, grid=(a // tr, 2), in_specs=[lay, row, row, lay, lay], out_specs=[lay] * 4,
        out_shape=[jax.ShapeDtypeStruct(w.shape, F32)] * 4, compiler_params=_params(("parallel", "arbitrary")),
    )(w, g0, g1, m, v)


ANY = pl.BlockSpec(memory_space=pl.ANY)


def _place():
    x, y, c = lax.axis_index("x"), lax.axis_index("y"), lax.axis_index("c")
    return x, y, c, 2 * x + y


def _other_chip(x, y, mask):
    px = 1 - x if mask & 2 else x
    py = 1 - y if mask & 1 else y
    return px, py, 2 * px + py


MASKS = (1, 2, 3)


def _half(ref, c, lead=()):
    ah = ref.shape[-2] // 2
    return ref.at[(*lead, pl.ds(c * ah, ah), slice(None))]


def _gather_weights(owns, *, name):
    n = len(owns)

    def body(*refs):
        own, out = refs[:n], refs[n:2 * n]
        send_sems, recv_sems = refs[2 * n], refs[2 * n + 1]
        x, y, c, chip = _place()
        sends = []
        for i in range(n):
            for k, mask in enumerate(MASKS):
                px, py, _ = _other_chip(x, y, mask)
                cp = pltpu.make_async_remote_copy(_half(own[i], c), _half(out[i], c, (chip,)),
                                                  send_sems.at[i, k], recv_sems.at[i, k],
                                                  device_id=(px, py, c), device_id_type=MESH)
                cp.start()
                sends.append(cp)
        for i in range(n):
            for k, mask in enumerate(MASKS):
                px, py, pchip = _other_chip(x, y, mask)
                got = _half(out[i], c, (pchip,))
                pltpu.make_async_remote_copy(got, got, send_sems.at[i, k], recv_sems.at[i, k],
                                             device_id=(px, py, c), device_id_type=MESH).wait_recv()
                cp = pltpu.make_async_remote_copy(got, got, send_sems.at[i, 3 + k], recv_sems.at[i, 3 + k],
                                                  device_id=(x, y, 1 - c), device_id_type=MESH)
                cp.start()
                sends.append(cp)
        for i in range(n):
            for k, mask in enumerate(MASKS):
                _, _, pchip = _other_chip(x, y, mask)
                theirs = _half(out[i], 1 - c, (pchip,))
                pltpu.make_async_remote_copy(theirs, theirs, send_sems.at[i, 3 + k], recv_sems.at[i, 3 + k],
                                             device_id=(x, y, 1 - c), device_id_type=MESH).wait_recv()
        for cp in sends:
            cp.wait_send()

    return pl.pallas_call(
        body, name=name, in_specs=[ANY] * n, out_specs=[ANY] * n,
        out_shape=[jax.ShapeDtypeStruct((N_CHIPS, *o.shape), o.dtype) for o in owns],
        scratch_shapes=[pltpu.SemaphoreType.DMA((n, 6)), pltpu.SemaphoreType.DMA((n, 6))],
    )(*owns)


SIBLING_ID = 0


def _sibling_barrier(x, y, c):
    sem = pltpu.get_barrier_semaphore()
    pl.semaphore_signal(sem, inc=1, device_id=(x, y, 1 - c), device_id_type=MESH)
    pl.semaphore_wait(sem, 1)


def _swap_halves(gs, *, name):
    n = len(gs)

    def body(*refs):
        g, out = refs[:n], refs[n:2 * n]
        send_sems, recv_sems = refs[2 * n], refs[2 * n + 1]
        x, y, c, _ = _place()
        _sibling_barrier(x, y, c)
        cps = []
        for i in range(n):
            ah = g[i].shape[1] // 2
            cp = pltpu.make_async_remote_copy(g[i].at[:, pl.ds((1 - c) * ah, ah), :], out[i],
                                              send_sems.at[i], recv_sems.at[i],
                                              device_id=(x, y, 1 - c), device_id_type=MESH)
            cp.start()
            cps.append(cp)
        for cp in cps:
            cp.wait()

    return pl.pallas_call(
        body, name=name, in_specs=[ANY] * n, out_specs=[ANY] * n,
        out_shape=[jax.ShapeDtypeStruct((g.shape[0], g.shape[1] // 2, g.shape[2]), g.dtype) for g in gs],
        scratch_shapes=[pltpu.SemaphoreType.DMA((n,)), pltpu.SemaphoreType.DMA((n,))],
        compiler_params=pltpu.CompilerParams(collective_id=SIBLING_ID),
    )(*gs)


def _scatter_to_chips(ps, *, name):
    n = len(ps)

    def body(*refs):
        p, out = refs[:n], refs[n:2 * n]
        send_sems, recv_sems = refs[2 * n], refs[2 * n + 1]
        x, y, c, chip = _place()
        sends = []
        for i in range(n):
            for k, mask in enumerate(MASKS):
                px, py, pchip = _other_chip(x, y, mask)
                cp = pltpu.make_async_remote_copy(p[i].at[pchip], out[i].at[chip], send_sems.at[i, k],
                                                  recv_sems.at[i, k], device_id=(px, py, c), device_id_type=MESH)
                cp.start()
                sends.append(cp)
        for i in range(n):
            for k, mask in enumerate(MASKS):
                px, py, pchip = _other_chip(x, y, mask)
                pltpu.make_async_remote_copy(p[i].at[pchip], out[i].at[pchip], send_sems.at[i, k],
                                             recv_sems.at[i, k], device_id=(px, py, c),
                                             device_id_type=MESH).wait_recv()
        for cp in sends:
            cp.wait_send()

    return pl.pallas_call(
        body, name=name, in_specs=[ANY] * n, out_specs=[ANY] * n,
        out_shape=[jax.ShapeDtypeStruct(p.shape, p.dtype) for p in ps],
        scratch_shapes=[pltpu.SemaphoreType.DMA((n, 3)), pltpu.SemaphoreType.DMA((n, 3))],
    )(*ps)


HBM_SPEC = pl.BlockSpec(memory_space=pltpu.HBM)
SEM_SPEC = pl.BlockSpec(memory_space=pltpu.SEMAPHORE)
EFFECT = pltpu.SideEffectType.DATAFLOW_SIDE_EFFECTING


def _ici_ends(src, land, gather, x, y, c, chip, mask):
    px, py, pchip = _other_chip(x, y, mask)
    if gather:
        return _half(src, c), _half(land, c, (chip,)), _half(land, c, (pchip,)), (px, py, c)
    return src.at[pchip], land.at[chip], land.at[pchip], (px, py, c)


def _ici_start(groups, land_groups, gather, *, name, after=None):
    sizes = [len(g) for g in groups]
    n = sum(sizes)
    ng = len(groups)
    deps = [] if after is None else [after]

    def body(*refs):
        src, land = refs[:n], refs[n:2 * n]
        sems = refs[2 * n + len(deps):2 * n + len(deps) + 2 * ng]
        token = refs[-1]
        x, y, c, chip = _place()
        i = 0
        for g in range(ng):
            for j in range(sizes[g]):
                for k, mask in enumerate(MASKS):
                    s, d, _, peer = _ici_ends(src[i], land[i], gather, x, y, c, chip, mask)
                    pltpu.make_async_remote_copy(s, d, sems[2 * g].at[3 * j + k], sems[2 * g + 1].at[3 * j + k],
                                                 device_id=peer, device_id_type=MESH).start()
                i += 1
        token[...] = jnp.zeros_like(token)

    lands = [pltpu.with_memory_space_constraint(lax.empty(s.shape, s.dtype), pltpu.HBM)
             for g in land_groups for s in g]
    srcs = [pltpu.with_memory_space_constraint(s, pltpu.HBM) for g in groups for s in g]
    sem_shapes = [pltpu.SemaphoreType.DMA((3 * m,)) for m in sizes for _ in range(2)]
    out = pl.pallas_call(
        body, name=name,
        out_shape=(*sem_shapes, *[pltpu.HBM(s.shape, s.dtype) for s in srcs],
                   *[pltpu.HBM(s.shape, s.dtype) for s in lands], jax.ShapeDtypeStruct((8, 128), F32)),
        in_specs=[HBM_SPEC] * (2 * n) + [ANY] * len(deps),
        out_specs=(*[SEM_SPEC] * (2 * ng), *[HBM_SPEC] * (2 * n), pl.BlockSpec(memory_space=pltpu.VMEM)),
        input_output_aliases={i: 2 * ng + i for i in range(2 * n)},
        compiler_params=pltpu.CompilerParams(has_side_effects=EFFECT),
    )(*srcs, *lands, *deps)
    res, pos = [], 0
    for g in range(ng):
        res.append((out[2 * g], out[2 * g + 1], list(out[2 * ng + pos:2 * ng + pos + sizes[g]]),
                    list(out[2 * ng + n + pos:2 * ng + n + pos + sizes[g]])))
        pos += sizes[g]
    return res, out[-1]


def _ici_wait(send_sems, recv_sems, srcs, lands, gather, after, *, name):
    n = len(srcs)

    def body(*refs):
        src, land = refs[:n], refs[n:2 * n]
        send_sems, recv_sems = refs[2 * n], refs[2 * n + 1]
        x, y, c, chip = _place()
        for i in range(n):
            for k, mask in enumerate(MASKS):
                s, d, got, peer = _ici_ends(src[i], land[i], gather, x, y, c, chip, mask)
                pltpu.make_async_remote_copy(s, d, send_sems.at[3 * i + k], recv_sems.at[3 * i + k],
                                             device_id=peer, device_id_type=MESH).wait_send()
                pltpu.make_async_remote_copy(s, got, send_sems.at[3 * i + k], recv_sems.at[3 * i + k],
                                             device_id=peer, device_id_type=MESH).wait_recv()

    out = pl.pallas_call(
        body, name=name,
        out_shape=tuple(pltpu.HBM(s.shape, s.dtype) for s in (*srcs, *lands)),
        in_specs=[HBM_SPEC] * (2 * n) + [SEM_SPEC, SEM_SPEC, ANY],
        out_specs=tuple([HBM_SPEC] * (2 * n)),
        input_output_aliases={i: i for i in range(2 * n)},
        compiler_params=pltpu.CompilerParams(has_side_effects=EFFECT),
    )(*srcs, *lands, send_sems, recv_sems, after)
    return list(out[:n]), list(out[n:])


def _relay_halves(gs, *, name):
    n = len(gs)

    def body(*refs):
        out = refs[n:2 * n]
        send_sems, recv_sems = refs[2 * n], refs[2 * n + 1]
        x, y, c, _ = _place()
        _sibling_barrier(x, y, c)
        cps = []
        for i in range(n):
            for k, mask in enumerate(MASKS):
                _, _, pchip = _other_chip(x, y, mask)
                got = _half(out[i], c, (pchip,))
                cp = pltpu.make_async_remote_copy(got, got, send_sems.at[i, k], recv_sems.at[i, k],
                                                  device_id=(x, y, 1 - c), device_id_type=MESH)
                cp.start()
                cps.append(cp)
        for i in range(n):
            for k, mask in enumerate(MASKS):
                _, _, pchip = _other_chip(x, y, mask)
                theirs = _half(out[i], 1 - c, (pchip,))
                pltpu.make_async_remote_copy(theirs, theirs, send_sems.at[i, k], recv_sems.at[i, k],
                                             device_id=(x, y, 1 - c), device_id_type=MESH).wait_recv()
        for cp in cps:
            cp.wait_send()

    return pl.pallas_call(
        body, name=name, in_specs=[ANY] * n, out_specs=[ANY] * n,
        out_shape=[jax.ShapeDtypeStruct(g.shape, g.dtype) for g in gs],
        input_output_aliases={i: i for i in range(n)},
        scratch_shapes=[pltpu.SemaphoreType.DMA((n, 3)), pltpu.SemaphoreType.DMA((n, 3))],
        compiler_params=pltpu.CompilerParams(collective_id=SIBLING_ID),
    )(*gs)


def _share_halves(gs, *, name):
    n = len(gs)

    def body(*refs):
        out = refs[n:2 * n]
        send_sems, recv_sems = refs[2 * n], refs[2 * n + 1]
        x, y, c, _ = _place()
        _sibling_barrier(x, y, c)
        cps = []
        for i in range(n):
            cp = pltpu.make_async_remote_copy(_half(out[i], c), _half(out[i], c), send_sems.at[i], recv_sems.at[i],
                                              device_id=(x, y, 1 - c), device_id_type=MESH)
            cp.start()
            cps.append(cp)
        for i in range(n):
            theirs = _half(out[i], 1 - c)
            pltpu.make_async_remote_copy(theirs, theirs, send_sems.at[i], recv_sems.at[i],
                                         device_id=(x, y, 1 - c), device_id_type=MESH).wait_recv()
        for cp in cps:
            cp.wait_send()

    return pl.pallas_call(
        body, name=name, in_specs=[ANY] * n, out_specs=[ANY] * n,
        out_shape=[jax.ShapeDtypeStruct(g.shape, g.dtype) for g in gs],
        input_output_aliases={i: i for i in range(n)},
        scratch_shapes=[pltpu.SemaphoreType.DMA((n,)), pltpu.SemaphoreType.DMA((n,))],
        compiler_params=pltpu.CompilerParams(collective_id=SIBLING_ID),
    )(*gs)


def _gather_all(buf, *, name):
    r, L = buf.shape
    vmem = pl.BlockSpec(memory_space=pltpu.VMEM)
    masks = tuple(range(1, N_DEV))

    def body(buf_ref, out_ref, send_sems, recv_sems):
        x, y, c, _ = _place()
        me = 4 * x + 2 * y + c
        out_ref[me] = buf_ref[...]
        sends = []
        for k, mask in enumerate(masks):
            px = 1 - x if mask & 4 else x
            py = 1 - y if mask & 2 else y
            pc = 1 - c if mask & 1 else c
            cp = pltpu.make_async_remote_copy(buf_ref, out_ref.at[me], send_sems.at[k], recv_sems.at[k],
                                              device_id=(px, py, pc), device_id_type=MESH)
            cp.start()
            sends.append(cp)
        for k, mask in enumerate(masks):
            px = 1 - x if mask & 4 else x
            py = 1 - y if mask & 2 else y
            pc = 1 - c if mask & 1 else c
            pltpu.make_async_remote_copy(buf_ref, out_ref.at[4 * px + 2 * py + pc], send_sems.at[k],
                                         recv_sems.at[k], device_id=(px, py, pc), device_id_type=MESH).wait_recv()
        for cp in sends:
            cp.wait_send()

    return pl.pallas_call(
        body, name=name, in_specs=[vmem], out_specs=vmem,
        out_shape=jax.ShapeDtypeStruct((N_DEV, r, L), buf.dtype),
        scratch_shapes=[pltpu.SemaphoreType.DMA((N_DEV - 1,)), pltpu.SemaphoreType.DMA((N_DEV - 1,))],
    )(buf)


def _pack(arrs, lanes, row_mult=8):
    flat = jnp.concatenate([a.reshape(-1) for a in arrs])
    rows = -(-flat.shape[0] // lanes)
    rows = -(-rows // row_mult) * row_mult
    flat = jnp.pad(flat, (0, rows * lanes - flat.shape[0]))
    return flat.reshape(rows, lanes)


def _unpack(buf, shapes):
    flat = buf.reshape(-1)
    out, pos = [], 0
    for s in shapes:
        n = 1
        for d in s:
            n *= d
        out.append(flat[pos:pos + n].reshape(s))
        pos += n
    return out


def kernel(x, mem, norm_mix_g, w_in, conv_a_w, conv_b_w, conv_b_bias, ln_b_g, ln_b_b, w_out, norm_x_g, norm_mem_g, w_q, w_kv, w_xo, norm_ffn_g, w_up, w_down, final_g, loss_target, m_norm_mix_g, m_w_in, m_conv_a_w, m_conv_b_w, m_conv_b_bias, m_ln_b_g, m_ln_b_b, m_w_out, m_norm_x_g, m_norm_mem_g, m_w_q, m_w_kv, m_w_xo, m_norm_ffn_g, m_w_up, m_w_down, m_final_g, v_norm_mix_g, v_w_in, v_conv_a_w, v_conv_b_w, v_conv_b_bias, v_ln_b_g, v_ln_b_b, v_w_out, v_norm_x_g, v_norm_mem_g, v_w_q, v_w_kv, v_w_xo, v_norm_ffn_g, v_w_up, v_w_down, v_final_g):
    W = dict(norm_mix_g=norm_mix_g, w_in=w_in, conv_a_w=conv_a_w, conv_b_w=conv_b_w, conv_b_bias=conv_b_bias,
             ln_b_g=ln_b_g, ln_b_b=ln_b_b, w_out=w_out, norm_x_g=norm_x_g, norm_mem_g=norm_mem_g, w_q=w_q,
             w_kv=w_kv, w_xo=w_xo, norm_ffn_g=norm_ffn_g, w_up=w_up, w_down=w_down, final_g=final_g)
    MO = dict(norm_mix_g=m_norm_mix_g, w_in=m_w_in, conv_a_w=m_conv_a_w, conv_b_w=m_conv_b_w,
              conv_b_bias=m_conv_b_bias, ln_b_g=m_ln_b_g, ln_b_b=m_ln_b_b, w_out=m_w_out, norm_x_g=m_norm_x_g,
              norm_mem_g=m_norm_mem_g, w_q=m_w_q, w_kv=m_w_kv, w_xo=m_w_xo, norm_ffn_g=m_norm_ffn_g,
              w_up=m_w_up, w_down=m_w_down, final_g=m_final_g)
    VO = dict(norm_mix_g=v_norm_mix_g, w_in=v_w_in, conv_a_w=v_conv_a_w, conv_b_w=v_conv_b_w,
              conv_b_bias=v_conv_b_bias, ln_b_g=v_ln_b_g, ln_b_b=v_ln_b_b, w_out=v_w_out, norm_x_g=v_norm_x_g,
              norm_mem_g=v_norm_mem_g, w_q=v_w_q, w_kv=v_w_kv, w_xo=v_w_xo, norm_ffn_g=v_norm_ffn_g,
              w_up=v_w_up, w_down=v_w_down, final_g=v_final_g)
    names = list(W.keys())
    depth = norm_mix_g.shape[0]
    assert depth == 2, "the exchange splits the weights into one layer per core of a chip"
    c_idx = lax.axis_index("c")
    chip_idx = 2 * lax.axis_index("x") + lax.axis_index("y")

    xs = x[0]
    ms = mem[0]
    tgt = loss_target[0]
    S, D = xs.shape
    c_a = conv_a_w.shape[-1] * N_CHIPS
    c_loc = conv_a_w.shape[-1]

    meta = jnp.stack([c_idx, chip_idx]).astype(jnp.int32)
    shard_axis = dict(BIG)
    own = {(l, n): W[n][l].astype(BF16) for l in range(depth) for n, _ in BIG}

    conv_local = _pack([conv_a_w, conv_b_w], 128)
    conv_all = _gather_all(conv_local, name="gather_conv_weights")
    order = [(l, gi) for l in range(depth) for gi in range(len(FWD_GROUPS))]
    src_groups = [[own[(l, n)] for n in FWD_GROUPS[gi]] for l, gi in order]
    started, gather_token = _ici_start(
        src_groups, [[jax.ShapeDtypeStruct((N_CHIPS, *s.shape), s.dtype) for s in g] for g in src_groups], True,
        name="gather_weights_start", after=conv_all)
    started = dict(zip(order, started))
    Wb = [dict() for _ in range(depth)]

    def weights_ready(l, gi, after):
        send_sems, recv_sems, srcs, lands = started[(l, gi)]
        srcs, lands = _ici_wait(send_sems, recv_sems, srcs, lands, True, after, name=f"gather_weights_l{l}_g{gi}_wait")
        full = _relay_halves(lands, name=f"gather_weights_l{l}_g{gi}_relay")
        for n, g, o in zip(FWD_GROUPS[gi], full, srcs):
            g = lax.dynamic_update_slice(g, o[None], (chip_idx, 0, 0))
            Wb[l][n] = g.reshape(-1, g.shape[-1]) if shard_axis[n] == 0 else g
    na = depth * K_A * c_loc
    nbw = depth * K_B * c_loc
    ca_parts, cb_parts = [], []
    for j in range(N_CHIPS):
        fl = conv_all[2 * j].reshape(-1)
        ca_parts.append(fl[:na].reshape(depth, K_A, c_loc))
        cb_parts.append(fl[na:na + nbw].reshape(depth, K_B, c_loc))
    conv_a_full = jnp.concatenate(ca_parts, axis=-1)
    conv_b_full = jnp.concatenate(cb_parts, axis=-1)

    saved = []
    h = xs
    for l in range(depth):
        wl = Wb[l]
        t = f"l{l}_"
        if l == 0:
            u = _rms_fwd(h, norm_mix_g[l:l + 1], name=t + "rms_mix", after=gather_token)
        weights_ready(l, 0, u)
        z = _mm(u, wl["w_in"], b_stack=True, name=t + "mm_in", bm=2048)
        y_a = _mixer_a_fwd(z, conv_a_full[l], name=t + "mixer_a")
        cb = _mixer_b_fwd(z, conv_b_full[l], conv_b_bias[l:l + 1], name=t + "mixer_b")
        yy = _ln_silu_fwd(cb, ln_b_g[l:l + 1], ln_b_b[l:l + 1], y_a, name=t + "ln_silu")
        weights_ready(l, 1, yy)
        h2, q_in = _mm(yy, wl["w_out"], res=h, epi="rms_fwd", norm=norm_x_g[l:l + 1], name=t + "mm_out")
        q = _mm(q_in, wl["w_q"], out_dtype=BF16, name=t + "mm_q")
        mn = _rms_fwd(ms, norm_mem_g[l:l + 1], name=t + "rms_mem")
        kv = _mm(mn, wl["w_kv"], b_stack=True, out_dtype=BF16, name=t + "mm_kv")
        o = _attn_fwd(q, kv, name=t + "attn")
        h3, u3 = _mm(o, wl["w_xo"], res=h2, epi="rms_fwd", norm=norm_ffn_g[l:l + 1], name=t + "mm_xo")
        weights_ready(l, 2, h3)
        a_pre, hh = _mm(u3, wl["w_up"], b_stack=True, out_dtype=BF16, epi="sqrelu", name=t + "mm_up")
        saved.append(dict(h=h, u=u, z=z, cb=cb, yy=yy, h2=h2, q_in=q_in, q=q, mn=mn, kv=kv, o=o, h3=h3,
                          u3=u3, a_pre=a_pre, hh=hh))
        if l + 1 < depth:
            h, u = _mm(hh, wl["w_down"], res=h3, epi="rms_fwd", norm=norm_mix_g[l + 1:l + 2], name=t + "mm_down")
        else:
            h = _mm(hh, wl["w_down"], res=h3, name=t + "mm_down")

    loss_vec, dh, dhb, d_final = _loss_head(h, final_g.reshape(1, D), tgt, name="loss_head")
    loss = lax.psum(loss_vec[0, 0], ("x", "y", "c"))

    GW = [dict() for _ in range(depth)]
    GS = [dict() for _ in range(depth)]
    pending = []

    def reduce_start(l, gi):
        group = BWD_GROUPS[gi]
        gs = [GW[l][n] if GW[l][n].ndim == 3 else GW[l][n].reshape(N_CHIPS, *W[n].shape[1:]) for n in group]
        from_sibling = _swap_halves(gs, name=f"grad_swap_sibling_l{l}_g{gi}")
        prs = [_pair_sum(g, r, meta, name=f"grad_pair_sum_l{l}_{n}") for g, r, n in zip(gs, from_sibling, group)]
        (st,), token = _ici_start([prs], [prs], False, name=f"grad_scatter_chips_l{l}_g{gi}_start")
        pending.append((l, group, st))
        return token

    for l in reversed(range(depth)):
        wl, sv = Wb[l], saved[l]
        t = f"l{l}_b_"
        fused = dict(epi="rms_bwd", bm=512)
        GW[l]["w_down"] = _mm(sv["hh"], dhb, ta=True, name=t + "dw_down")
        da = _mm(dhb, wl["w_down"], tb=True, out_dtype=BF16, epi="dsqrelu", aux=sv["a_pre"], name=t + "d_hidden")
        GW[l]["w_up"] = _mm(sv["u3"], da, ta=True, o_stack=N_CHIPS, name=t + "dw_up")
        dh, dhb, GS[l]["norm_ffn_g"] = _mm(da, wl["w_up"], tb=True, b_stack=True, res=dh,
                                           norm=(sv["h3"], norm_ffn_g[l:l + 1]), after=reduce_start(l, 0),
                                           name=t + "d_u3", **fused)
        GW[l]["w_xo"] = _mm(sv["o"], dhb, ta=True, name=t + "dw_xo")
        d_o = _mm(dhb, wl["w_xo"], tb=True, out_dtype=BF16, name=t + "d_o")
        dq, dkv = _attn_bwd(sv["q"], sv["kv"], d_o, name=t + "attn")
        GW[l]["w_q"] = _mm(sv["q_in"], dq, ta=True, name=t + "dw_q")
        dkvb = dkv.astype(BF16)
        GW[l]["w_kv"] = _mm(sv["mn"], dkvb, ta=True, o_stack=N_CHIPS, name=t + "dw_kv")
        dmn = _mm(dkvb, wl["w_kv"], tb=True, b_stack=True, name=t + "d_mem")
        _, _, GS[l]["norm_mem_g"] = _rms_bwd(ms, norm_mem_g[l:l + 1], dmn, None, name=t + "rms_mem")
        dh, dhb, GS[l]["norm_x_g"] = _mm(dq, wl["w_q"], tb=True, res=dh, norm=(sv["h2"], norm_x_g[l:l + 1]),
                                         after=reduce_start(l, 1), name=t + "d_q_in", **fused)
        GW[l]["w_out"] = _mm(sv["yy"], dhb, ta=True, name=t + "dw_out")
        dyy = _mm(dhb, wl["w_out"], tb=True, name=t + "d_y")
        dcb, GS[l]["ln_b_g"], GS[l]["ln_b_b"] = _ln_silu_bwd(sv["cb"], ln_b_g[l:l + 1], ln_b_b[l:l + 1], dyy, 1,
                                                             name=t + "ln_silu")
        db_, dc_, dh_, GS[l]["conv_a_w"] = _mixer_a_bwd(sv["z"], conv_a_full[l], dyy, name=t + "mixer_a")
        dv_, dg_, GS[l]["conv_b_w"], GS[l]["conv_b_bias"] = _mixer_b_bwd(sv["z"], conv_b_full[l], dcb,
                                                                         name=t + "mixer_b")
        dz = jnp.concatenate([db_, dc_, dh_, dv_, dg_], axis=1)
        GW[l]["w_in"] = _mm(sv["u"], dz, ta=True, o_stack=N_CHIPS, name=t + "dw_in")
        dh, dhb, GS[l]["norm_mix_g"] = _mm(dz, wl["w_in"], tb=True, b_stack=True, res=dh,
                                           norm=(sv["h"], norm_mix_g[l:l + 1]), after=reduce_start(l, 2),
                                           name=t + "d_u", **fused)
    grad_x = dh[None]

    after = GS[0]["norm_mix_g"]
    keys, halves = [], []
    for l, group, (send_sems, recv_sems, srcs, lands) in pending:
        prs, pieces = _ici_wait(send_sems, recv_sems, srcs, lands, False, after,
                                name=f"grad_scatter_chips_l{l}_{group[0]}_wait")
        for n, p, q in zip(group, prs, pieces):
            keys.append((l, n))
            halves.append(_chip_sum(p, q, meta, name=f"grad_chip_sum_l{l}_{n}"))
        after = halves[-1]
    reduced = dict(zip(keys, _share_halves(halves, name="grad_share_sibling")))

    grads, deltas, new_m, new_v = {}, {}, {}, {}
    for n, _ in BIG:
        grads[n], deltas[n], new_m[n], new_v[n] = _adamw_layers(W[n], reduced[(0, n)], reduced[(1, n)], MO[n], VO[n],
                                                               name="adamw_" + n)

    small = [n for n in names if n not in dict(BIG)]
    full_shapes = {n: ((depth, W[n].shape[1], c_a) if n in ("conv_a_w", "conv_b_w") else W[n].shape)
                   for n in small}

    def small_grad(n):
        if n == "final_g":
            return d_final.reshape(W[n].shape)
        return jnp.stack([GS[l][n].reshape(full_shapes[n][1:]) for l in range(depth)])

    part = _pack([small_grad(n) for n in small], LANES)
    everyone = _gather_all(part, name="gather_small_grads")
    total = _sum_leading(everyone, name="small_grad_sum")
    full_grads = dict(zip(small, _unpack(total, [full_shapes[n] for n in small])))
    for n in ("conv_a_w", "conv_b_w"):
        full_grads[n] = lax.dynamic_slice_in_dim(full_grads[n], chip_idx * c_loc, c_loc, axis=2)
    shapes = [W[n].shape for n in small]
    d_s, m_s, v_s = _adamw(_pack([W[n] for n in small], 128), _pack([full_grads[n] for n in small], 128),
                           _pack([MO[n] for n in small], 128), _pack([VO[n] for n in small], 128),
                           name="adamw_small")
    for n, d, nm, nv in zip(small, _unpack(d_s, shapes), _unpack(m_s, shapes), _unpack(v_s, shapes)):
        grads[n], deltas[n], new_m[n], new_v[n] = full_grads[n], d, nm, nv

    return (loss, grad_x, *[grads[n] for n in names], *[deltas[n] for n in names],
            *[new_m[n] for n in names], *[new_v[n] for n in names])
```

```python
import jax
import jax.numpy as jnp
from jax import lax
from jax.experimental import pallas as pl
from jax.experimental.pallas import tpu as pltpu

F32 = jnp.float32
BF16 = jnp.bfloat16
MESH = pl.DeviceIdType.MESH

EPS = 1e-6
N_XHEADS = 4
K_A = 3
K_B = 31
PAD_A = 8
PAD_B = 32
CONV_CHUNK = 256
ROW_TILE = 512
LANES = 1024
VMEM_LIMIT_BYTES = 56 * 1024 * 1024

ADAM_LR = 0.001
ADAM_B1 = 0.9
ADAM_B2 = 0.999
ADAM_EPS = 1e-08
ADAM_WD = 0.01
ADAM_STEP = 10

BIG = (("w_in", 1), ("w_out", 0), ("w_q", 0), ("w_kv", 1), ("w_xo", 0), ("w_up", 1), ("w_down", 0))
FWD_GROUPS = (("w_in",), ("w_out", "w_q", "w_kv", "w_xo"), ("w_up", "w_down"))
BWD_GROUPS = (("w_down", "w_up"), ("w_xo", "w_q", "w_kv"), ("w_out", "w_in"))
N_CHIPS = 4
N_DEV = 8


def _params(sem=None):
    return pltpu.CompilerParams(dimension_semantics=sem, vmem_limit_bytes=VMEM_LIMIT_BYTES)


def _pick(cands, n):
    for c in cands:
        if c <= n and n % c == 0:
            return c
    return n


def _mm(a, b, *, name, ta=False, tb=False, out_dtype=F32, res=None, epi=None, aux=None, norm=None, after=None,
        b_stack=False, o_stack=0, bm=1024, bn=1024, bk=1024):
    if ta:
        K, M = a.shape
    else:
        M, K = a.shape
    if b_stack:
        n_st, d1, d2 = b.shape
        N, kb = (d1, d2) if tb else (n_st * d2, d1)
        assert K == (n_st * d2 if tb else d1), (name, a.shape, b.shape)
    else:
        N = b.shape[0] if tb else b.shape[1]
    n_unit = b.shape[2] if (b_stack and not tb) else (N // o_stack if o_stack else N)
    k_unit = b.shape[2] if (b_stack and tb) else K
    bm = _pick((bm, 512, 256, 128), M)
    bn = _pick((bn, 512, 640, 256, 384, 128), n_unit)
    bk = _pick((bk, 640, 512, 256, 128), k_unit)
    assert M % bm == 0 and N % bn == 0 and K % bk == 0, (name, M, N, K)
    nk = K // bk
    per_n = n_unit // bn
    per_k = k_unit // bk
    a_spec = (pl.BlockSpec((bk, bm), lambda i, j, k: (k, i)) if ta
              else pl.BlockSpec((bm, bk), lambda i, j, k: (i, k)))
    if b_stack and tb:
        b_spec = pl.BlockSpec((None, bn, bk), lambda i, j, k: (k // per_k, j, k % per_k))
    elif b_stack:
        b_spec = pl.BlockSpec((None, bk, bn), lambda i, j, k: (j // per_n, k, j % per_n))
    elif tb:
        b_spec = pl.BlockSpec((bn, bk), lambda i, j, k: (j, k))
    else:
        b_spec = pl.BlockSpec((bk, bn), lambda i, j, k: (k, j))
    o_spec = pl.BlockSpec((bm, bn), lambda i, j, k: (i, j))
    dims = (((0 if ta else 1,), (1 if tb else 0,)), ((), ()))
    ins, in_specs = [a, b], [a_spec, b_spec]
    if res is not None:
        ins.append(res)
        in_specs.append(o_spec)
    if aux is not None:
        ins.append(aux)
        in_specs.append(o_spec)
    vec_spec = pl.BlockSpec((1, bn), lambda i, j, k: (0, j))
    if epi == "rms_fwd":
        assert bn == N, (name, bn, N)
        ins.append(norm)
        in_specs.append(vec_spec)
    elif epi == "rms_bwd":
        assert bn == N, (name, bn, N)
        ins += [norm[0], norm[1]]
        in_specs += [o_spec, vec_spec]
    n_norm = {"rms_fwd": 1, "rms_bwd": 2}.get(epi, 0)
    if after is not None:
        ins.append(after)
        in_specs.append(pl.BlockSpec(memory_space=pl.ANY))
    n_out = {"sqrelu": 2, "rms_fwd": 2, "rms_bwd": 3, "with_bf16": 2}.get(epi, 1)
    out_shape = [jax.ShapeDtypeStruct((M, N), out_dtype)] * n_out
    out_specs = [o_spec] * n_out
    if epi in ("rms_fwd", "with_bf16"):
        out_shape = [jax.ShapeDtypeStruct((M, N), F32), jax.ShapeDtypeStruct((M, N), BF16)]
    elif epi == "rms_bwd":
        out_shape = [jax.ShapeDtypeStruct((M, N), F32), jax.ShapeDtypeStruct((M, N), BF16),
                     jax.ShapeDtypeStruct((1, N), F32)]
        out_specs = [o_spec, o_spec, vec_spec]
    if o_stack:
        assert epi in (None, "with_bf16") and res is None and aux is None
        out_shape = [jax.ShapeDtypeStruct((o_stack, M, N // o_stack), s.dtype) for s in out_shape]
        out_specs = [pl.BlockSpec((None, bm, bn), lambda i, j, k: (j // per_n, i, j % per_n))] * n_out

    def body(*refs):
        a_ref, b_ref = refs[0], refs[1]
        pos = 2
        res_ref = aux_ref = None
        if res is not None:
            res_ref = refs[pos]
            pos += 1
        if aux is not None:
            aux_ref = refs[pos]
            pos += 1
        norm_refs = refs[pos:pos + n_norm]
        pos += n_norm + (after is not None)
        outs = refs[pos:pos + n_out]

        def product():
            return lax.dot_general(a_ref[...], b_ref[...], dims, preferred_element_type=F32)

        def finish(r):
            if epi == "rms_bwd":
                x_ref, g_ref = norm_refs
                xv = x_ref[...]
                rs = lax.rsqrt(jnp.mean(xv * xv, axis=-1, keepdims=True) + EPS)
                xh = xv * rs
                dxh = r * g_ref[...]
                dx = rs * (dxh - xh * jnp.mean(dxh * xh, axis=-1, keepdims=True))
                if res_ref is not None:
                    dx = dx + res_ref[...]
                outs[0][...] = dx
                outs[1][...] = dx.astype(BF16)

                @pl.when(pl.program_id(0) == 0)
                def _():
                    outs[2][...] = jnp.zeros_like(outs[2])

                outs[2][...] += jnp.sum(r * xh, axis=0, keepdims=True)
                return
            if res_ref is not None:
                r = r + res_ref[...]
            if epi == "rms_fwd":
                outs[0][...] = r
                rs = lax.rsqrt(jnp.mean(r * r, axis=-1, keepdims=True) + EPS)
                outs[1][...] = (r * rs * norm_refs[0][...]).astype(BF16)
            elif epi == "with_bf16":
                outs[0][...] = r
                outs[1][...] = r.astype(BF16)
            elif epi == "sqrelu":
                outs[0][...] = r.astype(out_dtype)
                rl = jnp.maximum(r, 0.0)
                outs[1][...] = (rl * rl).astype(out_dtype)
            elif epi == "dsqrelu":
                outs[0][...] = (r * (2.0 * jnp.maximum(aux_ref[...].astype(F32), 0.0))).astype(out_dtype)
            else:
                outs[0][...] = r.astype(out_dtype)

        if nk == 1:
            finish(product())
            return
        acc = refs[pos + n_out]
        k = pl.program_id(2)

        @pl.when(k == 0)
        def _():
            acc[...] = product()

        @pl.when(jnp.logical_and(k > 0, k < nk - 1))
        def _():
            acc[...] += product()

        @pl.when(k == nk - 1)
        def _():
            finish(acc[...] + product())

    out = pl.pallas_call(
        body, name=name, grid=(M // bm, N // bn, nk),
        in_specs=in_specs, out_specs=out_specs, out_shape=out_shape,
        scratch_shapes=[pltpu.VMEM((bm, bn), F32)] if nk > 1 else [],
        compiler_params=_params(("arbitrary",) * 3 if epi == "rms_bwd" else ("parallel", "parallel", "arbitrary")),
    )(*ins)
    return out if n_out > 1 else out[0]


def _row_tile(rows):
    return min(ROW_TILE, rows)


def _rms_fwd(x, g, *, name, after=None):
    S, D = x.shape
    tr = _row_tile(S)

    def body(x_ref, g_ref, *rest):
        o_ref = rest[-1]
        xv = x_ref[...]
        r = lax.rsqrt(jnp.mean(xv * xv, axis=-1, keepdims=True) + EPS)
        o_ref[...] = (xv * r * g_ref[...]).astype(BF16)

    deps = [] if after is None else [after]
    return pl.pallas_call(
        body, name=name, grid=(S // tr,),
        in_specs=[pl.BlockSpec((tr, D), lambda i: (i, 0)), pl.BlockSpec((1, D), lambda i: (0, 0))]
        + [ANY] * len(deps),
        out_specs=pl.BlockSpec((tr, D), lambda i: (i, 0)),
        out_shape=jax.ShapeDtypeStruct((S, D), BF16),
        compiler_params=_params(("parallel",)),
    )(x, g, *deps)


def _rms_bwd(x, g, du, dres, *, name, after=None):
    S, D = x.shape
    tr = _row_tile(S)
    has_res = dres is not None
    deps = [] if after is None else [after]

    def body(*refs):
        dx_ref, dxb_ref, dg_ref = refs[-3:]
        if has_res:
            x_ref, g_ref, du_ref, dres_ref = refs[:4]
        else:
            x_ref, g_ref, du_ref = refs[:3]
        xv = x_ref[...]
        r = lax.rsqrt(jnp.mean(xv * xv, axis=-1, keepdims=True) + EPS)
        xh = xv * r
        dy = du_ref[...]
        dxh = dy * g_ref[...]
        dx = r * (dxh - xh * jnp.mean(dxh * xh, axis=-1, keepdims=True))
        if has_res:
            dx = dx + dres_ref[...]
        dx_ref[...] = dx
        dxb_ref[...] = dx.astype(BF16)

        @pl.when(pl.program_id(0) == 0)
        def _():
            dg_ref[...] = jnp.zeros_like(dg_ref)

        dg_ref[...] += jnp.sum(dy * xh, axis=0, keepdims=True)

    row = pl.BlockSpec((tr, D), lambda i: (i, 0))
    vec = pl.BlockSpec((1, D), lambda i: (0, 0))
    ins = [x, g, du] + ([dres] if has_res else []) + deps
    in_specs = [row, vec, row] + ([row] if has_res else []) + [ANY] * len(deps)
    return pl.pallas_call(
        body, name=name, grid=(S // tr,),
        in_specs=in_specs, out_specs=[row, row, vec],
        out_shape=[jax.ShapeDtypeStruct((S, D), F32), jax.ShapeDtypeStruct((S, D), BF16),
                   jax.ShapeDtypeStruct((1, D), F32)],
        compiler_params=_params(("arbitrary",)),
    )(*ins)


def _loss_head(h, g, target, *, name):
    S, D = h.shape
    tr = _row_tile(S)

    def body(x_ref, g_ref, t_ref, loss_ref, dx_ref, dxb_ref, dg_ref):
        xv = x_ref[...]
        r = lax.rsqrt(jnp.mean(xv * xv, axis=-1, keepdims=True) + EPS)
        xh = xv * r
        gv = g_ref[...]
        err = xh * gv - t_ref[...]
        part = 0.5 * jnp.sum(jnp.mean(err * err, axis=-1, keepdims=True), axis=0, keepdims=True)
        dy = err * (1.0 / D)
        dxh = dy * gv
        dx = r * (dxh - xh * jnp.mean(dxh * xh, axis=-1, keepdims=True))
        dx_ref[...] = dx
        dxb_ref[...] = dx.astype(BF16)

        @pl.when(pl.program_id(0) == 0)
        def _():
            dg_ref[...] = jnp.zeros_like(dg_ref)
            loss_ref[...] = jnp.zeros_like(loss_ref)

        dg_ref[...] += jnp.sum(dy * xh, axis=0, keepdims=True)
        loss_ref[...] += jnp.broadcast_to(part, loss_ref.shape)

    row = pl.BlockSpec((tr, D), lambda i: (i, 0))
    vec = pl.BlockSpec((1, D), lambda i: (0, 0))
    return pl.pallas_call(
        body, name=name, grid=(S // tr,),
        in_specs=[row, vec, row],
        out_specs=[pl.BlockSpec((1, 128), lambda i: (0, 0)), row, row, vec],
        out_shape=[jax.ShapeDtypeStruct((1, 128), F32), jax.ShapeDtypeStruct((S, D), F32),
                   jax.ShapeDtypeStruct((S, D), BF16), jax.ShapeDtypeStruct((1, D), F32)],
        compiler_params=_params(("arbitrary",)),
    )(h, g, target)


def _sigmoid(x):
    return 1.0 / (1.0 + jnp.exp(-x))


def _ln_silu_fwd(cb, g, b, into, *, name):
    S, C = cb.shape
    tr = _row_tile(S)

    def body(x_ref, g_ref, b_ref, into_ref, o_ref):
        xv = x_ref[...]
        mu = jnp.mean(xv, axis=-1, keepdims=True)
        xc = xv - mu
        rs = lax.rsqrt(jnp.mean(xc * xc, axis=-1, keepdims=True) + EPS)
        l = xc * rs * g_ref[...] + b_ref[...]
        o_ref[...] = (l * _sigmoid(l)).astype(BF16)

    row = pl.BlockSpec((tr, C), lambda i: (i, 0))
    vec = pl.BlockSpec((1, C), lambda i: (0, 0))
    return pl.pallas_call(
        body, name=name, grid=(S // tr,), in_specs=[row, vec, vec, pl.BlockSpec(memory_space=pl.ANY)],
        out_specs=pl.BlockSpec((tr, C), lambda i: (i, 1)),
        out_shape=jax.ShapeDtypeStruct(into.shape, BF16), input_output_aliases={3: 0},
        compiler_params=_params(("parallel",)),
    )(cb, g, b, into)


def _ln_silu_bwd(cb, g, b, dy, col_block, *, name):
    S, C = cb.shape
    tr = _row_tile(S)

    def body(x_ref, g_ref, b_ref, dy_ref, dx_ref, dg_ref, db_ref):
        xv = x_ref[...]
        mu = jnp.mean(xv, axis=-1, keepdims=True)
        xc = xv - mu
        rs = lax.rsqrt(jnp.mean(xc * xc, axis=-1, keepdims=True) + EPS)
        xh = xc * rs
        gv = g_ref[...]
        l = xh * gv + b_ref[...]
        sg = _sigmoid(l)
        dl = dy_ref[...] * (sg + l * sg * (1.0 - sg))
        dxh = dl * gv
        dx_ref[...] = rs * (dxh - jnp.mean(dxh, axis=-1, keepdims=True)
                            - xh * jnp.mean(dxh * xh, axis=-1, keepdims=True))

        @pl.when(pl.program_id(0) == 0)
        def _():
            dg_ref[...] = jnp.zeros_like(dg_ref)
            db_ref[...] = jnp.zeros_like(db_ref)

        dg_ref[...] += jnp.sum(dl * xh, axis=0, keepdims=True)
        db_ref[...] += jnp.sum(dl, axis=0, keepdims=True)

    row = pl.BlockSpec((tr, C), lambda i: (i, 0))
    vec = pl.BlockSpec((1, C), lambda i: (0, 0))
    return pl.pallas_call(
        body, name=name, grid=(S // tr,),
        in_specs=[row, vec, vec, pl.BlockSpec((tr, C), lambda i: (i, col_block))],
        out_specs=[row, vec, vec],
        out_shape=[jax.ShapeDtypeStruct((S, C), F32), jax.ShapeDtypeStruct((1, C), F32),
                   jax.ShapeDtypeStruct((1, C), F32)],
        compiler_params=_params(("arbitrary",)),
    )(cb, g, b, dy)


def _attn_fwd(q, kv, *, name, tq=ROW_TILE):
    S, D = q.shape
    M = kv.shape[0]
    hd = D // N_XHEADS
    scale = 1.0 / float(hd) ** 0.5
    tq = min(tq, S)

    def body(q_ref, k_ref, v_ref, o_ref):
        for h in range(N_XHEADS):
            cols = slice(h * hd, (h + 1) * hd)
            s = lax.dot_general(q_ref[:, cols], k_ref[:, cols], (((1,), (1,)), ((), ())),
                                preferred_element_type=F32) * scale
            e = jnp.exp(s - jnp.max(s, axis=-1, keepdims=True))
            p = e / jnp.sum(e, axis=-1, keepdims=True)
            o = jnp.dot(p.astype(BF16), v_ref[:, cols], preferred_element_type=F32)
            o_ref[:, cols] = o.astype(BF16)

    return pl.pallas_call(
        body, name=name, grid=(S // tq,),
        in_specs=[pl.BlockSpec((tq, D), lambda i: (i, 0)), pl.BlockSpec((M, D), lambda i: (0, 0)),
                  pl.BlockSpec((M, D), lambda i: (0, 1))],
        out_specs=pl.BlockSpec((tq, D), lambda i: (i, 0)),
        out_shape=jax.ShapeDtypeStruct((S, D), BF16), compiler_params=_params(("parallel",)),
    )(q, kv, kv)


def _attn_bwd(q, kv, do, *, name, tq=ROW_TILE):
    S, D = q.shape
    M = kv.shape[0]
    hd = D // N_XHEADS
    scale = 1.0 / float(hd) ** 0.5
    tq = min(tq, S)

    def body(q_ref, k_ref, v_ref, do_ref, dq_ref, dkv_ref):
        @pl.when(pl.program_id(0) == 0)
        def _():
            dkv_ref[...] = jnp.zeros_like(dkv_ref)

        for h in range(N_XHEADS):
            cols = slice(h * hd, (h + 1) * hd)
            vcols = slice(D + h * hd, D + (h + 1) * hd)
            qh, kh, vh, doh = q_ref[:, cols], k_ref[:, cols], v_ref[:, cols], do_ref[:, cols]
            s = lax.dot_general(qh, kh, (((1,), (1,)), ((), ())), preferred_element_type=F32) * scale
            e = jnp.exp(s - jnp.max(s, axis=-1, keepdims=True))
            p = e / jnp.sum(e, axis=-1, keepdims=True)
            pb = p.astype(BF16)
            dp = lax.dot_general(doh, vh, (((1,), (1,)), ((), ())), preferred_element_type=F32)
            ds = (p * (dp - jnp.sum(dp * p, axis=-1, keepdims=True)) * scale).astype(BF16)
            dq_ref[:, cols] = jnp.dot(ds, kh, preferred_element_type=F32).astype(BF16)
            dkv_ref[:, cols] += lax.dot_general(ds, qh, (((0,), (0,)), ((), ())), preferred_element_type=F32)
            dkv_ref[:, vcols] += lax.dot_general(pb, doh, (((0,), (0,)), ((), ())), preferred_element_type=F32)

    row = pl.BlockSpec((tq, D), lambda i: (i, 0))
    return pl.pallas_call(
        body, name=name, grid=(S // tq,),
        in_specs=[row, pl.BlockSpec((M, D), lambda i: (0, 0)), pl.BlockSpec((M, D), lambda i: (0, 1)), row],
        out_specs=[row, pl.BlockSpec((M, 2 * D), lambda i: (0, 0))],
        out_shape=[jax.ShapeDtypeStruct((S, D), BF16), jax.ShapeDtypeStruct((M, 2 * D), F32)],
        compiler_params=_params(("arbitrary",)),
    )(q, kv, kv, do)


def _delayed(win, j, pad):
    return (win if j == 0 else pltpu.roll(win, j, 0))[pad:, :]


def _advanced(win, j, ch):
    return (win if j == 0 else pltpu.roll(win, win.shape[0] - j, 0))[:ch, :]


def _mixer_a_fwd(z, w, *, name):
    S = z.shape[0]
    C = w.shape[1]
    nb = C // 128
    ch = min(CONV_CHUNK, S)

    def body(b_ref, c_ref, h_ref, w_ref, y_ref, xp):
        xp[0:PAD_A, :] = jnp.zeros((PAD_A, 128), F32)
        xp[PAD_A:, :] = c_ref[...] * h_ref[...]

        def chunk(i, carry):
            base = pl.multiple_of(i * ch, ch)
            win = xp[pl.ds(base, ch + PAD_A), :]
            acc = _delayed(win, 0, PAD_A) * w_ref[K_A - 1:K_A, :]
            for j in range(1, K_A):
                acc = acc + _delayed(win, j, PAD_A) * w_ref[K_A - 1 - j:K_A - j, :]
            y_ref[pl.ds(base, ch), :] = (b_ref[pl.ds(base, ch), :] * acc).astype(BF16)
            return carry

        lax.fori_loop(0, S // ch, chunk, 0)

    def col(g):
        return pl.BlockSpec((S, 128), lambda j: (0, g * nb + j))

    return pl.pallas_call(
        body, name=name, grid=(nb,),
        in_specs=[col(0), col(1), col(2), pl.BlockSpec((K_A, 128), lambda j: (0, j))],
        out_specs=pl.BlockSpec((S, 128), lambda j: (0, j)),
        out_shape=jax.ShapeDtypeStruct((S, 2 * C), BF16),
        scratch_shapes=[pltpu.VMEM((PAD_A + S, 128), F32)],
        compiler_params=_params(("parallel",)),
    )(z, z, z, w)


def _mixer_a_bwd(z, w, dy, *, name):
    S = z.shape[0]
    C = w.shape[1]
    nb = C // 128
    ch = min(CONV_CHUNK, S)

    def body(b_ref, c_ref, h_ref, w_ref, dy_ref, db_ref, dc_ref, dh_ref, dw_ref, xp, dp):
        xp[0:PAD_A, :] = jnp.zeros((PAD_A, 128), F32)
        xp[PAD_A:, :] = c_ref[...] * h_ref[...]
        dp[S:, :] = jnp.zeros((PAD_A, 128), F32)
        dw_ref[...] = jnp.zeros_like(dw_ref)

        def chunk(i, carry):
            base = pl.multiple_of(i * ch, ch)
            win = xp[pl.ds(base, ch + PAD_A), :]
            dya = dy_ref[pl.ds(base, ch), :]
            dcv = dya * b_ref[pl.ds(base, ch), :]
            dp[pl.ds(base, ch), :] = dcv
            acc = None
            for j in range(K_A):
                xs = _delayed(win, j, PAD_A)
                k = K_A - 1 - j
                term = xs * w_ref[k:k + 1, :]
                acc = term if acc is None else acc + term
                dw_ref[k:k + 1, :] += jnp.sum(dcv * xs, axis=0, keepdims=True)
            db_ref[pl.ds(base, ch), :] = (dya * acc).astype(BF16)
            return carry

        lax.fori_loop(0, S // ch, chunk, 0)

        def chunk2(i, carry):
            base = pl.multiple_of(i * ch, ch)
            win = dp[pl.ds(base, ch + PAD_A), :]
            acc = None
            for j in range(K_A):
                term = _advanced(win, j, ch) * w_ref[K_A - 1 - j:K_A - j, :]
                acc = term if acc is None else acc + term
            dc_ref[pl.ds(base, ch), :] = (acc * h_ref[pl.ds(base, ch), :]).astype(BF16)
            dh_ref[pl.ds(base, ch), :] = (acc * c_ref[pl.ds(base, ch), :]).astype(BF16)
            return carry

        lax.fori_loop(0, S // ch, chunk2, 0)

    def col(g):
        return pl.BlockSpec((S, 128), lambda j: (0, g * nb + j))

    out_col = pl.BlockSpec((S, 128), lambda j: (0, j))
    wspec = pl.BlockSpec((K_A, 128), lambda j: (0, j))
    return pl.pallas_call(
        body, name=name, grid=(nb,),
        in_specs=[col(0), col(1), col(2), wspec, out_col],
        out_specs=[out_col, out_col, out_col, wspec],
        out_shape=[jax.ShapeDtypeStruct((S, C), BF16)] * 3 + [jax.ShapeDtypeStruct((K_A, C), F32)],
        scratch_shapes=[pltpu.VMEM((PAD_A + S, 128), F32), pltpu.VMEM((S + PAD_A, 128), F32)],
        compiler_params=_params(("parallel",)),
    )(z, z, z, w, dy)


def _mixer_b_fwd(z, w, bias, *, name):
    S = z.shape[0]
    C = w.shape[1]
    nb = C // 128
    ch = min(CONV_CHUNK, S)

    def body(v_ref, g_ref, w_ref, bias_ref, cb_ref, xp):
        xp[0:PAD_B, :] = jnp.zeros((PAD_B, 128), F32)
        xp[PAD_B:, :] = v_ref[...] * _sigmoid(g_ref[...])

        def chunk(i, carry):
            base = pl.multiple_of(i * ch, ch)
            win = xp[pl.ds(base, ch + PAD_B), :]
            acc = None
            for j in range(K_B):
                term = _delayed(win, j, PAD_B) * w_ref[K_B - 1 - j:K_B - j, :]
                acc = term if acc is None else acc + term
            cb_ref[pl.ds(base, ch), :] = acc + bias_ref[...]
            return carry

        lax.fori_loop(0, S // ch, chunk, 0)

    def col(g):
        return pl.BlockSpec((S, 128), lambda j: (0, g * nb + j))

    return pl.pallas_call(
        body, name=name, grid=(nb,),
        in_specs=[col(3), col(4), pl.BlockSpec((K_B, 128), lambda j: (0, j)),
                  pl.BlockSpec((1, 128), lambda j: (0, j))],
        out_specs=pl.BlockSpec((S, 128), lambda j: (0, j)),
        out_shape=jax.ShapeDtypeStruct((S, C), F32),
        scratch_shapes=[pltpu.VMEM((PAD_B + S, 128), F32)],
        compiler_params=_params(("parallel",)),
    )(z, z, w, bias)


def _mixer_b_bwd(z, w, dcb, *, name):
    S = z.shape[0]
    C = w.shape[1]
    nb = C // 128
    ch = min(CONV_CHUNK, S)

    def body(v_ref, g_ref, w_ref, dcb_ref, dv_ref, dg_ref, dw_ref, dbias_ref, xp, dp):
        xp[0:PAD_B, :] = jnp.zeros((PAD_B, 128), F32)
        xp[PAD_B:, :] = v_ref[...] * _sigmoid(g_ref[...])
        dp[0:S, :] = dcb_ref[...]
        dp[S:, :] = jnp.zeros((PAD_B, 128), F32)
        dw_ref[...] = jnp.zeros_like(dw_ref)
        dbias_ref[...] = jnp.sum(dcb_ref[...], axis=0, keepdims=True)

        def chunk(i, carry):
            base = pl.multiple_of(i * ch, ch)
            win = xp[pl.ds(base, ch + PAD_B), :]
            d = dcb_ref[pl.ds(base, ch), :]
            for j in range(K_B):
                k = K_B - 1 - j
                dw_ref[k:k + 1, :] += jnp.sum(d * _delayed(win, j, PAD_B), axis=0, keepdims=True)
            return carry

        lax.fori_loop(0, S // ch, chunk, 0)

        def chunk2(i, carry):
            base = pl.multiple_of(i * ch, ch)
            win = dp[pl.ds(base, ch + PAD_B), :]
            acc = None
            for j in range(K_B):
                term = _advanced(win, j, ch) * w_ref[K_B - 1 - j:K_B - j, :]
                acc = term if acc is None else acc + term
            sg = _sigmoid(g_ref[pl.ds(base, ch), :])
            vv = v_ref[pl.ds(base, ch), :]
            dv_ref[pl.ds(base, ch), :] = (acc * sg).astype(BF16)
            dg_ref[pl.ds(base, ch), :] = (acc * vv * sg * (1.0 - sg)).astype(BF16)
            return carry

        lax.fori_loop(0, S // ch, chunk2, 0)

    def col(g):
        return pl.BlockSpec((S, 128), lambda j: (0, g * nb + j))

    out_col = pl.BlockSpec((S, 128), lambda j: (0, j))
    wspec = pl.BlockSpec((K_B, 128), lambda j: (0, j))
    bspec = pl.BlockSpec((1, 128), lambda j: (0, j))
    return pl.pallas_call(
        body, name=name, grid=(nb,),
        in_specs=[col(3), col(4), wspec, out_col],
        out_specs=[out_col, out_col, wspec, bspec],
        out_shape=[jax.ShapeDtypeStruct((S, C), BF16)] * 2
        + [jax.ShapeDtypeStruct((K_B, C), F32), jax.ShapeDtypeStruct((1, C), F32)],
        scratch_shapes=[pltpu.VMEM((PAD_B + S, 128), F32), pltpu.VMEM((S + PAD_B, 128), F32)],
        compiler_params=_params(("parallel",)),
    )(z, z, w, dcb)


def _ew_tile(R):
    for t in (512, 256, 128, 64, 32, 16, 8):
        if R % t == 0:
            return t
    return R


def _pair_sum(g, r, meta, *, name):
    n, a, b = g.shape
    ah = a // 2
    tr = _pick((256, 128, 64, 32, 16), ah)
    nh = ah // tr

    def body(meta_ref, g_ref, r_ref, o_ref):
        o_ref[...] = (g_ref[...] + r_ref[...].astype(F32)).astype(BF16)

    half = pl.BlockSpec((None, tr, b), lambda j, i, meta_ref: (j, i, 0))
    return pl.pallas_call(
        body, name=name,
        grid_spec=pltpu.PrefetchScalarGridSpec(
            num_scalar_prefetch=1, grid=(n, nh),
            in_specs=[pl.BlockSpec((None, tr, b), lambda j, i, meta_ref: (j, meta_ref[0] * nh + i, 0)), half],
            out_specs=half),
        out_shape=jax.ShapeDtypeStruct((n, ah, b), BF16), compiler_params=_params(("parallel", "parallel")),
    )(meta, g, r)


def _chip_sum(p, q, meta, *, name):
    n, ah, b = p.shape
    tr = _pick((256, 128, 64, 32, 16), ah)
    nh = ah // tr

    def body(meta_ref, p_ref, q1_ref, q2_ref, q3_ref, o_ref):
        o_ref[...] = ((p_ref[...].astype(F32) + q1_ref[...].astype(F32)) + q2_ref[...].astype(F32)
                      ) + q3_ref[...].astype(F32)

    def piece(mask):
        return pl.BlockSpec((None, tr, b), lambda i, meta_ref: (meta_ref[1] ^ mask, i, 0))

    return pl.pallas_call(
        body, name=name,
        grid_spec=pltpu.PrefetchScalarGridSpec(
            num_scalar_prefetch=1, grid=(nh,),
            in_specs=[piece(0), piece(1), piece(2), piece(3)],
            out_specs=pl.BlockSpec((tr, b), lambda i, meta_ref: (meta_ref[0] * nh + i, 0))),
        out_shape=jax.ShapeDtypeStruct((2 * ah, b), F32), compiler_params=_params(("parallel",)),
    )(meta, p, q, q, q)


def _sum_leading(x, *, name):
    n, R, C = x.shape
    tr = _ew_tile(R)

    def body(x_ref, o_ref):
        acc = x_ref[0].astype(F32)
        for k in range(1, n):
            acc = acc + x_ref[k].astype(F32)
        o_ref[...] = acc

    return pl.pallas_call(
        body, name=name, grid=(R // tr,),
        in_specs=[pl.BlockSpec((n, tr, C), lambda i: (0, i, 0))],
        out_specs=pl.BlockSpec((tr, C), lambda i: (i, 0)),
        out_shape=jax.ShapeDtypeStruct((R, C), F32), compiler_params=_params(("parallel",)),
    )(x)


def _adamw(w, g, m, v, *, name):
    R, C = w.shape
    tr = _ew_tile(R)

    def body(w_ref, g_ref, m_ref, v_ref, d_ref, nm_ref, nv_ref):
        gv = g_ref[...]
        nm = ADAM_B1 * m_ref[...] + (1.0 - ADAM_B1) * gv
        nv = ADAM_B2 * v_ref[...] + (1.0 - ADAM_B2) * (gv * gv)
        m_hat = nm / (1.0 - ADAM_B1 ** ADAM_STEP)
        v_hat = nv / (1.0 - ADAM_B2 ** ADAM_STEP)
        d_ref[...] = -ADAM_LR * (m_hat / (jnp.sqrt(v_hat) + ADAM_EPS) + ADAM_WD * w_ref[...])
        nm_ref[...] = nm
        nv_ref[...] = nv

    row = pl.BlockSpec((tr, C), lambda i: (i, 0))
    return pl.pallas_call(
        body, name=name, grid=(R // tr,), in_specs=[row] * 4, out_specs=[row] * 3,
        out_shape=[jax.ShapeDtypeStruct((R, C), F32)] * 3, compiler_params=_params(("parallel",)),
    )(w, g, m, v)


def _adamw_layers(w, g0, g1, m, v, *, name):
    _, a, b = w.shape
    tr = _pick((512, 256, 128, 64, 32, 16, 8), a)

    def body(w_ref, g0_ref, g1_ref, m_ref, v_ref, g_ref, d_ref, nm_ref, nv_ref):
        gv = jnp.where(pl.program_id(1) == 0, g0_ref[...], g1_ref[...])
        nm = ADAM_B1 * m_ref[...] + (1.0 - ADAM_B1) * gv
        nv = ADAM_B2 * v_ref[...] + (1.0 - ADAM_B2) * (gv * gv)
        m_hat = nm / (1.0 - ADAM_B1 ** ADAM_STEP)
        v_hat = nv / (1.0 - ADAM_B2 ** ADAM_STEP)
        g_ref[...] = gv
        d_ref[...] = -ADAM_LR * (m_hat / (jnp.sqrt(v_hat) + ADAM_EPS) + ADAM_WD * w_ref[...])
        nm_ref[...] = nm
        nv_ref[...] = nv

    lay = pl.BlockSpec((None, tr, b), lambda i, l: (l, i, 0))
    row = pl.BlockSpec((tr, b), lambda i, l: (i, 0))
    return pl.pallas_call(
        body, name=name, grid=(a // tr, 2), in_specs=[lay, row, row, lay, lay], out_specs=[lay] * 4,
        out_shape=[jax.ShapeDtypeStruct(w.shape, F32)] * 4, compiler_params=_params(("parallel", "arbitrary")),
    )(w, g0, g1, m, v)


ANY = pl.BlockSpec(memory_space=pl.ANY)


def _place():
    x, y, c = lax.axis_index("x"), lax.axis_index("y"), lax.axis_index("c")
    return x, y, c, 2 * x + y


def _other_chip(x, y, mask):
    px = 1 - x if mask & 2 else x
    py = 1 - y if mask & 1 else y
    return px, py, 2 * px + py


MASKS = (1, 2, 3)


def _half(ref, c, lead=()):
    ah = ref.shape[-2] // 2
    return ref.at[(*lead, pl.ds(c * ah, ah), slice(None))]


def _gather_weights(owns, *, name):
    n = len(owns)

    def body(*refs):
        own, out = refs[:n], refs[n:2 * n]
        send_sems, recv_sems = refs[2 * n], refs[2 * n + 1]
        x, y, c, chip = _place()
        sends = []
        for i in range(n):
            for k, mask in enumerate(MASKS):
                px, py, _ = _other_chip(x, y, mask)
                cp = pltpu.make_async_remote_copy(_half(own[i], c), _half(out[i], c, (chip,)),
                                                  send_sems.at[i, k], recv_sems.at[i, k],
                                                  device_id=(px, py, c), device_id_type=MESH)
                cp.start()
                sends.append(cp)
        for i in range(n):
            for k, mask in enumerate(MASKS):
                px, py, pchip = _other_chip(x, y, mask)
                got = _half(out[i], c, (pchip,))
                pltpu.make_async_remote_copy(got, got, send_sems.at[i, k], recv_sems.at[i, k],
                                             device_id=(px, py, c), device_id_type=MESH).wait_recv()
                cp = pltpu.make_async_remote_copy(got, got, send_sems.at[i, 3 + k], recv_sems.at[i, 3 + k],
                                                  device_id=(x, y, 1 - c), device_id_type=MESH)
                cp.start()
                sends.append(cp)
        for i in range(n):
            for k, mask in enumerate(MASKS):
                _, _, pchip = _other_chip(x, y, mask)
                theirs = _half(out[i], 1 - c, (pchip,))
                pltpu.make_async_remote_copy(theirs, theirs, send_sems.at[i, 3 + k], recv_sems.at[i, 3 + k],
                                             device_id=(x, y, 1 - c), device_id_type=MESH).wait_recv()
        for cp in sends:
            cp.wait_send()

    return pl.pallas_call(
        body, name=name, in_specs=[ANY] * n, out_specs=[ANY] * n,
        out_shape=[jax.ShapeDtypeStruct((N_CHIPS, *o.shape), o.dtype) for o in owns],
        scratch_shapes=[pltpu.SemaphoreType.DMA((n, 6)), pltpu.SemaphoreType.DMA((n, 6))],
    )(*owns)


SIBLING_ID = 0


def _sibling_barrier(x, y, c):
    sem = pltpu.get_barrier_semaphore()
    pl.semaphore_signal(sem, inc=1, device_id=(x, y, 1 - c), device_id_type=MESH)
    pl.semaphore_wait(sem, 1)


def _swap_halves(gs, *, name):
    n = len(gs)

    def body(*refs):
        g, out = refs[:n], refs[n:2 * n]
        send_sems, recv_sems = refs[2 * n], refs[2 * n + 1]
        x, y, c, _ = _place()
        _sibling_barrier(x, y, c)
        cps = []
        for i in range(n):
            ah = g[i].shape[1] // 2
            cp = pltpu.make_async_remote_copy(g[i].at[:, pl.ds((1 - c) * ah, ah), :], out[i],
                                              send_sems.at[i], recv_sems.at[i],
                                              device_id=(x, y, 1 - c), device_id_type=MESH)
            cp.start()
            cps.append(cp)
        for cp in cps:
            cp.wait()

    return pl.pallas_call(
        body, name=name, in_specs=[ANY] * n, out_specs=[ANY] * n,
        out_shape=[jax.ShapeDtypeStruct((g.shape[0], g.shape[1] // 2, g.shape[2]), g.dtype) for g in gs],
        scratch_shapes=[pltpu.SemaphoreType.DMA((n,)), pltpu.SemaphoreType.DMA((n,))],
        compiler_params=pltpu.CompilerParams(collective_id=SIBLING_ID),
    )(*gs)


def _scatter_to_chips(ps, *, name):
    n = len(ps)

    def body(*refs):
        p, out = refs[:n], refs[n:2 * n]
        send_sems, recv_sems = refs[2 * n], refs[2 * n + 1]
        x, y, c, chip = _place()
        sends = []
        for i in range(n):
            for k, mask in enumerate(MASKS):
                px, py, pchip = _other_chip(x, y, mask)
                cp = pltpu.make_async_remote_copy(p[i].at[pchip], out[i].at[chip], send_sems.at[i, k],
                                                  recv_sems.at[i, k], device_id=(px, py, c), device_id_type=MESH)
                cp.start()
                sends.append(cp)
        for i in range(n):
            for k, mask in enumerate(MASKS):
                px, py, pchip = _other_chip(x, y, mask)
                pltpu.make_async_remote_copy(p[i].at[pchip], out[i].at[pchip], send_sems.at[i, k],
                                             recv_sems.at[i, k], device_id=(px, py, c),
                                             device_id_type=MESH).wait_recv()
        for cp in sends:
            cp.wait_send()

    return pl.pallas_call(
        body, name=name, in_specs=[ANY] * n, out_specs=[ANY] * n,
        out_shape=[jax.ShapeDtypeStruct(p.shape, p.dtype) for p in ps],
        scratch_shapes=[pltpu.SemaphoreType.DMA((n, 3)), pltpu.SemaphoreType.DMA((n, 3))],
    )(*ps)


HBM_SPEC = pl.BlockSpec(memory_space=pltpu.HBM)
SEM_SPEC = pl.BlockSpec(memory_space=pltpu.SEMAPHORE)
EFFECT = pltpu.SideEffectType.DATAFLOW_SIDE_EFFECTING


def _ici_ends(src, land, gather, x, y, c, chip, mask):
    px, py, pchip = _other_chip(x, y, mask)
    if gather:
        return _half(src, c), _half(land, c, (chip,)), _half(land, c, (pchip,)), (px, py, c)
    return src.at[pchip], land.at[chip], land.at[pchip], (px, py, c)


def _ici_start(groups, land_groups, gather, *, name, after=None):
    sizes = [len(g) for g in groups]
    n = sum(sizes)
    ng = len(groups)
    deps = [] if after is None else [after]

    def body(*refs):
        src, land = refs[:n], refs[n:2 * n]
        sems = refs[2 * n + len(deps):2 * n + len(deps) + 2 * ng]
        token = refs[-1]
        x, y, c, chip = _place()
        i = 0
        for g in range(ng):
            for j in range(sizes[g]):
                for k, mask in enumerate(MASKS):
                    s, d, _, peer = _ici_ends(src[i], land[i], gather, x, y, c, chip, mask)
                    pltpu.make_async_remote_copy(s, d, sems[2 * g].at[3 * j + k], sems[2 * g + 1].at[3 * j + k],
                                                 device_id=peer, device_id_type=MESH).start()
                i += 1
        token[...] = jnp.zeros_like(token)

    lands = [pltpu.with_memory_space_constraint(lax.empty(s.shape, s.dtype), pltpu.HBM)
             for g in land_groups for s in g]
    srcs = [pltpu.with_memory_space_constraint(s, pltpu.HBM) for g in groups for s in g]
    sem_shapes = [pltpu.SemaphoreType.DMA((3 * m,)) for m in sizes for _ in range(2)]
    out = pl.pallas_call(
        body, name=name,
        out_shape=(*sem_shapes, *[pltpu.HBM(s.shape, s.dtype) for s in srcs],
                   *[pltpu.HBM(s.shape, s.dtype) for s in lands], jax.ShapeDtypeStruct((8, 128), F32)),
        in_specs=[HBM_SPEC] * (2 * n) + [ANY] * len(deps),
        out_specs=(*[SEM_SPEC] * (2 * ng), *[HBM_SPEC] * (2 * n), pl.BlockSpec(memory_space=pltpu.VMEM)),
        input_output_aliases={i: 2 * ng + i for i in range(2 * n)},
        compiler_params=pltpu.CompilerParams(has_side_effects=EFFECT),
    )(*srcs, *lands, *deps)
    res, pos = [], 0
    for g in range(ng):
        res.append((out[2 * g], out[2 * g + 1], list(out[2 * ng + pos:2 * ng + pos + sizes[g]]),
                    list(out[2 * ng + n + pos:2 * ng + n + pos + sizes[g]])))
        pos += sizes[g]
    return res, out[-1]


def _ici_wait(send_sems, recv_sems, srcs, lands, gather, after, *, name):
    n = len(srcs)

    def body(*refs):
        src, land = refs[:n], refs[n:2 * n]
        send_sems, recv_sems = refs[2 * n], refs[2 * n + 1]
        x, y, c, chip = _place()
        for i in range(n):
            for k, mask in enumerate(MASKS):
                s, d, got, peer = _ici_ends(src[i], land[i], gather, x, y, c, chip, mask)
                pltpu.make_async_remote_copy(s, d, send_sems.at[3 * i + k], recv_sems.at[3 * i + k],
                                             device_id=peer, device_id_type=MESH).wait_send()
                pltpu.make_async_remote_copy(s, got, send_sems.at[3 * i + k], recv_sems.at[3 * i + k],
                                             device_id=peer, device_id_type=MESH).wait_recv()

    out = pl.pallas_call(
        body, name=name,
        out_shape=tuple(pltpu.HBM(s.shape, s.dtype) for s in (*srcs, *lands)),
        in_specs=[HBM_SPEC] * (2 * n) + [SEM_SPEC, SEM_SPEC, ANY],
        out_specs=tuple([HBM_SPEC] * (2 * n)),
        input_output_aliases={i: i for i in range(2 * n)},
        compiler_params=pltpu.CompilerParams(has_side_effects=EFFECT),
    )(*srcs, *lands, send_sems, recv_sems, after)
    return list(out[:n]), list(out[n:])


def _relay_halves(gs, *, name):
    n = len(gs)

    def body(*refs):
        out = refs[n:2 * n]
        send_sems, recv_sems = refs[2 * n], refs[2 * n + 1]
        x, y, c, _ = _place()
        _sibling_barrier(x, y, c)
        cps = []
        for i in range(n):
            for k, mask in enumerate(MASKS):
                _, _, pchip = _other_chip(x, y, mask)
                got = _half(out[i], c, (pchip,))
                cp = pltpu.make_async_remote_copy(got, got, send_sems.at[i, k], recv_sems.at[i, k],
                                                  device_id=(x, y, 1 - c), device_id_type=MESH)
                cp.start()
                cps.append(cp)
        for i in range(n):
            for k, mask in enumerate(MASKS):
                _, _, pchip = _other_chip(x, y, mask)
                theirs = _half(out[i], 1 - c, (pchip,))
                pltpu.make_async_remote_copy(theirs, theirs, send_sems.at[i, k], recv_sems.at[i, k],
                                             device_id=(x, y, 1 - c), device_id_type=MESH).wait_recv()
        for cp in cps:
            cp.wait_send()

    return pl.pallas_call(
        body, name=name, in_specs=[ANY] * n, out_specs=[ANY] * n,
        out_shape=[jax.ShapeDtypeStruct(g.shape, g.dtype) for g in gs],
        input_output_aliases={i: i for i in range(n)},
        scratch_shapes=[pltpu.SemaphoreType.DMA((n, 3)), pltpu.SemaphoreType.DMA((n, 3))],
        compiler_params=pltpu.CompilerParams(collective_id=SIBLING_ID),
    )(*gs)


def _share_halves(gs, *, name):
    n = len(gs)

    def body(*refs):
        out = refs[n:2 * n]
        send_sems, recv_sems = refs[2 * n], refs[2 * n + 1]
        x, y, c, _ = _place()
        _sibling_barrier(x, y, c)
        cps = []
        for i in range(n):
            cp = pltpu.make_async_remote_copy(_half(out[i], c), _half(out[i], c), send_sems.at[i], recv_sems.at[i],
                                              device_id=(x, y, 1 - c), device_id_type=MESH)
            cp.start()
            cps.append(cp)
        for i in range(n):
            theirs = _half(out[i], 1 - c)
            pltpu.make_async_remote_copy(theirs, theirs, send_sems.at[i], recv_sems.at[i],
                                         device_id=(x, y, 1 - c), device_id_type=MESH).wait_recv()
        for cp in cps:
            cp.wait_send()

    return pl.pallas_call(
        body, name=name, in_specs=[ANY] * n, out_specs=[ANY] * n,
        out_shape=[jax.ShapeDtypeStruct(g.shape, g.dtype) for g in gs],
        input_output_aliases={i: i for i in range(n)},
        scratch_shapes=[pltpu.SemaphoreType.DMA((n,)), pltpu.SemaphoreType.DMA((n,))],
        compiler_params=pltpu.CompilerParams(collective_id=SIBLING_ID),
    )(*gs)


def _gather_all(buf, *, name):
    r, L = buf.shape
    vmem = pl.BlockSpec(memory_space=pltpu.VMEM)
    masks = tuple(range(1, N_DEV))

    def body(buf_ref, out_ref, send_sems, recv_sems):
        x, y, c, _ = _place()
        me = 4 * x + 2 * y + c
        out_ref[me] = buf_ref[...]
        sends = []
        for k, mask in enumerate(masks):
            px = 1 - x if mask & 4 else x
            py = 1 - y if mask & 2 else y
            pc = 1 - c if mask & 1 else c
            cp = pltpu.make_async_remote_copy(buf_ref, out_ref.at[me], send_sems.at[k], recv_sems.at[k],
                                              device_id=(px, py, pc), device_id_type=MESH)
            cp.start()
            sends.append(cp)
        for k, mask in enumerate(masks):
            px = 1 - x if mask & 4 else x
            py = 1 - y if mask & 2 else y
            pc = 1 - c if mask & 1 else c
            pltpu.make_async_remote_copy(buf_ref, out_ref.at[4 * px + 2 * py + pc], send_sems.at[k],
                                         recv_sems.at[k], device_id=(px, py, pc), device_id_type=MESH).wait_recv()
        for cp in sends:
            cp.wait_send()

    return pl.pallas_call(
        body, name=name, in_specs=[vmem], out_specs=vmem,
        out_shape=jax.ShapeDtypeStruct((N_DEV, r, L), buf.dtype),
        scratch_shapes=[pltpu.SemaphoreType.DMA((N_DEV - 1,)), pltpu.SemaphoreType.DMA((N_DEV - 1,))],
    )(buf)


def _pack(arrs, lanes, row_mult=8):
    flat = jnp.concatenate([a.reshape(-1) for a in arrs])
    rows = -(-flat.shape[0] // lanes)
    rows = -(-rows // row_mult) * row_mult
    flat = jnp.pad(flat, (0, rows * lanes - flat.shape[0]))
    return flat.reshape(rows, lanes)


def _unpack(buf, shapes):
    flat = buf.reshape(-1)
    out, pos = [], 0
    for s in shapes:
        n = 1
        for d in s:
            n *= d
        out.append(flat[pos:pos + n].reshape(s))
        pos += n
    return out


def kernel(x, mem, norm_mix_g, w_in, conv_a_w, conv_b_w, conv_b_bias, ln_b_g, ln_b_b, w_out, norm_x_g, norm_mem_g, w_q, w_kv, w_xo, norm_ffn_g, w_up, w_down, final_g, loss_target, m_norm_mix_g, m_w_in, m_conv_a_w, m_conv_b_w, m_conv_b_bias, m_ln_b_g, m_ln_b_b, m_w_out, m_norm_x_g, m_norm_mem_g, m_w_q, m_w_kv, m_w_xo, m_norm_ffn_g, m_w_up, m_w_down, m_final_g, v_norm_mix_g, v_w_in, v_conv_a_w, v_conv_b_w, v_conv_b_bias, v_ln_b_g, v_ln_b_b, v_w_out, v_norm_x_g, v_norm_mem_g, v_w_q, v_w_kv, v_w_xo, v_norm_ffn_g, v_w_up, v_w_down, v_final_g):
    W = dict(norm_mix_g=norm_mix_g, w_in=w_in, conv_a_w=conv_a_w, conv_b_w=conv_b_w, conv_b_bias=conv_b_bias,
             ln_b_g=ln_b_g, ln_b_b=ln_b_b, w_out=w_out, norm_x_g=norm_x_g, norm_mem_g=norm_mem_g, w_q=w_q,
             w_kv=w_kv, w_xo=w_xo, norm_ffn_g=norm_ffn_g, w_up=w_up, w_down=w_down, final_g=final_g)
    MO = dict(norm_mix_g=m_norm_mix_g, w_in=m_w_in, conv_a_w=m_conv_a_w, conv_b_w=m_conv_b_w,
              conv_b_bias=m_conv_b_bias, ln_b_g=m_ln_b_g, ln_b_b=m_ln_b_b, w_out=m_w_out, norm_x_g=m_norm_x_g,
              norm_mem_g=m_norm_mem_g, w_q=m_w_q, w_kv=m_w_kv, w_xo=m_w_xo, norm_ffn_g=m_norm_ffn_g,
              w_up=m_w_up, w_down=m_w_down, final_g=m_final_g)
    VO = dict(norm_mix_g=v_norm_mix_g, w_in=v_w_in, conv_a_w=v_conv_a_w, conv_b_w=v_conv_b_w,
              conv_b_bias=v_conv_b_bias, ln_b_g=v_ln_b_g, ln_b_b=v_ln_b_b, w_out=v_w_out, norm_x_g=v_norm_x_g,
              norm_mem_g=v_norm_mem_g, w_q=v_w_q, w_kv=v_w_kv, w_xo=v_w_xo, norm_ffn_g=v_norm_ffn_g,
              w_up=v_w_up, w_down=v_w_down, final_g=v_final_g)
    names = list(W.keys())
    depth = norm_mix_g.shape[0]
    assert depth == 2, "the exchange splits the weights into one layer per core of a chip"
    c_idx = lax.axis_index("c")
    chip_idx = 2 * lax.axis_index("x") + lax.axis_index("y")

    xs = x[0]
    ms = mem[0]
    tgt = loss_target[0]
    S, D = xs.shape
    c_a = conv_a_w.shape[-1] * N_CHIPS
    c_loc = conv_a_w.shape[-1]

    meta = jnp.stack([c_idx, chip_idx]).astype(jnp.int32)
    shard_axis = dict(BIG)
    own = {(l, n): W[n][l].astype(BF16) for l in range(depth) for n, _ in BIG}

    conv_local = _pack([conv_a_w, conv_b_w], 128)
    conv_all = _gather_all(conv_local, name="gather_conv_weights")
    order = [(l, gi) for l in range(depth) for gi in range(len(FWD_GROUPS))]
    src_groups = [[own[(l, n)] for n in FWD_GROUPS[gi]] for l, gi in order]
    started, gather_token = _ici_start(
        src_groups, [[jax.ShapeDtypeStruct((N_CHIPS, *s.shape), s.dtype) for s in g] for g in src_groups], True,
        name="gather_weights_start", after=conv_all)
    started = dict(zip(order, started))
    Wb = [dict() for _ in range(depth)]

    def weights_ready(l, gi, after):
        send_sems, recv_sems, srcs, lands = started[(l, gi)]
        srcs, lands = _ici_wait(send_sems, recv_sems, srcs, lands, True, after, name=f"gather_weights_l{l}_g{gi}_wait")
        full = _relay_halves(lands, name=f"gather_weights_l{l}_g{gi}_relay")
        for n, g, o in zip(FWD_GROUPS[gi], full, srcs):
            g = lax.dynamic_update_slice(g, o[None], (chip_idx, 0, 0))
            Wb[l][n] = g.reshape(-1, g.shape[-1]) if shard_axis[n] == 0 else g
    na = depth * K_A * c_loc
    nbw = depth * K_B * c_loc
    ca_parts, cb_parts = [], []
    for j in range(N_CHIPS):
        fl = conv_all[2 * j].reshape(-1)
        ca_parts.append(fl[:na].reshape(depth, K_A, c_loc))
        cb_parts.append(fl[na:na + nbw].reshape(depth, K_B, c_loc))
    conv_a_full = jnp.concatenate(ca_parts, axis=-1)
    conv_b_full = jnp.concatenate(cb_parts, axis=-1)

    saved = []
    h = xs
    for l in range(depth):
        wl = Wb[l]
        t = f"l{l}_"
        if l == 0:
            u = _rms_fwd(h, norm_mix_g[l:l + 1], name=t + "rms_mix", after=gather_token)
        weights_ready(l, 0, u)
        z = _mm(u, wl["w_in"], b_stack=True, name=t + "mm_in", bm=2048)
        y_a = _mixer_a_fwd(z, conv_a_full[l], name=t + "mixer_a")
        cb = _mixer_b_fwd(z, conv_b_full[l], conv_b_bias[l:l + 1], name=t + "mixer_b")
        yy = _ln_silu_fwd(cb, ln_b_g[l:l + 1], ln_b_b[l:l + 1], y_a, name=t + "ln_silu")
        weights_ready(l, 1, yy)
        h2, q_in = _mm(yy, wl["w_out"], res=h, epi="rms_fwd", norm=norm_x_g[l:l + 1], name=t + "mm_out")
        q = _mm(q_in, wl["w_q"], out_dtype=BF16, name=t + "mm_q")
        mn = _rms_fwd(ms, norm_mem_g[l:l + 1], name=t + "rms_mem")
        kv = _mm(mn, wl["w_kv"], b_stack=True, out_dtype=BF16, name=t + "mm_kv")
        o = _attn_fwd(q, kv, name=t + "attn", tq=ROW_TILE * (1 + l))
        h3, u3 = _mm(o, wl["w_xo"], res=h2, epi="rms_fwd", norm=norm_ffn_g[l:l + 1], name=t + "mm_xo")
        weights_ready(l, 2, h3)
        a_pre, hh = _mm(u3, wl["w_up"], b_stack=True, out_dtype=BF16, epi="sqrelu", name=t + "mm_up")
        saved.append(dict(h=h, u=u, z=z, cb=cb, yy=yy, h2=h2, q_in=q_in, q=q, mn=mn, kv=kv, o=o, h3=h3,
                          u3=u3, a_pre=a_pre, hh=hh))
        if l + 1 < depth:
            h, u = _mm(hh, wl["w_down"], res=h3, epi="rms_fwd", norm=norm_mix_g[l + 1:l + 2], name=t + "mm_down")
        else:
            h = _mm(hh, wl["w_down"], res=h3, name=t + "mm_down")

    loss_vec, dh, dhb, d_final = _loss_head(h, final_g.reshape(1, D), tgt, name="loss_head")
    loss = lax.psum(loss_vec[0, 0], ("x", "y", "c"))

    GW = [dict() for _ in range(depth)]
    GS = [dict() for _ in range(depth)]
    pending = []

    def reduce_start(l, gi):
        group = BWD_GROUPS[gi]

        def by_chip(g, n):
            return g if g.ndim == 3 else g.reshape(N_CHIPS, *W[n].shape[1:])

        gs = [by_chip(GW[l][n][0], n) for n in group]
        from_sibling = _swap_halves([by_chip(GW[l][n][1], n) for n in group],
                                    name=f"grad_swap_sibling_l{l}_g{gi}")
        prs = [_pair_sum(g, r, meta, name=f"grad_pair_sum_l{l}_{n}") for g, r, n in zip(gs, from_sibling, group)]
        (st,), token = _ici_start([prs], [prs], False, name=f"grad_scatter_chips_l{l}_g{gi}_start")
        pending.append((l, group, st))
        return token

    for l in reversed(range(depth)):
        wl, sv = Wb[l], saved[l]
        t = f"l{l}_b_"
        fused = dict(epi="rms_bwd", bm=512)
        GW[l]["w_down"] = _mm(sv["hh"], dhb, ta=True, epi="with_bf16", name=t + "dw_down")
        da = _mm(dhb, wl["w_down"], tb=True, out_dtype=BF16, epi="dsqrelu", aux=sv["a_pre"], name=t + "d_hidden")
        GW[l]["w_up"] = _mm(sv["u3"], da, ta=True, o_stack=N_CHIPS, epi="with_bf16", name=t + "dw_up")
        dh, dhb, GS[l]["norm_ffn_g"] = _mm(da, wl["w_up"], tb=True, b_stack=True, res=dh,
                                           norm=(sv["h3"], norm_ffn_g[l:l + 1]), after=reduce_start(l, 0),
                                           name=t + "d_u3", **fused)
        GW[l]["w_xo"] = _mm(sv["o"], dhb, ta=True, epi="with_bf16", name=t + "dw_xo")
        d_o = _mm(dhb, wl["w_xo"], tb=True, out_dtype=BF16, name=t + "d_o")
        dq, dkv = _attn_bwd(sv["q"], sv["kv"], d_o, name=t + "attn", tq=ROW_TILE * (1 + l))
        GW[l]["w_q"] = _mm(sv["q_in"], dq, ta=True, epi="with_bf16", name=t + "dw_q")
        dkvb = dkv.astype(BF16)
        GW[l]["w_kv"] = _mm(sv["mn"], dkvb, ta=True, o_stack=N_CHIPS, epi="with_bf16", name=t + "dw_kv")
        dmn = _mm(dkvb, wl["w_kv"], tb=True, b_stack=True, name=t + "d_mem")
        _, _, GS[l]["norm_mem_g"] = _rms_bwd(ms, norm_mem_g[l:l + 1], dmn, None, name=t + "rms_mem")
        dh, dhb, GS[l]["norm_x_g"] = _mm(dq, wl["w_q"], tb=True, res=dh, norm=(sv["h2"], norm_x_g[l:l + 1]),
                                         after=reduce_start(l, 1), name=t + "d_q_in", **fused)
        GW[l]["w_out"] = _mm(sv["yy"], dhb, ta=True, epi="with_bf16", name=t + "dw_out")
        dyy = _mm(dhb, wl["w_out"], tb=True, name=t + "d_y")
        dcb, GS[l]["ln_b_g"], GS[l]["ln_b_b"] = _ln_silu_bwd(sv["cb"], ln_b_g[l:l + 1], ln_b_b[l:l + 1], dyy, 1,
                                                             name=t + "ln_silu")
        db_, dc_, dh_, GS[l]["conv_a_w"] = _mixer_a_bwd(sv["z"], conv_a_full[l], dyy, name=t + "mixer_a")
        dv_, dg_, GS[l]["conv_b_w"], GS[l]["conv_b_bias"] = _mixer_b_bwd(sv["z"], conv_b_full[l], dcb,
                                                                         name=t + "mixer_b")
        dz = jnp.concatenate([db_, dc_, dh_, dv_, dg_], axis=1)
        GW[l]["w_in"] = _mm(sv["u"], dz, ta=True, o_stack=N_CHIPS, epi="with_bf16", name=t + "dw_in")
        dh, dhb, GS[l]["norm_mix_g"] = _mm(dz, wl["w_in"], tb=True, b_stack=True, res=dh,
                                           norm=(sv["h"], norm_mix_g[l:l + 1]), after=reduce_start(l, 2),
                                           name=t + "d_u", **fused)
    grad_x = dh[None]

    after = GS[0]["norm_mix_g"]
    keys, halves = [], []
    for l, group, (send_sems, recv_sems, srcs, lands) in pending:
        prs, pieces = _ici_wait(send_sems, recv_sems, srcs, lands, False, after,
                                name=f"grad_scatter_chips_l{l}_{group[0]}_wait")
        for n, p, q in zip(group, prs, pieces):
            keys.append((l, n))
            halves.append(_chip_sum(p, q, meta, name=f"grad_chip_sum_l{l}_{n}"))
        after = halves[-1]
    reduced = dict(zip(keys, _share_halves(halves, name="grad_share_sibling")))

    grads, deltas, new_m, new_v = {}, {}, {}, {}
    for n, _ in BIG:
        grads[n], deltas[n], new_m[n], new_v[n] = _adamw_layers(W[n], reduced[(0, n)], reduced[(1, n)], MO[n], VO[n],
                                                               name="adamw_" + n)

    small = [n for n in names if n not in dict(BIG)]
    full_shapes = {n: ((depth, W[n].shape[1], c_a) if n in ("conv_a_w", "conv_b_w") else W[n].shape)
                   for n in small}

    def small_grad(n):
        if n == "final_g":
            return d_final.reshape(W[n].shape)
        return jnp.stack([GS[l][n].reshape(full_shapes[n][1:]) for l in range(depth)])

    part = _pack([small_grad(n) for n in small], LANES)
    everyone = _gather_all(part, name="gather_small_grads")
    total = _sum_leading(everyone, name="small_grad_sum")
    full_grads = dict(zip(small, _unpack(total, [full_shapes[n] for n in small])))
    for n in ("conv_a_w", "conv_b_w"):
        full_grads[n] = lax.dynamic_slice_in_dim(full_grads[n], chip_idx * c_loc, c_loc, axis=2)
    shapes = [W[n].shape for n in small]
    d_s, m_s, v_s = _adamw(_pack([W[n] for n in small], 128), _pack([full_grads[n] for n in small], 128),
                           _pack([MO[n] for n in small], 128), _pack([VO[n] for n in small], 128),
                           name="adamw_small")
    for n, d, nm, nv in zip(small, _unpack(d_s, shapes), _unpack(m_s, shapes), _unpack(v_s, shapes)):
        grads[n], deltas[n], new_m[n], new_v[n] = full_grads[n], d, nm, nv

    return (loss, grad_x, *[grads[n] for n in names], *[deltas[n] for n in names],
            *[new_m[n] for n in names], *[new_v[n] for n in names])
```

```python
import jax
import jax.numpy as jnp
from jax import lax
from jax.experimental import pallas as pl
from jax.experimental.pallas import tpu as pltpu

F32 = jnp.float32
BF16 = jnp.bfloat16
MESH = pl.DeviceIdType.MESH

EPS = 1e-6
N_XHEADS = 4
K_A = 3
K_B = 31
PAD_A = 8
PAD_B = 32
CONV_CHUNK = 256
ROW_TILE = 512
LANES = 1024
VMEM_LIMIT_BYTES = 56 * 1024 * 1024

ADAM_LR = 0.001
ADAM_B1 = 0.9
ADAM_B2 = 0.999
ADAM_EPS = 1e-08
ADAM_WD = 0.01
ADAM_STEP = 10

BIG = (("w_in", 1), ("w_out", 0), ("w_q", 0), ("w_kv", 1), ("w_xo", 0), ("w_up", 1), ("w_down", 0))
FWD_GROUPS = (("w_in",), ("w_out", "w_q", "w_kv", "w_xo"), ("w_up", "w_down"))
BWD_GROUPS = (("w_down", "w_up"), ("w_xo", "w_q", "w_kv"), ("w_out", "w_in"))
N_CHIPS = 4
N_DEV = 8


def _params(sem=None):
    return pltpu.CompilerParams(dimension_semantics=sem, vmem_limit_bytes=VMEM_LIMIT_BYTES)


def _pick(cands, n):
    for c in cands:
        if c <= n and n % c == 0:
            return c
    return n


def _mm(a, b, *, name, ta=False, tb=False, out_dtype=F32, res=None, epi=None, aux=None, norm=None, after=None,
        b_stack=False, o_stack=0, bm=1024, bn=1024, bk=1024):
    if ta:
        K, M = a.shape
    else:
        M, K = a.shape
    if b_stack:
        n_st, d1, d2 = b.shape
        N, kb = (d1, d2) if tb else (n_st * d2, d1)
        assert K == (n_st * d2 if tb else d1), (name, a.shape, b.shape)
    else:
        N = b.shape[0] if tb else b.shape[1]
    n_unit = b.shape[2] if (b_stack and not tb) else (N // o_stack if o_stack else N)
    k_unit = b.shape[2] if (b_stack and tb) else K
    bm = _pick((bm, 512, 256, 128), M)
    bn = _pick((bn, 512, 640, 256, 384, 128), n_unit)
    bk = _pick((bk, 640, 512, 256, 128), k_unit)
    assert M % bm == 0 and N % bn == 0 and K % bk == 0, (name, M, N, K)
    nk = K // bk
    per_n = n_unit // bn
    per_k = k_unit // bk
    a_spec = (pl.BlockSpec((bk, bm), lambda i, j, k: (k, i)) if ta
              else pl.BlockSpec((bm, bk), lambda i, j, k: (i, k)))
    if b_stack and tb:
        b_spec = pl.BlockSpec((None, bn, bk), lambda i, j, k: (k // per_k, j, k % per_k))
    elif b_stack:
        b_spec = pl.BlockSpec((None, bk, bn), lambda i, j, k: (j // per_n, k, j % per_n))
    elif tb:
        b_spec = pl.BlockSpec((bn, bk), lambda i, j, k: (j, k))
    else:
        b_spec = pl.BlockSpec((bk, bn), lambda i, j, k: (k, j))
    o_spec = pl.BlockSpec((bm, bn), lambda i, j, k: (i, j))
    dims = (((0 if ta else 1,), (1 if tb else 0,)), ((), ()))
    ins, in_specs = [a, b], [a_spec, b_spec]
    if res is not None:
        ins.append(res)
        in_specs.append(o_spec)
    if aux is not None:
        ins.append(aux)
        in_specs.append(o_spec)
    vec_spec = pl.BlockSpec((1, bn), lambda i, j, k: (0, j))
    if epi == "rms_fwd":
        assert bn == N, (name, bn, N)
        ins.append(norm)
        in_specs.append(vec_spec)
    elif epi == "rms_bwd":
        assert bn == N, (name, bn, N)
        ins += [norm[0], norm[1]]
        in_specs += [o_spec, vec_spec]
    n_norm = {"rms_fwd": 1, "rms_bwd": 2}.get(epi, 0)
    if after is not None:
        ins.append(after)
        in_specs.append(pl.BlockSpec(memory_space=pl.ANY))
    n_out = {"sqrelu": 2, "rms_fwd": 2, "rms_bwd": 3, "with_bf16": 2}.get(epi, 1)
    out_shape = [jax.ShapeDtypeStruct((M, N), out_dtype)] * n_out
    out_specs = [o_spec] * n_out
    if epi in ("rms_fwd", "with_bf16"):
        out_shape = [jax.ShapeDtypeStruct((M, N), F32), jax.ShapeDtypeStruct((M, N), BF16)]
    elif epi == "rms_bwd":
        out_shape = [jax.ShapeDtypeStruct((M, N), F32), jax.ShapeDtypeStruct((M, N), BF16),
                     jax.ShapeDtypeStruct((1, N), F32)]
        out_specs = [o_spec, o_spec, vec_spec]
    if o_stack:
        assert epi in (None, "with_bf16") and res is None and aux is None
        out_shape = [jax.ShapeDtypeStruct((o_stack, M, N // o_stack), s.dtype) for s in out_shape]
        out_specs = [pl.BlockSpec((None, bm, bn), lambda i, j, k: (j // per_n, i, j % per_n))] * n_out

    def body(*refs):
        a_ref, b_ref = refs[0], refs[1]
        pos = 2
        res_ref = aux_ref = None
        if res is not None:
            res_ref = refs[pos]
            pos += 1
        if aux is not None:
            aux_ref = refs[pos]
            pos += 1
        norm_refs = refs[pos:pos + n_norm]
        pos += n_norm + (after is not None)
        outs = refs[pos:pos + n_out]

        def product():
            return lax.dot_general(a_ref[...], b_ref[...], dims, preferred_element_type=F32)

        def finish(r):
            if epi == "rms_bwd":
                x_ref, g_ref = norm_refs
                xv = x_ref[...]
                rs = lax.rsqrt(jnp.mean(xv * xv, axis=-1, keepdims=True) + EPS)
                xh = xv * rs
                dxh = r * g_ref[...]
                dx = rs * (dxh - xh * jnp.mean(dxh * xh, axis=-1, keepdims=True))
                if res_ref is not None:
                    dx = dx + res_ref[...]
                outs[0][...] = dx
                outs[1][...] = dx.astype(BF16)

                @pl.when(pl.program_id(0) == 0)
                def _():
                    outs[2][...] = jnp.zeros_like(outs[2])

                outs[2][...] += jnp.sum(r * xh, axis=0, keepdims=True)
                return
            if res_ref is not None:
                r = r + res_ref[...]
            if epi == "rms_fwd":
                outs[0][...] = r
                rs = lax.rsqrt(jnp.mean(r * r, axis=-1, keepdims=True) + EPS)
                outs[1][...] = (r * rs * norm_refs[0][...]).astype(BF16)
            elif epi == "with_bf16":
                outs[0][...] = r
                outs[1][...] = r.astype(BF16)
            elif epi == "sqrelu":
                outs[0][...] = r.astype(out_dtype)
                rl = jnp.maximum(r, 0.0)
                outs[1][...] = (rl * rl).astype(out_dtype)
            elif epi == "dsqrelu":
                outs[0][...] = (r * (2.0 * jnp.maximum(aux_ref[...].astype(F32), 0.0))).astype(out_dtype)
            else:
                outs[0][...] = r.astype(out_dtype)

        if nk == 1:
            finish(product())
            return
        acc = refs[pos + n_out]
        k = pl.program_id(2)

        @pl.when(k == 0)
        def _():
            acc[...] = product()

        @pl.when(jnp.logical_and(k > 0, k < nk - 1))
        def _():
            acc[...] += product()

        @pl.when(k == nk - 1)
        def _():
            finish(acc[...] + product())

    out = pl.pallas_call(
        body, name=name, grid=(M // bm, N // bn, nk),
        in_specs=in_specs, out_specs=out_specs, out_shape=out_shape,
        scratch_shapes=[pltpu.VMEM((bm, bn), F32)] if nk > 1 else [],
        compiler_params=_params(("arbitrary",) * 3 if epi == "rms_bwd" else ("parallel", "parallel", "arbitrary")),
    )(*ins)
    return out if n_out > 1 else out[0]


def _mm_rms_bwd(a, b, x, g, res, *, name, b_stack=False, after=None, bm=512, bk=1024):
    M, K = a.shape
    N = x.shape[1]
    k_unit = b.shape[2] if b_stack else K
    bm = _pick((bm, 256, 128, 64, 32, 16), M)
    bk = _pick((bk, 640, 512, 256, 128), k_unit)
    nk, per_k, nt = K // bk, k_unit // bk, M // bm

    def tile(i):
        return jnp.minimum(i, nt - 1)

    def chunk(i, k):
        return jnp.where(i < nt, k, nk - 1)

    a_spec = pl.BlockSpec((bm, bk), lambda i, k: (tile(i), chunk(i, k)))
    if b_stack:
        b_spec = pl.BlockSpec((None, N, bk), lambda i, k: (chunk(i, k) // per_k, 0, chunk(i, k) % per_k))
    else:
        b_spec = pl.BlockSpec((N, bk), lambda i, k: (0, chunk(i, k)))
    prev = pl.BlockSpec((bm, N), lambda i, k: (jnp.maximum(i - 1, 0), 0))
    vec = pl.BlockSpec((1, N), lambda i, k: (0, 0))
    deps = [] if after is None else [after]

    def body(a_ref, b_ref, x_ref, g_ref, res_ref, *rest):
        dx_ref, dxb_ref, dg_ref, acc, done = rest[len(deps):]
        i, k = pl.program_id(0), pl.program_id(1)

        def product():
            return lax.dot_general(a_ref[...], b_ref[...], (((1,), (1,)), ((), ())), preferred_element_type=F32)

        def norm_backward():
            dy = done[(i + 1) % 2]
            xv = x_ref[...]
            rs = lax.rsqrt(jnp.mean(xv * xv, axis=-1, keepdims=True) + EPS)
            xh = xv * rs
            dxh = dy * g_ref[...]
            dx = rs * (dxh - xh * jnp.mean(dxh * xh, axis=-1, keepdims=True)) + res_ref[...]
            dx_ref[...] = dx
            dxb_ref[...] = dx.astype(BF16)
            dg_ref[...] += jnp.sum(dy * xh, axis=0, keepdims=True)

        def keep(p):
            if nk == 1:
                done[i % 2] = p
            else:
                acc[...] = p

        first, last = k == 0, i == nt

        @pl.when(jnp.logical_and(first, i == 0))
        def _():
            dg_ref[...] = jnp.zeros_like(dg_ref)
            keep(product())

        @pl.when(jnp.logical_and(first, jnp.logical_and(i > 0, i < nt)))
        def _():
            p = product()
            norm_backward()
            keep(p)

        @pl.when(jnp.logical_and(first, last))
        def _():
            norm_backward()

        if nk > 1:
            @pl.when(jnp.logical_and(jnp.logical_not(last), jnp.logical_and(k > 0, k < nk - 1)))
            def _():
                acc[...] += product()

            @pl.when(jnp.logical_and(jnp.logical_not(last), k == nk - 1))
            def _():
                done[i % 2] = acc[...] + product()

    return pl.pallas_call(
        body, name=name, grid=(nt + 1, nk),
        in_specs=[a_spec, b_spec, prev, vec, prev] + [pl.BlockSpec(memory_space=pl.ANY)] * len(deps),
        out_specs=[prev, prev, vec],
        out_shape=[jax.ShapeDtypeStruct((M, N), F32), jax.ShapeDtypeStruct((M, N), BF16),
                   jax.ShapeDtypeStruct((1, N), F32)],
        scratch_shapes=[pltpu.VMEM((bm, N), F32), pltpu.VMEM((2, bm, N), F32)],
        compiler_params=_params(("arbitrary", "arbitrary")),
    )(a, b, x, g, res, *deps)


def _row_tile(rows):
    return min(ROW_TILE, rows)


def _rms_fwd(x, g, *, name, after=None):
    S, D = x.shape
    tr = _row_tile(S)

    def body(x_ref, g_ref, *rest):
        o_ref = rest[-1]
        xv = x_ref[...]
        r = lax.rsqrt(jnp.mean(xv * xv, axis=-1, keepdims=True) + EPS)
        o_ref[...] = (xv * r * g_ref[...]).astype(BF16)

    deps = [] if after is None else [after]
    return pl.pallas_call(
        body, name=name, grid=(S // tr,),
        in_specs=[pl.BlockSpec((tr, D), lambda i: (i, 0)), pl.BlockSpec((1, D), lambda i: (0, 0))]
        + [ANY] * len(deps),
        out_specs=pl.BlockSpec((tr, D), lambda i: (i, 0)),
        out_shape=jax.ShapeDtypeStruct((S, D), BF16),
        compiler_params=_params(("parallel",)),
    )(x, g, *deps)


def _rms_bwd(x, g, du, dres, *, name, after=None):
    S, D = x.shape
    tr = _row_tile(S)
    has_res = dres is not None
    deps = [] if after is None else [after]

    def body(*refs):
        dx_ref, dxb_ref, dg_ref = refs[-3:]
        if has_res:
            x_ref, g_ref, du_ref, dres_ref = refs[:4]
        else:
            x_ref, g_ref, du_ref = refs[:3]
        xv = x_ref[...]
        r = lax.rsqrt(jnp.mean(xv * xv, axis=-1, keepdims=True) + EPS)
        xh = xv * r
        dy = du_ref[...]
        dxh = dy * g_ref[...]
        dx = r * (dxh - xh * jnp.mean(dxh * xh, axis=-1, keepdims=True))
        if has_res:
            dx = dx + dres_ref[...]
        dx_ref[...] = dx
        dxb_ref[...] = dx.astype(BF16)

        @pl.when(pl.program_id(0) == 0)
        def _():
            dg_ref[...] = jnp.zeros_like(dg_ref)

        dg_ref[...] += jnp.sum(dy * xh, axis=0, keepdims=True)

    row = pl.BlockSpec((tr, D), lambda i: (i, 0))
    vec = pl.BlockSpec((1, D), lambda i: (0, 0))
    ins = [x, g, du] + ([dres] if has_res else []) + deps
    in_specs = [row, vec, row] + ([row] if has_res else []) + [ANY] * len(deps)
    return pl.pallas_call(
        body, name=name, grid=(S // tr,),
        in_specs=in_specs, out_specs=[row, row, vec],
        out_shape=[jax.ShapeDtypeStruct((S, D), F32), jax.ShapeDtypeStruct((S, D), BF16),
                   jax.ShapeDtypeStruct((1, D), F32)],
        compiler_params=_params(("arbitrary",)),
    )(*ins)


def _loss_head(h, g, target, *, name):
    S, D = h.shape
    tr = _row_tile(S)

    def body(x_ref, g_ref, t_ref, loss_ref, dx_ref, dxb_ref, dg_ref):
        xv = x_ref[...]
        r = lax.rsqrt(jnp.mean(xv * xv, axis=-1, keepdims=True) + EPS)
        xh = xv * r
        gv = g_ref[...]
        err = xh * gv - t_ref[...]
        part = 0.5 * jnp.sum(jnp.mean(err * err, axis=-1, keepdims=True), axis=0, keepdims=True)
        dy = err * (1.0 / D)
        dxh = dy * gv
        dx = r * (dxh - xh * jnp.mean(dxh * xh, axis=-1, keepdims=True))
        dx_ref[...] = dx
        dxb_ref[...] = dx.astype(BF16)

        @pl.when(pl.program_id(0) == 0)
        def _():
            dg_ref[...] = jnp.zeros_like(dg_ref)
            loss_ref[...] = jnp.zeros_like(loss_ref)

        dg_ref[...] += jnp.sum(dy * xh, axis=0, keepdims=True)
        loss_ref[...] += jnp.broadcast_to(part, loss_ref.shape)

    row = pl.BlockSpec((tr, D), lambda i: (i, 0))
    vec = pl.BlockSpec((1, D), lambda i: (0, 0))
    return pl.pallas_call(
        body, name=name, grid=(S // tr,),
        in_specs=[row, vec, row],
        out_specs=[pl.BlockSpec((1, 128), lambda i: (0, 0)), row, row, vec],
        out_shape=[jax.ShapeDtypeStruct((1, 128), F32), jax.ShapeDtypeStruct((S, D), F32),
                   jax.ShapeDtypeStruct((S, D), BF16), jax.ShapeDtypeStruct((1, D), F32)],
        compiler_params=_params(("arbitrary",)),
    )(h, g, target)


def _sigmoid(x):
    return 1.0 / (1.0 + jnp.exp(-x))


def _ln_silu_fwd(cb, g, b, into, *, name):
    S, C = cb.shape
    tr = _row_tile(S)

    def body(x_ref, g_ref, b_ref, into_ref, o_ref):
        xv = x_ref[...]
        mu = jnp.mean(xv, axis=-1, keepdims=True)
        xc = xv - mu
        rs = lax.rsqrt(jnp.mean(xc * xc, axis=-1, keepdims=True) + EPS)
        l = xc * rs * g_ref[...] + b_ref[...]
        o_ref[...] = (l * _sigmoid(l)).astype(BF16)

    row = pl.BlockSpec((tr, C), lambda i: (i, 0))
    vec = pl.BlockSpec((1, C), lambda i: (0, 0))
    return pl.pallas_call(
        body, name=name, grid=(S // tr,), in_specs=[row, vec, vec, pl.BlockSpec(memory_space=pl.ANY)],
        out_specs=pl.BlockSpec((tr, C), lambda i: (i, 1)),
        out_shape=jax.ShapeDtypeStruct(into.shape, BF16), input_output_aliases={3: 0},
        compiler_params=_params(("parallel",)),
    )(cb, g, b, into)


def _ln_silu_bwd(cb, g, b, dy, col_block, *, name):
    S, C = cb.shape
    tr = _row_tile(S)

    def body(x_ref, g_ref, b_ref, dy_ref, dx_ref, dg_ref, db_ref):
        xv = x_ref[...]
        mu = jnp.mean(xv, axis=-1, keepdims=True)
        xc = xv - mu
        rs = lax.rsqrt(jnp.mean(xc * xc, axis=-1, keepdims=True) + EPS)
        xh = xc * rs
        gv = g_ref[...]
        l = xh * gv + b_ref[...]
        sg = _sigmoid(l)
        dl = dy_ref[...] * (sg + l * sg * (1.0 - sg))
        dxh = dl * gv
        dx_ref[...] = rs * (dxh - jnp.mean(dxh, axis=-1, keepdims=True)
                            - xh * jnp.mean(dxh * xh, axis=-1, keepdims=True))

        @pl.when(pl.program_id(0) == 0)
        def _():
            dg_ref[...] = jnp.zeros_like(dg_ref)
            db_ref[...] = jnp.zeros_like(db_ref)

        dg_ref[...] += jnp.sum(dl * xh, axis=0, keepdims=True)
        db_ref[...] += jnp.sum(dl, axis=0, keepdims=True)

    row = pl.BlockSpec((tr, C), lambda i: (i, 0))
    vec = pl.BlockSpec((1, C), lambda i: (0, 0))
    return pl.pallas_call(
        body, name=name, grid=(S // tr,),
        in_specs=[row, vec, vec, pl.BlockSpec((tr, C), lambda i: (i, col_block))],
        out_specs=[row, vec, vec],
        out_shape=[jax.ShapeDtypeStruct((S, C), F32), jax.ShapeDtypeStruct((1, C), F32),
                   jax.ShapeDtypeStruct((1, C), F32)],
        compiler_params=_params(("arbitrary",)),
    )(cb, g, b, dy)


def _attn_fwd(q, kv, *, name, tq=ROW_TILE):
    S, D = q.shape
    M = kv.shape[0]
    hd = D // N_XHEADS
    scale = 1.0 / float(hd) ** 0.5
    tq = min(tq, S)

    def body(q_ref, k_ref, v_ref, o_ref):
        for h in range(N_XHEADS):
            cols = slice(h * hd, (h + 1) * hd)
            s = lax.dot_general(q_ref[:, cols], k_ref[:, cols], (((1,), (1,)), ((), ())),
                                preferred_element_type=F32) * scale
            e = jnp.exp(s - jnp.max(s, axis=-1, keepdims=True))
            p = e / jnp.sum(e, axis=-1, keepdims=True)
            o = jnp.dot(p.astype(BF16), v_ref[:, cols], preferred_element_type=F32)
            o_ref[:, cols] = o.astype(BF16)

    return pl.pallas_call(
        body, name=name, grid=(S // tq,),
        in_specs=[pl.BlockSpec((tq, D), lambda i: (i, 0)), pl.BlockSpec((M, D), lambda i: (0, 0)),
                  pl.BlockSpec((M, D), lambda i: (0, 1))],
        out_specs=pl.BlockSpec((tq, D), lambda i: (i, 0)),
        out_shape=jax.ShapeDtypeStruct((S, D), BF16), compiler_params=_params(("parallel",)),
    )(q, kv, kv)


def _attn_bwd(q, kv, do, *, name, tq=ROW_TILE):
    S, D = q.shape
    M = kv.shape[0]
    hd = D // N_XHEADS
    scale = 1.0 / float(hd) ** 0.5
    tq = min(tq, S)

    def body(q_ref, k_ref, v_ref, do_ref, dq_ref, dkv_ref):
        @pl.when(pl.program_id(0) == 0)
        def _():
            dkv_ref[...] = jnp.zeros_like(dkv_ref)

        for h in range(N_XHEADS):
            cols = slice(h * hd, (h + 1) * hd)
            vcols = slice(D + h * hd, D + (h + 1) * hd)
            qh, kh, vh, doh = q_ref[:, cols], k_ref[:, cols], v_ref[:, cols], do_ref[:, cols]
            s = lax.dot_general(qh, kh, (((1,), (1,)), ((), ())), preferred_element_type=F32) * scale
            e = jnp.exp(s - jnp.max(s, axis=-1, keepdims=True))
            p = e / jnp.sum(e, axis=-1, keepdims=True)
            pb = p.astype(BF16)
            dp = lax.dot_general(doh, vh, (((1,), (1,)), ((), ())), preferred_element_type=F32)
            ds = (p * (dp - jnp.sum(dp * p, axis=-1, keepdims=True)) * scale).astype(BF16)
            dq_ref[:, cols] = jnp.dot(ds, kh, preferred_element_type=F32).astype(BF16)
            dkv_ref[:, cols] += lax.dot_general(ds, qh, (((0,), (0,)), ((), ())), preferred_element_type=F32)
            dkv_ref[:, vcols] += lax.dot_general(pb, doh, (((0,), (0,)), ((), ())), preferred_element_type=F32)

    row = pl.BlockSpec((tq, D), lambda i: (i, 0))
    return pl.pallas_call(
        body, name=name, grid=(S // tq,),
        in_specs=[row, pl.BlockSpec((M, D), lambda i: (0, 0)), pl.BlockSpec((M, D), lambda i: (0, 1)), row],
        out_specs=[row, pl.BlockSpec((M, 2 * D), lambda i: (0, 0))],
        out_shape=[jax.ShapeDtypeStruct((S, D), BF16), jax.ShapeDtypeStruct((M, 2 * D), F32)],
        compiler_params=_params(("arbitrary",)),
    )(q, kv, kv, do)


def _delayed(win, j, pad):
    return (win if j == 0 else pltpu.roll(win, j, 0))[pad:, :]


def _advanced(win, j, ch):
    return (win if j == 0 else pltpu.roll(win, win.shape[0] - j, 0))[:ch, :]


def _mixer_a_fwd(z, w, *, name):
    S = z.shape[0]
    C = w.shape[1]
    nb = C // 128
    ch = min(CONV_CHUNK, S)

    def body(b_ref, c_ref, h_ref, w_ref, y_ref, xp):
        xp[0:PAD_A, :] = jnp.zeros((PAD_A, 128), F32)
        xp[PAD_A:, :] = c_ref[...] * h_ref[...]

        def chunk(i, carry):
            base = pl.multiple_of(i * ch, ch)
            win = xp[pl.ds(base, ch + PAD_A), :]
            acc = _delayed(win, 0, PAD_A) * w_ref[K_A - 1:K_A, :]
            for j in range(1, K_A):
                acc = acc + _delayed(win, j, PAD_A) * w_ref[K_A - 1 - j:K_A - j, :]
            y_ref[pl.ds(base, ch), :] = (b_ref[pl.ds(base, ch), :] * acc).astype(BF16)
            return carry

        lax.fori_loop(0, S // ch, chunk, 0)

    def col(g):
        return pl.BlockSpec((S, 128), lambda j: (0, g * nb + j))

    return pl.pallas_call(
        body, name=name, grid=(nb,),
        in_specs=[col(0), col(1), col(2), pl.BlockSpec((K_A, 128), lambda j: (0, j))],
        out_specs=pl.BlockSpec((S, 128), lambda j: (0, j)),
        out_shape=jax.ShapeDtypeStruct((S, 2 * C), BF16),
        scratch_shapes=[pltpu.VMEM((PAD_A + S, 128), F32)],
        compiler_params=_params(("parallel",)),
    )(z, z, z, w)


def _mixer_a_bwd(z, w, dy, *, name):
    S = z.shape[0]
    C = w.shape[1]
    nb = C // 128
    ch = min(CONV_CHUNK, S)

    def body(b_ref, c_ref, h_ref, w_ref, dy_ref, db_ref, dc_ref, dh_ref, dw_ref, xp, dp):
        xp[0:PAD_A, :] = jnp.zeros((PAD_A, 128), F32)
        xp[PAD_A:, :] = c_ref[...] * h_ref[...]
        dp[S:, :] = jnp.zeros((PAD_A, 128), F32)
        dw_ref[...] = jnp.zeros_like(dw_ref)

        def chunk(i, carry):
            base = pl.multiple_of(i * ch, ch)
            win = xp[pl.ds(base, ch + PAD_A), :]
            dya = dy_ref[pl.ds(base, ch), :]
            dcv = dya * b_ref[pl.ds(base, ch), :]
            dp[pl.ds(base, ch), :] = dcv
            acc = None
            for j in range(K_A):
                xs = _delayed(win, j, PAD_A)
                k = K_A - 1 - j
                term = xs * w_ref[k:k + 1, :]
                acc = term if acc is None else acc + term
                dw_ref[k:k + 1, :] += jnp.sum(dcv * xs, axis=0, keepdims=True)
            db_ref[pl.ds(base, ch), :] = (dya * acc).astype(BF16)
            return carry

        lax.fori_loop(0, S // ch, chunk, 0)

        def chunk2(i, carry):
            base = pl.multiple_of(i * ch, ch)
            win = dp[pl.ds(base, ch + PAD_A), :]
            acc = None
            for j in range(K_A):
                term = _advanced(win, j, ch) * w_ref[K_A - 1 - j:K_A - j, :]
                acc = term if acc is None else acc + term
            dc_ref[pl.ds(base, ch), :] = (acc * h_ref[pl.ds(base, ch), :]).astype(BF16)
            dh_ref[pl.ds(base, ch), :] = (acc * c_ref[pl.ds(base, ch), :]).astype(BF16)
            return carry

        lax.fori_loop(0, S // ch, chunk2, 0)

    def col(g):
        return pl.BlockSpec((S, 128), lambda j: (0, g * nb + j))

    out_col = pl.BlockSpec((S, 128), lambda j: (0, j))
    wspec = pl.BlockSpec((K_A, 128), lambda j: (0, j))
    return pl.pallas_call(
        body, name=name, grid=(nb,),
        in_specs=[col(0), col(1), col(2), wspec, out_col],
        out_specs=[out_col, out_col, out_col, wspec],
        out_shape=[jax.ShapeDtypeStruct((S, C), BF16)] * 3 + [jax.ShapeDtypeStruct((K_A, C), F32)],
        scratch_shapes=[pltpu.VMEM((PAD_A + S, 128), F32), pltpu.VMEM((S + PAD_A, 128), F32)],
        compiler_params=_params(("parallel",)),
    )(z, z, z, w, dy)


def _mixer_b_fwd(z, w, bias, *, name):
    S = z.shape[0]
    C = w.shape[1]
    nb = C // 128
    ch = min(CONV_CHUNK, S)

    def body(v_ref, g_ref, w_ref, bias_ref, cb_ref, xp):
        xp[0:PAD_B, :] = jnp.zeros((PAD_B, 128), F32)
        xp[PAD_B:, :] = v_ref[...] * _sigmoid(g_ref[...])

        def chunk(i, carry):
            base = pl.multiple_of(i * ch, ch)
            win = xp[pl.ds(base, ch + PAD_B), :]
            acc = None
            for j in range(K_B):
                term = _delayed(win, j, PAD_B) * w_ref[K_B - 1 - j:K_B - j, :]
                acc = term if acc is None else acc + term
            cb_ref[pl.ds(base, ch), :] = acc + bias_ref[...]
            return carry

        lax.fori_loop(0, S // ch, chunk, 0)

    def col(g):
        return pl.BlockSpec((S, 128), lambda j: (0, g * nb + j))

    return pl.pallas_call(
        body, name=name, grid=(nb,),
        in_specs=[col(3), col(4), pl.BlockSpec((K_B, 128), lambda j: (0, j)),
                  pl.BlockSpec((1, 128), lambda j: (0, j))],
        out_specs=pl.BlockSpec((S, 128), lambda j: (0, j)),
        out_shape=jax.ShapeDtypeStruct((S, C), F32),
        scratch_shapes=[pltpu.VMEM((PAD_B + S, 128), F32)],
        compiler_params=_params(("parallel",)),
    )(z, z, w, bias)


def _mixer_b_bwd(z, w, dcb, *, name):
    S = z.shape[0]
    C = w.shape[1]
    nb = C // 128
    ch = min(CONV_CHUNK, S)

    def body(v_ref, g_ref, w_ref, dcb_ref, dv_ref, dg_ref, dw_ref, dbias_ref, xp, dp):
        xp[0:PAD_B, :] = jnp.zeros((PAD_B, 128), F32)
        xp[PAD_B:, :] = v_ref[...] * _sigmoid(g_ref[...])
        dp[0:S, :] = dcb_ref[...]
        dp[S:, :] = jnp.zeros((PAD_B, 128), F32)
        dw_ref[...] = jnp.zeros_like(dw_ref)
        dbias_ref[...] = jnp.sum(dcb_ref[...], axis=0, keepdims=True)

        def chunk(i, carry):
            base = pl.multiple_of(i * ch, ch)
            win = xp[pl.ds(base, ch + PAD_B), :]
            d = dcb_ref[pl.ds(base, ch), :]
            for j in range(K_B):
                k = K_B - 1 - j
                dw_ref[k:k + 1, :] += jnp.sum(d * _delayed(win, j, PAD_B), axis=0, keepdims=True)
            return carry

        lax.fori_loop(0, S // ch, chunk, 0)

        def chunk2(i, carry):
            base = pl.multiple_of(i * ch, ch)
            win = dp[pl.ds(base, ch + PAD_B), :]
            acc = None
            for j in range(K_B):
                term = _advanced(win, j, ch) * w_ref[K_B - 1 - j:K_B - j, :]
                acc = term if acc is None else acc + term
            sg = _sigmoid(g_ref[pl.ds(base, ch), :])
            vv = v_ref[pl.ds(base, ch), :]
            dv_ref[pl.ds(base, ch), :] = (acc * sg).astype(BF16)
            dg_ref[pl.ds(base, ch), :] = (acc * vv * sg * (1.0 - sg)).astype(BF16)
            return carry

        lax.fori_loop(0, S // ch, chunk2, 0)

    def col(g):
        return pl.BlockSpec((S, 128), lambda j: (0, g * nb + j))

    out_col = pl.BlockSpec((S, 128), lambda j: (0, j))
    wspec = pl.BlockSpec((K_B, 128), lambda j: (0, j))
    bspec = pl.BlockSpec((1, 128), lambda j: (0, j))
    return pl.pallas_call(
        body, name=name, grid=(nb,),
        in_specs=[col(3), col(4), wspec, out_col],
        out_specs=[out_col, out_col, wspec, bspec],
        out_shape=[jax.ShapeDtypeStruct((S, C), BF16)] * 2
        + [jax.ShapeDtypeStruct((K_B, C), F32), jax.ShapeDtypeStruct((1, C), F32)],
        scratch_shapes=[pltpu.VMEM((PAD_B + S, 128), F32), pltpu.VMEM((S + PAD_B, 128), F32)],
        compiler_params=_params(("parallel",)),
    )(z, z, w, dcb)


def _ew_tile(R):
    for t in (512, 256, 128, 64, 32, 16, 8):
        if R % t == 0:
            return t
    return R


def _pair_sum(g, r, meta, *, name):
    n, a, b = g.shape
    ah = a // 2
    tr = _pick((256, 128, 64, 32, 16), ah)
    nh = ah // tr

    def body(meta_ref, g_ref, r_ref, o_ref):
        o_ref[...] = (g_ref[...] + r_ref[...].astype(F32)).astype(BF16)

    half = pl.BlockSpec((None, tr, b), lambda j, i, meta_ref: (j, i, 0))
    return pl.pallas_call(
        body, name=name,
        grid_spec=pltpu.PrefetchScalarGridSpec(
            num_scalar_prefetch=1, grid=(n, nh),
            in_specs=[pl.BlockSpec((None, tr, b), lambda j, i, meta_ref: (j, meta_ref[0] * nh + i, 0)), half],
            out_specs=half),
        out_shape=jax.ShapeDtypeStruct((n, ah, b), BF16), compiler_params=_params(("parallel", "parallel")),
    )(meta, g, r)


def _chip_sum(p, q, meta, *, name):
    n, ah, b = p.shape
    tr = _pick((256, 128, 64, 32, 16), ah)
    nh = ah // tr

    def body(meta_ref, p_ref, q1_ref, q2_ref, q3_ref, o_ref):
        o_ref[...] = ((p_ref[...].astype(F32) + q1_ref[...].astype(F32)) + q2_ref[...].astype(F32)
                      ) + q3_ref[...].astype(F32)

    def piece(mask):
        return pl.BlockSpec((None, tr, b), lambda i, meta_ref: (meta_ref[1] ^ mask, i, 0))

    return pl.pallas_call(
        body, name=name,
        grid_spec=pltpu.PrefetchScalarGridSpec(
            num_scalar_prefetch=1, grid=(nh,),
            in_specs=[piece(0), piece(1), piece(2), piece(3)],
            out_specs=pl.BlockSpec((tr, b), lambda i, meta_ref: (meta_ref[0] * nh + i, 0))),
        out_shape=jax.ShapeDtypeStruct((2 * ah, b), F32), compiler_params=_params(("parallel",)),
    )(meta, p, q, q, q)


def _sum_leading(x, *, name):
    n, R, C = x.shape
    tr = _ew_tile(R)

    def body(x_ref, o_ref):
        acc = x_ref[0].astype(F32)
        for k in range(1, n):
            acc = acc + x_ref[k].astype(F32)
        o_ref[...] = acc

    return pl.pallas_call(
        body, name=name, grid=(R // tr,),
        in_specs=[pl.BlockSpec((n, tr, C), lambda i: (0, i, 0))],
        out_specs=pl.BlockSpec((tr, C), lambda i: (i, 0)),
        out_shape=jax.ShapeDtypeStruct((R, C), F32), compiler_params=_params(("parallel",)),
    )(x)


def _adamw(w, g, m, v, *, name):
    R, C = w.shape
    tr = _ew_tile(R)

    def body(w_ref, g_ref, m_ref, v_ref, d_ref, nm_ref, nv_ref):
        gv = g_ref[...]
        nm = ADAM_B1 * m_ref[...] + (1.0 - ADAM_B1) * gv
        nv = ADAM_B2 * v_ref[...] + (1.0 - ADAM_B2) * (gv * gv)
        m_hat = nm / (1.0 - ADAM_B1 ** ADAM_STEP)
        v_hat = nv / (1.0 - ADAM_B2 ** ADAM_STEP)
        d_ref[...] = -ADAM_LR * (m_hat / (jnp.sqrt(v_hat) + ADAM_EPS) + ADAM_WD * w_ref[...])
        nm_ref[...] = nm
        nv_ref[...] = nv

    row = pl.BlockSpec((tr, C), lambda i: (i, 0))
    return pl.pallas_call(
        body, name=name, grid=(R // tr,), in_specs=[row] * 4, out_specs=[row] * 3,
        out_shape=[jax.ShapeDtypeStruct((R, C), F32)] * 3, compiler_params=_params(("parallel",)),
    )(w, g, m, v)


def _adamw_layers(w, g0, g1, m, v, *, name):
    _, a, b = w.shape
    tr = _pick((512, 256, 128, 64, 32, 16, 8), a)

    def body(w_ref, g0_ref, g1_ref, m_ref, v_ref, g_ref, d_ref, nm_ref, nv_ref):
        gv = jnp.where(pl.program_id(1) == 0, g0_ref[...], g1_ref[...])
        nm = ADAM_B1 * m_ref[...] + (1.0 - ADAM_B1) * gv
        nv = ADAM_B2 * v_ref[...] + (1.0 - ADAM_B2) * (gv * gv)
        m_hat = nm / (1.0 - ADAM_B1 ** ADAM_STEP)
        v_hat = nv / (1.0 - ADAM_B2 ** ADAM_STEP)
        g_ref[...] = gv
        d_ref[...] = -ADAM_LR * (m_hat / (jnp.sqrt(v_hat) + ADAM_EPS) + ADAM_WD * w_ref[...])
        nm_ref[...] = nm
        nv_ref[...] = nv

    lay = pl.BlockSpec((None, tr, b), lambda i, l: (l, i, 0))
    row = pl.BlockSpec((tr, b), lambda i, l: (i, 0))
    return pl.pallas_call(
        body, name=name, grid=(a // tr, 2), in_specs=[lay, row, row, lay, lay], out_specs=[lay] * 4,
        out_shape=[jax.ShapeDtypeStruct(w.shape, F32)] * 4, compiler_params=_params(("parallel", "arbitrary")),
    )(w, g0, g1, m, v)


ANY = pl.BlockSpec(memory_space=pl.ANY)


def _place():
    x, y, c = lax.axis_index("x"), lax.axis_index("y"), lax.axis_index("c")
    return x, y, c, 2 * x + y


def _other_chip(x, y, mask):
    px = 1 - x if mask & 2 else x
    py = 1 - y if mask & 1 else y
    return px, py, 2 * px + py


MASKS = (1, 2, 3)


def _half(ref, c, lead=()):
    ah = ref.shape[-2] // 2
    return ref.at[(*lead, pl.ds(c * ah, ah), slice(None))]


def _gather_weights(owns, *, name):
    n = len(owns)

    def body(*refs):
        own, out = refs[:n], refs[n:2 * n]
        send_sems, recv_sems = refs[2 * n], refs[2 * n + 1]
        x, y, c, chip = _place()
        sends = []
        for i in range(n):
            for k, mask in enumerate(MASKS):
                px, py, _ = _other_chip(x, y, mask)
                cp = pltpu.make_async_remote_copy(_half(own[i], c), _half(out[i], c, (chip,)),
                                                  send_sems.at[i, k], recv_sems.at[i, k],
                                                  device_id=(px, py, c), device_id_type=MESH)
                cp.start()
                sends.append(cp)
        for i in range(n):
            for k, mask in enumerate(MASKS):
                px, py, pchip = _other_chip(x, y, mask)
                got = _half(out[i], c, (pchip,))
                pltpu.make_async_remote_copy(got, got, send_sems.at[i, k], recv_sems.at[i, k],
                                             device_id=(px, py, c), device_id_type=MESH).wait_recv()
                cp = pltpu.make_async_remote_copy(got, got, send_sems.at[i, 3 + k], recv_sems.at[i, 3 + k],
                                                  device_id=(x, y, 1 - c), device_id_type=MESH)
                cp.start()
                sends.append(cp)
        for i in range(n):
            for k, mask in enumerate(MASKS):
                _, _, pchip = _other_chip(x, y, mask)
                theirs = _half(out[i], 1 - c, (pchip,))
                pltpu.make_async_remote_copy(theirs, theirs, send_sems.at[i, 3 + k], recv_sems.at[i, 3 + k],
                                             device_id=(x, y, 1 - c), device_id_type=MESH).wait_recv()
        for cp in sends:
            cp.wait_send()

    return pl.pallas_call(
        body, name=name, in_specs=[ANY] * n, out_specs=[ANY] * n,
        out_shape=[jax.ShapeDtypeStruct((N_CHIPS, *o.shape), o.dtype) for o in owns],
        scratch_shapes=[pltpu.SemaphoreType.DMA((n, 6)), pltpu.SemaphoreType.DMA((n, 6))],
    )(*owns)


SIBLING_ID = 0


def _sibling_barrier(x, y, c):
    sem = pltpu.get_barrier_semaphore()
    pl.semaphore_signal(sem, inc=1, device_id=(x, y, 1 - c), device_id_type=MESH)
    pl.semaphore_wait(sem, 1)


def _swap_halves(gs, *, name):
    n = len(gs)

    def body(*refs):
        g, out = refs[:n], refs[n:2 * n]
        send_sems, recv_sems = refs[2 * n], refs[2 * n + 1]
        x, y, c, _ = _place()
        _sibling_barrier(x, y, c)
        cps = []
        for i in range(n):
            ah = g[i].shape[1] // 2
            cp = pltpu.make_async_remote_copy(g[i].at[:, pl.ds((1 - c) * ah, ah), :], out[i],
                                              send_sems.at[i], recv_sems.at[i],
                                              device_id=(x, y, 1 - c), device_id_type=MESH)
            cp.start()
            cps.append(cp)
        for cp in cps:
            cp.wait()

    return pl.pallas_call(
        body, name=name, in_specs=[ANY] * n, out_specs=[ANY] * n,
        out_shape=[jax.ShapeDtypeStruct((g.shape[0], g.shape[1] // 2, g.shape[2]), g.dtype) for g in gs],
        scratch_shapes=[pltpu.SemaphoreType.DMA((n,)), pltpu.SemaphoreType.DMA((n,))],
        compiler_params=pltpu.CompilerParams(collective_id=SIBLING_ID),
    )(*gs)


def _scatter_to_chips(ps, *, name):
    n = len(ps)

    def body(*refs):
        p, out = refs[:n], refs[n:2 * n]
        send_sems, recv_sems = refs[2 * n], refs[2 * n + 1]
        x, y, c, chip = _place()
        sends = []
        for i in range(n):
            for k, mask in enumerate(MASKS):
                px, py, pchip = _other_chip(x, y, mask)
                cp = pltpu.make_async_remote_copy(p[i].at[pchip], out[i].at[chip], send_sems.at[i, k],
                                                  recv_sems.at[i, k], device_id=(px, py, c), device_id_type=MESH)
                cp.start()
                sends.append(cp)
        for i in range(n):
            for k, mask in enumerate(MASKS):
                px, py, pchip = _other_chip(x, y, mask)
                pltpu.make_async_remote_copy(p[i].at[pchip], out[i].at[pchip], send_sems.at[i, k],
                                             recv_sems.at[i, k], device_id=(px, py, c),
                                             device_id_type=MESH).wait_recv()
        for cp in sends:
            cp.wait_send()

    return pl.pallas_call(
        body, name=name, in_specs=[ANY] * n, out_specs=[ANY] * n,
        out_shape=[jax.ShapeDtypeStruct(p.shape, p.dtype) for p in ps],
        scratch_shapes=[pltpu.SemaphoreType.DMA((n, 3)), pltpu.SemaphoreType.DMA((n, 3))],
    )(*ps)


HBM_SPEC = pl.BlockSpec(memory_space=pltpu.HBM)
SEM_SPEC = pl.BlockSpec(memory_space=pltpu.SEMAPHORE)
EFFECT = pltpu.SideEffectType.DATAFLOW_SIDE_EFFECTING


def _ici_ends(src, land, gather, x, y, c, chip, mask):
    px, py, pchip = _other_chip(x, y, mask)
    if gather:
        return _half(src, c), _half(land, c, (chip,)), _half(land, c, (pchip,)), (px, py, c)
    return src.at[pchip], land.at[chip], land.at[pchip], (px, py, c)


def _ici_start(groups, land_groups, gather, *, name, after=None):
    sizes = [len(g) for g in groups]
    n = sum(sizes)
    ng = len(groups)
    deps = [] if after is None else [after]

    def body(*refs):
        src, land = refs[:n], refs[n:2 * n]
        sems = refs[2 * n + len(deps):2 * n + len(deps) + 2 * ng]
        token = refs[-1]
        x, y, c, chip = _place()
        i = 0
        for g in range(ng):
            for j in range(sizes[g]):
                for k, mask in enumerate(MASKS):
                    s, d, _, peer = _ici_ends(src[i], land[i], gather, x, y, c, chip, mask)
                    pltpu.make_async_remote_copy(s, d, sems[2 * g].at[3 * j + k], sems[2 * g + 1].at[3 * j + k],
                                                 device_id=peer, device_id_type=MESH).start()
                i += 1
        token[...] = jnp.zeros_like(token)

    lands = [pltpu.with_memory_space_constraint(lax.empty(s.shape, s.dtype), pltpu.HBM)
             for g in land_groups for s in g]
    srcs = [pltpu.with_memory_space_constraint(s, pltpu.HBM) for g in groups for s in g]
    sem_shapes = [pltpu.SemaphoreType.DMA((3 * m,)) for m in sizes for _ in range(2)]
    out = pl.pallas_call(
        body, name=name,
        out_shape=(*sem_shapes, *[pltpu.HBM(s.shape, s.dtype) for s in srcs],
                   *[pltpu.HBM(s.shape, s.dtype) for s in lands], jax.ShapeDtypeStruct((8, 128), F32)),
        in_specs=[HBM_SPEC] * (2 * n) + [ANY] * len(deps),
        out_specs=(*[SEM_SPEC] * (2 * ng), *[HBM_SPEC] * (2 * n), pl.BlockSpec(memory_space=pltpu.VMEM)),
        input_output_aliases={i: 2 * ng + i for i in range(2 * n)},
        compiler_params=pltpu.CompilerParams(has_side_effects=EFFECT),
    )(*srcs, *lands, *deps)
    res, pos = [], 0
    for g in range(ng):
        res.append((out[2 * g], out[2 * g + 1], list(out[2 * ng + pos:2 * ng + pos + sizes[g]]),
                    list(out[2 * ng + n + pos:2 * ng + n + pos + sizes[g]])))
        pos += sizes[g]
    return res, out[-1]


def _ici_wait(send_sems, recv_sems, srcs, lands, gather, after, *, name):
    n = len(srcs)

    def body(*refs):
        src, land = refs[:n], refs[n:2 * n]
        send_sems, recv_sems = refs[2 * n], refs[2 * n + 1]
        x, y, c, chip = _place()
        for i in range(n):
            for k, mask in enumerate(MASKS):
                s, d, got, peer = _ici_ends(src[i], land[i], gather, x, y, c, chip, mask)
                pltpu.make_async_remote_copy(s, d, send_sems.at[3 * i + k], recv_sems.at[3 * i + k],
                                             device_id=peer, device_id_type=MESH).wait_send()
                pltpu.make_async_remote_copy(s, got, send_sems.at[3 * i + k], recv_sems.at[3 * i + k],
                                             device_id=peer, device_id_type=MESH).wait_recv()

    out = pl.pallas_call(
        body, name=name,
        out_shape=tuple(pltpu.HBM(s.shape, s.dtype) for s in (*srcs, *lands)),
        in_specs=[HBM_SPEC] * (2 * n) + [SEM_SPEC, SEM_SPEC, ANY],
        out_specs=tuple([HBM_SPEC] * (2 * n)),
        input_output_aliases={i: i for i in range(2 * n)},
        compiler_params=pltpu.CompilerParams(has_side_effects=EFFECT),
    )(*srcs, *lands, send_sems, recv_sems, after)
    return list(out[:n]), list(out[n:])


def _relay_halves(gs, *, name):
    n = len(gs)

    def body(*refs):
        out = refs[n:2 * n]
        send_sems, recv_sems = refs[2 * n], refs[2 * n + 1]
        x, y, c, _ = _place()
        _sibling_barrier(x, y, c)
        cps = []
        for i in range(n):
            for k, mask in enumerate(MASKS):
                _, _, pchip = _other_chip(x, y, mask)
                got = _half(out[i], c, (pchip,))
                cp = pltpu.make_async_remote_copy(got, got, send_sems.at[i, k], recv_sems.at[i, k],
                                                  device_id=(x, y, 1 - c), device_id_type=MESH)
                cp.start()
                cps.append(cp)
        for i in range(n):
            for k, mask in enumerate(MASKS):
                _, _, pchip = _other_chip(x, y, mask)
                theirs = _half(out[i], 1 - c, (pchip,))
                pltpu.make_async_remote_copy(theirs, theirs, send_sems.at[i, k], recv_sems.at[i, k],
                                             device_id=(x, y, 1 - c), device_id_type=MESH).wait_recv()
        for cp in cps:
            cp.wait_send()

    return pl.pallas_call(
        body, name=name, in_specs=[ANY] * n, out_specs=[ANY] * n,
        out_shape=[jax.ShapeDtypeStruct(g.shape, g.dtype) for g in gs],
        input_output_aliases={i: i for i in range(n)},
        scratch_shapes=[pltpu.SemaphoreType.DMA((n, 3)), pltpu.SemaphoreType.DMA((n, 3))],
        compiler_params=pltpu.CompilerParams(collective_id=SIBLING_ID),
    )(*gs)


def _share_halves(gs, *, name):
    n = len(gs)

    def body(*refs):
        out = refs[n:2 * n]
        send_sems, recv_sems = refs[2 * n], refs[2 * n + 1]
        x, y, c, _ = _place()
        _sibling_barrier(x, y, c)
        cps = []
        for i in range(n):
            cp = pltpu.make_async_remote_copy(_half(out[i], c), _half(out[i], c), send_sems.at[i], recv_sems.at[i],
                                              device_id=(x, y, 1 - c), device_id_type=MESH)
            cp.start()
            cps.append(cp)
        for i in range(n):
            theirs = _half(out[i], 1 - c)
            pltpu.make_async_remote_copy(theirs, theirs, send_sems.at[i], recv_sems.at[i],
                                         device_id=(x, y, 1 - c), device_id_type=MESH).wait_recv()
        for cp in cps:
            cp.wait_send()

    return pl.pallas_call(
        body, name=name, in_specs=[ANY] * n, out_specs=[ANY] * n,
        out_shape=[jax.ShapeDtypeStruct(g.shape, g.dtype) for g in gs],
        input_output_aliases={i: i for i in range(n)},
        scratch_shapes=[pltpu.SemaphoreType.DMA((n,)), pltpu.SemaphoreType.DMA((n,))],
        compiler_params=pltpu.CompilerParams(collective_id=SIBLING_ID),
    )(*gs)


def _gather_all(buf, *, name):
    r, L = buf.shape
    vmem = pl.BlockSpec(memory_space=pltpu.VMEM)
    masks = tuple(range(1, N_DEV))

    def body(buf_ref, out_ref, send_sems, recv_sems):
        x, y, c, _ = _place()
        me = 4 * x + 2 * y + c
        out_ref[me] = buf_ref[...]
        sends = []
        for k, mask in enumerate(masks):
            px = 1 - x if mask & 4 else x
            py = 1 - y if mask & 2 else y
            pc = 1 - c if mask & 1 else c
            cp = pltpu.make_async_remote_copy(buf_ref, out_ref.at[me], send_sems.at[k], recv_sems.at[k],
                                              device_id=(px, py, pc), device_id_type=MESH)
            cp.start()
            sends.append(cp)
        for k, mask in enumerate(masks):
            px = 1 - x if mask & 4 else x
            py = 1 - y if mask & 2 else y
            pc = 1 - c if mask & 1 else c
            pltpu.make_async_remote_copy(buf_ref, out_ref.at[4 * px + 2 * py + pc], send_sems.at[k],
                                         recv_sems.at[k], device_id=(px, py, pc), device_id_type=MESH).wait_recv()
        for cp in sends:
            cp.wait_send()

    return pl.pallas_call(
        body, name=name, in_specs=[vmem], out_specs=vmem,
        out_shape=jax.ShapeDtypeStruct((N_DEV, r, L), buf.dtype),
        scratch_shapes=[pltpu.SemaphoreType.DMA((N_DEV - 1,)), pltpu.SemaphoreType.DMA((N_DEV - 1,))],
    )(buf)


def _pack(arrs, lanes, row_mult=8):
    flat = jnp.concatenate([a.reshape(-1) for a in arrs])
    rows = -(-flat.shape[0] // lanes)
    rows = -(-rows // row_mult) * row_mult
    flat = jnp.pad(flat, (0, rows * lanes - flat.shape[0]))
    return flat.reshape(rows, lanes)


def _unpack(buf, shapes):
    flat = buf.reshape(-1)
    out, pos = [], 0
    for s in shapes:
        n = 1
        for d in s:
            n *= d
        out.append(flat[pos:pos + n].reshape(s))
        pos += n
    return out


def kernel(x, mem, norm_mix_g, w_in, conv_a_w, conv_b_w, conv_b_bias, ln_b_g, ln_b_b, w_out, norm_x_g, norm_mem_g, w_q, w_kv, w_xo, norm_ffn_g, w_up, w_down, final_g, loss_target, m_norm_mix_g, m_w_in, m_conv_a_w, m_conv_b_w, m_conv_b_bias, m_ln_b_g, m_ln_b_b, m_w_out, m_norm_x_g, m_norm_mem_g, m_w_q, m_w_kv, m_w_xo, m_norm_ffn_g, m_w_up, m_w_down, m_final_g, v_norm_mix_g, v_w_in, v_conv_a_w, v_conv_b_w, v_conv_b_bias, v_ln_b_g, v_ln_b_b, v_w_out, v_norm_x_g, v_norm_mem_g, v_w_q, v_w_kv, v_w_xo, v_norm_ffn_g, v_w_up, v_w_down, v_final_g):
    W = dict(norm_mix_g=norm_mix_g, w_in=w_in, conv_a_w=conv_a_w, conv_b_w=conv_b_w, conv_b_bias=conv_b_bias,
             ln_b_g=ln_b_g, ln_b_b=ln_b_b, w_out=w_out, norm_x_g=norm_x_g, norm_mem_g=norm_mem_g, w_q=w_q,
             w_kv=w_kv, w_xo=w_xo, norm_ffn_g=norm_ffn_g, w_up=w_up, w_down=w_down, final_g=final_g)
    MO = dict(norm_mix_g=m_norm_mix_g, w_in=m_w_in, conv_a_w=m_conv_a_w, conv_b_w=m_conv_b_w,
              conv_b_bias=m_conv_b_bias, ln_b_g=m_ln_b_g, ln_b_b=m_ln_b_b, w_out=m_w_out, norm_x_g=m_norm_x_g,
              norm_mem_g=m_norm_mem_g, w_q=m_w_q, w_kv=m_w_kv, w_xo=m_w_xo, norm_ffn_g=m_norm_ffn_g,
              w_up=m_w_up, w_down=m_w_down, final_g=m_final_g)
    VO = dict(norm_mix_g=v_norm_mix_g, w_in=v_w_in, conv_a_w=v_conv_a_w, conv_b_w=v_conv_b_w,
              conv_b_bias=v_conv_b_bias, ln_b_g=v_ln_b_g, ln_b_b=v_ln_b_b, w_out=v_w_out, norm_x_g=v_norm_x_g,
              norm_mem_g=v_norm_mem_g, w_q=v_w_q, w_kv=v_w_kv, w_xo=v_w_xo, norm_ffn_g=v_norm_ffn_g,
              w_up=v_w_up, w_down=v_w_down, final_g=v_final_g)
    names = list(W.keys())
    depth = norm_mix_g.shape[0]
    assert depth == 2, "the exchange splits the weights into one layer per core of a chip"
    c_idx = lax.axis_index("c")
    chip_idx = 2 * lax.axis_index("x") + lax.axis_index("y")

    xs = x[0]
    ms = mem[0]
    tgt = loss_target[0]
    S, D = xs.shape
    c_a = conv_a_w.shape[-1] * N_CHIPS
    c_loc = conv_a_w.shape[-1]

    meta = jnp.stack([c_idx, chip_idx]).astype(jnp.int32)
    shard_axis = dict(BIG)
    own = {(l, n): W[n][l].astype(BF16) for l in range(depth) for n, _ in BIG}

    conv_local = _pack([conv_a_w, conv_b_w], 128)
    conv_all = _gather_all(conv_local, name="gather_conv_weights")
    order = [(l, gi) for l in range(depth) for gi in range(len(FWD_GROUPS))]
    src_groups = [[own[(l, n)] for n in FWD_GROUPS[gi]] for l, gi in order]
    started, gather_token = _ici_start(
        src_groups, [[jax.ShapeDtypeStruct((N_CHIPS, *s.shape), s.dtype) for s in g] for g in src_groups], True,
        name="gather_weights_start", after=conv_all)
    started = dict(zip(order, started))
    Wb = [dict() for _ in range(depth)]

    def weights_ready(l, gi, after):
        send_sems, recv_sems, srcs, lands = started[(l, gi)]
        srcs, lands = _ici_wait(send_sems, recv_sems, srcs, lands, True, after, name=f"gather_weights_l{l}_g{gi}_wait")
        full = _relay_halves(lands, name=f"gather_weights_l{l}_g{gi}_relay")
        for n, g, o in zip(FWD_GROUPS[gi], full, srcs):
            g = lax.dynamic_update_slice(g, o[None], (chip_idx, 0, 0))
            Wb[l][n] = g.reshape(-1, g.shape[-1]) if shard_axis[n] == 0 else g
    na = depth * K_A * c_loc
    nbw = depth * K_B * c_loc
    ca_parts, cb_parts = [], []
    for j in range(N_CHIPS):
        fl = conv_all[2 * j].reshape(-1)
        ca_parts.append(fl[:na].reshape(depth, K_A, c_loc))
        cb_parts.append(fl[na:na + nbw].reshape(depth, K_B, c_loc))
    conv_a_full = jnp.concatenate(ca_parts, axis=-1)
    conv_b_full = jnp.concatenate(cb_parts, axis=-1)

    saved = []
    h = xs
    for l in range(depth):
        wl = Wb[l]
        t = f"l{l}_"
        if l == 0:
            u = _rms_fwd(h, norm_mix_g[l:l + 1], name=t + "rms_mix", after=gather_token)
        weights_ready(l, 0, u)
        z = _mm(u, wl["w_in"], b_stack=True, name=t + "mm_in", bm=2048)
        y_a = _mixer_a_fwd(z, conv_a_full[l], name=t + "mixer_a")
        cb = _mixer_b_fwd(z, conv_b_full[l], conv_b_bias[l:l + 1], name=t + "mixer_b")
        yy = _ln_silu_fwd(cb, ln_b_g[l:l + 1], ln_b_b[l:l + 1], y_a, name=t + "ln_silu")
        weights_ready(l, 1, yy)
        h2, q_in = _mm(yy, wl["w_out"], res=h, epi="rms_fwd", norm=norm_x_g[l:l + 1], name=t + "mm_out")
        q = _mm(q_in, wl["w_q"], out_dtype=BF16, name=t + "mm_q")
        mn = _rms_fwd(ms, norm_mem_g[l:l + 1], name=t + "rms_mem")
        kv = _mm(mn, wl["w_kv"], b_stack=True, out_dtype=BF16, name=t + "mm_kv")
        o = _attn_fwd(q, kv, name=t + "attn", tq=2 * ROW_TILE * (1 + l))
        h3, u3 = _mm(o, wl["w_xo"], res=h2, epi="rms_fwd", norm=norm_ffn_g[l:l + 1], name=t + "mm_xo")
        weights_ready(l, 2, h3)
        a_pre, hh = _mm(u3, wl["w_up"], b_stack=True, out_dtype=BF16, epi="sqrelu", name=t + "mm_up")
        saved.append(dict(h=h, u=u, z=z, cb=cb, yy=yy, h2=h2, q_in=q_in, q=q, mn=mn, kv=kv, o=o, h3=h3,
                          u3=u3, a_pre=a_pre, hh=hh))
        if l + 1 < depth:
            h, u = _mm(hh, wl["w_down"], res=h3, epi="rms_fwd", norm=norm_mix_g[l + 1:l + 2], name=t + "mm_down")
        else:
            h = _mm(hh, wl["w_down"], res=h3, name=t + "mm_down")

    loss_vec, dh, dhb, d_final = _loss_head(h, final_g.reshape(1, D), tgt, name="loss_head")

    GW = [dict() for _ in range(depth)]
    GS = [dict() for _ in range(depth)]
    pending = []

    def reduce_start(l, gi):
        group = BWD_GROUPS[gi]

        def by_chip(g, n):
            return g if g.ndim == 3 else g.reshape(N_CHIPS, *W[n].shape[1:])

        gs = [by_chip(GW[l][n][0], n) for n in group]
        from_sibling = _swap_halves([by_chip(GW[l][n][1], n) for n in group],
                                    name=f"grad_swap_sibling_l{l}_g{gi}")
        prs = [_pair_sum(g, r, meta, name=f"grad_pair_sum_l{l}_{n}") for g, r, n in zip(gs, from_sibling, group)]
        (st,), token = _ici_start([prs], [prs], False, name=f"grad_scatter_chips_l{l}_g{gi}_start")
        pending.append((l, group, st))
        return token

    for l in reversed(range(depth)):
        wl, sv = Wb[l], saved[l]
        t = f"l{l}_b_"
        GW[l]["w_down"] = _mm(sv["hh"], dhb, ta=True, epi="with_bf16", name=t + "dw_down")
        da = _mm(dhb, wl["w_down"], tb=True, out_dtype=BF16, epi="dsqrelu", aux=sv["a_pre"], name=t + "d_hidden")
        GW[l]["w_up"] = _mm(sv["u3"], da, ta=True, o_stack=N_CHIPS, epi="with_bf16", name=t + "dw_up")
        dh, dhb, GS[l]["norm_ffn_g"] = _mm_rms_bwd(da, wl["w_up"], sv["h3"], norm_ffn_g[l:l + 1], dh, b_stack=True,
                                                   after=reduce_start(l, 0), name=t + "d_u3")
        GW[l]["w_xo"] = _mm(sv["o"], dhb, ta=True, epi="with_bf16", name=t + "dw_xo")
        d_o = _mm(dhb, wl["w_xo"], tb=True, out_dtype=BF16, name=t + "d_o")
        dq, dkv = _attn_bwd(sv["q"], sv["kv"], d_o, name=t + "attn", tq=2 * ROW_TILE * (1 + l))
        GW[l]["w_q"] = _mm(sv["q_in"], dq, ta=True, epi="with_bf16", name=t + "dw_q")
        dkvb = dkv.astype(BF16)
        GW[l]["w_kv"] = _mm(sv["mn"], dkvb, ta=True, o_stack=N_CHIPS, epi="with_bf16", name=t + "dw_kv")
        dmn = _mm(dkvb, wl["w_kv"], tb=True, b_stack=True, name=t + "d_mem")
        _, _, GS[l]["norm_mem_g"] = _rms_bwd(ms, norm_mem_g[l:l + 1], dmn, None, name=t + "rms_mem")
        dh, dhb, GS[l]["norm_x_g"] = _mm_rms_bwd(dq, wl["w_q"], sv["h2"], norm_x_g[l:l + 1], dh,
                                                 after=reduce_start(l, 1), name=t + "d_q_in")
        GW[l]["w_out"] = _mm(sv["yy"], dhb, ta=True, epi="with_bf16", name=t + "dw_out")
        dyy = _mm(dhb, wl["w_out"], tb=True, name=t + "d_y")
        dcb, GS[l]["ln_b_g"], GS[l]["ln_b_b"] = _ln_silu_bwd(sv["cb"], ln_b_g[l:l + 1], ln_b_b[l:l + 1], dyy, 1,
                                                             name=t + "ln_silu")
        db_, dc_, dh_, GS[l]["conv_a_w"] = _mixer_a_bwd(sv["z"], conv_a_full[l], dyy, name=t + "mixer_a")
        dv_, dg_, GS[l]["conv_b_w"], GS[l]["conv_b_bias"] = _mixer_b_bwd(sv["z"], conv_b_full[l], dcb,
                                                                         name=t + "mixer_b")
        dz = jnp.concatenate([db_, dc_, dh_, dv_, dg_], axis=1)
        GW[l]["w_in"] = _mm(sv["u"], dz, ta=True, o_stack=N_CHIPS, epi="with_bf16", name=t + "dw_in")
        dh, dhb, GS[l]["norm_mix_g"] = _mm_rms_bwd(dz, wl["w_in"], sv["h"], norm_mix_g[l:l + 1], dh, b_stack=True,
                                                   after=reduce_start(l, 2), name=t + "d_u")
    grad_x = dh[None]

    after = GS[0]["norm_mix_g"]
    keys, halves = [], []
    for l, group, (send_sems, recv_sems, srcs, lands) in pending:
        prs, pieces = _ici_wait(send_sems, recv_sems, srcs, lands, False, after,
                                name=f"grad_scatter_chips_l{l}_{group[0]}_wait")
        for n, p, q in zip(group, prs, pieces):
            keys.append((l, n))
            halves.append(_chip_sum(p, q, meta, name=f"grad_chip_sum_l{l}_{n}"))
        after = halves[-1]
    reduced = dict(zip(keys, _share_halves(halves, name="grad_share_sibling")))

    grads, deltas, new_m, new_v = {}, {}, {}, {}
    for n, _ in BIG:
        grads[n], deltas[n], new_m[n], new_v[n] = _adamw_layers(W[n], reduced[(0, n)], reduced[(1, n)], MO[n], VO[n],
                                                               name="adamw_" + n)

    small = [n for n in names if n not in dict(BIG)]
    full_shapes = {n: ((depth, W[n].shape[1], c_a) if n in ("conv_a_w", "conv_b_w") else W[n].shape)
                   for n in small}

    def small_grad(n):
        if n == "final_g":
            return d_final.reshape(W[n].shape)
        return jnp.stack([GS[l][n].reshape(full_shapes[n][1:]) for l in range(depth)])

    part = _pack([small_grad(n) for n in small] + [loss_vec[0, :1]], LANES)
    everyone = _gather_all(part, name="gather_small_grads")
    total = _sum_leading(everyone, name="small_grad_sum")
    *small_sums, loss = _unpack(total, [full_shapes[n] for n in small] + [()])
    full_grads = dict(zip(small, small_sums))
    for n in ("conv_a_w", "conv_b_w"):
        full_grads[n] = lax.dynamic_slice_in_dim(full_grads[n], chip_idx * c_loc, c_loc, axis=2)
    shapes = [W[n].shape for n in small]
    d_s, m_s, v_s = _adamw(_pack([W[n] for n in small], 128), _pack([full_grads[n] for n in small], 128),
                           _pack([MO[n] for n in small], 128), _pack([VO[n] for n in small], 128),
                           name="adamw_small")
    for n, d, nm, nv in zip(small, _unpack(d_s, shapes), _unpack(m_s, shapes), _unpack(v_s, shapes)):
        grads[n], deltas[n], new_m[n], new_v[n] = full_grads[n], d, nm, nv

    return (loss, grad_x, *[grads[n] for n in names], *[deltas[n] for n in names],
            *[new_m[n] for n in names], *[new_v[n] for n in names])
```

```python
import jax
import jax.numpy as jnp
from jax import lax
from jax.experimental import pallas as pl
from jax.experimental.pallas import tpu as pltpu

F32 = jnp.float32
BF16 = jnp.bfloat16
MESH = pl.DeviceIdType.MESH

EPS = 1e-6
N_XHEADS = 4
K_A = 3
K_B = 31
PAD_A = 8
PAD_B = 32
CONV_CHUNK = 256
ROW_TILE = 512
ATTN_ROW_TILE = 1024
LANES = 1024
VMEM_LIMIT_BYTES = 56 * 1024 * 1024

ADAM_LR = 0.001
ADAM_B1 = 0.9
ADAM_B2 = 0.999
ADAM_EPS = 1e-08
ADAM_WD = 0.01
ADAM_STEP = 10

BIG = (("w_in", 1), ("w_out", 0), ("w_q", 0), ("w_kv", 1), ("w_xo", 0), ("w_up", 1), ("w_down", 0))
FWD_GROUPS = (("w_in",), ("w_out", "w_q", "w_kv", "w_xo"), ("w_up", "w_down"))
BWD_GROUPS = (("w_down", "w_up"), ("w_xo", "w_q", "w_kv"), ("w_out", "w_in"))
N_CHIPS = 4
N_DEV = 8


def _params(sem=None):
    return pltpu.CompilerParams(dimension_semantics=sem, vmem_limit_bytes=VMEM_LIMIT_BYTES)


def _pick(cands, n):
    for c in cands:
        if c <= n and n % c == 0:
            return c
    return n


def _mm(a, b, *, name, ta=False, tb=False, out_dtype=F32, res=None, epi=None, aux=None, norm=None, after=None,
        b_stack=False, o_stack=0, bm=1024, bn=1024, bk=1024):
    if ta:
        K, M = a.shape
    else:
        M, K = a.shape
    if b_stack:
        n_st, d1, d2 = b.shape
        N, kb = (d1, d2) if tb else (n_st * d2, d1)
        assert K == (n_st * d2 if tb else d1), (name, a.shape, b.shape)
    else:
        N = b.shape[0] if tb else b.shape[1]
    n_unit = b.shape[2] if (b_stack and not tb) else (N // o_stack if o_stack else N)
    k_unit = b.shape[2] if (b_stack and tb) else K
    bm = _pick((bm, 512, 256, 128), M)
    bn = _pick((bn, 512, 640, 256, 384, 128), n_unit)
    bk = _pick((bk, 640, 512, 256, 128), k_unit)
    assert M % bm == 0 and N % bn == 0 and K % bk == 0, (name, M, N, K)
    nk = K // bk
    per_n = n_unit // bn
    per_k = k_unit // bk
    a_spec = (pl.BlockSpec((bk, bm), lambda i, j, k: (k, i)) if ta
              else pl.BlockSpec((bm, bk), lambda i, j, k: (i, k)))
    if b_stack and tb:
        b_spec = pl.BlockSpec((None, bn, bk), lambda i, j, k: (k // per_k, j, k % per_k))
    elif b_stack:
        b_spec = pl.BlockSpec((None, bk, bn), lambda i, j, k: (j // per_n, k, j % per_n))
    elif tb:
        b_spec = pl.BlockSpec((bn, bk), lambda i, j, k: (j, k))
    else:
        b_spec = pl.BlockSpec((bk, bn), lambda i, j, k: (k, j))
    o_spec = pl.BlockSpec((bm, bn), lambda i, j, k: (i, j))
    dims = (((0 if ta else 1,), (1 if tb else 0,)), ((), ()))
    ins, in_specs = [a, b], [a_spec, b_spec]
    if res is not None:
        ins.append(res)
        in_specs.append(o_spec)
    if aux is not None:
        ins.append(aux)
        in_specs.append(o_spec)
    vec_spec = pl.BlockSpec((1, bn), lambda i, j, k: (0, j))
    if epi == "rms_fwd":
        assert bn == N, (name, bn, N)
        ins.append(norm)
        in_specs.append(vec_spec)
    elif epi == "rms_bwd":
        assert bn == N, (name, bn, N)
        ins += [norm[0], norm[1]]
        in_specs += [o_spec, vec_spec]
    n_norm = {"rms_fwd": 1, "rms_bwd": 2}.get(epi, 0)
    if after is not None:
        ins.append(after)
        in_specs.append(pl.BlockSpec(memory_space=pl.ANY))
    n_out = {"sqrelu": 2, "rms_fwd": 2, "rms_bwd": 3, "with_bf16": 2}.get(epi, 1)
    out_shape = [jax.ShapeDtypeStruct((M, N), out_dtype)] * n_out
    out_specs = [o_spec] * n_out
    if epi in ("rms_fwd", "with_bf16"):
        out_shape = [jax.ShapeDtypeStruct((M, N), F32), jax.ShapeDtypeStruct((M, N), BF16)]
    elif epi == "rms_bwd":
        out_shape = [jax.ShapeDtypeStruct((M, N), F32), jax.ShapeDtypeStruct((M, N), BF16),
                     jax.ShapeDtypeStruct((1, N), F32)]
        out_specs = [o_spec, o_spec, vec_spec]
    if o_stack:
        assert epi in (None, "with_bf16") and res is None and aux is None
        out_shape = [jax.ShapeDtypeStruct((o_stack, M, N // o_stack), s.dtype) for s in out_shape]
        out_specs = [pl.BlockSpec((None, bm, bn), lambda i, j, k: (j // per_n, i, j % per_n))] * n_out

    def body(*refs):
        a_ref, b_ref = refs[0], refs[1]
        pos = 2
        res_ref = aux_ref = None
        if res is not None:
            res_ref = refs[pos]
            pos += 1
        if aux is not None:
            aux_ref = refs[pos]
            pos += 1
        norm_refs = refs[pos:pos + n_norm]
        pos += n_norm + (after is not None)
        outs = refs[pos:pos + n_out]

        def product():
            return lax.dot_general(a_ref[...], b_ref[...], dims, preferred_element_type=F32)

        def finish(r):
            if epi == "rms_bwd":
                x_ref, g_ref = norm_refs
                xv = x_ref[...]
                rs = lax.rsqrt(jnp.mean(xv * xv, axis=-1, keepdims=True) + EPS)
                xh = xv * rs
                dxh = r * g_ref[...]
                dx = rs * (dxh - xh * jnp.mean(dxh * xh, axis=-1, keepdims=True))
                if res_ref is not None:
                    dx = dx + res_ref[...]
                outs[0][...] = dx
                outs[1][...] = dx.astype(BF16)

                @pl.when(pl.program_id(0) == 0)
                def _():
                    outs[2][...] = jnp.zeros_like(outs[2])

                outs[2][...] += jnp.sum(r * xh, axis=0, keepdims=True)
                return
            if res_ref is not None:
                r = r + res_ref[...]
            if epi == "rms_fwd":
                outs[0][...] = r
                rs = lax.rsqrt(jnp.mean(r * r, axis=-1, keepdims=True) + EPS)
                outs[1][...] = (r * rs * norm_refs[0][...]).astype(BF16)
            elif epi == "with_bf16":
                outs[0][...] = r
                outs[1][...] = r.astype(BF16)
            elif epi == "sqrelu":
                outs[0][...] = r.astype(out_dtype)
                rl = jnp.maximum(r, 0.0)
                outs[1][...] = (rl * rl).astype(out_dtype)
            elif epi == "dsqrelu":
                outs[0][...] = (r * (2.0 * jnp.maximum(aux_ref[...].astype(F32), 0.0))).astype(out_dtype)
            else:
                outs[0][...] = r.astype(out_dtype)

        if nk == 1:
            finish(product())
            return
        acc = refs[pos + n_out]
        k = pl.program_id(2)

        @pl.when(k == 0)
        def _():
            acc[...] = product()

        @pl.when(jnp.logical_and(k > 0, k < nk - 1))
        def _():
            acc[...] += product()

        @pl.when(k == nk - 1)
        def _():
            finish(acc[...] + product())

    out = pl.pallas_call(
        body, name=name, grid=(M // bm, N // bn, nk),
        in_specs=in_specs, out_specs=out_specs, out_shape=out_shape,
        scratch_shapes=[pltpu.VMEM((bm, bn), F32)] if nk > 1 else [],
        compiler_params=_params(("arbitrary",) * 3 if epi == "rms_bwd" else ("parallel", "parallel", "arbitrary")),
    )(*ins)
    return out if n_out > 1 else out[0]


def _row_tile(rows):
    return min(ROW_TILE, rows)


def _rms_fwd(x, g, *, name, after=None):
    S, D = x.shape
    tr = _row_tile(S)

    def body(x_ref, g_ref, *rest):
        o_ref = rest[-1]
        xv = x_ref[...]
        r = lax.rsqrt(jnp.mean(xv * xv, axis=-1, keepdims=True) + EPS)
        o_ref[...] = (xv * r * g_ref[...]).astype(BF16)

    deps = [] if after is None else [after]
    return pl.pallas_call(
        body, name=name, grid=(S // tr,),
        in_specs=[pl.BlockSpec((tr, D), lambda i: (i, 0)), pl.BlockSpec((1, D), lambda i: (0, 0))]
        + [ANY] * len(deps),
        out_specs=pl.BlockSpec((tr, D), lambda i: (i, 0)),
        out_shape=jax.ShapeDtypeStruct((S, D), BF16),
        compiler_params=_params(("parallel",)),
    )(x, g, *deps)


def _rms_bwd(x, g, du, dres, *, name, after=None):
    S, D = x.shape
    tr = _row_tile(S)
    has_res = dres is not None
    deps = [] if after is None else [after]

    def body(*refs):
        dx_ref, dxb_ref, dg_ref = refs[-3:]
        if has_res:
            x_ref, g_ref, du_ref, dres_ref = refs[:4]
        else:
            x_ref, g_ref, du_ref = refs[:3]
        xv = x_ref[...]
        r = lax.rsqrt(jnp.mean(xv * xv, axis=-1, keepdims=True) + EPS)
        xh = xv * r
        dy = du_ref[...]
        dxh = dy * g_ref[...]
        dx = r * (dxh - xh * jnp.mean(dxh * xh, axis=-1, keepdims=True))
        if has_res:
            dx = dx + dres_ref[...]
        dx_ref[...] = dx
        dxb_ref[...] = dx.astype(BF16)

        @pl.when(pl.program_id(0) == 0)
        def _():
            dg_ref[...] = jnp.zeros_like(dg_ref)

        dg_ref[...] += jnp.sum(dy * xh, axis=0, keepdims=True)

    row = pl.BlockSpec((tr, D), lambda i: (i, 0))
    vec = pl.BlockSpec((1, D), lambda i: (0, 0))
    ins = [x, g, du] + ([dres] if has_res else []) + deps
    in_specs = [row, vec, row] + ([row] if has_res else []) + [ANY] * len(deps)
    return pl.pallas_call(
        body, name=name, grid=(S // tr,),
        in_specs=in_specs, out_specs=[row, row, vec],
        out_shape=[jax.ShapeDtypeStruct((S, D), F32), jax.ShapeDtypeStruct((S, D), BF16),
                   jax.ShapeDtypeStruct((1, D), F32)],
        compiler_params=_params(("arbitrary",)),
    )(*ins)


def _loss_head(h, g, target, *, name):
    S, D = h.shape
    tr = _row_tile(S)

    def body(x_ref, g_ref, t_ref, loss_ref, dx_ref, dxb_ref, dg_ref):
        xv = x_ref[...]
        r = lax.rsqrt(jnp.mean(xv * xv, axis=-1, keepdims=True) + EPS)
        xh = xv * r
        gv = g_ref[...]
        err = xh * gv - t_ref[...]
        part = 0.5 * jnp.sum(jnp.mean(err * err, axis=-1, keepdims=True), axis=0, keepdims=True)
        dy = err * (1.0 / D)
        dxh = dy * gv
        dx = r * (dxh - xh * jnp.mean(dxh * xh, axis=-1, keepdims=True))
        dx_ref[...] = dx
        dxb_ref[...] = dx.astype(BF16)

        @pl.when(pl.program_id(0) == 0)
        def _():
            dg_ref[...] = jnp.zeros_like(dg_ref)
            loss_ref[...] = jnp.zeros_like(loss_ref)

        dg_ref[...] += jnp.sum(dy * xh, axis=0, keepdims=True)
        loss_ref[...] += jnp.broadcast_to(part, loss_ref.shape)

    row = pl.BlockSpec((tr, D), lambda i: (i, 0))
    vec = pl.BlockSpec((1, D), lambda i: (0, 0))
    return pl.pallas_call(
        body, name=name, grid=(S // tr,),
        in_specs=[row, vec, row],
        out_specs=[pl.BlockSpec((1, 128), lambda i: (0, 0)), row, row, vec],
        out_shape=[jax.ShapeDtypeStruct((1, 128), F32), jax.ShapeDtypeStruct((S, D), F32),
                   jax.ShapeDtypeStruct((S, D), BF16), jax.ShapeDtypeStruct((1, D), F32)],
        compiler_params=_params(("arbitrary",)),
    )(h, g, target)


def _sigmoid(x):
    return 1.0 / (1.0 + jnp.exp(-x))


def _ln_silu_fwd(cb, g, b, into, *, name):
    S, C = cb.shape
    tr = _row_tile(S)

    def body(x_ref, g_ref, b_ref, into_ref, o_ref):
        xv = x_ref[...]
        mu = jnp.mean(xv, axis=-1, keepdims=True)
        xc = xv - mu
        rs = lax.rsqrt(jnp.mean(xc * xc, axis=-1, keepdims=True) + EPS)
        l = xc * rs * g_ref[...] + b_ref[...]
        o_ref[...] = (l * _sigmoid(l)).astype(BF16)

    row = pl.BlockSpec((tr, C), lambda i: (i, 0))
    vec = pl.BlockSpec((1, C), lambda i: (0, 0))
    return pl.pallas_call(
        body, name=name, grid=(S // tr,), in_specs=[row, vec, vec, pl.BlockSpec(memory_space=pl.ANY)],
        out_specs=pl.BlockSpec((tr, C), lambda i: (i, 1)),
        out_shape=jax.ShapeDtypeStruct(into.shape, BF16), input_output_aliases={3: 0},
        compiler_params=_params(("parallel",)),
    )(cb, g, b, into)


def _ln_silu_bwd(cb, g, b, dy, col_block, *, name):
    S, C = cb.shape
    tr = _row_tile(S)

    def body(x_ref, g_ref, b_ref, dy_ref, dx_ref, dg_ref, db_ref):
        xv = x_ref[...]
        mu = jnp.mean(xv, axis=-1, keepdims=True)
        xc = xv - mu
        rs = lax.rsqrt(jnp.mean(xc * xc, axis=-1, keepdims=True) + EPS)
        xh = xc * rs
        gv = g_ref[...]
        l = xh * gv + b_ref[...]
        sg = _sigmoid(l)
        dl = dy_ref[...] * (sg + l * sg * (1.0 - sg))
        dxh = dl * gv
        dx_ref[...] = rs * (dxh - jnp.mean(dxh, axis=-1, keepdims=True)
                            - xh * jnp.mean(dxh * xh, axis=-1, keepdims=True))

        @pl.when(pl.program_id(0) == 0)
        def _():
            dg_ref[...] = jnp.zeros_like(dg_ref)
            db_ref[...] = jnp.zeros_like(db_ref)

        dg_ref[...] += jnp.sum(dl * xh, axis=0, keepdims=True)
        db_ref[...] += jnp.sum(dl, axis=0, keepdims=True)

    row = pl.BlockSpec((tr, C), lambda i: (i, 0))
    vec = pl.BlockSpec((1, C), lambda i: (0, 0))
    return pl.pallas_call(
        body, name=name, grid=(S // tr,),
        in_specs=[row, vec, vec, pl.BlockSpec((tr, C), lambda i: (i, col_block))],
        out_specs=[row, vec, vec],
        out_shape=[jax.ShapeDtypeStruct((S, C), F32), jax.ShapeDtypeStruct((1, C), F32),
                   jax.ShapeDtypeStruct((1, C), F32)],
        compiler_params=_params(("arbitrary",)),
    )(cb, g, b, dy)


def _attn_fwd(q, kv, *, name, tq=ROW_TILE):
    S, D = q.shape
    M = kv.shape[0]
    hd = D // N_XHEADS
    scale = 1.0 / float(hd) ** 0.5
    tq = min(tq, S)

    def body(q_ref, k_ref, v_ref, o_ref):
        for h in range(N_XHEADS):
            cols = slice(h * hd, (h + 1) * hd)
            s = lax.dot_general(q_ref[:, cols], k_ref[:, cols], (((1,), (1,)), ((), ())),
                                preferred_element_type=F32) * scale
            e = jnp.exp(s - jnp.max(s, axis=-1, keepdims=True))
            p = e / jnp.sum(e, axis=-1, keepdims=True)
            o = jnp.dot(p.astype(BF16), v_ref[:, cols], preferred_element_type=F32)
            o_ref[:, cols] = o.astype(BF16)

    return pl.pallas_call(
        body, name=name, grid=(S // tq,),
        in_specs=[pl.BlockSpec((tq, D), lambda i: (i, 0)), pl.BlockSpec((M, D), lambda i: (0, 0)),
                  pl.BlockSpec((M, D), lambda i: (0, 1))],
        out_specs=pl.BlockSpec((tq, D), lambda i: (i, 0)),
        out_shape=jax.ShapeDtypeStruct((S, D), BF16), compiler_params=_params(("parallel",)),
    )(q, kv, kv)


def _attn_bwd(q, kv, do, *, name, tq=ROW_TILE):
    S, D = q.shape
    M = kv.shape[0]
    hd = D // N_XHEADS
    scale = 1.0 / float(hd) ** 0.5
    tq = min(tq, S)

    def body(q_ref, k_ref, v_ref, do_ref, dq_ref, dkv_ref):
        @pl.when(pl.program_id(0) == 0)
        def _():
            dkv_ref[...] = jnp.zeros_like(dkv_ref)

        for h in range(N_XHEADS):
            cols = slice(h * hd, (h + 1) * hd)
            vcols = slice(D + h * hd, D + (h + 1) * hd)
            qh, kh, vh, doh = q_ref[:, cols], k_ref[:, cols], v_ref[:, cols], do_ref[:, cols]
            s = lax.dot_general(qh, kh, (((1,), (1,)), ((), ())), preferred_element_type=F32) * scale
            e = jnp.exp(s - jnp.max(s, axis=-1, keepdims=True))
            p = e / jnp.sum(e, axis=-1, keepdims=True)
            pb = p.astype(BF16)
            dp = lax.dot_general(doh, vh, (((1,), (1,)), ((), ())), preferred_element_type=F32)
            ds = (p * (dp - jnp.sum(dp * p, axis=-1, keepdims=True)) * scale).astype(BF16)
            dq_ref[:, cols] = jnp.dot(ds, kh, preferred_element_type=F32).astype(BF16)
            dkv_ref[:, cols] += lax.dot_general(ds, qh, (((0,), (0,)), ((), ())), preferred_element_type=F32)
            dkv_ref[:, vcols] += lax.dot_general(pb, doh, (((0,), (0,)), ((), ())), preferred_element_type=F32)

    row = pl.BlockSpec((tq, D), lambda i: (i, 0))
    return pl.pallas_call(
        body, name=name, grid=(S // tq,),
        in_specs=[row, pl.BlockSpec((M, D), lambda i: (0, 0)), pl.BlockSpec((M, D), lambda i: (0, 1)), row],
        out_specs=[row, pl.BlockSpec((M, 2 * D), lambda i: (0, 0))],
        out_shape=[jax.ShapeDtypeStruct((S, D), BF16), jax.ShapeDtypeStruct((M, 2 * D), F32)],
        compiler_params=_params(("arbitrary",)),
    )(q, kv, kv, do)


def _delayed(win, j, pad):
    return (win if j == 0 else pltpu.roll(win, j, 0))[pad:, :]


def _advanced(win, j, ch):
    return (win if j == 0 else pltpu.roll(win, win.shape[0] - j, 0))[:ch, :]


def _mixer_a_fwd(z, w, *, name):
    S = z.shape[0]
    C = w.shape[1]
    nb = C // 128
    ch = min(CONV_CHUNK, S)

    def body(b_ref, c_ref, h_ref, w_ref, y_ref, xp):
        xp[0:PAD_A, :] = jnp.zeros((PAD_A, 128), F32)
        xp[PAD_A:, :] = c_ref[...] * h_ref[...]

        def chunk(i, carry):
            base = pl.multiple_of(i * ch, ch)
            win = xp[pl.ds(base, ch + PAD_A), :]
            acc = _delayed(win, 0, PAD_A) * w_ref[K_A - 1:K_A, :]
            for j in range(1, K_A):
                acc = acc + _delayed(win, j, PAD_A) * w_ref[K_A - 1 - j:K_A - j, :]
            y_ref[pl.ds(base, ch), :] = (b_ref[pl.ds(base, ch), :] * acc).astype(BF16)
            return carry

        lax.fori_loop(0, S // ch, chunk, 0)

    def col(g):
        return pl.BlockSpec((S, 128), lambda j: (0, g * nb + j))

    return pl.pallas_call(
        body, name=name, grid=(nb,),
        in_specs=[col(0), col(1), col(2), pl.BlockSpec((K_A, 128), lambda j: (0, j))],
        out_specs=pl.BlockSpec((S, 128), lambda j: (0, j)),
        out_shape=jax.ShapeDtypeStruct((S, 2 * C), BF16),
        scratch_shapes=[pltpu.VMEM((PAD_A + S, 128), F32)],
        compiler_params=_params(("parallel",)),
    )(z, z, z, w)


def _mixer_a_bwd(z, w, dy, *, name):
    S = z.shape[0]
    C = w.shape[1]
    nb = C // 128
    ch = min(CONV_CHUNK, S)

    def body(b_ref, c_ref, h_ref, w_ref, dy_ref, db_ref, dc_ref, dh_ref, dw_ref, xp, dp):
        xp[0:PAD_A, :] = jnp.zeros((PAD_A, 128), F32)
        xp[PAD_A:, :] = c_ref[...] * h_ref[...]
        dp[S:, :] = jnp.zeros((PAD_A, 128), F32)
        dw_ref[...] = jnp.zeros_like(dw_ref)

        def chunk(i, carry):
            base = pl.multiple_of(i * ch, ch)
            win = xp[pl.ds(base, ch + PAD_A), :]
            dya = dy_ref[pl.ds(base, ch), :]
            dcv = dya * b_ref[pl.ds(base, ch), :]
            dp[pl.ds(base, ch), :] = dcv
            acc = None
            for j in range(K_A):
                xs = _delayed(win, j, PAD_A)
                k = K_A - 1 - j
                term = xs * w_ref[k:k + 1, :]
                acc = term if acc is None else acc + term
                dw_ref[k:k + 1, :] += jnp.sum(dcv * xs, axis=0, keepdims=True)
            db_ref[pl.ds(base, ch), :] = (dya * acc).astype(BF16)
            return carry

        lax.fori_loop(0, S // ch, chunk, 0)

        def chunk2(i, carry):
            base = pl.multiple_of(i * ch, ch)
            win = dp[pl.ds(base, ch + PAD_A), :]
            acc = None
            for j in range(K_A):
                term = _advanced(win, j, ch) * w_ref[K_A - 1 - j:K_A - j, :]
                acc = term if acc is None else acc + term
            dc_ref[pl.ds(base, ch), :] = (acc * h_ref[pl.ds(base, ch), :]).astype(BF16)
            dh_ref[pl.ds(base, ch), :] = (acc * c_ref[pl.ds(base, ch), :]).astype(BF16)
            return carry

        lax.fori_loop(0, S // ch, chunk2, 0)

    def col(g):
        return pl.BlockSpec((S, 128), lambda j: (0, g * nb + j))

    out_col = pl.BlockSpec((S, 128), lambda j: (0, j))
    wspec = pl.BlockSpec((K_A, 128), lambda j: (0, j))
    return pl.pallas_call(
        body, name=name, grid=(nb,),
        in_specs=[col(0), col(1), col(2), wspec, out_col],
        out_specs=[out_col, out_col, out_col, wspec],
        out_shape=[jax.ShapeDtypeStruct((S, C), BF16)] * 3 + [jax.ShapeDtypeStruct((K_A, C), F32)],
        scratch_shapes=[pltpu.VMEM((PAD_A + S, 128), F32), pltpu.VMEM((S + PAD_A, 128), F32)],
        compiler_params=_params(("parallel",)),
    )(z, z, z, w, dy)


def _mixer_b_fwd(z, w, bias, *, name):
    S = z.shape[0]
    C = w.shape[1]
    nb = C // 128
    ch = min(CONV_CHUNK, S)

    def body(v_ref, g_ref, w_ref, bias_ref, cb_ref, xp):
        xp[0:PAD_B, :] = jnp.zeros((PAD_B, 128), F32)
        xp[PAD_B:, :] = v_ref[...] * _sigmoid(g_ref[...])

        def chunk(i, carry):
            base = pl.multiple_of(i * ch, ch)
            win = xp[pl.ds(base, ch + PAD_B), :]
            acc = None
            for j in range(K_B):
                term = _delayed(win, j, PAD_B) * w_ref[K_B - 1 - j:K_B - j, :]
                acc = term if acc is None else acc + term
            cb_ref[pl.ds(base, ch), :] = acc + bias_ref[...]
            return carry

        lax.fori_loop(0, S // ch, chunk, 0)

    def col(g):
        return pl.BlockSpec((S, 128), lambda j: (0, g * nb + j))

    return pl.pallas_call(
        body, name=name, grid=(nb,),
        in_specs=[col(3), col(4), pl.BlockSpec((K_B, 128), lambda j: (0, j)),
                  pl.BlockSpec((1, 128), lambda j: (0, j))],
        out_specs=pl.BlockSpec((S, 128), lambda j: (0, j)),
        out_shape=jax.ShapeDtypeStruct((S, C), F32),
        scratch_shapes=[pltpu.VMEM((PAD_B + S, 128), F32)],
        compiler_params=_params(("parallel",)),
    )(z, z, w, bias)


def _mixer_b_bwd(z, w, dcb, *, name):
    S = z.shape[0]
    C = w.shape[1]
    nb = C // 128
    ch = min(CONV_CHUNK, S)

    def body(v_ref, g_ref, w_ref, dcb_ref, dv_ref, dg_ref, dw_ref, dbias_ref, xp, dp):
        xp[0:PAD_B, :] = jnp.zeros((PAD_B, 128), F32)
        xp[PAD_B:, :] = v_ref[...] * _sigmoid(g_ref[...])
        dp[0:S, :] = dcb_ref[...]
        dp[S:, :] = jnp.zeros((PAD_B, 128), F32)
        dw_ref[...] = jnp.zeros_like(dw_ref)
        dbias_ref[...] = jnp.sum(dcb_ref[...], axis=0, keepdims=True)

        def chunk(i, carry):
            base = pl.multiple_of(i * ch, ch)
            win = xp[pl.ds(base, ch + PAD_B), :]
            d = dcb_ref[pl.ds(base, ch), :]
            for j in range(K_B):
                k = K_B - 1 - j
                dw_ref[k:k + 1, :] += jnp.sum(d * _delayed(win, j, PAD_B), axis=0, keepdims=True)
            return carry

        lax.fori_loop(0, S // ch, chunk, 0)

        def chunk2(i, carry):
            base = pl.multiple_of(i * ch, ch)
            win = dp[pl.ds(base, ch + PAD_B), :]
            acc = None
            for j in range(K_B):
                term = _advanced(win, j, ch) * w_ref[K_B - 1 - j:K_B - j, :]
                acc = term if acc is None else acc + term
            sg = _sigmoid(g_ref[pl.ds(base, ch), :])
            vv = v_ref[pl.ds(base, ch), :]
            dv_ref[pl.ds(base, ch), :] = (acc * sg).astype(BF16)
            dg_ref[pl.ds(base, ch), :] = (acc * vv * sg * (1.0 - sg)).astype(BF16)
            return carry

        lax.fori_loop(0, S // ch, chunk2, 0)

    def col(g):
        return pl.BlockSpec((S, 128), lambda j: (0, g * nb + j))

    out_col = pl.BlockSpec((S, 128), lambda j: (0, j))
    wspec = pl.BlockSpec((K_B, 128), lambda j: (0, j))
    bspec = pl.BlockSpec((1, 128), lambda j: (0, j))
    return pl.pallas_call(
        body, name=name, grid=(nb,),
        in_specs=[col(3), col(4), wspec, out_col],
        out_specs=[out_col, out_col, wspec, bspec],
        out_shape=[jax.ShapeDtypeStruct((S, C), BF16)] * 2
        + [jax.ShapeDtypeStruct((K_B, C), F32), jax.ShapeDtypeStruct((1, C), F32)],
        scratch_shapes=[pltpu.VMEM((PAD_B + S, 128), F32), pltpu.VMEM((S + PAD_B, 128), F32)],
        compiler_params=_params(("parallel",)),
    )(z, z, w, dcb)


def _ew_tile(R):
    for t in (512, 256, 128, 64, 32, 16, 8):
        if R % t == 0:
            return t
    return R


def _pair_sum(g, r, meta, *, name):
    n, a, b = g.shape
    ah = a // 2
    tr = _pick((256, 128, 64, 32, 16), ah)
    nh = ah // tr

    def body(meta_ref, g_ref, r_ref, o_ref):
        o_ref[...] = (g_ref[...] + r_ref[...].astype(F32)).astype(BF16)

    half = pl.BlockSpec((None, tr, b), lambda j, i, meta_ref: (j, i, 0))
    return pl.pallas_call(
        body, name=name,
        grid_spec=pltpu.PrefetchScalarGridSpec(
            num_scalar_prefetch=1, grid=(n, nh),
            in_specs=[pl.BlockSpec((None, tr, b), lambda j, i, meta_ref: (j, meta_ref[0] * nh + i, 0)), half],
            out_specs=half),
        out_shape=jax.ShapeDtypeStruct((n, ah, b), BF16), compiler_params=_params(("parallel", "parallel")),
    )(meta, g, r)


def _chip_sum(p, q, meta, *, name):
    n, ah, b = p.shape
    tr = _pick((256, 128, 64, 32, 16), ah)
    nh = ah // tr

    def body(meta_ref, p_ref, q1_ref, q2_ref, q3_ref, o_ref):
        o_ref[...] = ((p_ref[...].astype(F32) + q1_ref[...].astype(F32)) + q2_ref[...].astype(F32)
                      ) + q3_ref[...].astype(F32)

    def piece(mask):
        return pl.BlockSpec((None, tr, b), lambda i, meta_ref: (meta_ref[1] ^ mask, i, 0))

    return pl.pallas_call(
        body, name=name,
        grid_spec=pltpu.PrefetchScalarGridSpec(
            num_scalar_prefetch=1, grid=(nh,),
            in_specs=[piece(0), piece(1), piece(2), piece(3)],
            out_specs=pl.BlockSpec((tr, b), lambda i, meta_ref: (meta_ref[0] * nh + i, 0))),
        out_shape=jax.ShapeDtypeStruct((2 * ah, b), F32), compiler_params=_params(("parallel",)),
    )(meta, p, q, q, q)


def _sum_leading(x, *, name):
    n, R, C = x.shape
    tr = _ew_tile(R)

    def body(x_ref, o_ref):
        acc = x_ref[0].astype(F32)
        for k in range(1, n):
            acc = acc + x_ref[k].astype(F32)
        o_ref[...] = acc

    return pl.pallas_call(
        body, name=name, grid=(R // tr,),
        in_specs=[pl.BlockSpec((n, tr, C), lambda i: (0, i, 0))],
        out_specs=pl.BlockSpec((tr, C), lambda i: (i, 0)),
        out_shape=jax.ShapeDtypeStruct((R, C), F32), compiler_params=_params(("parallel",)),
    )(x)


def _adamw(w, g, m, v, *, name):
    R, C = w.shape
    tr = _ew_tile(R)

    def body(w_ref, g_ref, m_ref, v_ref, d_ref, nm_ref, nv_ref):
        gv = g_ref[...]
        nm = ADAM_B1 * m_ref[...] + (1.0 - ADAM_B1) * gv
        nv = ADAM_B2 * v_ref[...] + (1.0 - ADAM_B2) * (gv * gv)
        m_hat = nm / (1.0 - ADAM_B1 ** ADAM_STEP)
        v_hat = nv / (1.0 - ADAM_B2 ** ADAM_STEP)
        d_ref[...] = -ADAM_LR * (m_hat / (jnp.sqrt(v_hat) + ADAM_EPS) + ADAM_WD * w_ref[...])
        nm_ref[...] = nm
        nv_ref[...] = nv

    row = pl.BlockSpec((tr, C), lambda i: (i, 0))
    return pl.pallas_call(
        body, name=name, grid=(R // tr,), in_specs=[row] * 4, out_specs=[row] * 3,
        out_shape=[jax.ShapeDtypeStruct((R, C), F32)] * 3, compiler_params=_params(("parallel",)),
    )(w, g, m, v)


def _adamw_layers(w, g0, g1, m, v, *, name):
    _, a, b = w.shape
    tr = _pick((512, 256, 128, 64, 32, 16, 8), a)

    def body(w_ref, g0_ref, g1_ref, m_ref, v_ref, g_ref, d_ref, nm_ref, nv_ref):
        gv = jnp.where(pl.program_id(1) == 0, g0_ref[...], g1_ref[...])
        nm = ADAM_B1 * m_ref[...] + (1.0 - ADAM_B1) * gv
        nv = ADAM_B2 * v_ref[...] + (1.0 - ADAM_B2) * (gv * gv)
        m_hat = nm / (1.0 - ADAM_B1 ** ADAM_STEP)
        v_hat = nv / (1.0 - ADAM_B2 ** ADAM_STEP)
        g_ref[...] = gv
        d_ref[...] = -ADAM_LR * (m_hat / (jnp.sqrt(v_hat) + ADAM_EPS) + ADAM_WD * w_ref[...])
        nm_ref[...] = nm
        nv_ref[...] = nv

    lay = pl.BlockSpec((None, tr, b), lambda i, l: (l, i, 0))
    row = pl.BlockSpec((tr, b), lambda i, l: (i, 0))
    return pl.pallas_call(
        body, name=name, grid=(a // tr, 2), in_specs=[lay, row, row, lay, lay], out_specs=[lay] * 4,
        out_shape=[jax.ShapeDtypeStruct(w.shape, F32)] * 4, compiler_params=_params(("parallel", "arbitrary")),
    )(w, g0, g1, m, v)


ANY = pl.BlockSpec(memory_space=pl.ANY)


def _place():
    x, y, c = lax.axis_index("x"), lax.axis_index("y"), lax.axis_index("c")
    return x, y, c, 2 * x + y


def _other_chip(x, y, mask):
    px = 1 - x if mask & 2 else x
    py = 1 - y if mask & 1 else y
    return px, py, 2 * px + py


MASKS = (1, 2, 3)


def _half(ref, c, lead=()):
    ah = ref.shape[-2] // 2
    return ref.at[(*lead, pl.ds(c * ah, ah), slice(None))]


SIBLING_ID = 0


def _sibling_barrier(x, y, c):
    sem = pltpu.get_barrier_semaphore()
    pl.semaphore_signal(sem, inc=1, device_id=(x, y, 1 - c), device_id_type=MESH)
    pl.semaphore_wait(sem, 1)


def _swap_halves(gs, *, name):
    n = len(gs)

    def body(*refs):
        g, out = refs[:n], refs[n:2 * n]
        send_sems, recv_sems = refs[2 * n], refs[2 * n + 1]
        x, y, c, _ = _place()
        _sibling_barrier(x, y, c)
        cps = []
        for i in range(n):
            ah = g[i].shape[1] // 2
            cp = pltpu.make_async_remote_copy(g[i].at[:, pl.ds((1 - c) * ah, ah), :], out[i],
                                              send_sems.at[i], recv_sems.at[i],
                                              device_id=(x, y, 1 - c), device_id_type=MESH)
            cp.start()
            cps.append(cp)
        for cp in cps:
            cp.wait()

    return pl.pallas_call(
        body, name=name, in_specs=[ANY] * n, out_specs=[ANY] * n,
        out_shape=[jax.ShapeDtypeStruct((g.shape[0], g.shape[1] // 2, g.shape[2]), g.dtype) for g in gs],
        scratch_shapes=[pltpu.SemaphoreType.DMA((n,)), pltpu.SemaphoreType.DMA((n,))],
        compiler_params=pltpu.CompilerParams(collective_id=SIBLING_ID),
    )(*gs)


HBM_SPEC = pl.BlockSpec(memory_space=pltpu.HBM)
SEM_SPEC = pl.BlockSpec(memory_space=pltpu.SEMAPHORE)
EFFECT = pltpu.SideEffectType.DATAFLOW_SIDE_EFFECTING


def _ici_ends(src, land, gather, x, y, c, chip, mask):
    px, py, pchip = _other_chip(x, y, mask)
    if gather:
        return _half(src, c), _half(land, c, (chip,)), _half(land, c, (pchip,)), (px, py, c)
    return src.at[pchip], land.at[chip], land.at[pchip], (px, py, c)


def _ici_start(groups, land_groups, gather, *, name, after=None):
    sizes = [len(g) for g in groups]
    n = sum(sizes)
    ng = len(groups)
    deps = [] if after is None else [after]

    def body(*refs):
        src, land = refs[:n], refs[n:2 * n]
        sems = refs[2 * n + len(deps):2 * n + len(deps) + 2 * ng]
        token = refs[-1]
        x, y, c, chip = _place()
        i = 0
        for g in range(ng):
            for j in range(sizes[g]):
                for k, mask in enumerate(MASKS):
                    s, d, _, peer = _ici_ends(src[i], land[i], gather, x, y, c, chip, mask)
                    pltpu.make_async_remote_copy(s, d, sems[2 * g].at[3 * j + k], sems[2 * g + 1].at[3 * j + k],
                                                 device_id=peer, device_id_type=MESH).start()
                i += 1
        token[...] = jnp.zeros_like(token)

    lands = [pltpu.with_memory_space_constraint(lax.empty(s.shape, s.dtype), pltpu.HBM)
             for g in land_groups for s in g]
    srcs = [pltpu.with_memory_space_constraint(s, pltpu.HBM) for g in groups for s in g]
    sem_shapes = [pltpu.SemaphoreType.DMA((3 * m,)) for m in sizes for _ in range(2)]
    out = pl.pallas_call(
        body, name=name,
        out_shape=(*sem_shapes, *[pltpu.HBM(s.shape, s.dtype) for s in srcs],
                   *[pltpu.HBM(s.shape, s.dtype) for s in lands], jax.ShapeDtypeStruct((8, 128), F32)),
        in_specs=[HBM_SPEC] * (2 * n) + [ANY] * len(deps),
        out_specs=(*[SEM_SPEC] * (2 * ng), *[HBM_SPEC] * (2 * n), pl.BlockSpec(memory_space=pltpu.VMEM)),
        input_output_aliases={i: 2 * ng + i for i in range(2 * n)},
        compiler_params=pltpu.CompilerParams(has_side_effects=EFFECT),
    )(*srcs, *lands, *deps)
    res, pos = [], 0
    for g in range(ng):
        res.append((out[2 * g], out[2 * g + 1], list(out[2 * ng + pos:2 * ng + pos + sizes[g]]),
                    list(out[2 * ng + n + pos:2 * ng + n + pos + sizes[g]])))
        pos += sizes[g]
    return res, out[-1]


def _ici_wait(send_sems, recv_sems, srcs, lands, gather, after, *, name):
    n = len(srcs)

    def body(*refs):
        src, land = refs[:n], refs[n:2 * n]
        send_sems, recv_sems = refs[2 * n], refs[2 * n + 1]
        x, y, c, chip = _place()
        for i in range(n):
            for k, mask in enumerate(MASKS):
                s, d, got, peer = _ici_ends(src[i], land[i], gather, x, y, c, chip, mask)
                pltpu.make_async_remote_copy(s, d, send_sems.at[3 * i + k], recv_sems.at[3 * i + k],
                                             device_id=peer, device_id_type=MESH).wait_send()
                pltpu.make_async_remote_copy(s, got, send_sems.at[3 * i + k], recv_sems.at[3 * i + k],
                                             device_id=peer, device_id_type=MESH).wait_recv()

    out = pl.pallas_call(
        body, name=name,
        out_shape=tuple(pltpu.HBM(s.shape, s.dtype) for s in (*srcs, *lands)),
        in_specs=[HBM_SPEC] * (2 * n) + [SEM_SPEC, SEM_SPEC, ANY],
        out_specs=tuple([HBM_SPEC] * (2 * n)),
        input_output_aliases={i: i for i in range(2 * n)},
        compiler_params=pltpu.CompilerParams(has_side_effects=EFFECT),
    )(*srcs, *lands, send_sems, recv_sems, after)
    return list(out[:n]), list(out[n:])


def _relay_halves(gs, *, name):
    n = len(gs)

    def body(*refs):
        out = refs[n:2 * n]
        send_sems, recv_sems = refs[2 * n], refs[2 * n + 1]
        x, y, c, _ = _place()
        _sibling_barrier(x, y, c)
        cps = []
        for i in range(n):
            for k, mask in enumerate(MASKS):
                _, _, pchip = _other_chip(x, y, mask)
                got = _half(out[i], c, (pchip,))
                cp = pltpu.make_async_remote_copy(got, got, send_sems.at[i, k], recv_sems.at[i, k],
                                                  device_id=(x, y, 1 - c), device_id_type=MESH)
                cp.start()
                cps.append(cp)
        for i in range(n):
            for k, mask in enumerate(MASKS):
                _, _, pchip = _other_chip(x, y, mask)
                theirs = _half(out[i], 1 - c, (pchip,))
                pltpu.make_async_remote_copy(theirs, theirs, send_sems.at[i, k], recv_sems.at[i, k],
                                             device_id=(x, y, 1 - c), device_id_type=MESH).wait_recv()
        for cp in cps:
            cp.wait_send()

    return pl.pallas_call(
        body, name=name, in_specs=[ANY] * n, out_specs=[ANY] * n,
        out_shape=[jax.ShapeDtypeStruct(g.shape, g.dtype) for g in gs],
        input_output_aliases={i: i for i in range(n)},
        scratch_shapes=[pltpu.SemaphoreType.DMA((n, 3)), pltpu.SemaphoreType.DMA((n, 3))],
        compiler_params=pltpu.CompilerParams(collective_id=SIBLING_ID),
    )(*gs)


def _share_halves(gs, *, name):
    n = len(gs)

    def body(*refs):
        out = refs[n:2 * n]
        send_sems, recv_sems = refs[2 * n], refs[2 * n + 1]
        x, y, c, _ = _place()
        _sibling_barrier(x, y, c)
        cps = []
        for i in range(n):
            cp = pltpu.make_async_remote_copy(_half(out[i], c), _half(out[i], c), send_sems.at[i], recv_sems.at[i],
                                              device_id=(x, y, 1 - c), device_id_type=MESH)
            cp.start()
            cps.append(cp)
        for i in range(n):
            theirs = _half(out[i], 1 - c)
            pltpu.make_async_remote_copy(theirs, theirs, send_sems.at[i], recv_sems.at[i],
                                         device_id=(x, y, 1 - c), device_id_type=MESH).wait_recv()
        for cp in cps:
            cp.wait_send()

    return pl.pallas_call(
        body, name=name, in_specs=[ANY] * n, out_specs=[ANY] * n,
        out_shape=[jax.ShapeDtypeStruct(g.shape, g.dtype) for g in gs],
        input_output_aliases={i: i for i in range(n)},
        scratch_shapes=[pltpu.SemaphoreType.DMA((n,)), pltpu.SemaphoreType.DMA((n,))],
        compiler_params=pltpu.CompilerParams(collective_id=SIBLING_ID),
    )(*gs)


def _gather_all(buf, *, name):
    r, L = buf.shape
    vmem = pl.BlockSpec(memory_space=pltpu.VMEM)
    masks = tuple(range(1, N_DEV))

    def body(buf_ref, out_ref, send_sems, recv_sems):
        x, y, c, _ = _place()
        me = 4 * x + 2 * y + c
        out_ref[me] = buf_ref[...]
        sends = []
        for k, mask in enumerate(masks):
            px = 1 - x if mask & 4 else x
            py = 1 - y if mask & 2 else y
            pc = 1 - c if mask & 1 else c
            cp = pltpu.make_async_remote_copy(buf_ref, out_ref.at[me], send_sems.at[k], recv_sems.at[k],
                                              device_id=(px, py, pc), device_id_type=MESH)
            cp.start()
            sends.append(cp)
        for k, mask in enumerate(masks):
            px = 1 - x if mask & 4 else x
            py = 1 - y if mask & 2 else y
            pc = 1 - c if mask & 1 else c
            pltpu.make_async_remote_copy(buf_ref, out_ref.at[4 * px + 2 * py + pc], send_sems.at[k],
                                         recv_sems.at[k], device_id=(px, py, pc), device_id_type=MESH).wait_recv()
        for cp in sends:
            cp.wait_send()

    return pl.pallas_call(
        body, name=name, in_specs=[vmem], out_specs=vmem,
        out_shape=jax.ShapeDtypeStruct((N_DEV, r, L), buf.dtype),
        scratch_shapes=[pltpu.SemaphoreType.DMA((N_DEV - 1,)), pltpu.SemaphoreType.DMA((N_DEV - 1,))],
    )(buf)


def _pack(arrs, lanes, row_mult=8):
    flat = jnp.concatenate([a.reshape(-1) for a in arrs])
    rows = -(-flat.shape[0] // lanes)
    rows = -(-rows // row_mult) * row_mult
    flat = jnp.pad(flat, (0, rows * lanes - flat.shape[0]))
    return flat.reshape(rows, lanes)


def _unpack(buf, shapes):
    flat = buf.reshape(-1)
    out, pos = [], 0
    for s in shapes:
        n = 1
        for d in s:
            n *= d
        out.append(flat[pos:pos + n].reshape(s))
        pos += n
    return out


def kernel(x, mem, norm_mix_g, w_in, conv_a_w, conv_b_w, conv_b_bias, ln_b_g, ln_b_b, w_out, norm_x_g, norm_mem_g, w_q, w_kv, w_xo, norm_ffn_g, w_up, w_down, final_g, loss_target, m_norm_mix_g, m_w_in, m_conv_a_w, m_conv_b_w, m_conv_b_bias, m_ln_b_g, m_ln_b_b, m_w_out, m_norm_x_g, m_norm_mem_g, m_w_q, m_w_kv, m_w_xo, m_norm_ffn_g, m_w_up, m_w_down, m_final_g, v_norm_mix_g, v_w_in, v_conv_a_w, v_conv_b_w, v_conv_b_bias, v_ln_b_g, v_ln_b_b, v_w_out, v_norm_x_g, v_norm_mem_g, v_w_q, v_w_kv, v_w_xo, v_norm_ffn_g, v_w_up, v_w_down, v_final_g):
    W = dict(norm_mix_g=norm_mix_g, w_in=w_in, conv_a_w=conv_a_w, conv_b_w=conv_b_w, conv_b_bias=conv_b_bias,
             ln_b_g=ln_b_g, ln_b_b=ln_b_b, w_out=w_out, norm_x_g=norm_x_g, norm_mem_g=norm_mem_g, w_q=w_q,
             w_kv=w_kv, w_xo=w_xo, norm_ffn_g=norm_ffn_g, w_up=w_up, w_down=w_down, final_g=final_g)
    MO = dict(norm_mix_g=m_norm_mix_g, w_in=m_w_in, conv_a_w=m_conv_a_w, conv_b_w=m_conv_b_w,
              conv_b_bias=m_conv_b_bias, ln_b_g=m_ln_b_g, ln_b_b=m_ln_b_b, w_out=m_w_out, norm_x_g=m_norm_x_g,
              norm_mem_g=m_norm_mem_g, w_q=m_w_q, w_kv=m_w_kv, w_xo=m_w_xo, norm_ffn_g=m_norm_ffn_g,
              w_up=m_w_up, w_down=m_w_down, final_g=m_final_g)
    VO = dict(norm_mix_g=v_norm_mix_g, w_in=v_w_in, conv_a_w=v_conv_a_w, conv_b_w=v_conv_b_w,
              conv_b_bias=v_conv_b_bias, ln_b_g=v_ln_b_g, ln_b_b=v_ln_b_b, w_out=v_w_out, norm_x_g=v_norm_x_g,
              norm_mem_g=v_norm_mem_g, w_q=v_w_q, w_kv=v_w_kv, w_xo=v_w_xo, norm_ffn_g=v_norm_ffn_g,
              w_up=v_w_up, w_down=v_w_down, final_g=v_final_g)
    names = list(W.keys())
    depth = norm_mix_g.shape[0]
    assert depth == 2, "the exchange splits the weights into one layer per core of a chip"
    c_idx = lax.axis_index("c")
    chip_idx = 2 * lax.axis_index("x") + lax.axis_index("y")

    xs = x[0]
    ms = mem[0]
    tgt = loss_target[0]
    S, D = xs.shape
    c_a = conv_a_w.shape[-1] * N_CHIPS
    c_loc = conv_a_w.shape[-1]

    meta = jnp.stack([c_idx, chip_idx]).astype(jnp.int32)
    shard_axis = dict(BIG)
    own = {(l, n): W[n][l].astype(BF16) for l in range(depth) for n, _ in BIG}

    conv_local = _pack([conv_a_w, conv_b_w], 128)
    conv_all = _gather_all(conv_local, name="gather_conv_weights")
    order = [(l, gi) for l in range(depth) for gi in range(len(FWD_GROUPS))]
    src_groups = [[own[(l, n)] for n in FWD_GROUPS[gi]] for l, gi in order]
    started, gather_token = _ici_start(
        src_groups, [[jax.ShapeDtypeStruct((N_CHIPS, *s.shape), s.dtype) for s in g] for g in src_groups], True,
        name="gather_weights_start", after=conv_all)
    started = dict(zip(order, started))
    Wb = [dict() for _ in range(depth)]

    def weights_ready(l, gi, after):
        send_sems, recv_sems, srcs, lands = started[(l, gi)]
        srcs, lands = _ici_wait(send_sems, recv_sems, srcs, lands, True, after, name=f"gather_weights_l{l}_g{gi}_wait")
        full = _relay_halves(lands, name=f"gather_weights_l{l}_g{gi}_relay")
        for n, g, o in zip(FWD_GROUPS[gi], full, srcs):
            g = lax.dynamic_update_slice(g, o[None], (chip_idx, 0, 0))
            Wb[l][n] = g.reshape(-1, g.shape[-1]) if shard_axis[n] == 0 else g
    na = depth * K_A * c_loc
    nbw = depth * K_B * c_loc
    ca_parts, cb_parts = [], []
    for j in range(N_CHIPS):
        fl = conv_all[2 * j].reshape(-1)
        ca_parts.append(fl[:na].reshape(depth, K_A, c_loc))
        cb_parts.append(fl[na:na + nbw].reshape(depth, K_B, c_loc))
    conv_a_full = jnp.concatenate(ca_parts, axis=-1)
    conv_b_full = jnp.concatenate(cb_parts, axis=-1)

    saved = []
    h = xs
    for l in range(depth):
        wl = Wb[l]
        t = f"l{l}_"
        if l == 0:
            u = _rms_fwd(h, norm_mix_g[l:l + 1], name=t + "rms_mix", after=gather_token)
        weights_ready(l, 0, u)
        z = _mm(u, wl["w_in"], b_stack=True, name=t + "mm_in", bm=2048)
        y_a = _mixer_a_fwd(z, conv_a_full[l], name=t + "mixer_a")
        cb = _mixer_b_fwd(z, conv_b_full[l], conv_b_bias[l:l + 1], name=t + "mixer_b")
        yy = _ln_silu_fwd(cb, ln_b_g[l:l + 1], ln_b_b[l:l + 1], y_a, name=t + "ln_silu")
        weights_ready(l, 1, yy)
        h2, q_in = _mm(yy, wl["w_out"], res=h, epi="rms_fwd", norm=norm_x_g[l:l + 1], name=t + "mm_out")
        q = _mm(q_in, wl["w_q"], out_dtype=BF16, name=t + "mm_q")
        mn = _rms_fwd(ms, norm_mem_g[l:l + 1], name=t + "rms_mem")
        kv = _mm(mn, wl["w_kv"], b_stack=True, out_dtype=BF16, name=t + "mm_kv")
        o = _attn_fwd(q, kv, name=t + "attn", tq=ATTN_ROW_TILE)
        h3, u3 = _mm(o, wl["w_xo"], res=h2, epi="rms_fwd", norm=norm_ffn_g[l:l + 1], name=t + "mm_xo")
        weights_ready(l, 2, h3)
        a_pre, hh = _mm(u3, wl["w_up"], b_stack=True, out_dtype=BF16, epi="sqrelu", name=t + "mm_up")
        saved.append(dict(h=h, u=u, z=z, cb=cb, yy=yy, h2=h2, q_in=q_in, q=q, mn=mn, kv=kv, o=o, h3=h3,
                          u3=u3, a_pre=a_pre, hh=hh))
        if l + 1 < depth:
            h, u = _mm(hh, wl["w_down"], res=h3, epi="rms_fwd", norm=norm_mix_g[l + 1:l + 2], name=t + "mm_down")
        else:
            h = _mm(hh, wl["w_down"], res=h3, name=t + "mm_down")

    loss_vec, dh, dhb, d_final = _loss_head(h, final_g.reshape(1, D), tgt, name="loss_head")

    GW = [dict() for _ in range(depth)]
    GS = [dict() for _ in range(depth)]
    pending = []

    def reduce_start(l, gi):
        group = BWD_GROUPS[gi]

        def by_chip(g, n):
            return g if g.ndim == 3 else g.reshape(N_CHIPS, *W[n].shape[1:])

        gs = [by_chip(GW[l][n][0], n) for n in group]
        from_sibling = _swap_halves([by_chip(GW[l][n][1], n) for n in group],
                                    name=f"grad_swap_sibling_l{l}_g{gi}")
        prs = [_pair_sum(g, r, meta, name=f"grad_pair_sum_l{l}_{n}") for g, r, n in zip(gs, from_sibling, group)]
        (st,), token = _ici_start([prs], [prs], False, name=f"grad_scatter_chips_l{l}_g{gi}_start")
        pending.append((l, group, st))
        return token

    for l in reversed(range(depth)):
        wl, sv = Wb[l], saved[l]
        t = f"l{l}_b_"
        GW[l]["w_down"] = _mm(sv["hh"], dhb, ta=True, epi="with_bf16", name=t + "dw_down")
        da = _mm(dhb, wl["w_down"], tb=True, out_dtype=BF16, epi="dsqrelu", aux=sv["a_pre"], name=t + "d_hidden")
        GW[l]["w_up"] = _mm(sv["u3"], da, ta=True, o_stack=N_CHIPS, epi="with_bf16", name=t + "dw_up")
        fused = dict(epi="rms_bwd", bm=512)
        dh, dhb, GS[l]["norm_ffn_g"] = _mm(da, wl["w_up"], tb=True, b_stack=True, res=dh,
                                           norm=(sv["h3"], norm_ffn_g[l:l + 1]), after=reduce_start(l, 0),
                                           name=t + "d_u3", **fused)
        GW[l]["w_xo"] = _mm(sv["o"], dhb, ta=True, epi="with_bf16", name=t + "dw_xo")
        d_o = _mm(dhb, wl["w_xo"], tb=True, out_dtype=BF16, name=t + "d_o")
        dq, dkv = _attn_bwd(sv["q"], sv["kv"], d_o, name=t + "attn", tq=ATTN_ROW_TILE)
        GW[l]["w_q"] = _mm(sv["q_in"], dq, ta=True, epi="with_bf16", name=t + "dw_q")
        dkvb = dkv.astype(BF16)
        GW[l]["w_kv"] = _mm(sv["mn"], dkvb, ta=True, o_stack=N_CHIPS, epi="with_bf16", name=t + "dw_kv")
        dmn = _mm(dkvb, wl["w_kv"], tb=True, b_stack=True, name=t + "d_mem")
        _, _, GS[l]["norm_mem_g"] = _rms_bwd(ms, norm_mem_g[l:l + 1], dmn, None, name=t + "rms_mem")
        dh, dhb, GS[l]["norm_x_g"] = _mm(dq, wl["w_q"], tb=True, res=dh, norm=(sv["h2"], norm_x_g[l:l + 1]),
                                         after=reduce_start(l, 1), name=t + "d_q_in", **fused)
        GW[l]["w_out"] = _mm(sv["yy"], dhb, ta=True, epi="with_bf16", name=t + "dw_out")
        dyy = _mm(dhb, wl["w_out"], tb=True, name=t + "d_y")
        dcb, GS[l]["ln_b_g"], GS[l]["ln_b_b"] = _ln_silu_bwd(sv["cb"], ln_b_g[l:l + 1], ln_b_b[l:l + 1], dyy, 1,
                                                             name=t + "ln_silu")
        db_, dc_, dh_, GS[l]["conv_a_w"] = _mixer_a_bwd(sv["z"], conv_a_full[l], dyy, name=t + "mixer_a")
        dv_, dg_, GS[l]["conv_b_w"], GS[l]["conv_b_bias"] = _mixer_b_bwd(sv["z"], conv_b_full[l], dcb,
                                                                         name=t + "mixer_b")
        dz = jnp.concatenate([db_, dc_, dh_, dv_, dg_], axis=1)
        GW[l]["w_in"] = _mm(sv["u"], dz, ta=True, o_stack=N_CHIPS, epi="with_bf16", name=t + "dw_in")
        dh, dhb, GS[l]["norm_mix_g"] = _mm(dz, wl["w_in"], tb=True, b_stack=True, res=dh,
                                           norm=(sv["h"], norm_mix_g[l:l + 1]), after=reduce_start(l, 2),
                                           name=t + "d_u", **fused)
    grad_x = dh[None]

    after = GS[0]["norm_mix_g"]
    keys, halves = [], []
    for l, group, (send_sems, recv_sems, srcs, lands) in pending:
        prs, pieces = _ici_wait(send_sems, recv_sems, srcs, lands, False, after,
                                name=f"grad_scatter_chips_l{l}_{group[0]}_wait")
        for n, p, q in zip(group, prs, pieces):
            keys.append((l, n))
            halves.append(_chip_sum(p, q, meta, name=f"grad_chip_sum_l{l}_{n}"))
        after = halves[-1]
    reduced = dict(zip(keys, _share_halves(halves, name="grad_share_sibling")))

    grads, deltas, new_m, new_v = {}, {}, {}, {}
    for n, _ in BIG:
        grads[n], deltas[n], new_m[n], new_v[n] = _adamw_layers(W[n], reduced[(0, n)], reduced[(1, n)], MO[n], VO[n],
                                                               name="adamw_" + n)

    small = [n for n in names if n not in dict(BIG)]
    full_shapes = {n: ((depth, W[n].shape[1], c_a) if n in ("conv_a_w", "conv_b_w") else W[n].shape)
                   for n in small}

    def small_grad(n):
        if n == "final_g":
            return d_final.reshape(W[n].shape)
        return jnp.stack([GS[l][n].reshape(full_shapes[n][1:]) for l in range(depth)])

    part = _pack([small_grad(n) for n in small] + [loss_vec[0, :1]], LANES)
    everyone = _gather_all(part, name="gather_small_grads")
    total = _sum_leading(everyone, name="small_grad_sum")
    *small_sums, loss = _unpack(total, [full_shapes[n] for n in small] + [()])
    full_grads = dict(zip(small, small_sums))
    for n in ("conv_a_w", "conv_b_w"):
        full_grads[n] = lax.dynamic_slice_in_dim(full_grads[n], chip_idx * c_loc, c_loc, axis=2)
    shapes = [W[n].shape for n in small]
    d_s, m_s, v_s = _adamw(_pack([W[n] for n in small], 128), _pack([full_grads[n] for n in small], 128),
                           _pack([MO[n] for n in small], 128), _pack([VO[n] for n in small], 128),
                           name="adamw_small")
    for n, d, nm, nv in zip(small, _unpack(d_s, shapes), _unpack(m_s, shapes), _unpack(v_s, shapes)):
        grads[n], deltas[n], new_m[n], new_v[n] = full_grads[n], d, nm, nv

    return (loss, grad_x, *[grads[n] for n in names], *[deltas[n] for n in names],
            *[new_m[n] for n in names], *[new_v[n] for n in names])
```

```python
import jax
import jax.numpy as jnp
from jax import lax
from jax.experimental import pallas as pl
from jax.experimental.pallas import tpu as pltpu

F32 = jnp.float32
BF16 = jnp.bfloat16
MESH = pl.DeviceIdType.MESH

EPS = 1e-6
N_XHEADS = 4
K_A = 3
K_B = 31
PAD_A = 8
PAD_B = 32
CONV_CHUNK = 256
ROW_TILE = 512
ATTN_ROW_TILE = 1024
LANES = 1024
VMEM_LIMIT_BYTES = 56 * 1024 * 1024

ADAM_LR = 0.001
ADAM_B1 = 0.9
ADAM_B2 = 0.999
ADAM_EPS = 1e-08
ADAM_WD = 0.01
ADAM_STEP = 10

BIG = (("w_in", 1), ("w_out", 0), ("w_q", 0), ("w_kv", 1), ("w_xo", 0), ("w_up", 1), ("w_down", 0))
FWD_GROUPS = (("w_in",), ("w_out", "w_q", "w_kv", "w_xo"), ("w_up", "w_down"))
BWD_GROUPS = (("w_down", "w_up"), ("w_xo", "w_q", "w_kv"), ("w_out", "w_in"))
N_CHIPS = 4
N_DEV = 8


def _params(sem=None):
    return pltpu.CompilerParams(dimension_semantics=sem, vmem_limit_bytes=VMEM_LIMIT_BYTES)


def _pick(cands, n):
    for c in cands:
        if c <= n and n % c == 0:
            return c
    return n


def _mm(a, b, *, name, ta=False, tb=False, out_dtype=F32, res=None, epi=None, aux=None, norm=None, after=None,
        b_stack=False, o_stack=0, bm=1024, bn=1024, bk=1024):
    if ta:
        K, M = a.shape
    else:
        M, K = a.shape
    if b_stack:
        n_st, d1, d2 = b.shape
        N, kb = (d1, d2) if tb else (n_st * d2, d1)
        assert K == (n_st * d2 if tb else d1), (name, a.shape, b.shape)
    else:
        N = b.shape[0] if tb else b.shape[1]
    n_unit = b.shape[2] if (b_stack and not tb) else (N // o_stack if o_stack else N)
    k_unit = b.shape[2] if (b_stack and tb) else K
    bm = _pick((bm, 512, 256, 128), M)
    bn = _pick((bn, 512, 640, 256, 384, 128), n_unit)
    bk = _pick((bk, 640, 512, 256, 128), k_unit)
    assert M % bm == 0 and N % bn == 0 and K % bk == 0, (name, M, N, K)
    nk = K // bk
    per_n = n_unit // bn
    per_k = k_unit // bk
    a_spec = (pl.BlockSpec((bk, bm), lambda i, j, k: (k, i)) if ta
              else pl.BlockSpec((bm, bk), lambda i, j, k: (i, k)))
    if b_stack and tb:
        b_spec = pl.BlockSpec((None, bn, bk), lambda i, j, k: (k // per_k, j, k % per_k))
    elif b_stack:
        b_spec = pl.BlockSpec((None, bk, bn), lambda i, j, k: (j // per_n, k, j % per_n))
    elif tb:
        b_spec = pl.BlockSpec((bn, bk), lambda i, j, k: (j, k))
    else:
        b_spec = pl.BlockSpec((bk, bn), lambda i, j, k: (k, j))
    o_spec = pl.BlockSpec((bm, bn), lambda i, j, k: (i, j))
    dims = (((0 if ta else 1,), (1 if tb else 0,)), ((), ()))
    ins, in_specs = [a, b], [a_spec, b_spec]
    if res is not None:
        ins.append(res)
        in_specs.append(o_spec)
    if aux is not None:
        ins.append(aux)
        in_specs.append(o_spec)
    vec_spec = pl.BlockSpec((1, bn), lambda i, j, k: (0, j))
    if epi == "rms_fwd":
        assert bn == N, (name, bn, N)
        ins.append(norm)
        in_specs.append(vec_spec)
    elif epi == "rms_bwd":
        assert bn == N, (name, bn, N)
        ins += [norm[0], norm[1]]
        in_specs += [o_spec, vec_spec]
    n_norm = {"rms_fwd": 1, "rms_bwd": 2}.get(epi, 0)
    if after is not None:
        ins.append(after)
        in_specs.append(pl.BlockSpec(memory_space=pl.ANY))
    n_out = {"sqrelu": 2, "rms_fwd": 2, "rms_bwd": 3, "with_bf16": 2}.get(epi, 1)
    out_shape = [jax.ShapeDtypeStruct((M, N), out_dtype)] * n_out
    out_specs = [o_spec] * n_out
    if epi in ("rms_fwd", "with_bf16"):
        out_shape = [jax.ShapeDtypeStruct((M, N), F32), jax.ShapeDtypeStruct((M, N), BF16)]
    elif epi == "rms_bwd":
        out_shape = [jax.ShapeDtypeStruct((M, N), F32), jax.ShapeDtypeStruct((M, N), BF16),
                     jax.ShapeDtypeStruct((1, N), F32)]
        out_specs = [o_spec, o_spec, vec_spec]
    if o_stack:
        assert epi in (None, "with_bf16") and res is None and aux is None
        out_shape = [jax.ShapeDtypeStruct((o_stack, M, N // o_stack), s.dtype) for s in out_shape]
        out_specs = [pl.BlockSpec((None, bm, bn), lambda i, j, k: (j // per_n, i, j % per_n))] * n_out

    def body(*refs):
        a_ref, b_ref = refs[0], refs[1]
        pos = 2
        res_ref = aux_ref = None
        if res is not None:
            res_ref = refs[pos]
            pos += 1
        if aux is not None:
            aux_ref = refs[pos]
            pos += 1
        norm_refs = refs[pos:pos + n_norm]
        pos += n_norm + (after is not None)
        outs = refs[pos:pos + n_out]

        def product():
            return lax.dot_general(a_ref[...], b_ref[...], dims, preferred_element_type=F32)

        def finish(r):
            if epi == "rms_bwd":
                x_ref, g_ref = norm_refs
                xv = x_ref[...]
                rs = lax.rsqrt(jnp.mean(xv * xv, axis=-1, keepdims=True) + EPS)
                xh = xv * rs
                dxh = r * g_ref[...]
                dx = rs * (dxh - xh * jnp.mean(dxh * xh, axis=-1, keepdims=True))
                if res_ref is not None:
                    dx = dx + res_ref[...]
                outs[0][...] = dx
                outs[1][...] = dx.astype(BF16)

                @pl.when(pl.program_id(0) == 0)
                def _():
                    outs[2][...] = jnp.zeros_like(outs[2])

                outs[2][...] += jnp.sum(r * xh, axis=0, keepdims=True)
                return
            if res_ref is not None:
                r = r + res_ref[...]
            if epi == "rms_fwd":
                outs[0][...] = r
                rs = lax.rsqrt(jnp.mean(r * r, axis=-1, keepdims=True) + EPS)
                outs[1][...] = (r * rs * norm_refs[0][...]).astype(BF16)
            elif epi == "with_bf16":
                outs[0][...] = r
                outs[1][...] = r.astype(BF16)
            elif epi == "sqrelu":
                outs[0][...] = r.astype(out_dtype)
                rl = jnp.maximum(r, 0.0)
                outs[1][...] = (rl * rl).astype(out_dtype)
            elif epi == "dsqrelu":
                outs[0][...] = (r * (2.0 * jnp.maximum(aux_ref[...].astype(F32), 0.0))).astype(out_dtype)
            else:
                outs[0][...] = r.astype(out_dtype)

        if nk == 1:
            finish(product())
            return
        acc = refs[pos + n_out]
        k = pl.program_id(2)

        @pl.when(k == 0)
        def _():
            acc[...] = product()

        @pl.when(jnp.logical_and(k > 0, k < nk - 1))
        def _():
            acc[...] += product()

        @pl.when(k == nk - 1)
        def _():
            finish(acc[...] + product())

    out = pl.pallas_call(
        body, name=name, grid=(M // bm, N // bn, nk),
        in_specs=in_specs, out_specs=out_specs, out_shape=out_shape,
        scratch_shapes=[pltpu.VMEM((bm, bn), F32)] if nk > 1 else [],
        compiler_params=_params(("arbitrary",) * 3 if epi == "rms_bwd" else ("parallel", "parallel", "arbitrary")),
    )(*ins)
    return out if n_out > 1 else out[0]


def _row_tile(rows):
    return min(ROW_TILE, rows)


def _rms_fwd(x, g, *, name, after=None):
    S, D = x.shape
    tr = _row_tile(S)

    def body(x_ref, g_ref, *rest):
        o_ref = rest[-1]
        xv = x_ref[...]
        r = lax.rsqrt(jnp.mean(xv * xv, axis=-1, keepdims=True) + EPS)
        o_ref[...] = (xv * r * g_ref[...]).astype(BF16)

    deps = [] if after is None else [after]
    return pl.pallas_call(
        body, name=name, grid=(S // tr,),
        in_specs=[pl.BlockSpec((tr, D), lambda i: (i, 0)), pl.BlockSpec((1, D), lambda i: (0, 0))]
        + [ANY] * len(deps),
        out_specs=pl.BlockSpec((tr, D), lambda i: (i, 0)),
        out_shape=jax.ShapeDtypeStruct((S, D), BF16),
        compiler_params=_params(("parallel",)),
    )(x, g, *deps)


def _rms_bwd(x, g, du, dres, *, name):
    S, D = x.shape
    tr = _row_tile(S)
    has_res = dres is not None

    def body(*refs):
        dx_ref, dxb_ref, dg_ref = refs[-3:]
        if has_res:
            x_ref, g_ref, du_ref, dres_ref = refs[:4]
        else:
            x_ref, g_ref, du_ref = refs[:3]
        xv = x_ref[...]
        r = lax.rsqrt(jnp.mean(xv * xv, axis=-1, keepdims=True) + EPS)
        xh = xv * r
        dy = du_ref[...]
        dxh = dy * g_ref[...]
        dx = r * (dxh - xh * jnp.mean(dxh * xh, axis=-1, keepdims=True))
        if has_res:
            dx = dx + dres_ref[...]
        dx_ref[...] = dx
        dxb_ref[...] = dx.astype(BF16)

        @pl.when(pl.program_id(0) == 0)
        def _():
            dg_ref[...] = jnp.zeros_like(dg_ref)

        dg_ref[...] += jnp.sum(dy * xh, axis=0, keepdims=True)

    row = pl.BlockSpec((tr, D), lambda i: (i, 0))
    vec = pl.BlockSpec((1, D), lambda i: (0, 0))
    ins = [x, g, du] + ([dres] if has_res else [])
    in_specs = [row, vec, row] + ([row] if has_res else [])
    return pl.pallas_call(
        body, name=name, grid=(S // tr,),
        in_specs=in_specs, out_specs=[row, row, vec],
        out_shape=[jax.ShapeDtypeStruct((S, D), F32), jax.ShapeDtypeStruct((S, D), BF16),
                   jax.ShapeDtypeStruct((1, D), F32)],
        compiler_params=_params(("arbitrary",)),
    )(*ins)


def _loss_head(h, g, target, *, name):
    S, D = h.shape
    tr = _row_tile(S)

    def body(x_ref, g_ref, t_ref, loss_ref, dx_ref, dxb_ref, dg_ref):
        xv = x_ref[...]
        r = lax.rsqrt(jnp.mean(xv * xv, axis=-1, keepdims=True) + EPS)
        xh = xv * r
        gv = g_ref[...]
        err = xh * gv - t_ref[...]
        part = 0.5 * jnp.sum(jnp.mean(err * err, axis=-1, keepdims=True), axis=0, keepdims=True)
        dy = err * (1.0 / D)
        dxh = dy * gv
        dx = r * (dxh - xh * jnp.mean(dxh * xh, axis=-1, keepdims=True))
        dx_ref[...] = dx
        dxb_ref[...] = dx.astype(BF16)

        @pl.when(pl.program_id(0) == 0)
        def _():
            dg_ref[...] = jnp.zeros_like(dg_ref)
            loss_ref[...] = jnp.zeros_like(loss_ref)

        dg_ref[...] += jnp.sum(dy * xh, axis=0, keepdims=True)
        loss_ref[...] += jnp.broadcast_to(part, loss_ref.shape)

    row = pl.BlockSpec((tr, D), lambda i: (i, 0))
    vec = pl.BlockSpec((1, D), lambda i: (0, 0))
    return pl.pallas_call(
        body, name=name, grid=(S // tr,),
        in_specs=[row, vec, row],
        out_specs=[pl.BlockSpec((1, 128), lambda i: (0, 0)), row, row, vec],
        out_shape=[jax.ShapeDtypeStruct((1, 128), F32), jax.ShapeDtypeStruct((S, D), F32),
                   jax.ShapeDtypeStruct((S, D), BF16), jax.ShapeDtypeStruct((1, D), F32)],
        compiler_params=_params(("arbitrary",)),
    )(h, g, target)


def _sigmoid(x):
    return 1.0 / (1.0 + jnp.exp(-x))


def _ln_silu_fwd(cb, g, b, into, *, name):
    S, C = cb.shape
    tr = _row_tile(S)

    def body(x_ref, g_ref, b_ref, into_ref, o_ref):
        xv = x_ref[...]
        mu = jnp.mean(xv, axis=-1, keepdims=True)
        xc = xv - mu
        rs = lax.rsqrt(jnp.mean(xc * xc, axis=-1, keepdims=True) + EPS)
        l = xc * rs * g_ref[...] + b_ref[...]
        o_ref[...] = (l * _sigmoid(l)).astype(BF16)

    row = pl.BlockSpec((tr, C), lambda i: (i, 0))
    vec = pl.BlockSpec((1, C), lambda i: (0, 0))
    return pl.pallas_call(
        body, name=name, grid=(S // tr,), in_specs=[row, vec, vec, pl.BlockSpec(memory_space=pl.ANY)],
        out_specs=pl.BlockSpec((tr, C), lambda i: (i, 1)),
        out_shape=jax.ShapeDtypeStruct(into.shape, BF16), input_output_aliases={3: 0},
        compiler_params=_params(("parallel",)),
    )(cb, g, b, into)


def _ln_silu_bwd(cb, g, b, dy, col_block, *, name):
    S, C = cb.shape
    tr = _row_tile(S)

    def body(x_ref, g_ref, b_ref, dy_ref, dx_ref, dg_ref, db_ref):
        xv = x_ref[...]
        mu = jnp.mean(xv, axis=-1, keepdims=True)
        xc = xv - mu
        rs = lax.rsqrt(jnp.mean(xc * xc, axis=-1, keepdims=True) + EPS)
        xh = xc * rs
        gv = g_ref[...]
        l = xh * gv + b_ref[...]
        sg = _sigmoid(l)
        dl = dy_ref[...] * (sg + l * sg * (1.0 - sg))
        dxh = dl * gv
        dx_ref[...] = rs * (dxh - jnp.mean(dxh, axis=-1, keepdims=True)
                            - xh * jnp.mean(dxh * xh, axis=-1, keepdims=True))

        @pl.when(pl.program_id(0) == 0)
        def _():
            dg_ref[...] = jnp.zeros_like(dg_ref)
            db_ref[...] = jnp.zeros_like(db_ref)

        dg_ref[...] += jnp.sum(dl * xh, axis=0, keepdims=True)
        db_ref[...] += jnp.sum(dl, axis=0, keepdims=True)

    row = pl.BlockSpec((tr, C), lambda i: (i, 0))
    vec = pl.BlockSpec((1, C), lambda i: (0, 0))
    return pl.pallas_call(
        body, name=name, grid=(S // tr,),
        in_specs=[row, vec, vec, pl.BlockSpec((tr, C), lambda i: (i, col_block))],
        out_specs=[row, vec, vec],
        out_shape=[jax.ShapeDtypeStruct((S, C), F32), jax.ShapeDtypeStruct((1, C), F32),
                   jax.ShapeDtypeStruct((1, C), F32)],
        compiler_params=_params(("arbitrary",)),
    )(cb, g, b, dy)


def _attn_fwd(q, kv, *, name, tq=ROW_TILE):
    S, D = q.shape
    M = kv.shape[0]
    hd = D // N_XHEADS
    scale = 1.0 / float(hd) ** 0.5
    tq = min(tq, S)

    def body(q_ref, k_ref, v_ref, o_ref):
        for h in range(N_XHEADS):
            cols = slice(h * hd, (h + 1) * hd)
            s = lax.dot_general(q_ref[:, cols], k_ref[:, cols], (((1,), (1,)), ((), ())),
                                preferred_element_type=F32) * scale
            e = jnp.exp(s - jnp.max(s, axis=-1, keepdims=True))
            p = e / jnp.sum(e, axis=-1, keepdims=True)
            o = jnp.dot(p.astype(BF16), v_ref[:, cols], preferred_element_type=F32)
            o_ref[:, cols] = o.astype(BF16)

    return pl.pallas_call(
        body, name=name, grid=(S // tq,),
        in_specs=[pl.BlockSpec((tq, D), lambda i: (i, 0)), pl.BlockSpec((M, D), lambda i: (0, 0)),
                  pl.BlockSpec((M, D), lambda i: (0, 1))],
        out_specs=pl.BlockSpec((tq, D), lambda i: (i, 0)),
        out_shape=jax.ShapeDtypeStruct((S, D), BF16), compiler_params=_params(("parallel",)),
    )(q, kv, kv)


def _attn_bwd(q, kv, do, *, name, tq=ROW_TILE):
    S, D = q.shape
    M = kv.shape[0]
    hd = D // N_XHEADS
    scale = 1.0 / float(hd) ** 0.5
    tq = min(tq, S)

    def body(q_ref, k_ref, v_ref, do_ref, dq_ref, dkv_ref):
        @pl.when(pl.program_id(0) == 0)
        def _():
            dkv_ref[...] = jnp.zeros_like(dkv_ref)

        for h in range(N_XHEADS):
            cols = slice(h * hd, (h + 1) * hd)
            vcols = slice(D + h * hd, D + (h + 1) * hd)
            qh, kh, vh, doh = q_ref[:, cols], k_ref[:, cols], v_ref[:, cols], do_ref[:, cols]
            s = lax.dot_general(qh, kh, (((1,), (1,)), ((), ())), preferred_element_type=F32) * scale
            e = jnp.exp(s - jnp.max(s, axis=-1, keepdims=True))
            p = e / jnp.sum(e, axis=-1, keepdims=True)
            pb = p.astype(BF16)
            dp = lax.dot_general(doh, vh, (((1,), (1,)), ((), ())), preferred_element_type=F32)
            ds = (p * (dp - jnp.sum(dp * p, axis=-1, keepdims=True)) * scale).astype(BF16)
            dq_ref[:, cols] = jnp.dot(ds, kh, preferred_element_type=F32).astype(BF16)
            dkv_ref[:, cols] += lax.dot_general(ds, qh, (((0,), (0,)), ((), ())), preferred_element_type=F32)
            dkv_ref[:, vcols] += lax.dot_general(pb, doh, (((0,), (0,)), ((), ())), preferred_element_type=F32)

    row = pl.BlockSpec((tq, D), lambda i: (i, 0))
    return pl.pallas_call(
        body, name=name, grid=(S // tq,),
        in_specs=[row, pl.BlockSpec((M, D), lambda i: (0, 0)), pl.BlockSpec((M, D), lambda i: (0, 1)), row],
        out_specs=[row, pl.BlockSpec((M, 2 * D), lambda i: (0, 0))],
        out_shape=[jax.ShapeDtypeStruct((S, D), BF16), jax.ShapeDtypeStruct((M, 2 * D), F32)],
        compiler_params=_params(("arbitrary",)),
    )(q, kv, kv, do)


def _delayed(win, j, pad):
    return (win if j == 0 else pltpu.roll(win, j, 0))[pad:, :]


def _advanced(win, j, ch):
    return (win if j == 0 else pltpu.roll(win, win.shape[0] - j, 0))[:ch, :]


def _mixer_a_fwd(z, w, *, name):
    S = z.shape[0]
    C = w.shape[1]
    nb = C // 128
    ch = min(CONV_CHUNK, S)

    def body(b_ref, c_ref, h_ref, w_ref, y_ref, xp):
        xp[0:PAD_A, :] = jnp.zeros((PAD_A, 128), F32)
        xp[PAD_A:, :] = c_ref[...] * h_ref[...]

        def chunk(i, carry):
            base = pl.multiple_of(i * ch, ch)
            win = xp[pl.ds(base, ch + PAD_A), :]
            acc = _delayed(win, 0, PAD_A) * w_ref[K_A - 1:K_A, :]
            for j in range(1, K_A):
                acc = acc + _delayed(win, j, PAD_A) * w_ref[K_A - 1 - j:K_A - j, :]
            y_ref[pl.ds(base, ch), :] = (b_ref[pl.ds(base, ch), :] * acc).astype(BF16)
            return carry

        lax.fori_loop(0, S // ch, chunk, 0)

    def col(g):
        return pl.BlockSpec((S, 128), lambda j: (0, g * nb + j))

    return pl.pallas_call(
        body, name=name, grid=(nb,),
        in_specs=[col(0), col(1), col(2), pl.BlockSpec((K_A, 128), lambda j: (0, j))],
        out_specs=pl.BlockSpec((S, 128), lambda j: (0, j)),
        out_shape=jax.ShapeDtypeStruct((S, 2 * C), BF16),
        scratch_shapes=[pltpu.VMEM((PAD_A + S, 128), F32)],
        compiler_params=_params(("parallel",)),
    )(z, z, z, w)


def _mixer_a_bwd(z, w, dy, *, name):
    S = z.shape[0]
    C = w.shape[1]
    nb = C // 128
    ch = min(CONV_CHUNK, S)

    def body(b_ref, c_ref, h_ref, w_ref, dy_ref, db_ref, dc_ref, dh_ref, dw_ref, xp, dp):
        xp[0:PAD_A, :] = jnp.zeros((PAD_A, 128), F32)
        xp[PAD_A:, :] = c_ref[...] * h_ref[...]
        dp[S:, :] = jnp.zeros((PAD_A, 128), F32)
        dw_ref[...] = jnp.zeros_like(dw_ref)

        def chunk(i, carry):
            base = pl.multiple_of(i * ch, ch)
            win = xp[pl.ds(base, ch + PAD_A), :]
            dya = dy_ref[pl.ds(base, ch), :]
            dcv = dya * b_ref[pl.ds(base, ch), :]
            dp[pl.ds(base, ch), :] = dcv
            acc = None
            for j in range(K_A):
                xs = _delayed(win, j, PAD_A)
                k = K_A - 1 - j
                term = xs * w_ref[k:k + 1, :]
                acc = term if acc is None else acc + term
                dw_ref[k:k + 1, :] += jnp.sum(dcv * xs, axis=0, keepdims=True)
            db_ref[pl.ds(base, ch), :] = (dya * acc).astype(BF16)
            return carry

        lax.fori_loop(0, S // ch, chunk, 0)

        def chunk2(i, carry):
            base = pl.multiple_of(i * ch, ch)
            win = dp[pl.ds(base, ch + PAD_A), :]
            acc = None
            for j in range(K_A):
                term = _advanced(win, j, ch) * w_ref[K_A - 1 - j:K_A - j, :]
                acc = term if acc is None else acc + term
            dc_ref[pl.ds(base, ch), :] = (acc * h_ref[pl.ds(base, ch), :]).astype(BF16)
            dh_ref[pl.ds(base, ch), :] = (acc * c_ref[pl.ds(base, ch), :]).astype(BF16)
            return carry

        lax.fori_loop(0, S // ch, chunk2, 0)

    def col(g):
        return pl.BlockSpec((S, 128), lambda j: (0, g * nb + j))

    out_col = pl.BlockSpec((S, 128), lambda j: (0, j))
    wspec = pl.BlockSpec((K_A, 128), lambda j: (0, j))
    return pl.pallas_call(
        body, name=name, grid=(nb,),
        in_specs=[col(0), col(1), col(2), wspec, out_col],
        out_specs=[out_col, out_col, out_col, wspec],
        out_shape=[jax.ShapeDtypeStruct((S, C), BF16)] * 3 + [jax.ShapeDtypeStruct((K_A, C), F32)],
        scratch_shapes=[pltpu.VMEM((PAD_A + S, 128), F32), pltpu.VMEM((S + PAD_A, 128), F32)],
        compiler_params=_params(("parallel",)),
    )(z, z, z, w, dy)


def _mixer_b_fwd(z, w, bias, *, name):
    S = z.shape[0]
    C = w.shape[1]
    nb = C // 128
    ch = min(CONV_CHUNK, S)

    def body(v_ref, g_ref, w_ref, bias_ref, cb_ref, xp):
        xp[0:PAD_B, :] = jnp.zeros((PAD_B, 128), F32)
        xp[PAD_B:, :] = v_ref[...] * _sigmoid(g_ref[...])

        def chunk(i, carry):
            base = pl.multiple_of(i * ch, ch)
            win = xp[pl.ds(base, ch + PAD_B), :]
            acc = None
            for j in range(K_B):
                term = _delayed(win, j, PAD_B) * w_ref[K_B - 1 - j:K_B - j, :]
                acc = term if acc is None else acc + term
            cb_ref[pl.ds(base, ch), :] = acc + bias_ref[...]
            return carry

        lax.fori_loop(0, S // ch, chunk, 0)

    def col(g):
        return pl.BlockSpec((S, 128), lambda j: (0, g * nb + j))

    return pl.pallas_call(
        body, name=name, grid=(nb,),
        in_specs=[col(3), col(4), pl.BlockSpec((K_B, 128), lambda j: (0, j)),
                  pl.BlockSpec((1, 128), lambda j: (0, j))],
        out_specs=pl.BlockSpec((S, 128), lambda j: (0, j)),
        out_shape=jax.ShapeDtypeStruct((S, C), F32),
        scratch_shapes=[pltpu.VMEM((PAD_B + S, 128), F32)],
        compiler_params=_params(("parallel",)),
    )(z, z, w, bias)


def _mixer_b_bwd(z, w, dcb, *, name):
    S = z.shape[0]
    C = w.shape[1]
    nb = C // 128
    ch = min(CONV_CHUNK, S)

    def body(v_ref, g_ref, w_ref, dcb_ref, dv_ref, dg_ref, dw_ref, dbias_ref, xp, dp):
        xp[0:PAD_B, :] = jnp.zeros((PAD_B, 128), F32)
        xp[PAD_B:, :] = v_ref[...] * _sigmoid(g_ref[...])
        dp[0:S, :] = dcb_ref[...]
        dp[S:, :] = jnp.zeros((PAD_B, 128), F32)
        dw_ref[...] = jnp.zeros_like(dw_ref)
        dbias_ref[...] = jnp.sum(dcb_ref[...], axis=0, keepdims=True)

        def chunk(i, carry):
            base = pl.multiple_of(i * ch, ch)
            win = xp[pl.ds(base, ch + PAD_B), :]
            d = dcb_ref[pl.ds(base, ch), :]
            for j in range(K_B):
                k = K_B - 1 - j
                dw_ref[k:k + 1, :] += jnp.sum(d * _delayed(win, j, PAD_B), axis=0, keepdims=True)
            return carry

        lax.fori_loop(0, S // ch, chunk, 0)

        def chunk2(i, carry):
            base = pl.multiple_of(i * ch, ch)
            win = dp[pl.ds(base, ch + PAD_B), :]
            acc = None
            for j in range(K_B):
                term = _advanced(win, j, ch) * w_ref[K_B - 1 - j:K_B - j, :]
                acc = term if acc is None else acc + term
            sg = _sigmoid(g_ref[pl.ds(base, ch), :])
            vv = v_ref[pl.ds(base, ch), :]
            dv_ref[pl.ds(base, ch), :] = (acc * sg).astype(BF16)
            dg_ref[pl.ds(base, ch), :] = (acc * vv * sg * (1.0 - sg)).astype(BF16)
            return carry

        lax.fori_loop(0, S // ch, chunk2, 0)

    def col(g):
        return pl.BlockSpec((S, 128), lambda j: (0, g * nb + j))

    out_col = pl.BlockSpec((S, 128), lambda j: (0, j))
    wspec = pl.BlockSpec((K_B, 128), lambda j: (0, j))
    bspec = pl.BlockSpec((1, 128), lambda j: (0, j))
    return pl.pallas_call(
        body, name=name, grid=(nb,),
        in_specs=[col(3), col(4), wspec, out_col],
        out_specs=[out_col, out_col, wspec, bspec],
        out_shape=[jax.ShapeDtypeStruct((S, C), BF16)] * 2
        + [jax.ShapeDtypeStruct((K_B, C), F32), jax.ShapeDtypeStruct((1, C), F32)],
        scratch_shapes=[pltpu.VMEM((PAD_B + S, 128), F32), pltpu.VMEM((S + PAD_B, 128), F32)],
        compiler_params=_params(("parallel",)),
    )(z, z, w, dcb)


def _ew_tile(R):
    for t in (512, 256, 128, 64, 32, 16, 8):
        if R % t == 0:
            return t
    return R


def _pair_sum(g, r, meta, *, name):
    n, a, b = g.shape
    ah = a // 2
    tr = _pick((256, 128, 64, 32, 16), ah)
    nh = ah // tr

    def body(meta_ref, g_ref, r_ref, o_ref):
        o_ref[...] = (g_ref[...] + r_ref[...].astype(F32)).astype(BF16)

    half = pl.BlockSpec((None, tr, b), lambda j, i, meta_ref: (j, i, 0))
    return pl.pallas_call(
        body, name=name,
        grid_spec=pltpu.PrefetchScalarGridSpec(
            num_scalar_prefetch=1, grid=(n, nh),
            in_specs=[pl.BlockSpec((None, tr, b), lambda j, i, meta_ref: (j, meta_ref[0] * nh + i, 0)), half],
            out_specs=half),
        out_shape=jax.ShapeDtypeStruct((n, ah, b), BF16), compiler_params=_params(("parallel", "parallel")),
    )(meta, g, r)


def _chip_sum(p, q, meta, *, name):
    n, ah, b = p.shape
    tr = _pick((256, 128, 64, 32, 16), ah)
    nh = ah // tr

    def body(meta_ref, p_ref, q1_ref, q2_ref, q3_ref, o_ref):
        o_ref[...] = ((p_ref[...].astype(F32) + q1_ref[...].astype(F32)) + q2_ref[...].astype(F32)
                      ) + q3_ref[...].astype(F32)

    def piece(mask):
        return pl.BlockSpec((None, tr, b), lambda i, meta_ref: (meta_ref[1] ^ mask, i, 0))

    return pl.pallas_call(
        body, name=name,
        grid_spec=pltpu.PrefetchScalarGridSpec(
            num_scalar_prefetch=1, grid=(nh,),
            in_specs=[piece(0), piece(1), piece(2), piece(3)],
            out_specs=pl.BlockSpec((tr, b), lambda i, meta_ref: (meta_ref[0] * nh + i, 0))),
        out_shape=jax.ShapeDtypeStruct((2 * ah, b), F32), compiler_params=_params(("parallel",)),
    )(meta, p, q, q, q)


def _sum_leading(x, *, name):
    n, R, C = x.shape
    tr = _ew_tile(R)

    def body(x_ref, o_ref):
        acc = x_ref[0].astype(F32)
        for k in range(1, n):
            acc = acc + x_ref[k].astype(F32)
        o_ref[...] = acc

    return pl.pallas_call(
        body, name=name, grid=(R // tr,),
        in_specs=[pl.BlockSpec((n, tr, C), lambda i: (0, i, 0))],
        out_specs=pl.BlockSpec((tr, C), lambda i: (i, 0)),
        out_shape=jax.ShapeDtypeStruct((R, C), F32), compiler_params=_params(("parallel",)),
    )(x)


def _adamw(w, g, m, v, *, name):
    R, C = w.shape
    tr = _ew_tile(R)

    def body(w_ref, g_ref, m_ref, v_ref, d_ref, nm_ref, nv_ref):
        gv = g_ref[...]
        nm = ADAM_B1 * m_ref[...] + (1.0 - ADAM_B1) * gv
        nv = ADAM_B2 * v_ref[...] + (1.0 - ADAM_B2) * (gv * gv)
        m_hat = nm / (1.0 - ADAM_B1 ** ADAM_STEP)
        v_hat = nv / (1.0 - ADAM_B2 ** ADAM_STEP)
        d_ref[...] = -ADAM_LR * (m_hat / (jnp.sqrt(v_hat) + ADAM_EPS) + ADAM_WD * w_ref[...])
        nm_ref[...] = nm
        nv_ref[...] = nv

    row = pl.BlockSpec((tr, C), lambda i: (i, 0))
    return pl.pallas_call(
        body, name=name, grid=(R // tr,), in_specs=[row] * 4, out_specs=[row] * 3,
        out_shape=[jax.ShapeDtypeStruct((R, C), F32)] * 3, compiler_params=_params(("parallel",)),
    )(w, g, m, v)


def _adamw_layers(w, g0, g1, m, v, *, name):
    _, a, b = w.shape
    tr = _pick((512, 256, 128, 64, 32, 16, 8), a)

    def body(w_ref, g0_ref, g1_ref, m_ref, v_ref, g_ref, d_ref, nm_ref, nv_ref):
        gv = jnp.where(pl.program_id(1) == 0, g0_ref[...], g1_ref[...])
        nm = ADAM_B1 * m_ref[...] + (1.0 - ADAM_B1) * gv
        nv = ADAM_B2 * v_ref[...] + (1.0 - ADAM_B2) * (gv * gv)
        m_hat = nm / (1.0 - ADAM_B1 ** ADAM_STEP)
        v_hat = nv / (1.0 - ADAM_B2 ** ADAM_STEP)
        g_ref[...] = gv
        d_ref[...] = -ADAM_LR * (m_hat / (jnp.sqrt(v_hat) + ADAM_EPS) + ADAM_WD * w_ref[...])
        nm_ref[...] = nm
        nv_ref[...] = nv

    lay = pl.BlockSpec((None, tr, b), lambda i, l: (l, i, 0))
    row = pl.BlockSpec((tr, b), lambda i, l: (i, 0))
    return pl.pallas_call(
        body, name=name, grid=(a // tr, 2), in_specs=[lay, row, row, lay, lay], out_specs=[lay] * 4,
        out_shape=[jax.ShapeDtypeStruct(w.shape, F32)] * 4, compiler_params=_params(("parallel", "arbitrary")),
    )(w, g0, g1, m, v)


ANY = pl.BlockSpec(memory_space=pl.ANY)


def _place():
    x, y, c = lax.axis_index("x"), lax.axis_index("y"), lax.axis_index("c")
    return x, y, c, 2 * x + y


def _other_chip(x, y, mask):
    px = 1 - x if mask & 2 else x
    py = 1 - y if mask & 1 else y
    return px, py, 2 * px + py


MASKS = (1, 2, 3)


def _half(ref, c, lead=()):
    ah = ref.shape[-2] // 2
    return ref.at[(*lead, pl.ds(c * ah, ah), slice(None))]


SIBLING_ID = 0


def _sibling_barrier(x, y, c):
    sem = pltpu.get_barrier_semaphore()
    pl.semaphore_signal(sem, inc=1, device_id=(x, y, 1 - c), device_id_type=MESH)
    pl.semaphore_wait(sem, 1)


def _swap_halves(gs, *, name):
    n = len(gs)

    def body(*refs):
        g, out = refs[:n], refs[n:2 * n]
        send_sems, recv_sems = refs[2 * n], refs[2 * n + 1]
        x, y, c, _ = _place()
        _sibling_barrier(x, y, c)
        cps = []
        for i in range(n):
            ah = g[i].shape[1] // 2
            cp = pltpu.make_async_remote_copy(g[i].at[:, pl.ds((1 - c) * ah, ah), :], out[i],
                                              send_sems.at[i], recv_sems.at[i],
                                              device_id=(x, y, 1 - c), device_id_type=MESH)
            cp.start()
            cps.append(cp)
        for cp in cps:
            cp.wait()

    return pl.pallas_call(
        body, name=name, in_specs=[ANY] * n, out_specs=[ANY] * n,
        out_shape=[jax.ShapeDtypeStruct((g.shape[0], g.shape[1] // 2, g.shape[2]), g.dtype) for g in gs],
        scratch_shapes=[pltpu.SemaphoreType.DMA((n,)), pltpu.SemaphoreType.DMA((n,))],
        compiler_params=pltpu.CompilerParams(collective_id=SIBLING_ID),
    )(*gs)


HBM_SPEC = pl.BlockSpec(memory_space=pltpu.HBM)
SEM_SPEC = pl.BlockSpec(memory_space=pltpu.SEMAPHORE)
EFFECT = pltpu.SideEffectType.DATAFLOW_SIDE_EFFECTING


def _ici_ends(src, land, gather, x, y, c, chip, mask):
    px, py, pchip = _other_chip(x, y, mask)
    if gather:
        return _half(src, c), _half(land, c, (chip,)), _half(land, c, (pchip,)), (px, py, c)
    return src.at[pchip], land.at[chip], land.at[pchip], (px, py, c)


def _ici_start(groups, land_groups, gather, *, name, after=None):
    sizes = [len(g) for g in groups]
    n = sum(sizes)
    ng = len(groups)
    deps = [] if after is None else [after]

    def body(*refs):
        src, land = refs[:n], refs[n:2 * n]
        sems = refs[2 * n + len(deps):2 * n + len(deps) + 2 * ng]
        token = refs[-1]
        x, y, c, chip = _place()
        i = 0
        for g in range(ng):
            for j in range(sizes[g]):
                for k, mask in enumerate(MASKS):
                    s, d, _, peer = _ici_ends(src[i], land[i], gather, x, y, c, chip, mask)
                    pltpu.make_async_remote_copy(s, d, sems[2 * g].at[3 * j + k], sems[2 * g + 1].at[3 * j + k],
                                                 device_id=peer, device_id_type=MESH).start()
                i += 1
        token[...] = jnp.zeros_like(token)

    lands = [pltpu.with_memory_space_constraint(lax.empty(s.shape, s.dtype), pltpu.HBM)
             for g in land_groups for s in g]
    srcs = [pltpu.with_memory_space_constraint(s, pltpu.HBM) for g in groups for s in g]
    sem_shapes = [pltpu.SemaphoreType.DMA((3 * m,)) for m in sizes for _ in range(2)]
    out = pl.pallas_call(
        body, name=name,
        out_shape=(*sem_shapes, *[pltpu.HBM(s.shape, s.dtype) for s in srcs],
                   *[pltpu.HBM(s.shape, s.dtype) for s in lands], jax.ShapeDtypeStruct((8, 128), F32)),
        in_specs=[HBM_SPEC] * (2 * n) + [ANY] * len(deps),
        out_specs=(*[SEM_SPEC] * (2 * ng), *[HBM_SPEC] * (2 * n), pl.BlockSpec(memory_space=pltpu.VMEM)),
        input_output_aliases={i: 2 * ng + i for i in range(2 * n)},
        compiler_params=pltpu.CompilerParams(has_side_effects=EFFECT),
    )(*srcs, *lands, *deps)
    res, pos = [], 0
    for g in range(ng):
        res.append((out[2 * g], out[2 * g + 1], list(out[2 * ng + pos:2 * ng + pos + sizes[g]]),
                    list(out[2 * ng + n + pos:2 * ng + n + pos + sizes[g]])))
        pos += sizes[g]
    return res, out[-1]


def _ici_wait(send_sems, recv_sems, srcs, lands, gather, after, *, name):
    n = len(srcs)

    def body(*refs):
        src, land = refs[:n], refs[n:2 * n]
        send_sems, recv_sems = refs[2 * n], refs[2 * n + 1]
        x, y, c, chip = _place()
        for i in range(n):
            for k, mask in enumerate(MASKS):
                s, d, got, peer = _ici_ends(src[i], land[i], gather, x, y, c, chip, mask)
                pltpu.make_async_remote_copy(s, d, send_sems.at[3 * i + k], recv_sems.at[3 * i + k],
                                             device_id=peer, device_id_type=MESH).wait_send()
                pltpu.make_async_remote_copy(s, got, send_sems.at[3 * i + k], recv_sems.at[3 * i + k],
                                             device_id=peer, device_id_type=MESH).wait_recv()

    out = pl.pallas_call(
        body, name=name,
        out_shape=tuple(pltpu.HBM(s.shape, s.dtype) for s in (*srcs, *lands)),
        in_specs=[HBM_SPEC] * (2 * n) + [SEM_SPEC, SEM_SPEC, ANY],
        out_specs=tuple([HBM_SPEC] * (2 * n)),
        input_output_aliases={i: i for i in range(2 * n)},
        compiler_params=pltpu.CompilerParams(has_side_effects=EFFECT),
    )(*srcs, *lands, send_sems, recv_sems, after)
    return list(out[:n]), list(out[n:])


def _relay_halves(gs, *, name):
    n = len(gs)

    def body(*refs):
        out = refs[n:2 * n]
        send_sems, recv_sems = refs[2 * n], refs[2 * n + 1]
        x, y, c, _ = _place()
        _sibling_barrier(x, y, c)
        cps = []
        for i in range(n):
            for k, mask in enumerate(MASKS):
                _, _, pchip = _other_chip(x, y, mask)
                got = _half(out[i], c, (pchip,))
                cp = pltpu.make_async_remote_copy(got, got, send_sems.at[i, k], recv_sems.at[i, k],
                                                  device_id=(x, y, 1 - c), device_id_type=MESH)
                cp.start()
                cps.append(cp)
        for i in range(n):
            for k, mask in enumerate(MASKS):
                _, _, pchip = _other_chip(x, y, mask)
                theirs = _half(out[i], 1 - c, (pchip,))
                pltpu.make_async_remote_copy(theirs, theirs, send_sems.at[i, k], recv_sems.at[i, k],
                                             device_id=(x, y, 1 - c), device_id_type=MESH).wait_recv()
        for cp in cps:
            cp.wait_send()

    return pl.pallas_call(
        body, name=name, in_specs=[ANY] * n, out_specs=[ANY] * n,
        out_shape=[jax.ShapeDtypeStruct(g.shape, g.dtype) for g in gs],
        input_output_aliases={i: i for i in range(n)},
        scratch_shapes=[pltpu.SemaphoreType.DMA((n, 3)), pltpu.SemaphoreType.DMA((n, 3))],
        compiler_params=pltpu.CompilerParams(collective_id=SIBLING_ID),
    )(*gs)


def _share_halves(gs, *, name):
    n = len(gs)

    def body(*refs):
        out = refs[n:2 * n]
        send_sems, recv_sems = refs[2 * n], refs[2 * n + 1]
        x, y, c, _ = _place()
        _sibling_barrier(x, y, c)
        cps = []
        for i in range(n):
            cp = pltpu.make_async_remote_copy(_half(out[i], c), _half(out[i], c), send_sems.at[i], recv_sems.at[i],
                                              device_id=(x, y, 1 - c), device_id_type=MESH)
            cp.start()
            cps.append(cp)
        for i in range(n):
            theirs = _half(out[i], 1 - c)
            pltpu.make_async_remote_copy(theirs, theirs, send_sems.at[i], recv_sems.at[i],
                                         device_id=(x, y, 1 - c), device_id_type=MESH).wait_recv()
        for cp in cps:
            cp.wait_send()

    return pl.pallas_call(
        body, name=name, in_specs=[ANY] * n, out_specs=[ANY] * n,
        out_shape=[jax.ShapeDtypeStruct(g.shape, g.dtype) for g in gs],
        input_output_aliases={i: i for i in range(n)},
        scratch_shapes=[pltpu.SemaphoreType.DMA((n,)), pltpu.SemaphoreType.DMA((n,))],
        compiler_params=pltpu.CompilerParams(collective_id=SIBLING_ID),
    )(*gs)


def _gather_all(buf, *, name):
    r, L = buf.shape
    vmem = pl.BlockSpec(memory_space=pltpu.VMEM)
    masks = tuple(range(1, N_DEV))

    def body(buf_ref, out_ref, send_sems, recv_sems):
        x, y, c, _ = _place()
        me = 4 * x + 2 * y + c
        out_ref[me] = buf_ref[...]
        sends = []
        for k, mask in enumerate(masks):
            px = 1 - x if mask & 4 else x
            py = 1 - y if mask & 2 else y
            pc = 1 - c if mask & 1 else c
            cp = pltpu.make_async_remote_copy(buf_ref, out_ref.at[me], send_sems.at[k], recv_sems.at[k],
                                              device_id=(px, py, pc), device_id_type=MESH)
            cp.start()
            sends.append(cp)
        for k, mask in enumerate(masks):
            px = 1 - x if mask & 4 else x
            py = 1 - y if mask & 2 else y
            pc = 1 - c if mask & 1 else c
            pltpu.make_async_remote_copy(buf_ref, out_ref.at[4 * px + 2 * py + pc], send_sems.at[k],
                                         recv_sems.at[k], device_id=(px, py, pc), device_id_type=MESH).wait_recv()
        for cp in sends:
            cp.wait_send()

    return pl.pallas_call(
        body, name=name, in_specs=[vmem], out_specs=vmem,
        out_shape=jax.ShapeDtypeStruct((N_DEV, r, L), buf.dtype),
        scratch_shapes=[pltpu.SemaphoreType.DMA((N_DEV - 1,)), pltpu.SemaphoreType.DMA((N_DEV - 1,))],
    )(buf)


def _pack(arrs, lanes, row_mult=8):
    flat = jnp.concatenate([a.reshape(-1) for a in arrs])
    rows = -(-flat.shape[0] // lanes)
    rows = -(-rows // row_mult) * row_mult
    flat = jnp.pad(flat, (0, rows * lanes - flat.shape[0]))
    return flat.reshape(rows, lanes)


def _unpack(buf, shapes):
    flat = buf.reshape(-1)
    out, pos = [], 0
    for s in shapes:
        n = 1
        for d in s:
            n *= d
        out.append(flat[pos:pos + n].reshape(s))
        pos += n
    return out


def kernel(x, mem, norm_mix_g, w_in, conv_a_w, conv_b_w, conv_b_bias, ln_b_g, ln_b_b, w_out, norm_x_g, norm_mem_g, w_q, w_kv, w_xo, norm_ffn_g, w_up, w_down, final_g, loss_target, m_norm_mix_g, m_w_in, m_conv_a_w, m_conv_b_w, m_conv_b_bias, m_ln_b_g, m_ln_b_b, m_w_out, m_norm_x_g, m_norm_mem_g, m_w_q, m_w_kv, m_w_xo, m_norm_ffn_g, m_w_up, m_w_down, m_final_g, v_norm_mix_g, v_w_in, v_conv_a_w, v_conv_b_w, v_conv_b_bias, v_ln_b_g, v_ln_b_b, v_w_out, v_norm_x_g, v_norm_mem_g, v_w_q, v_w_kv, v_w_xo, v_norm_ffn_g, v_w_up, v_w_down, v_final_g):
    W = dict(norm_mix_g=norm_mix_g, w_in=w_in, conv_a_w=conv_a_w, conv_b_w=conv_b_w, conv_b_bias=conv_b_bias,
             ln_b_g=ln_b_g, ln_b_b=ln_b_b, w_out=w_out, norm_x_g=norm_x_g, norm_mem_g=norm_mem_g, w_q=w_q,
             w_kv=w_kv, w_xo=w_xo, norm_ffn_g=norm_ffn_g, w_up=w_up, w_down=w_down, final_g=final_g)
    MO = dict(norm_mix_g=m_norm_mix_g, w_in=m_w_in, conv_a_w=m_conv_a_w, conv_b_w=m_conv_b_w,
              conv_b_bias=m_conv_b_bias, ln_b_g=m_ln_b_g, ln_b_b=m_ln_b_b, w_out=m_w_out, norm_x_g=m_norm_x_g,
              norm_mem_g=m_norm_mem_g, w_q=m_w_q, w_kv=m_w_kv, w_xo=m_w_xo, norm_ffn_g=m_norm_ffn_g,
              w_up=m_w_up, w_down=m_w_down, final_g=m_final_g)
    VO = dict(norm_mix_g=v_norm_mix_g, w_in=v_w_in, conv_a_w=v_conv_a_w, conv_b_w=v_conv_b_w,
              conv_b_bias=v_conv_b_bias, ln_b_g=v_ln_b_g, ln_b_b=v_ln_b_b, w_out=v_w_out, norm_x_g=v_norm_x_g,
              norm_mem_g=v_norm_mem_g, w_q=v_w_q, w_kv=v_w_kv, w_xo=v_w_xo, norm_ffn_g=v_norm_ffn_g,
              w_up=v_w_up, w_down=v_w_down, final_g=v_final_g)
    names = list(W.keys())
    depth = norm_mix_g.shape[0]
    assert depth == 2, "the exchange splits the weights into one layer per core of a chip"
    c_idx = lax.axis_index("c")
    chip_idx = 2 * lax.axis_index("x") + lax.axis_index("y")

    xs = x[0]
    ms = mem[0]
    tgt = loss_target[0]
    S, D = xs.shape
    c_a = conv_a_w.shape[-1] * N_CHIPS
    c_loc = conv_a_w.shape[-1]

    meta = jnp.stack([c_idx, chip_idx]).astype(jnp.int32)
    shard_axis = dict(BIG)
    own = {(l, n): W[n][l].astype(BF16) for l in range(depth) for n, _ in BIG}

    conv_local = _pack([conv_a_w, conv_b_w], 128)
    conv_all = _gather_all(conv_local, name="gather_conv_weights")
    order = [(l, gi) for l in range(depth) for gi in range(len(FWD_GROUPS))]
    src_groups = [[own[(l, n)] for n in FWD_GROUPS[gi]] for l, gi in order]
    started, gather_token = _ici_start(
        src_groups, [[jax.ShapeDtypeStruct((N_CHIPS, *s.shape), s.dtype) for s in g] for g in src_groups], True,
        name="gather_weights_start", after=conv_all)
    started = dict(zip(order, started))
    Wb = [dict() for _ in range(depth)]

    def weights_ready(l, gi, after):
        send_sems, recv_sems, srcs, lands = started[(l, gi)]
        srcs, lands = _ici_wait(send_sems, recv_sems, srcs, lands, True, after, name=f"gather_weights_l{l}_g{gi}_wait")
        full = _relay_halves(lands, name=f"gather_weights_l{l}_g{gi}_relay")
        for n, g, o in zip(FWD_GROUPS[gi], full, srcs):
            g = lax.dynamic_update_slice(g, o[None], (chip_idx, 0, 0))
            Wb[l][n] = g.reshape(-1, g.shape[-1]) if shard_axis[n] == 0 else g
    na = depth * K_A * c_loc
    nbw = depth * K_B * c_loc
    ca_parts, cb_parts = [], []
    for j in range(N_CHIPS):
        fl = conv_all[2 * j].reshape(-1)
        ca_parts.append(fl[:na].reshape(depth, K_A, c_loc))
        cb_parts.append(fl[na:na + nbw].reshape(depth, K_B, c_loc))
    conv_a_full = jnp.concatenate(ca_parts, axis=-1)
    conv_b_full = jnp.concatenate(cb_parts, axis=-1)

    saved = []
    h = xs
    for l in range(depth):
        wl = Wb[l]
        t = f"l{l}_"
        if l == 0:
            u = _rms_fwd(h, norm_mix_g[l:l + 1], name=t + "rms_mix", after=gather_token)
        weights_ready(l, 0, u)
        z = _mm(u, wl["w_in"], b_stack=True, name=t + "mm_in", bm=2048)
        y_a = _mixer_a_fwd(z, conv_a_full[l], name=t + "mixer_a")
        cb = _mixer_b_fwd(z, conv_b_full[l], conv_b_bias[l:l + 1], name=t + "mixer_b")
        yy = _ln_silu_fwd(cb, ln_b_g[l:l + 1], ln_b_b[l:l + 1], y_a, name=t + "ln_silu")
        weights_ready(l, 1, yy)
        h2, q_in = _mm(yy, wl["w_out"], res=h, epi="rms_fwd", norm=norm_x_g[l:l + 1], name=t + "mm_out")
        q = _mm(q_in, wl["w_q"], out_dtype=BF16, name=t + "mm_q", bm=2048)
        mn = _rms_fwd(ms, norm_mem_g[l:l + 1], name=t + "rms_mem")
        kv = _mm(mn, wl["w_kv"], b_stack=True, out_dtype=BF16, name=t + "mm_kv")
        o = _attn_fwd(q, kv, name=t + "attn", tq=ATTN_ROW_TILE)
        h3, u3 = _mm(o, wl["w_xo"], res=h2, epi="rms_fwd", norm=norm_ffn_g[l:l + 1], name=t + "mm_xo")
        weights_ready(l, 2, h3)
        a_pre, hh = _mm(u3, wl["w_up"], b_stack=True, out_dtype=BF16, epi="sqrelu", name=t + "mm_up", bm=2048)
        saved.append(dict(h=h, u=u, z=z, cb=cb, yy=yy, h2=h2, q_in=q_in, q=q, mn=mn, kv=kv, o=o, h3=h3,
                          u3=u3, a_pre=a_pre, hh=hh))
        if l + 1 < depth:
            h, u = _mm(hh, wl["w_down"], res=h3, epi="rms_fwd", norm=norm_mix_g[l + 1:l + 2], name=t + "mm_down")
        else:
            h = _mm(hh, wl["w_down"], res=h3, name=t + "mm_down")

    loss_vec, dh, dhb, d_final = _loss_head(h, final_g.reshape(1, D), tgt, name="loss_head")

    GW = [dict() for _ in range(depth)]
    GS = [dict() for _ in range(depth)]
    pending = []

    def reduce_start(l, gi):
        group = BWD_GROUPS[gi]

        def by_chip(g, n):
            return g if g.ndim == 3 else g.reshape(N_CHIPS, *W[n].shape[1:])

        gs = [by_chip(GW[l][n][0], n) for n in group]
        from_sibling = _swap_halves([by_chip(GW[l][n][1], n) for n in group],
                                    name=f"grad_swap_sibling_l{l}_g{gi}")
        prs = [_pair_sum(g, r, meta, name=f"grad_pair_sum_l{l}_{n}") for g, r, n in zip(gs, from_sibling, group)]
        (st,), token = _ici_start([prs], [prs], False, name=f"grad_scatter_chips_l{l}_g{gi}_start")
        pending.append((l, group, st))
        return token

    for l in reversed(range(depth)):
        wl, sv = Wb[l], saved[l]
        t = f"l{l}_b_"
        GW[l]["w_down"] = _mm(sv["hh"], dhb, ta=True, epi="with_bf16", name=t + "dw_down")
        da = _mm(dhb, wl["w_down"], tb=True, out_dtype=BF16, epi="dsqrelu", aux=sv["a_pre"], name=t + "d_hidden",
                 bm=2048)
        GW[l]["w_up"] = _mm(sv["u3"], da, ta=True, o_stack=N_CHIPS, epi="with_bf16", name=t + "dw_up")
        fused = dict(epi="rms_bwd", bm=512)
        dh, dhb, GS[l]["norm_ffn_g"] = _mm(da, wl["w_up"], tb=True, b_stack=True, res=dh,
                                           norm=(sv["h3"], norm_ffn_g[l:l + 1]), after=reduce_start(l, 0),
                                           name=t + "d_u3", **fused)
        GW[l]["w_xo"] = _mm(sv["o"], dhb, ta=True, epi="with_bf16", name=t + "dw_xo")
        d_o = _mm(dhb, wl["w_xo"], tb=True, out_dtype=BF16, name=t + "d_o", bm=2048)
        dq, dkv = _attn_bwd(sv["q"], sv["kv"], d_o, name=t + "attn", tq=ATTN_ROW_TILE)
        GW[l]["w_q"] = _mm(sv["q_in"], dq, ta=True, epi="with_bf16", name=t + "dw_q")
        dkvb = dkv.astype(BF16)
        GW[l]["w_kv"] = _mm(sv["mn"], dkvb, ta=True, o_stack=N_CHIPS, epi="with_bf16", name=t + "dw_kv")
        dmn = _mm(dkvb, wl["w_kv"], tb=True, b_stack=True, name=t + "d_mem")
        _, _, GS[l]["norm_mem_g"] = _rms_bwd(ms, norm_mem_g[l:l + 1], dmn, None, name=t + "rms_mem")
        dh, dhb, GS[l]["norm_x_g"] = _mm(dq, wl["w_q"], tb=True, res=dh, norm=(sv["h2"], norm_x_g[l:l + 1]),
                                         after=reduce_start(l, 1), name=t + "d_q_in", **fused)
        GW[l]["w_out"] = _mm(sv["yy"], dhb, ta=True, epi="with_bf16", name=t + "dw_out")
        dyy = _mm(dhb, wl["w_out"], tb=True, name=t + "d_y", bm=2048)
        dcb, GS[l]["ln_b_g"], GS[l]["ln_b_b"] = _ln_silu_bwd(sv["cb"], ln_b_g[l:l + 1], ln_b_b[l:l + 1], dyy, 1,
                                                             name=t + "ln_silu")
        db_, dc_, dh_, GS[l]["conv_a_w"] = _mixer_a_bwd(sv["z"], conv_a_full[l], dyy, name=t + "mixer_a")
        dv_, dg_, GS[l]["conv_b_w"], GS[l]["conv_b_bias"] = _mixer_b_bwd(sv["z"], conv_b_full[l], dcb,
                                                                         name=t + "mixer_b")
        dz = jnp.concatenate([db_, dc_, dh_, dv_, dg_], axis=1)
        GW[l]["w_in"] = _mm(sv["u"], dz, ta=True, o_stack=N_CHIPS, epi="with_bf16", name=t + "dw_in")
        dh, dhb, GS[l]["norm_mix_g"] = _mm(dz, wl["w_in"], tb=True, b_stack=True, res=dh,
                                           norm=(sv["h"], norm_mix_g[l:l + 1]), after=reduce_start(l, 2),
                                           name=t + "d_u", **fused)
    grad_x = dh[None]

    after = GS[0]["norm_mix_g"]
    keys, halves = [], []
    for l, group, (send_sems, recv_sems, srcs, lands) in pending:
        prs, pieces = _ici_wait(send_sems, recv_sems, srcs, lands, False, after,
                                name=f"grad_scatter_chips_l{l}_{group[0]}_wait")
        for n, p, q in zip(group, prs, pieces):
            keys.append((l, n))
            halves.append(_chip_sum(p, q, meta, name=f"grad_chip_sum_l{l}_{n}"))
        after = halves[-1]
    reduced = dict(zip(keys, _share_halves(halves, name="grad_share_sibling")))

    grads, deltas, new_m, new_v = {}, {}, {}, {}
    for n, _ in BIG:
        grads[n], deltas[n], new_m[n], new_v[n] = _adamw_layers(W[n], reduced[(0, n)], reduced[(1, n)], MO[n], VO[n],
                                                               name="adamw_" + n)

    small = [n for n in names if n not in dict(BIG)]
    full_shapes = {n: ((depth, W[n].shape[1], c_a) if n in ("conv_a_w", "conv_b_w") else W[n].shape)
                   for n in small}

    def small_grad(n):
        if n == "final_g":
            return d_final.reshape(W[n].shape)
        return jnp.stack([GS[l][n].reshape(full_shapes[n][1:]) for l in range(depth)])

    part = _pack([small_grad(n) for n in small] + [loss_vec[0, :1]], LANES)
    everyone = _gather_all(part, name="gather_small_grads")
    total = _sum_leading(everyone, name="small_grad_sum")
    *small_sums, loss = _unpack(total, [full_shapes[n] for n in small] + [()])
    full_grads = dict(zip(small, small_sums))
    for n in ("conv_a_w", "conv_b_w"):
        full_grads[n] = lax.dynamic_slice_in_dim(full_grads[n], chip_idx * c_loc, c_loc, axis=2)
    shapes = [W[n].shape for n in small]
    d_s, m_s, v_s = _adamw(_pack([W[n] for n in small], 128), _pack([full_grads[n] for n in small], 128),
                           _pack([MO[n] for n in small], 128), _pack([VO[n] for n in small], 128),
                           name="adamw_small")
    for n, d, nm, nv in zip(small, _unpack(d_s, shapes), _unpack(m_s, shapes), _unpack(v_s, shapes)):
        grads[n], deltas[n], new_m[n], new_v[n] = full_grads[n], d, nm, nv

    return (loss, grad_x, *[grads[n] for n in names], *[deltas[n] for n in names],
            *[new_m[n] for n in names], *[new_v[n] for n in names])
```

```python
import jax
import jax.numpy as jnp
from jax import lax
from jax.experimental import pallas as pl
from jax.experimental.pallas import tpu as pltpu

F32 = jnp.float32
BF16 = jnp.bfloat16
MESH = pl.DeviceIdType.MESH

EPS = 1e-6
N_XHEADS = 4
K_A = 3
K_B = 31
PAD_A = 8
PAD_B = 32
CONV_CHUNK = 256
ROW_TILE = 512
ATTN_ROW_TILE = 1024
LANES = 1024
VMEM_LIMIT_BYTES = 56 * 1024 * 1024

ADAM_LR = 0.001
ADAM_B1 = 0.9
ADAM_B2 = 0.999
ADAM_EPS = 1e-08
ADAM_WD = 0.01
ADAM_STEP = 10

BIG = (("w_in", 1), ("w_out", 0), ("w_q", 0), ("w_kv", 1), ("w_xo", 0), ("w_up", 1), ("w_down", 0))
FWD_GROUPS = (("w_in",), ("w_out", "w_q", "w_kv", "w_xo"), ("w_up", "w_down"))
BWD_GROUPS = (("w_down", "w_up", "w_xo", "w_q", "w_kv"), ("w_out", "w_in"))
N_CHIPS = 4
N_DEV = 8


def _params(sem=None):
    return pltpu.CompilerParams(dimension_semantics=sem, vmem_limit_bytes=VMEM_LIMIT_BYTES)


def _pick(cands, n):
    for c in cands:
        if c <= n and n % c == 0:
            return c
    return n


def _mm(a, b, *, name, ta=False, tb=False, out_dtype=F32, res=None, epi=None, aux=None, norm=None, after=None,
        b_stack=False, o_stack=0, bm=1024, bn=1024, bk=1024):
    if ta:
        K, M = a.shape
    else:
        M, K = a.shape
    if b_stack:
        n_st, d1, d2 = b.shape
        N, kb = (d1, d2) if tb else (n_st * d2, d1)
        assert K == (n_st * d2 if tb else d1), (name, a.shape, b.shape)
    else:
        N = b.shape[0] if tb else b.shape[1]
    n_unit = b.shape[2] if (b_stack and not tb) else (N // o_stack if o_stack else N)
    k_unit = b.shape[2] if (b_stack and tb) else K
    bm = _pick((bm, 512, 256, 128), M)
    bn = _pick((bn, 512, 640, 256, 384, 128), n_unit)
    bk = _pick((bk, 640, 512, 256, 128), k_unit)
    assert M % bm == 0 and N % bn == 0 and K % bk == 0, (name, M, N, K)
    nk = K // bk
    per_n = n_unit // bn
    per_k = k_unit // bk
    a_spec = (pl.BlockSpec((bk, bm), lambda i, j, k: (k, i)) if ta
              else pl.BlockSpec((bm, bk), lambda i, j, k: (i, k)))
    if b_stack and tb:
        b_spec = pl.BlockSpec((None, bn, bk), lambda i, j, k: (k // per_k, j, k % per_k))
    elif b_stack:
        b_spec = pl.BlockSpec((None, bk, bn), lambda i, j, k: (j // per_n, k, j % per_n))
    elif tb:
        b_spec = pl.BlockSpec((bn, bk), lambda i, j, k: (j, k))
    else:
        b_spec = pl.BlockSpec((bk, bn), lambda i, j, k: (k, j))
    o_spec = pl.BlockSpec((bm, bn), lambda i, j, k: (i, j))
    dims = (((0 if ta else 1,), (1 if tb else 0,)), ((), ()))
    ins, in_specs = [a, b], [a_spec, b_spec]
    if res is not None:
        ins.append(res)
        in_specs.append(o_spec)
    if aux is not None:
        ins.append(aux)
        in_specs.append(o_spec)
    vec_spec = pl.BlockSpec((1, bn), lambda i, j, k: (0, j))
    if epi == "rms_fwd":
        assert bn == N, (name, bn, N)
        ins.append(norm)
        in_specs.append(vec_spec)
    elif epi == "rms_bwd":
        assert bn == N, (name, bn, N)
        ins += [norm[0], norm[1]]
        in_specs += [o_spec, vec_spec]
    n_norm = {"rms_fwd": 1, "rms_bwd": 2}.get(epi, 0)
    if after is not None:
        ins.append(after)
        in_specs.append(pl.BlockSpec(memory_space=pl.ANY))
    n_out = {"sqrelu": 2, "rms_fwd": 2, "rms_bwd": 3, "with_bf16": 2}.get(epi, 1)
    out_shape = [jax.ShapeDtypeStruct((M, N), out_dtype)] * n_out
    out_specs = [o_spec] * n_out
    if epi in ("rms_fwd", "with_bf16"):
        out_shape = [jax.ShapeDtypeStruct((M, N), F32), jax.ShapeDtypeStruct((M, N), BF16)]
    elif epi == "rms_bwd":
        out_shape = [jax.ShapeDtypeStruct((M, N), F32), jax.ShapeDtypeStruct((M, N), BF16),
                     jax.ShapeDtypeStruct((1, N), F32)]
        out_specs = [o_spec, o_spec, vec_spec]
    if o_stack:
        assert epi in (None, "with_bf16") and res is None and aux is None
        out_shape = [jax.ShapeDtypeStruct((o_stack, M, N // o_stack), s.dtype) for s in out_shape]
        out_specs = [pl.BlockSpec((None, bm, bn), lambda i, j, k: (j // per_n, i, j % per_n))] * n_out

    def body(*refs):
        a_ref, b_ref = refs[0], refs[1]
        pos = 2
        res_ref = aux_ref = None
        if res is not None:
            res_ref = refs[pos]
            pos += 1
        if aux is not None:
            aux_ref = refs[pos]
            pos += 1
        norm_refs = refs[pos:pos + n_norm]
        pos += n_norm + (after is not None)
        outs = refs[pos:pos + n_out]

        def product():
            return lax.dot_general(a_ref[...], b_ref[...], dims, preferred_element_type=F32)

        def finish(r):
            if epi == "rms_bwd":
                x_ref, g_ref = norm_refs
                xv = x_ref[...]
                rs = lax.rsqrt(jnp.mean(xv * xv, axis=-1, keepdims=True) + EPS)
                xh = xv * rs
                dxh = r * g_ref[...]
                dx = rs * (dxh - xh * jnp.mean(dxh * xh, axis=-1, keepdims=True))
                if res_ref is not None:
                    dx = dx + res_ref[...]
                outs[0][...] = dx
                outs[1][...] = dx.astype(BF16)

                @pl.when(pl.program_id(0) == 0)
                def _():
                    outs[2][...] = jnp.zeros_like(outs[2])

                outs[2][...] += jnp.sum(r * xh, axis=0, keepdims=True)
                return
            if res_ref is not None:
                r = r + res_ref[...]
            if epi == "rms_fwd":
                outs[0][...] = r
                rs = lax.rsqrt(jnp.mean(r * r, axis=-1, keepdims=True) + EPS)
                outs[1][...] = (r * rs * norm_refs[0][...]).astype(BF16)
            elif epi == "with_bf16":
                outs[0][...] = r
                outs[1][...] = r.astype(BF16)
            elif epi == "sqrelu":
                outs[0][...] = r.astype(out_dtype)
                rl = jnp.maximum(r, 0.0)
                outs[1][...] = (rl * rl).astype(out_dtype)
            elif epi == "dsqrelu":
                outs[0][...] = (r * (2.0 * jnp.maximum(aux_ref[...].astype(F32), 0.0))).astype(out_dtype)
            else:
                outs[0][...] = r.astype(out_dtype)

        if nk == 1:
            finish(product())
            return
        acc = refs[pos + n_out]
        k = pl.program_id(2)

        @pl.when(k == 0)
        def _():
            acc[...] = product()

        @pl.when(jnp.logical_and(k > 0, k < nk - 1))
        def _():
            acc[...] += product()

        @pl.when(k == nk - 1)
        def _():
            finish(acc[...] + product())

    out = pl.pallas_call(
        body, name=name, grid=(M // bm, N // bn, nk),
        in_specs=in_specs, out_specs=out_specs, out_shape=out_shape,
        scratch_shapes=[pltpu.VMEM((bm, bn), F32)] if nk > 1 else [],
        compiler_params=_params(("arbitrary",) * 3 if epi == "rms_bwd" else ("parallel", "parallel", "arbitrary")),
    )(*ins)
    return out if n_out > 1 else out[0]


def _row_tile(rows):
    return min(ROW_TILE, rows)


def _rms_fwd(x, g, *, name, after=None):
    S, D = x.shape
    tr = _row_tile(S)

    def body(x_ref, g_ref, *rest):
        o_ref = rest[-1]
        xv = x_ref[...]
        r = lax.rsqrt(jnp.mean(xv * xv, axis=-1, keepdims=True) + EPS)
        o_ref[...] = (xv * r * g_ref[...]).astype(BF16)

    deps = [] if after is None else [after]
    return pl.pallas_call(
        body, name=name, grid=(S // tr,),
        in_specs=[pl.BlockSpec((tr, D), lambda i: (i, 0)), pl.BlockSpec((1, D), lambda i: (0, 0))]
        + [ANY] * len(deps),
        out_specs=pl.BlockSpec((tr, D), lambda i: (i, 0)),
        out_shape=jax.ShapeDtypeStruct((S, D), BF16),
        compiler_params=_params(("parallel",)),
    )(x, g, *deps)


def _rms_bwd(x, g, du, dres, *, name):
    S, D = x.shape
    tr = _row_tile(S)
    has_res = dres is not None

    def body(*refs):
        dx_ref, dxb_ref, dg_ref = refs[-3:]
        if has_res:
            x_ref, g_ref, du_ref, dres_ref = refs[:4]
        else:
            x_ref, g_ref, du_ref = refs[:3]
        xv = x_ref[...]
        r = lax.rsqrt(jnp.mean(xv * xv, axis=-1, keepdims=True) + EPS)
        xh = xv * r
        dy = du_ref[...]
        dxh = dy * g_ref[...]
        dx = r * (dxh - xh * jnp.mean(dxh * xh, axis=-1, keepdims=True))
        if has_res:
            dx = dx + dres_ref[...]
        dx_ref[...] = dx
        dxb_ref[...] = dx.astype(BF16)

        @pl.when(pl.program_id(0) == 0)
        def _():
            dg_ref[...] = jnp.zeros_like(dg_ref)

        dg_ref[...] += jnp.sum(dy * xh, axis=0, keepdims=True)

    row = pl.BlockSpec((tr, D), lambda i: (i, 0))
    vec = pl.BlockSpec((1, D), lambda i: (0, 0))
    ins = [x, g, du] + ([dres] if has_res else [])
    in_specs = [row, vec, row] + ([row] if has_res else [])
    return pl.pallas_call(
        body, name=name, grid=(S // tr,),
        in_specs=in_specs, out_specs=[row, row, vec],
        out_shape=[jax.ShapeDtypeStruct((S, D), F32), jax.ShapeDtypeStruct((S, D), BF16),
                   jax.ShapeDtypeStruct((1, D), F32)],
        compiler_params=_params(("arbitrary",)),
    )(*ins)


def _loss_head(h, g, target, *, name):
    S, D = h.shape
    tr = _row_tile(S)

    def body(x_ref, g_ref, t_ref, loss_ref, dx_ref, dxb_ref, dg_ref):
        xv = x_ref[...]
        r = lax.rsqrt(jnp.mean(xv * xv, axis=-1, keepdims=True) + EPS)
        xh = xv * r
        gv = g_ref[...]
        err = xh * gv - t_ref[...]
        part = 0.5 * jnp.sum(jnp.mean(err * err, axis=-1, keepdims=True), axis=0, keepdims=True)
        dy = err * (1.0 / D)
        dxh = dy * gv
        dx = r * (dxh - xh * jnp.mean(dxh * xh, axis=-1, keepdims=True))
        dx_ref[...] = dx
        dxb_ref[...] = dx.astype(BF16)

        @pl.when(pl.program_id(0) == 0)
        def _():
            dg_ref[...] = jnp.zeros_like(dg_ref)
            loss_ref[...] = jnp.zeros_like(loss_ref)

        dg_ref[...] += jnp.sum(dy * xh, axis=0, keepdims=True)
        loss_ref[...] += jnp.broadcast_to(part, loss_ref.shape)

    row = pl.BlockSpec((tr, D), lambda i: (i, 0))
    vec = pl.BlockSpec((1, D), lambda i: (0, 0))
    return pl.pallas_call(
        body, name=name, grid=(S // tr,),
        in_specs=[row, vec, row],
        out_specs=[pl.BlockSpec((1, 128), lambda i: (0, 0)), row, row, vec],
        out_shape=[jax.ShapeDtypeStruct((1, 128), F32), jax.ShapeDtypeStruct((S, D), F32),
                   jax.ShapeDtypeStruct((S, D), BF16), jax.ShapeDtypeStruct((1, D), F32)],
        compiler_params=_params(("arbitrary",)),
    )(h, g, target)


def _sigmoid(x):
    return 1.0 / (1.0 + jnp.exp(-x))


def _ln_silu_fwd(cb, g, b, into, *, name):
    S, C = cb.shape
    tr = _row_tile(S)

    def body(x_ref, g_ref, b_ref, into_ref, o_ref):
        xv = x_ref[...]
        mu = jnp.mean(xv, axis=-1, keepdims=True)
        xc = xv - mu
        rs = lax.rsqrt(jnp.mean(xc * xc, axis=-1, keepdims=True) + EPS)
        l = xc * rs * g_ref[...] + b_ref[...]
        o_ref[...] = (l * _sigmoid(l)).astype(BF16)

    row = pl.BlockSpec((tr, C), lambda i: (i, 0))
    vec = pl.BlockSpec((1, C), lambda i: (0, 0))
    return pl.pallas_call(
        body, name=name, grid=(S // tr,), in_specs=[row, vec, vec, pl.BlockSpec(memory_space=pl.ANY)],
        out_specs=pl.BlockSpec((tr, C), lambda i: (i, 1)),
        out_shape=jax.ShapeDtypeStruct(into.shape, BF16), input_output_aliases={3: 0},
        compiler_params=_params(("parallel",)),
    )(cb, g, b, into)


def _ln_silu_bwd(cb, g, b, dy, col_block, *, name):
    S, C = cb.shape
    tr = _row_tile(S)

    def body(x_ref, g_ref, b_ref, dy_ref, dx_ref, dg_ref, db_ref):
        xv = x_ref[...]
        mu = jnp.mean(xv, axis=-1, keepdims=True)
        xc = xv - mu
        rs = lax.rsqrt(jnp.mean(xc * xc, axis=-1, keepdims=True) + EPS)
        xh = xc * rs
        gv = g_ref[...]
        l = xh * gv + b_ref[...]
        sg = _sigmoid(l)
        dl = dy_ref[...] * (sg + l * sg * (1.0 - sg))
        dxh = dl * gv
        dx_ref[...] = rs * (dxh - jnp.mean(dxh, axis=-1, keepdims=True)
                            - xh * jnp.mean(dxh * xh, axis=-1, keepdims=True))

        @pl.when(pl.program_id(0) == 0)
        def _():
            dg_ref[...] = jnp.zeros_like(dg_ref)
            db_ref[...] = jnp.zeros_like(db_ref)

        dg_ref[...] += jnp.sum(dl * xh, axis=0, keepdims=True)
        db_ref[...] += jnp.sum(dl, axis=0, keepdims=True)

    row = pl.BlockSpec((tr, C), lambda i: (i, 0))
    vec = pl.BlockSpec((1, C), lambda i: (0, 0))
    return pl.pallas_call(
        body, name=name, grid=(S // tr,),
        in_specs=[row, vec, vec, pl.BlockSpec((tr, C), lambda i: (i, col_block))],
        out_specs=[row, vec, vec],
        out_shape=[jax.ShapeDtypeStruct((S, C), F32), jax.ShapeDtypeStruct((1, C), F32),
                   jax.ShapeDtypeStruct((1, C), F32)],
        compiler_params=_params(("arbitrary",)),
    )(cb, g, b, dy)


def _attn_fwd(q, kv, *, name, tq=ROW_TILE):
    S, D = q.shape
    M = kv.shape[0]
    hd = D // N_XHEADS
    scale = 1.0 / float(hd) ** 0.5
    tq = min(tq, S)

    def body(q_ref, k_ref, v_ref, o_ref):
        for h in range(N_XHEADS):
            cols = slice(h * hd, (h + 1) * hd)
            s = lax.dot_general(q_ref[:, cols], k_ref[:, cols], (((1,), (1,)), ((), ())),
                                preferred_element_type=F32) * scale
            e = jnp.exp(s - jnp.max(s, axis=-1, keepdims=True))
            p = e / jnp.sum(e, axis=-1, keepdims=True)
            o = jnp.dot(p.astype(BF16), v_ref[:, cols], preferred_element_type=F32)
            o_ref[:, cols] = o.astype(BF16)

    return pl.pallas_call(
        body, name=name, grid=(S // tq,),
        in_specs=[pl.BlockSpec((tq, D), lambda i: (i, 0)), pl.BlockSpec((M, D), lambda i: (0, 0)),
                  pl.BlockSpec((M, D), lambda i: (0, 1))],
        out_specs=pl.BlockSpec((tq, D), lambda i: (i, 0)),
        out_shape=jax.ShapeDtypeStruct((S, D), BF16), compiler_params=_params(("parallel",)),
    )(q, kv, kv)


def _attn_bwd(q, kv, do, *, name, tq=ROW_TILE):
    S, D = q.shape
    M = kv.shape[0]
    hd = D // N_XHEADS
    scale = 1.0 / float(hd) ** 0.5
    tq = min(tq, S)

    def body(q_ref, k_ref, v_ref, do_ref, dq_ref, dkv_ref):
        @pl.when(pl.program_id(0) == 0)
        def _():
            dkv_ref[...] = jnp.zeros_like(dkv_ref)

        for h in range(N_XHEADS):
            cols = slice(h * hd, (h + 1) * hd)
            vcols = slice(D + h * hd, D + (h + 1) * hd)
            qh, kh, vh, doh = q_ref[:, cols], k_ref[:, cols], v_ref[:, cols], do_ref[:, cols]
            s = lax.dot_general(qh, kh, (((1,), (1,)), ((), ())), preferred_element_type=F32) * scale
            e = jnp.exp(s - jnp.max(s, axis=-1, keepdims=True))
            p = e / jnp.sum(e, axis=-1, keepdims=True)
            pb = p.astype(BF16)
            dp = lax.dot_general(doh, vh, (((1,), (1,)), ((), ())), preferred_element_type=F32)
            ds = (p * (dp - jnp.sum(dp * p, axis=-1, keepdims=True)) * scale).astype(BF16)
            dq_ref[:, cols] = jnp.dot(ds, kh, preferred_element_type=F32).astype(BF16)
            dkv_ref[:, cols] += lax.dot_general(ds, qh, (((0,), (0,)), ((), ())), preferred_element_type=F32)
            dkv_ref[:, vcols] += lax.dot_general(pb, doh, (((0,), (0,)), ((), ())), preferred_element_type=F32)

    row = pl.BlockSpec((tq, D), lambda i: (i, 0))
    return pl.pallas_call(
        body, name=name, grid=(S // tq,),
        in_specs=[row, pl.BlockSpec((M, D), lambda i: (0, 0)), pl.BlockSpec((M, D), lambda i: (0, 1)), row],
        out_specs=[row, pl.BlockSpec((M, 2 * D), lambda i: (0, 0))],
        out_shape=[jax.ShapeDtypeStruct((S, D), BF16), jax.ShapeDtypeStruct((M, 2 * D), F32)],
        compiler_params=_params(("arbitrary",)),
    )(q, kv, kv, do)


def _delayed(win, j, pad):
    return (win if j == 0 else pltpu.roll(win, j, 0))[pad:, :]


def _advanced(win, j, ch):
    return (win if j == 0 else pltpu.roll(win, win.shape[0] - j, 0))[:ch, :]


def _mixer_a_fwd(z, w, *, name):
    S = z.shape[0]
    C = w.shape[1]
    nb = C // 128
    ch = min(CONV_CHUNK, S)

    def body(b_ref, c_ref, h_ref, w_ref, y_ref, xp):
        xp[0:PAD_A, :] = jnp.zeros((PAD_A, 128), F32)
        xp[PAD_A:, :] = c_ref[...] * h_ref[...]

        def chunk(i, carry):
            base = pl.multiple_of(i * ch, ch)
            win = xp[pl.ds(base, ch + PAD_A), :]
            acc = _delayed(win, 0, PAD_A) * w_ref[K_A - 1:K_A, :]
            for j in range(1, K_A):
                acc = acc + _delayed(win, j, PAD_A) * w_ref[K_A - 1 - j:K_A - j, :]
            y_ref[pl.ds(base, ch), :] = (b_ref[pl.ds(base, ch), :] * acc).astype(BF16)
            return carry

        lax.fori_loop(0, S // ch, chunk, 0)

    def col(g):
        return pl.BlockSpec((S, 128), lambda j: (0, g * nb + j))

    return pl.pallas_call(
        body, name=name, grid=(nb,),
        in_specs=[col(0), col(1), col(2), pl.BlockSpec((K_A, 128), lambda j: (0, j))],
        out_specs=pl.BlockSpec((S, 128), lambda j: (0, j)),
        out_shape=jax.ShapeDtypeStruct((S, 2 * C), BF16),
        scratch_shapes=[pltpu.VMEM((PAD_A + S, 128), F32)],
        compiler_params=_params(("parallel",)),
    )(z, z, z, w)


def _mixer_a_bwd(z, w, dy, *, name):
    S = z.shape[0]
    C = w.shape[1]
    nb = C // 128
    ch = min(CONV_CHUNK, S)

    def body(b_ref, c_ref, h_ref, w_ref, dy_ref, db_ref, dc_ref, dh_ref, dw_ref, xp, dp):
        xp[0:PAD_A, :] = jnp.zeros((PAD_A, 128), F32)
        xp[PAD_A:, :] = c_ref[...] * h_ref[...]
        dp[S:, :] = jnp.zeros((PAD_A, 128), F32)
        dw_ref[...] = jnp.zeros_like(dw_ref)

        def chunk(i, carry):
            base = pl.multiple_of(i * ch, ch)
            win = xp[pl.ds(base, ch + PAD_A), :]
            dya = dy_ref[pl.ds(base, ch), :]
            dcv = dya * b_ref[pl.ds(base, ch), :]
            dp[pl.ds(base, ch), :] = dcv
            acc = None
            for j in range(K_A):
                xs = _delayed(win, j, PAD_A)
                k = K_A - 1 - j
                term = xs * w_ref[k:k + 1, :]
                acc = term if acc is None else acc + term
                dw_ref[k:k + 1, :] += jnp.sum(dcv * xs, axis=0, keepdims=True)
            db_ref[pl.ds(base, ch), :] = (dya * acc).astype(BF16)
            return carry

        lax.fori_loop(0, S // ch, chunk, 0)

        def chunk2(i, carry):
            base = pl.multiple_of(i * ch, ch)
            win = dp[pl.ds(base, ch + PAD_A), :]
            acc = None
            for j in range(K_A):
                term = _advanced(win, j, ch) * w_ref[K_A - 1 - j:K_A - j, :]
                acc = term if acc is None else acc + term
            dc_ref[pl.ds(base, ch), :] = (acc * h_ref[pl.ds(base, ch), :]).astype(BF16)
            dh_ref[pl.ds(base, ch), :] = (acc * c_ref[pl.ds(base, ch), :]).astype(BF16)
            return carry

        lax.fori_loop(0, S // ch, chunk2, 0)

    def col(g):
        return pl.BlockSpec((S, 128), lambda j: (0, g * nb + j))

    out_col = pl.BlockSpec((S, 128), lambda j: (0, j))
    wspec = pl.BlockSpec((K_A, 128), lambda j: (0, j))
    return pl.pallas_call(
        body, name=name, grid=(nb,),
        in_specs=[col(0), col(1), col(2), wspec, out_col],
        out_specs=[out_col, out_col, out_col, wspec],
        out_shape=[jax.ShapeDtypeStruct((S, C), BF16)] * 3 + [jax.ShapeDtypeStruct((K_A, C), F32)],
        scratch_shapes=[pltpu.VMEM((PAD_A + S, 128), F32), pltpu.VMEM((S + PAD_A, 128), F32)],
        compiler_params=_params(("parallel",)),
    )(z, z, z, w, dy)


def _mixer_b_fwd(z, w, bias, *, name):
    S = z.shape[0]
    C = w.shape[1]
    nb = C // 128
    ch = min(CONV_CHUNK, S)

    def body(v_ref, g_ref, w_ref, bias_ref, cb_ref, xp):
        xp[0:PAD_B, :] = jnp.zeros((PAD_B, 128), F32)
        xp[PAD_B:, :] = v_ref[...] * _sigmoid(g_ref[...])

        def chunk(i, carry):
            base = pl.multiple_of(i * ch, ch)
            win = xp[pl.ds(base, ch + PAD_B), :]
            acc = None
            for j in range(K_B):
                term = _delayed(win, j, PAD_B) * w_ref[K_B - 1 - j:K_B - j, :]
                acc = term if acc is None else acc + term
            cb_ref[pl.ds(base, ch), :] = acc + bias_ref[...]
            return carry

        lax.fori_loop(0, S // ch, chunk, 0)

    def col(g):
        return pl.BlockSpec((S, 128), lambda j: (0, g * nb + j))

    return pl.pallas_call(
        body, name=name, grid=(nb,),
        in_specs=[col(3), col(4), pl.BlockSpec((K_B, 128), lambda j: (0, j)),
                  pl.BlockSpec((1, 128), lambda j: (0, j))],
        out_specs=pl.BlockSpec((S, 128), lambda j: (0, j)),
        out_shape=jax.ShapeDtypeStruct((S, C), F32),
        scratch_shapes=[pltpu.VMEM((PAD_B + S, 128), F32)],
        compiler_params=_params(("parallel",)),
    )(z, z, w, bias)


def _mixer_b_bwd(z, w, dcb, *, name):
    S = z.shape[0]
    C = w.shape[1]
    nb = C // 128
    ch = min(CONV_CHUNK, S)

    def body(v_ref, g_ref, w_ref, dcb_ref, dv_ref, dg_ref, dw_ref, dbias_ref, xp, dp):
        xp[0:PAD_B, :] = jnp.zeros((PAD_B, 128), F32)
        xp[PAD_B:, :] = v_ref[...] * _sigmoid(g_ref[...])
        dp[0:S, :] = dcb_ref[...]
        dp[S:, :] = jnp.zeros((PAD_B, 128), F32)
        dw_ref[...] = jnp.zeros_like(dw_ref)
        dbias_ref[...] = jnp.sum(dcb_ref[...], axis=0, keepdims=True)

        def chunk(i, carry):
            base = pl.multiple_of(i * ch, ch)
            win = xp[pl.ds(base, ch + PAD_B), :]
            d = dcb_ref[pl.ds(base, ch), :]
            for j in range(K_B):
                k = K_B - 1 - j
                dw_ref[k:k + 1, :] += jnp.sum(d * _delayed(win, j, PAD_B), axis=0, keepdims=True)
            return carry

        lax.fori_loop(0, S // ch, chunk, 0)

        def chunk2(i, carry):
            base = pl.multiple_of(i * ch, ch)
            win = dp[pl.ds(base, ch + PAD_B), :]
            acc = None
            for j in range(K_B):
                term = _advanced(win, j, ch) * w_ref[K_B - 1 - j:K_B - j, :]
                acc = term if acc is None else acc + term
            sg = _sigmoid(g_ref[pl.ds(base, ch), :])
            vv = v_ref[pl.ds(base, ch), :]
            dv_ref[pl.ds(base, ch), :] = (acc * sg).astype(BF16)
            dg_ref[pl.ds(base, ch), :] = (acc * vv * sg * (1.0 - sg)).astype(BF16)
            return carry

        lax.fori_loop(0, S // ch, chunk2, 0)

    def col(g):
        return pl.BlockSpec((S, 128), lambda j: (0, g * nb + j))

    out_col = pl.BlockSpec((S, 128), lambda j: (0, j))
    wspec = pl.BlockSpec((K_B, 128), lambda j: (0, j))
    bspec = pl.BlockSpec((1, 128), lambda j: (0, j))
    return pl.pallas_call(
        body, name=name, grid=(nb,),
        in_specs=[col(3), col(4), wspec, out_col],
        out_specs=[out_col, out_col, wspec, bspec],
        out_shape=[jax.ShapeDtypeStruct((S, C), BF16)] * 2
        + [jax.ShapeDtypeStruct((K_B, C), F32), jax.ShapeDtypeStruct((1, C), F32)],
        scratch_shapes=[pltpu.VMEM((PAD_B + S, 128), F32), pltpu.VMEM((S + PAD_B, 128), F32)],
        compiler_params=_params(("parallel",)),
    )(z, z, w, dcb)


def _ew_tile(R):
    for t in (512, 256, 128, 64, 32, 16, 8):
        if R % t == 0:
            return t
    return R


def _pair_sum(g, r, meta, *, name):
    n, a, b = g.shape
    ah = a // 2
    tr = _pick((256, 128, 64, 32, 16), ah)
    nh = ah // tr

    def body(meta_ref, g_ref, r_ref, o_ref):
        o_ref[...] = (g_ref[...] + r_ref[...].astype(F32)).astype(BF16)

    half = pl.BlockSpec((None, tr, b), lambda j, i, meta_ref: (j, i, 0))
    return pl.pallas_call(
        body, name=name,
        grid_spec=pltpu.PrefetchScalarGridSpec(
            num_scalar_prefetch=1, grid=(n, nh),
            in_specs=[pl.BlockSpec((None, tr, b), lambda j, i, meta_ref: (j, meta_ref[0] * nh + i, 0)), half],
            out_specs=half),
        out_shape=jax.ShapeDtypeStruct((n, ah, b), BF16), compiler_params=_params(("parallel", "parallel")),
    )(meta, g, r)


def _chip_sum(p, q, meta, *, name):
    n, ah, b = p.shape
    tr = _pick((256, 128, 64, 32, 16), ah)
    nh = ah // tr

    def body(meta_ref, p_ref, q1_ref, q2_ref, q3_ref, o_ref):
        o_ref[...] = ((p_ref[...].astype(F32) + q1_ref[...].astype(F32)) + q2_ref[...].astype(F32)
                      ) + q3_ref[...].astype(F32)

    def piece(mask):
        return pl.BlockSpec((None, tr, b), lambda i, meta_ref: (meta_ref[1] ^ mask, i, 0))

    return pl.pallas_call(
        body, name=name,
        grid_spec=pltpu.PrefetchScalarGridSpec(
            num_scalar_prefetch=1, grid=(nh,),
            in_specs=[piece(0), piece(1), piece(2), piece(3)],
            out_specs=pl.BlockSpec((tr, b), lambda i, meta_ref: (meta_ref[0] * nh + i, 0))),
        out_shape=jax.ShapeDtypeStruct((2 * ah, b), F32), compiler_params=_params(("parallel",)),
    )(meta, p, q, q, q)


def _sum_leading(x, *, name):
    n, R, C = x.shape
    tr = _ew_tile(R)

    def body(x_ref, o_ref):
        acc = x_ref[0].astype(F32)
        for k in range(1, n):
            acc = acc + x_ref[k].astype(F32)
        o_ref[...] = acc

    return pl.pallas_call(
        body, name=name, grid=(R // tr,),
        in_specs=[pl.BlockSpec((n, tr, C), lambda i: (0, i, 0))],
        out_specs=pl.BlockSpec((tr, C), lambda i: (i, 0)),
        out_shape=jax.ShapeDtypeStruct((R, C), F32), compiler_params=_params(("parallel",)),
    )(x)


def _adamw(w, g, m, v, *, name):
    R, C = w.shape
    tr = _ew_tile(R)

    def body(w_ref, g_ref, m_ref, v_ref, d_ref, nm_ref, nv_ref):
        gv = g_ref[...]
        nm = ADAM_B1 * m_ref[...] + (1.0 - ADAM_B1) * gv
        nv = ADAM_B2 * v_ref[...] + (1.0 - ADAM_B2) * (gv * gv)
        m_hat = nm / (1.0 - ADAM_B1 ** ADAM_STEP)
        v_hat = nv / (1.0 - ADAM_B2 ** ADAM_STEP)
        d_ref[...] = -ADAM_LR * (m_hat / (jnp.sqrt(v_hat) + ADAM_EPS) + ADAM_WD * w_ref[...])
        nm_ref[...] = nm
        nv_ref[...] = nv

    row = pl.BlockSpec((tr, C), lambda i: (i, 0))
    return pl.pallas_call(
        body, name=name, grid=(R // tr,), in_specs=[row] * 4, out_specs=[row] * 3,
        out_shape=[jax.ShapeDtypeStruct((R, C), F32)] * 3, compiler_params=_params(("parallel",)),
    )(w, g, m, v)


def _adamw_layers(w, g0, g1, m, v, *, name):
    _, a, b = w.shape
    tr = _pick((512, 256, 128, 64, 32, 16, 8), a)

    def body(w_ref, g0_ref, g1_ref, m_ref, v_ref, g_ref, d_ref, nm_ref, nv_ref):
        gv = jnp.where(pl.program_id(1) == 0, g0_ref[...], g1_ref[...])
        nm = ADAM_B1 * m_ref[...] + (1.0 - ADAM_B1) * gv
        nv = ADAM_B2 * v_ref[...] + (1.0 - ADAM_B2) * (gv * gv)
        m_hat = nm / (1.0 - ADAM_B1 ** ADAM_STEP)
        v_hat = nv / (1.0 - ADAM_B2 ** ADAM_STEP)
        g_ref[...] = gv
        d_ref[...] = -ADAM_LR * (m_hat / (jnp.sqrt(v_hat) + ADAM_EPS) + ADAM_WD * w_ref[...])
        nm_ref[...] = nm
        nv_ref[...] = nv

    lay = pl.BlockSpec((None, tr, b), lambda i, l: (l, i, 0))
    row = pl.BlockSpec((tr, b), lambda i, l: (i, 0))
    return pl.pallas_call(
        body, name=name, grid=(a // tr, 2), in_specs=[lay, row, row, lay, lay], out_specs=[lay] * 4,
        out_shape=[jax.ShapeDtypeStruct(w.shape, F32)] * 4, compiler_params=_params(("parallel", "arbitrary")),
    )(w, g0, g1, m, v)


ANY = pl.BlockSpec(memory_space=pl.ANY)


def _place():
    x, y, c = lax.axis_index("x"), lax.axis_index("y"), lax.axis_index("c")
    return x, y, c, 2 * x + y


def _other_chip(x, y, mask):
    px = 1 - x if mask & 2 else x
    py = 1 - y if mask & 1 else y
    return px, py, 2 * px + py


MASKS = (1, 2, 3)


def _half(ref, c, lead=()):
    ah = ref.shape[-2] // 2
    return ref.at[(*lead, pl.ds(c * ah, ah), slice(None))]


SIBLING_ID = 0


def _sibling_barrier(x, y, c):
    sem = pltpu.get_barrier_semaphore()
    pl.semaphore_signal(sem, inc=1, device_id=(x, y, 1 - c), device_id_type=MESH)
    pl.semaphore_wait(sem, 1)


def _swap_halves(gs, *, name):
    n = len(gs)

    def body(*refs):
        g, out = refs[:n], refs[n:2 * n]
        send_sems, recv_sems = refs[2 * n], refs[2 * n + 1]
        x, y, c, _ = _place()
        _sibling_barrier(x, y, c)
        cps = []
        for i in range(n):
            ah = g[i].shape[1] // 2
            cp = pltpu.make_async_remote_copy(g[i].at[:, pl.ds((1 - c) * ah, ah), :], out[i],
                                              send_sems.at[i], recv_sems.at[i],
                                              device_id=(x, y, 1 - c), device_id_type=MESH)
            cp.start()
            cps.append(cp)
        for cp in cps:
            cp.wait()

    return pl.pallas_call(
        body, name=name, in_specs=[ANY] * n, out_specs=[ANY] * n,
        out_shape=[jax.ShapeDtypeStruct((g.shape[0], g.shape[1] // 2, g.shape[2]), g.dtype) for g in gs],
        scratch_shapes=[pltpu.SemaphoreType.DMA((n,)), pltpu.SemaphoreType.DMA((n,))],
        compiler_params=pltpu.CompilerParams(collective_id=SIBLING_ID),
    )(*gs)


HBM_SPEC = pl.BlockSpec(memory_space=pltpu.HBM)
SEM_SPEC = pl.BlockSpec(memory_space=pltpu.SEMAPHORE)
EFFECT = pltpu.SideEffectType.DATAFLOW_SIDE_EFFECTING


def _ici_ends(src, land, gather, x, y, c, chip, mask):
    px, py, pchip = _other_chip(x, y, mask)
    if gather:
        return _half(src, c), _half(land, c, (chip,)), _half(land, c, (pchip,)), (px, py, c)
    return src.at[pchip], land.at[chip], land.at[pchip], (px, py, c)


def _ici_start(groups, land_groups, gather, *, name, after=None):
    sizes = [len(g) for g in groups]
    n = sum(sizes)
    ng = len(groups)
    deps = [] if after is None else [after]

    def body(*refs):
        src, land = refs[:n], refs[n:2 * n]
        sems = refs[2 * n + len(deps):2 * n + len(deps) + 2 * ng]
        token = refs[-1]
        x, y, c, chip = _place()
        i = 0
        for g in range(ng):
            for j in range(sizes[g]):
                for k, mask in enumerate(MASKS):
                    s, d, _, peer = _ici_ends(src[i], land[i], gather, x, y, c, chip, mask)
                    pltpu.make_async_remote_copy(s, d, sems[2 * g].at[3 * j + k], sems[2 * g + 1].at[3 * j + k],
                                                 device_id=peer, device_id_type=MESH).start()
                i += 1
        token[...] = jnp.zeros_like(token)

    lands = [pltpu.with_memory_space_constraint(lax.empty(s.shape, s.dtype), pltpu.HBM)
             for g in land_groups for s in g]
    srcs = [pltpu.with_memory_space_constraint(s, pltpu.HBM) for g in groups for s in g]
    sem_shapes = [pltpu.SemaphoreType.DMA((3 * m,)) for m in sizes for _ in range(2)]
    out = pl.pallas_call(
        body, name=name,
        out_shape=(*sem_shapes, *[pltpu.HBM(s.shape, s.dtype) for s in srcs],
                   *[pltpu.HBM(s.shape, s.dtype) for s in lands], jax.ShapeDtypeStruct((8, 128), F32)),
        in_specs=[HBM_SPEC] * (2 * n) + [ANY] * len(deps),
        out_specs=(*[SEM_SPEC] * (2 * ng), *[HBM_SPEC] * (2 * n), pl.BlockSpec(memory_space=pltpu.VMEM)),
        input_output_aliases={i: 2 * ng + i for i in range(2 * n)},
        compiler_params=pltpu.CompilerParams(has_side_effects=EFFECT),
    )(*srcs, *lands, *deps)
    res, pos = [], 0
    for g in range(ng):
        res.append((out[2 * g], out[2 * g + 1], list(out[2 * ng + pos:2 * ng + pos + sizes[g]]),
                    list(out[2 * ng + n + pos:2 * ng + n + pos + sizes[g]])))
        pos += sizes[g]
    return res, out[-1]


def _ici_wait(send_sems, recv_sems, srcs, lands, gather, after, *, name):
    n = len(srcs)

    def body(*refs):
        src, land = refs[:n], refs[n:2 * n]
        send_sems, recv_sems = refs[2 * n], refs[2 * n + 1]
        x, y, c, chip = _place()
        for i in range(n):
            for k, mask in enumerate(MASKS):
                s, d, got, peer = _ici_ends(src[i], land[i], gather, x, y, c, chip, mask)
                pltpu.make_async_remote_copy(s, d, send_sems.at[3 * i + k], recv_sems.at[3 * i + k],
                                             device_id=peer, device_id_type=MESH).wait_send()
                pltpu.make_async_remote_copy(s, got, send_sems.at[3 * i + k], recv_sems.at[3 * i + k],
                                             device_id=peer, device_id_type=MESH).wait_recv()

    out = pl.pallas_call(
        body, name=name,
        out_shape=tuple(pltpu.HBM(s.shape, s.dtype) for s in (*srcs, *lands)),
        in_specs=[HBM_SPEC] * (2 * n) + [SEM_SPEC, SEM_SPEC, ANY],
        out_specs=tuple([HBM_SPEC] * (2 * n)),
        input_output_aliases={i: i for i in range(2 * n)},
        compiler_params=pltpu.CompilerParams(has_side_effects=EFFECT),
    )(*srcs, *lands, send_sems, recv_sems, after)
    return list(out[:n]), list(out[n:])


def _relay_halves(gs, *, name):
    n = len(gs)

    def body(*refs):
        out = refs[n:2 * n]
        send_sems, recv_sems = refs[2 * n], refs[2 * n + 1]
        x, y, c, _ = _place()
        _sibling_barrier(x, y, c)
        cps = []
        for i in range(n):
            for k, mask in enumerate(MASKS):
                _, _, pchip = _other_chip(x, y, mask)
                got = _half(out[i], c, (pchip,))
                cp = pltpu.make_async_remote_copy(got, got, send_sems.at[i, k], recv_sems.at[i, k],
                                                  device_id=(x, y, 1 - c), device_id_type=MESH)
                cp.start()
                cps.append(cp)
        for i in range(n):
            for k, mask in enumerate(MASKS):
                _, _, pchip = _other_chip(x, y, mask)
                theirs = _half(out[i], 1 - c, (pchip,))
                pltpu.make_async_remote_copy(theirs, theirs, send_sems.at[i, k], recv_sems.at[i, k],
                                             device_id=(x, y, 1 - c), device_id_type=MESH).wait_recv()
        for cp in cps:
            cp.wait_send()

    return pl.pallas_call(
        body, name=name, in_specs=[ANY] * n, out_specs=[ANY] * n,
        out_shape=[jax.ShapeDtypeStruct(g.shape, g.dtype) for g in gs],
        input_output_aliases={i: i for i in range(n)},
        scratch_shapes=[pltpu.SemaphoreType.DMA((n, 3)), pltpu.SemaphoreType.DMA((n, 3))],
        compiler_params=pltpu.CompilerParams(collective_id=SIBLING_ID),
    )(*gs)


def _share_halves(gs, *, name):
    n = len(gs)

    def body(*refs):
        out = refs[n:2 * n]
        send_sems, recv_sems = refs[2 * n], refs[2 * n + 1]
        x, y, c, _ = _place()
        _sibling_barrier(x, y, c)
        cps = []
        for i in range(n):
            cp = pltpu.make_async_remote_copy(_half(out[i], c), _half(out[i], c), send_sems.at[i], recv_sems.at[i],
                                              device_id=(x, y, 1 - c), device_id_type=MESH)
            cp.start()
            cps.append(cp)
        for i in range(n):
            theirs = _half(out[i], 1 - c)
            pltpu.make_async_remote_copy(theirs, theirs, send_sems.at[i], recv_sems.at[i],
                                         device_id=(x, y, 1 - c), device_id_type=MESH).wait_recv()
        for cp in cps:
            cp.wait_send()

    return pl.pallas_call(
        body, name=name, in_specs=[ANY] * n, out_specs=[ANY] * n,
        out_shape=[jax.ShapeDtypeStruct(g.shape, g.dtype) for g in gs],
        input_output_aliases={i: i for i in range(n)},
        scratch_shapes=[pltpu.SemaphoreType.DMA((n,)), pltpu.SemaphoreType.DMA((n,))],
        compiler_params=pltpu.CompilerParams(collective_id=SIBLING_ID),
    )(*gs)


def _gather_all(buf, *, name):
    r, L = buf.shape
    vmem = pl.BlockSpec(memory_space=pltpu.VMEM)
    masks = tuple(range(1, N_DEV))

    def body(buf_ref, out_ref, send_sems, recv_sems):
        x, y, c, _ = _place()
        me = 4 * x + 2 * y + c
        out_ref[me] = buf_ref[...]
        sends = []
        for k, mask in enumerate(masks):
            px = 1 - x if mask & 4 else x
            py = 1 - y if mask & 2 else y
            pc = 1 - c if mask & 1 else c
            cp = pltpu.make_async_remote_copy(buf_ref, out_ref.at[me], send_sems.at[k], recv_sems.at[k],
                                              device_id=(px, py, pc), device_id_type=MESH)
            cp.start()
            sends.append(cp)
        for k, mask in enumerate(masks):
            px = 1 - x if mask & 4 else x
            py = 1 - y if mask & 2 else y
            pc = 1 - c if mask & 1 else c
            pltpu.make_async_remote_copy(buf_ref, out_ref.at[4 * px + 2 * py + pc], send_sems.at[k],
                                         recv_sems.at[k], device_id=(px, py, pc), device_id_type=MESH).wait_recv()
        for cp in sends:
            cp.wait_send()

    return pl.pallas_call(
        body, name=name, in_specs=[vmem], out_specs=vmem,
        out_shape=jax.ShapeDtypeStruct((N_DEV, r, L), buf.dtype),
        scratch_shapes=[pltpu.SemaphoreType.DMA((N_DEV - 1,)), pltpu.SemaphoreType.DMA((N_DEV - 1,))],
    )(buf)


def _pack(arrs, lanes, row_mult=8):
    flat = jnp.concatenate([a.reshape(-1) for a in arrs])
    rows = -(-flat.shape[0] // lanes)
    rows = -(-rows // row_mult) * row_mult
    flat = jnp.pad(flat, (0, rows * lanes - flat.shape[0]))
    return flat.reshape(rows, lanes)


def _unpack(buf, shapes):
    flat = buf.reshape(-1)
    out, pos = [], 0
    for s in shapes:
        n = 1
        for d in s:
            n *= d
        out.append(flat[pos:pos + n].reshape(s))
        pos += n
    return out


def kernel(x, mem, norm_mix_g, w_in, conv_a_w, conv_b_w, conv_b_bias, ln_b_g, ln_b_b, w_out, norm_x_g, norm_mem_g, w_q, w_kv, w_xo, norm_ffn_g, w_up, w_down, final_g, loss_target, m_norm_mix_g, m_w_in, m_conv_a_w, m_conv_b_w, m_conv_b_bias, m_ln_b_g, m_ln_b_b, m_w_out, m_norm_x_g, m_norm_mem_g, m_w_q, m_w_kv, m_w_xo, m_norm_ffn_g, m_w_up, m_w_down, m_final_g, v_norm_mix_g, v_w_in, v_conv_a_w, v_conv_b_w, v_conv_b_bias, v_ln_b_g, v_ln_b_b, v_w_out, v_norm_x_g, v_norm_mem_g, v_w_q, v_w_kv, v_w_xo, v_norm_ffn_g, v_w_up, v_w_down, v_final_g):
    W = dict(norm_mix_g=norm_mix_g, w_in=w_in, conv_a_w=conv_a_w, conv_b_w=conv_b_w, conv_b_bias=conv_b_bias,
             ln_b_g=ln_b_g, ln_b_b=ln_b_b, w_out=w_out, norm_x_g=norm_x_g, norm_mem_g=norm_mem_g, w_q=w_q,
             w_kv=w_kv, w_xo=w_xo, norm_ffn_g=norm_ffn_g, w_up=w_up, w_down=w_down, final_g=final_g)
    MO = dict(norm_mix_g=m_norm_mix_g, w_in=m_w_in, conv_a_w=m_conv_a_w, conv_b_w=m_conv_b_w,
              conv_b_bias=m_conv_b_bias, ln_b_g=m_ln_b_g, ln_b_b=m_ln_b_b, w_out=m_w_out, norm_x_g=m_norm_x_g,
              norm_mem_g=m_norm_mem_g, w_q=m_w_q, w_kv=m_w_kv, w_xo=m_w_xo, norm_ffn_g=m_norm_ffn_g,
              w_up=m_w_up, w_down=m_w_down, final_g=m_final_g)
    VO = dict(norm_mix_g=v_norm_mix_g, w_in=v_w_in, conv_a_w=v_conv_a_w, conv_b_w=v_conv_b_w,
              conv_b_bias=v_conv_b_bias, ln_b_g=v_ln_b_g, ln_b_b=v_ln_b_b, w_out=v_w_out, norm_x_g=v_norm_x_g,
              norm_mem_g=v_norm_mem_g, w_q=v_w_q, w_kv=v_w_kv, w_xo=v_w_xo, norm_ffn_g=v_norm_ffn_g,
              w_up=v_w_up, w_down=v_w_down, final_g=v_final_g)
    names = list(W.keys())
    depth = norm_mix_g.shape[0]
    assert depth == 2, "the exchange splits the weights into one layer per core of a chip"
    c_idx = lax.axis_index("c")
    chip_idx = 2 * lax.axis_index("x") + lax.axis_index("y")

    xs = x[0]
    ms = mem[0]
    tgt = loss_target[0]
    S, D = xs.shape
    c_a = conv_a_w.shape[-1] * N_CHIPS
    c_loc = conv_a_w.shape[-1]

    meta = jnp.stack([c_idx, chip_idx]).astype(jnp.int32)
    shard_axis = dict(BIG)
    own = {(l, n): W[n][l].astype(BF16) for l in range(depth) for n, _ in BIG}

    conv_local = _pack([conv_a_w, conv_b_w], 128)
    conv_all = _gather_all(conv_local, name="gather_conv_weights")
    order = [(l, gi) for l in range(depth) for gi in range(len(FWD_GROUPS))]
    src_groups = [[own[(l, n)] for n in FWD_GROUPS[gi]] for l, gi in order]
    started, gather_token = _ici_start(
        src_groups, [[jax.ShapeDtypeStruct((N_CHIPS, *s.shape), s.dtype) for s in g] for g in src_groups], True,
        name="gather_weights_start", after=conv_all)
    started = dict(zip(order, started))
    Wb = [dict() for _ in range(depth)]

    def weights_ready(l, gi, after):
        send_sems, recv_sems, srcs, lands = started[(l, gi)]
        srcs, lands = _ici_wait(send_sems, recv_sems, srcs, lands, True, after, name=f"gather_weights_l{l}_g{gi}_wait")
        full = _relay_halves(lands, name=f"gather_weights_l{l}_g{gi}_relay")
        for n, g, o in zip(FWD_GROUPS[gi], full, srcs):
            g = lax.dynamic_update_slice(g, o[None], (chip_idx, 0, 0))
            Wb[l][n] = g.reshape(-1, g.shape[-1]) if shard_axis[n] == 0 else g
    na = depth * K_A * c_loc
    nbw = depth * K_B * c_loc
    ca_parts, cb_parts = [], []
    for j in range(N_CHIPS):
        fl = conv_all[2 * j].reshape(-1)
        ca_parts.append(fl[:na].reshape(depth, K_A, c_loc))
        cb_parts.append(fl[na:na + nbw].reshape(depth, K_B, c_loc))
    conv_a_full = jnp.concatenate(ca_parts, axis=-1)
    conv_b_full = jnp.concatenate(cb_parts, axis=-1)

    saved = []
    h = xs
    for l in range(depth):
        wl = Wb[l]
        t = f"l{l}_"
        if l == 0:
            u = _rms_fwd(h, norm_mix_g[l:l + 1], name=t + "rms_mix", after=gather_token)
        weights_ready(l, 0, u)
        z = _mm(u, wl["w_in"], b_stack=True, name=t + "mm_in", bm=2048)
        y_a = _mixer_a_fwd(z, conv_a_full[l], name=t + "mixer_a")
        cb = _mixer_b_fwd(z, conv_b_full[l], conv_b_bias[l:l + 1], name=t + "mixer_b")
        yy = _ln_silu_fwd(cb, ln_b_g[l:l + 1], ln_b_b[l:l + 1], y_a, name=t + "ln_silu")
        weights_ready(l, 1, yy)
        h2, q_in = _mm(yy, wl["w_out"], res=h, epi="rms_fwd", norm=norm_x_g[l:l + 1], name=t + "mm_out")
        q = _mm(q_in, wl["w_q"], out_dtype=BF16, name=t + "mm_q")
        mn = _rms_fwd(ms, norm_mem_g[l:l + 1], name=t + "rms_mem")
        kv = _mm(mn, wl["w_kv"], b_stack=True, out_dtype=BF16, name=t + "mm_kv")
        o = _attn_fwd(q, kv, name=t + "attn", tq=ATTN_ROW_TILE)
        h3, u3 = _mm(o, wl["w_xo"], res=h2, epi="rms_fwd", norm=norm_ffn_g[l:l + 1], name=t + "mm_xo")
        weights_ready(l, 2, h3)
        a_pre, hh = _mm(u3, wl["w_up"], b_stack=True, out_dtype=BF16, epi="sqrelu", name=t + "mm_up")
        saved.append(dict(h=h, u=u, z=z, cb=cb, yy=yy, h2=h2, q_in=q_in, q=q, mn=mn, kv=kv, o=o, h3=h3,
                          u3=u3, a_pre=a_pre, hh=hh))
        if l + 1 < depth:
            h, u = _mm(hh, wl["w_down"], res=h3, epi="rms_fwd", norm=norm_mix_g[l + 1:l + 2], name=t + "mm_down")
        else:
            h = _mm(hh, wl["w_down"], res=h3, name=t + "mm_down")

    loss_vec, dh, dhb, d_final = _loss_head(h, final_g.reshape(1, D), tgt, name="loss_head")

    GW = [dict() for _ in range(depth)]
    GS = [dict() for _ in range(depth)]
    pending = []

    def reduce_start(l, gi):
        group = BWD_GROUPS[gi]

        def by_chip(g, n):
            return g if g.ndim == 3 else g.reshape(N_CHIPS, *W[n].shape[1:])

        gs = [by_chip(GW[l][n][0], n) for n in group]
        from_sibling = _swap_halves([by_chip(GW[l][n][1], n) for n in group],
                                    name=f"grad_swap_sibling_l{l}_g{gi}")
        prs = [_pair_sum(g, r, meta, name=f"grad_pair_sum_l{l}_{n}") for g, r, n in zip(gs, from_sibling, group)]
        (st,), token = _ici_start([prs], [prs], False, name=f"grad_scatter_chips_l{l}_g{gi}_start")
        pending.append((l, group, st))
        return token

    for l in reversed(range(depth)):
        wl, sv = Wb[l], saved[l]
        t = f"l{l}_b_"
        GW[l]["w_down"] = _mm(sv["hh"], dhb, ta=True, epi="with_bf16", name=t + "dw_down")
        da = _mm(dhb, wl["w_down"], tb=True, out_dtype=BF16, epi="dsqrelu", aux=sv["a_pre"], name=t + "d_hidden")
        GW[l]["w_up"] = _mm(sv["u3"], da, ta=True, o_stack=N_CHIPS, epi="with_bf16", name=t + "dw_up")
        fused = dict(epi="rms_bwd", bm=512)
        dh, dhb, GS[l]["norm_ffn_g"] = _mm(da, wl["w_up"], tb=True, b_stack=True, res=dh,
                                           norm=(sv["h3"], norm_ffn_g[l:l + 1]), name=t + "d_u3", **fused)
        GW[l]["w_xo"] = _mm(sv["o"], dhb, ta=True, epi="with_bf16", name=t + "dw_xo")
        d_o = _mm(dhb, wl["w_xo"], tb=True, out_dtype=BF16, name=t + "d_o")
        dq, dkv = _attn_bwd(sv["q"], sv["kv"], d_o, name=t + "attn", tq=ATTN_ROW_TILE)
        GW[l]["w_q"] = _mm(sv["q_in"], dq, ta=True, epi="with_bf16", name=t + "dw_q")
        dkvb = dkv.astype(BF16)
        GW[l]["w_kv"] = _mm(sv["mn"], dkvb, ta=True, o_stack=N_CHIPS, epi="with_bf16", name=t + "dw_kv")
        dmn = _mm(dkvb, wl["w_kv"], tb=True, b_stack=True, name=t + "d_mem")
        _, _, GS[l]["norm_mem_g"] = _rms_bwd(ms, norm_mem_g[l:l + 1], dmn, None, name=t + "rms_mem")
        dh, dhb, GS[l]["norm_x_g"] = _mm(dq, wl["w_q"], tb=True, res=dh, norm=(sv["h2"], norm_x_g[l:l + 1]),
                                         after=reduce_start(l, 0), name=t + "d_q_in", **fused)
        GW[l]["w_out"] = _mm(sv["yy"], dhb, ta=True, epi="with_bf16", name=t + "dw_out")
        dyy = _mm(dhb, wl["w_out"], tb=True, name=t + "d_y")
        dcb, GS[l]["ln_b_g"], GS[l]["ln_b_b"] = _ln_silu_bwd(sv["cb"], ln_b_g[l:l + 1], ln_b_b[l:l + 1], dyy, 1,
                                                             name=t + "ln_silu")
        db_, dc_, dh_, GS[l]["conv_a_w"] = _mixer_a_bwd(sv["z"], conv_a_full[l], dyy, name=t + "mixer_a")
        dv_, dg_, GS[l]["conv_b_w"], GS[l]["conv_b_bias"] = _mixer_b_bwd(sv["z"], conv_b_full[l], dcb,
                                                                         name=t + "mixer_b")
        dz = jnp.concatenate([db_, dc_, dh_, dv_, dg_], axis=1)
        GW[l]["w_in"] = _mm(sv["u"], dz, ta=True, o_stack=N_CHIPS, epi="with_bf16", name=t + "dw_in")
        dh, dhb, GS[l]["norm_mix_g"] = _mm(dz, wl["w_in"], tb=True, b_stack=True, res=dh,
                                           norm=(sv["h"], norm_mix_g[l:l + 1]), after=reduce_start(l, 1),
                                           name=t + "d_u", **fused)
    grad_x = dh[None]

    after = GS[0]["norm_mix_g"]
    keys, halves = [], []
    for l, group, (send_sems, recv_sems, srcs, lands) in pending:
        prs, pieces = _ici_wait(send_sems, recv_sems, srcs, lands, False, after,
                                name=f"grad_scatter_chips_l{l}_{group[0]}_wait")
        for n, p, q in zip(group, prs, pieces):
            keys.append((l, n))
            halves.append(_chip_sum(p, q, meta, name=f"grad_chip_sum_l{l}_{n}"))
        after = halves[-1]
    reduced = dict(zip(keys, _share_halves(halves, name="grad_share_sibling")))

    grads, deltas, new_m, new_v = {}, {}, {}, {}
    for n, _ in BIG:
        grads[n], deltas[n], new_m[n], new_v[n] = _adamw_layers(W[n], reduced[(0, n)], reduced[(1, n)], MO[n], VO[n],
                                                               name="adamw_" + n)

    small = [n for n in names if n not in dict(BIG)]
    full_shapes = {n: ((depth, W[n].shape[1], c_a) if n in ("conv_a_w", "conv_b_w") else W[n].shape)
                   for n in small}

    def small_grad(n):
        if n == "final_g":
            return d_final.reshape(W[n].shape)
        return jnp.stack([GS[l][n].reshape(full_shapes[n][1:]) for l in range(depth)])

    part = _pack([small_grad(n) for n in small] + [loss_vec[0, :1]], LANES)
    everyone = _gather_all(part, name="gather_small_grads")
    total = _sum_leading(everyone, name="small_grad_sum")
    *small_sums, loss = _unpack(total, [full_shapes[n] for n in small] + [()])
    full_grads = dict(zip(small, small_sums))
    for n in ("conv_a_w", "conv_b_w"):
        full_grads[n] = lax.dynamic_slice_in_dim(full_grads[n], chip_idx * c_loc, c_loc, axis=2)
    shapes = [W[n].shape for n in small]
    d_s, m_s, v_s = _adamw(_pack([W[n] for n in small], 128), _pack([full_grads[n] for n in small], 128),
                           _pack([MO[n] for n in small], 128), _pack([VO[n] for n in small], 128),
                           name="adamw_small")
    for n, d, nm, nv in zip(small, _unpack(d_s, shapes), _unpack(m_s, shapes), _unpack(v_s, shapes)):
        grads[n], deltas[n], new_m[n], new_v[n] = full_grads[n], d, nm, nv

    return (loss, grad_x, *[grads[n] for n in names], *[deltas[n] for n in names],
            *[new_m[n] for n in names], *[new_v[n] for n in names])
```
